```python
import math
import jax, jax.numpy as jnp
from jax import lax
import numpy as np

D_MODEL = 1024
BATCH = 2
SEQ = 8192
DEPTH = 2

N_META = 16
BLOCK = 128
PAD = BLOCK - N_META

SSD_EXPAND = 2
SSD_D_INNER = SSD_EXPAND * D_MODEL
SSD_HEAD_DIM = 64
SSD_HEADS = SSD_D_INNER // SSD_HEAD_DIM
SSD_GROUPS = 4
SSD_HEADS_PER_GROUP = SSD_HEADS // SSD_GROUPS
SSD_STATE = 128
SSD_CONV = 4
SSD_CONV_CH = SSD_D_INNER + 2 * SSD_GROUPS * SSD_STATE

FOX_HEADS = 16
FOX_HEAD_DIM = 64
FOX_WIDTH = FOX_HEADS * FOX_HEAD_DIM

EVEN_IN = SSD_D_INNER + SSD_CONV_CH + SSD_HEADS + 3 * FOX_WIDTH + FOX_HEADS
EVEN_MIX = SSD_D_INNER + FOX_WIDTH

DIFF_HEADS = 8
DIFF_HEAD_DIM = 64
DIFF_V_DIM = 2 * DIFF_HEAD_DIM
DIFF_Q_WIDTH = DIFF_HEADS * 2 * DIFF_HEAD_DIM
DIFF_V_WIDTH = DIFF_HEADS * DIFF_V_DIM
DIFF_IN = 2 * DIFF_Q_WIDTH + DIFF_V_WIDTH

N_BUCKETS = 32
MAX_DISTANCE = 128

D_FF = 2816

ALPHA = (2 * DEPTH) ** 0.25
BETA = (8 * DEPTH) ** -0.25
N_EVEN = (DEPTH + 1) // 2
N_ODD = DEPTH // 2

LN_EPS = 1e-5
RMS_EPS = 1e-5
NEG = -1e30

kernel_name = 'hybrid_ssd_fox_diffattn_macaron_deepnorm'


def _layer_norm(x, g, b):
    xf = x.astype(jnp.float32)
    mu = jnp.mean(xf, axis=-1, keepdims=True)
    var = jnp.mean(jnp.square(xf - mu), axis=-1, keepdims=True)
    return ((xf - mu) * lax.rsqrt(var + LN_EPS) * g + b).astype(x.dtype)


def _rms_norm(x, w):
    xf = x.astype(jnp.float32)
    return (xf * lax.rsqrt(jnp.mean(jnp.square(xf), axis=-1, keepdims=True) + RMS_EPS) * w).astype(x.dtype)


def _swiglu(x, w_gate, w_up, w_down):
    return (jax.nn.silu(x @ w_gate) * (x @ w_up)) @ w_down


def _left_pad(u):
    return jnp.pad(u, [(0, 0), (PAD, 0)] + [(0, 0)] * (u.ndim - 2))


def _causal_depthwise_conv(u, w, bias):
    ch = u.shape[-1]
    out = lax.conv_general_dilated(u, w[:, None, :].astype(u.dtype), window_strides=(1,),
                                   padding=[(SSD_CONV - 1, 0)],
                                   dimension_numbers=('NWC', 'WIO', 'NWC'),
                                   feature_group_count=ch)
    return out + bias


def _ssd_chunked(X, dA, Bm, Cm):
    b, lp = X.shape[:2]
    nc = lp // BLOCK
    G, R, P, N = SSD_GROUPS, SSD_HEADS_PER_GROUP, SSD_HEAD_DIM, SSD_STATE
    X = X.reshape(b, nc, BLOCK, G, R, P)
    Bm = Bm.reshape(b, nc, BLOCK, G, N)
    Cm = Cm.reshape(b, nc, BLOCK, G, N)
    dA = dA.astype(jnp.float32).reshape(b, nc, BLOCK, G, R).transpose(0, 3, 4, 1, 2)
    a_cum = jnp.cumsum(dA, axis=-1)
    idx = jnp.arange(BLOCK)
    causal = idx[:, None] >= idx[None, :]
    decay_ls = jnp.exp(jnp.where(causal, a_cum[..., :, None] - a_cum[..., None, :], -jnp.inf))
    cb = jnp.einsum('bclgn,bcsgn->bgcls', Cm, Bm)
    y_diag = jnp.einsum('bgcls,bgrcls,bcsgrp->bclgrp', cb, decay_ls, X)
    decay_to_end = jnp.exp(a_cum[..., -1:] - a_cum)
    chunk_states = jnp.einsum('bclgn,bgrcl,bclgrp->bcgrpn', Bm, decay_to_end, X)
    chunk_decay = jnp.exp(a_cum[..., -1])

    def step(state, inp):
        s_c, d_c = inp
        return state * d_c[..., None, None] + s_c, state

    init = jnp.zeros_like(chunk_states[:, 0])
    _, prev = lax.scan(step, init, (chunk_states.transpose(1, 0, 2, 3, 4, 5),
                                    chunk_decay.transpose(3, 0, 1, 2)))
    y_off = jnp.einsum('bclgn,cbgrpn,bgrcl->bclgrp', Cm, prev, jnp.exp(a_cum))
    return (y_diag + y_off).reshape(b, lp, G * R, P)


def _fox_attention(q, k, v, log_f):
    b, lp, H, dh = q.shape
    nb = lp // BLOCK
    c = jnp.cumsum(log_f, axis=1).transpose(0, 2, 1)
    key_pos = jnp.arange(lp)
    scale = dh ** -0.5

    def block(i):
        start = i * BLOCK
        qb = lax.dynamic_slice_in_dim(q, start, BLOCK, axis=1)
        cq = lax.dynamic_slice_in_dim(c, start, BLOCK, axis=2)
        qpos = start + jnp.arange(BLOCK)
        s = jnp.einsum('bqhd,bkhd->bhqk', qb, k).astype(jnp.float32) * scale
        s = s + cq[..., :, None] - c[..., None, :]
        mask = (key_pos[None, :] <= qpos[:, None]) & (key_pos[None, :] >= PAD)
        p = jax.nn.softmax(jnp.where(mask, s, NEG), axis=-1)
        return jnp.einsum('bhqk,bkhd->bqhd', p.astype(v.dtype), v)

    out = lax.map(block, jnp.arange(nb))
    return out.transpose(1, 0, 2, 3, 4).reshape(b, lp, H, dh)


def _t5_bucket(n):
    max_exact = N_BUCKETS // 2
    nf = jnp.maximum(n, 1).astype(jnp.float32)
    large = max_exact + (jnp.log(nf / max_exact) / math.log(MAX_DISTANCE / max_exact)
                         * (N_BUCKETS - max_exact)).astype(jnp.int32)
    large = jnp.minimum(large, N_BUCKETS - 1)
    return jnp.where(n < max_exact, n, large)


def _diff_attention(q, k, v, rel_table, lam):
    b, lp = q.shape[:2]
    nb = lp // BLOCK
    key_pos = jnp.arange(lp)
    scale = DIFF_HEAD_DIM ** -0.5

    def block(i):
        start = i * BLOCK
        qb = lax.dynamic_slice_in_dim(q, start, BLOCK, axis=1)
        qpos = start + jnp.arange(BLOCK)
        dist = qpos[:, None] - key_pos[None, :]
        bias = rel_table[_t5_bucket(jnp.maximum(dist, 0))].astype(jnp.float32).transpose(2, 0, 1)
        s = jnp.einsum('bqhcd,bkhcd->bchqk', qb, k).astype(jnp.float32) * scale + bias
        mask = (dist >= 0) & (key_pos[None, :] >= PAD)
        p = jax.nn.softmax(jnp.where(mask, s, NEG), axis=-1)
        attn = p[:, 0] - lam * p[:, 1]
        return jnp.einsum('bhqk,bkhe->bqhe', attn.astype(v.dtype), v)

    out = lax.map(block, jnp.arange(nb))
    return out.transpose(1, 0, 2, 3, 4).reshape(b, lp, DIFF_HEADS, DIFF_V_DIM)


def _even_mixer(h, w_in, conv_w, conv_b, dt_bias, a_log, d_skip, norm_w, f_bias, w_out):
    b, L, _ = h.shape
    proj = h @ w_in
    splits = np.cumsum([SSD_D_INNER, SSD_CONV_CH, SSD_HEADS, FOX_WIDTH, FOX_WIDTH, FOX_WIDTH]).tolist()
    z, xbc, dt_raw, q, k, v, f_raw = jnp.split(proj, splits, axis=-1)
    xbc = jax.nn.silu(_causal_depthwise_conv(xbc, conv_w, conv_b))
    xs, Bm, Cm = jnp.split(xbc, [SSD_D_INNER, SSD_D_INNER + SSD_GROUPS * SSD_STATE], axis=-1)
    dt = jax.nn.softplus(dt_raw.astype(jnp.float32) + dt_bias)
    A = -jnp.exp(a_log.astype(jnp.float32))
    xs = xs.reshape(b, L, SSD_HEADS, SSD_HEAD_DIM)
    y = _ssd_chunked(_left_pad(xs * dt[..., None]), _left_pad(dt * A),
                     _left_pad(Bm.reshape(b, L, SSD_GROUPS, SSD_STATE)),
                     _left_pad(Cm.reshape(b, L, SSD_GROUPS, SSD_STATE)))[:, PAD:]
    y = (y + d_skip[:, None] * xs).astype(h.dtype).reshape(b, L, SSD_D_INNER) * jax.nn.silu(z)
    y = _rms_norm(y.reshape(b, L, SSD_GROUPS, SSD_D_INNER // SSD_GROUPS), 1.0).reshape(b, L, SSD_D_INNER) * norm_w
    log_f = jax.nn.log_sigmoid(f_raw.astype(jnp.float32) + f_bias)
    o = _fox_attention(_left_pad(q.reshape(b, L, FOX_HEADS, FOX_HEAD_DIM)),
                       _left_pad(k.reshape(b, L, FOX_HEADS, FOX_HEAD_DIM)),
                       _left_pad(v.reshape(b, L, FOX_HEADS, FOX_HEAD_DIM)),
                       _left_pad(log_f))[:, PAD:]
    o = o.reshape(b, L, FOX_WIDTH)
    return jnp.concatenate([y.astype(h.dtype), o], axis=-1) @ w_out


def _odd_mixer(h, w_qkv, lam_q1, lam_k1, lam_q2, lam_k2, subln_w, w_o, rel_table, lambda_init):
    b, L, _ = h.shape
    q, k, v = jnp.split(h @ w_qkv, [DIFF_Q_WIDTH, 2 * DIFF_Q_WIDTH], axis=-1)
    q = q.reshape(b, L, DIFF_HEADS, 2, DIFF_HEAD_DIM)
    k = k.reshape(b, L, DIFF_HEADS, 2, DIFF_HEAD_DIM)
    v = v.reshape(b, L, DIFF_HEADS, DIFF_V_DIM)
    lam = (jnp.exp(jnp.sum(lam_q1.astype(jnp.float32) * lam_k1))
           - jnp.exp(jnp.sum(lam_q2.astype(jnp.float32) * lam_k2)) + lambda_init)
    o = _diff_attention(_left_pad(q), _left_pad(k), _left_pad(v), rel_table, lam)[:, PAD:]
    o = _rms_norm(o, subln_w) * (1.0 - lambda_init)
    return o.reshape(b, L, DIFF_V_WIDTH) @ w_o


def setup_inputs(seed: int = 0) -> dict:
    key = jax.random.key(seed)
    ks = jax.random.split(key, 32)
    f32 = jnp.float32

    def nrm(k, shape, scale):
        return jax.random.normal(k, shape, f32) * scale

    dt0 = jnp.exp(jax.random.uniform(ks[12], (N_EVEN, SSD_HEADS), f32, math.log(1e-3), math.log(1e-1)))
    return {
        'x': nrm(ks[0], (BATCH, SEQ, D_MODEL), 1.0),
        'meta_tokens': nrm(ks[1], (N_META, D_MODEL), 1.0),
        'ln_gain': 1.0 + nrm(ks[2], (DEPTH, 3, D_MODEL), 0.02),
        'ln_bias': nrm(ks[3], (DEPTH, 3, D_MODEL), 0.02),
        'ffn1_w_gate': nrm(ks[4], (DEPTH, D_MODEL, D_FF), D_MODEL ** -0.5),
        'ffn1_w_up': nrm(ks[5], (DEPTH, D_MODEL, D_FF), D_MODEL ** -0.5),
        'ffn1_w_down': nrm(ks[6], (DEPTH, D_FF, D_MODEL), BETA * D_FF ** -0.5),
        'ffn2_w_gate': nrm(ks[7], (DEPTH, D_MODEL, D_FF), D_MODEL ** -0.5),
        'ffn2_w_up': nrm(ks[8], (DEPTH, D_MODEL, D_FF), D_MODEL ** -0.5),
        'ffn2_w_down': nrm(ks[9], (DEPTH, D_FF, D_MODEL), BETA * D_FF ** -0.5),
        'even_w_in': nrm(ks[10], (N_EVEN, D_MODEL, EVEN_IN), D_MODEL ** -0.5),
        'even_conv_w': nrm(ks[11], (N_EVEN, SSD_CONV, SSD_CONV_CH), SSD_CONV ** -0.5),
        'even_conv_b': nrm(ks[13], (N_EVEN, SSD_CONV_CH), 0.02),
        'ssd_dt_bias': dt0 + jnp.log(-jnp.expm1(-dt0)),
        'ssd_a_log': jnp.log(jax.random.uniform(ks[14], (N_EVEN, SSD_HEADS), f32, 1.0, 16.0)),
        'ssd_d_skip': 1.0 + nrm(ks[15], (N_EVEN, SSD_HEADS), 0.02),
        'ssd_norm_w': 1.0 + nrm(ks[16], (N_EVEN, SSD_D_INNER), 0.02),
        'fox_f_bias': jax.random.uniform(ks[17], (N_EVEN, FOX_HEADS), f32, 1.0, 6.0),
        'even_w_out': nrm(ks[18], (N_EVEN, EVEN_MIX, D_MODEL), BETA * EVEN_MIX ** -0.5),
        'diff_w_qkv': nrm(ks[19], (N_ODD, D_MODEL, DIFF_IN), D_MODEL ** -0.5),
        'diff_lambda_q1': nrm(ks[20], (N_ODD, DIFF_HEAD_DIM), 0.1),
        'diff_lambda_k1': nrm(ks[21], (N_ODD, DIFF_HEAD_DIM), 0.1),
        'diff_lambda_q2': nrm(ks[22], (N_ODD, DIFF_HEAD_DIM), 0.1),
        'diff_lambda_k2': nrm(ks[23], (N_ODD, DIFF_HEAD_DIM), 0.1),
        'diff_subln_w': 1.0 + nrm(ks[24], (N_ODD, DIFF_V_DIM), 0.02),
        'diff_w_o': nrm(ks[25], (N_ODD, DIFF_V_WIDTH, D_MODEL), BETA * DIFF_V_WIDTH ** -0.5),
        'rel_bias_table': nrm(ks[26], (N_BUCKETS, DIFF_HEADS), 0.2),
    }


def reference(x, meta_tokens, ln_gain, ln_bias, ffn1_w_gate, ffn1_w_up, ffn1_w_down,
              ffn2_w_gate, ffn2_w_up, ffn2_w_down, even_w_in, even_conv_w, even_conv_b,
              ssd_dt_bias, ssd_a_log, ssd_d_skip, ssd_norm_w, fox_f_bias, even_w_out,
              diff_w_qkv, diff_lambda_q1, diff_lambda_k1, diff_lambda_q2, diff_lambda_k2,
              diff_subln_w, diff_w_o, rel_bias_table):
    b = x.shape[0]
    meta = jnp.broadcast_to(meta_tokens[None].astype(x.dtype), (b, N_META, D_MODEL))
    h = jnp.concatenate([meta, x], axis=1)
    for l in range(DEPTH):
        h = _layer_norm(ALPHA * h + 0.5 * _swiglu(h, ffn1_w_gate[l], ffn1_w_up[l], ffn1_w_down[l]),
                        ln_gain[l, 0], ln_bias[l, 0])
        if l % 2 == 0:
            e = l // 2
            m = _even_mixer(h, even_w_in[e], even_conv_w[e], even_conv_b[e], ssd_dt_bias[e],
                            ssd_a_log[e], ssd_d_skip[e], ssd_norm_w[e], fox_f_bias[e], even_w_out[e])
        else:
            o = l // 2
            lambda_init = 0.8 - 0.6 * math.exp(-0.3 * l)
            m = _odd_mixer(h, diff_w_qkv[o], diff_lambda_q1[o], diff_lambda_k1[o], diff_lambda_q2[o],
                           diff_lambda_k2[o], diff_subln_w[o], diff_w_o[o], rel_bias_table, lambda_init)
        h = _layer_norm(ALPHA * h + m.astype(h.dtype), ln_gain[l, 1], ln_bias[l, 1])
        h = _layer_norm(ALPHA * h + 0.5 * _swiglu(h, ffn2_w_gate[l], ffn2_w_up[l], ffn2_w_down[l]),
                        ln_gain[l, 2], ln_bias[l, 2])
    return h[:, N_META:]
```

```python
import functools
import math

import numpy as np
import jax
import jax.numpy as jnp
from jax import lax
from jax.experimental import pallas as pl
from jax.experimental.pallas import tpu as pltpu

F32 = jnp.float32
BF16 = jnp.bfloat16

D_MODEL = 1024
SEQ = 8192
DEPTH = 2
N_META = 16
CHUNK = 128
SSD_D_INNER = 2048
SSD_HEAD_DIM = 64
SSD_HEADS = 32
SSD_GROUPS = 4
SSD_GROUP_W = SSD_D_INNER // SSD_GROUPS
SSD_STATE = 128
SSD_CONV = 4
SSD_CONV_CH = SSD_D_INNER + 2 * SSD_GROUPS * SSD_STATE
FOX_HEADS = 16
FOX_WIDTH = 1024
HEAD_DIM = 64
DIFF_HEADS = 8
N_BUCKETS = 32
D_FF = 2816
ALPHA = (2 * DEPTH) ** 0.25
LN_EPS = 1e-5
RMS_EPS = 1e-5
NEG = -1e30

LANES = 128
LP = 8448
PADL = LP - SEQ - N_META
ROW_TILE = 512
OUT_TILE = 256
ATT_TILE = 768
CONV_HALO = 8
SMALL_W = LANES
VMEM_LIMIT = 56 * 1024 * 1024

assert PADL % CHUNK == CHUNK - N_META
assert LP % ATT_TILE == 0 and LP % CHUNK == 0 and (2 * LP) % ROW_TILE == 0
assert PADL <= OUT_TILE and (LP - OUT_TILE) == SEQ


def _const_spec(shape):
    nd = len(shape)
    return pl.BlockSpec(shape, lambda *_: (0,) * nd, pipeline_mode=pl.Buffered(1))


def _layer_norm(r, g, b):
    mu = jnp.mean(r, axis=-1, keepdims=True)
    d = r - mu
    var = jnp.mean(d * d, axis=-1, keepdims=True)
    return d * lax.rsqrt(var + LN_EPS) * g + b


def _silu(x):
    return x / (1.0 + jnp.exp(-x))


def _softplus(x):
    return jnp.maximum(x, 0.0) + jnp.log(1.0 + jnp.exp(-jnp.abs(x)))


def _ffn_ln_body(h_ref, wg_ref, wu_ref, wd_ref, g_ref, b_ref, o_ref):
    h = h_ref[...]
    hb = h.astype(BF16)
    g = jnp.dot(hb, wg_ref[...], preferred_element_type=F32)
    u = jnp.dot(hb, wu_ref[...], preferred_element_type=F32)
    a = (_silu(g) * u).astype(BF16)
    y = jnp.dot(a, wd_ref[...], preferred_element_type=F32)
    o_ref[...] = _layer_norm(ALPHA * h + 0.5 * y, g_ref[...], b_ref[...])


def _ffn_ln(hf, wg, wu, wd, g, b):
    rows = hf.shape[0]
    return pl.pallas_call(
        _ffn_ln_body,
        grid=(rows // ROW_TILE,),
        in_specs=[
            pl.BlockSpec((ROW_TILE, D_MODEL), lambda i: (i, 0)),
            _const_spec((D_MODEL, D_FF)),
            _const_spec((D_MODEL, D_FF)),
            _const_spec((D_FF, D_MODEL)),
            _const_spec((1, D_MODEL)),
            _const_spec((1, D_MODEL)),
        ],
        out_specs=pl.BlockSpec((ROW_TILE, D_MODEL), lambda i: (i, 0)),
        out_shape=jax.ShapeDtypeStruct((rows, D_MODEL), F32),
        compiler_params=pltpu.CompilerParams(
            dimension_semantics=("parallel",), vmem_limit_bytes=VMEM_LIMIT),
        name="ffn_ln",
    )(hf, wg, wu, wd, g, b)


def _ffn_ln_final_body(h_ref, wg_ref, wu_ref, wd_ref, g_ref, b_ref, o_ref):
    @pl.when(pl.program_id(1) > 0)
    def _():
        _ffn_ln_body(h_ref, wg_ref, wu_ref, wd_ref, g_ref, b_ref, o_ref)


def _ffn_ln_final(h3, wg, wu, wd, g, b):
    nb = h3.shape[0]
    return pl.pallas_call(
        _ffn_ln_final_body,
        grid=(nb, LP // OUT_TILE),
        in_specs=[
            pl.BlockSpec((None, OUT_TILE, D_MODEL), lambda bi, i: (bi, i, 0)),
            _const_spec((D_MODEL, D_FF)),
            _const_spec((D_MODEL, D_FF)),
            _const_spec((D_FF, D_MODEL)),
            _const_spec((1, D_MODEL)),
            _const_spec((1, D_MODEL)),
        ],
        out_specs=pl.BlockSpec((None, OUT_TILE, D_MODEL),
                               lambda bi, i: (bi, jnp.maximum(i - 1, 0), 0)),
        out_shape=jax.ShapeDtypeStruct((nb, SEQ, D_MODEL), F32),
        compiler_params=pltpu.CompilerParams(
            dimension_semantics=("arbitrary", "arbitrary"), vmem_limit_bytes=VMEM_LIMIT),
        name="ffn_ln_final",
    )(h3, wg, wu, wd, g, b)


def _proj_body(h_ref, w_ref, *o_refs):
    hb = h_ref[...].astype(BF16)
    off = 0
    for o_ref in o_refs:
        n = o_ref.shape[-1]
        o_ref[...] = jnp.dot(hb, w_ref[:, off:off + n],
                             preferred_element_type=F32).astype(o_ref.dtype)
        off += n


def _proj(hf, w, widths, dtypes, name):
    rows = hf.shape[0]
    tile = ROW_TILE // 2
    return pl.pallas_call(
        _proj_body,
        grid=(rows // tile,),
        in_specs=[pl.BlockSpec((tile, D_MODEL), lambda i: (i, 0)),
                  _const_spec(w.shape)],
        out_specs=[pl.BlockSpec((tile, n), lambda i: (i, 0)) for n in widths],
        out_shape=[jax.ShapeDtypeStruct((rows, n), dt) for n, dt in zip(widths, dtypes)],
        compiler_params=pltpu.CompilerParams(
            dimension_semantics=("parallel",), vmem_limit_bytes=VMEM_LIMIT),
        name=name,
    )(hf, w)


def _outproj_ln_body(n_in, h_ref, *refs):
    a_refs = refs[:n_in]
    w_refs = refs[n_in:2 * n_in]
    g_ref, b_ref, o_ref = refs[2 * n_in:]
    m = jnp.dot(a_refs[0][...], w_refs[0][...], preferred_element_type=F32)
    for a_ref, w_ref in zip(a_refs[1:], w_refs[1:]):
        m = m + jnp.dot(a_ref[...], w_ref[...], preferred_element_type=F32)
    o_ref[...] = _layer_norm(ALPHA * h_ref[...] + m, g_ref[...], b_ref[...])


def _outproj_ln(hf, acts, ws, g, b, name):
    rows = hf.shape[0]
    n_in = len(acts)
    in_specs = [pl.BlockSpec((ROW_TILE, D_MODEL), lambda i: (i, 0))]
    in_specs += [pl.BlockSpec((ROW_TILE, a.shape[1]), lambda i: (i, 0)) for a in acts]
    in_specs += [_const_spec(w.shape) for w in ws]
    in_specs += [_const_spec((1, D_MODEL)), _const_spec((1, D_MODEL))]
    return pl.pallas_call(
        functools.partial(_outproj_ln_body, n_in),
        grid=(rows // ROW_TILE,),
        in_specs=in_specs,
        out_specs=pl.BlockSpec((ROW_TILE, D_MODEL), lambda i: (i, 0)),
        out_shape=jax.ShapeDtypeStruct((rows, D_MODEL), F32),
        compiler_params=pltpu.CompilerParams(
            dimension_semantics=("parallel",), vmem_limit_bytes=VMEM_LIMIT),
        name=name,
    )(hf, *acts, *ws, g, b)


def _split_dot(x, e_ref):
    hi = x.astype(BF16)
    lo = (x - hi.astype(F32)).astype(BF16)
    e = e_ref[...]
    return (jnp.dot(hi, e, preferred_element_type=F32)
            + jnp.dot(lo, e, preferred_element_type=F32))


def _ssd_body(xbc_ref, z_ref, small_ref, convw_ref, convb_ref, sbias_ref, alog_ref,
              dskip_ref, normw_ref, expand_ref, y_ref, c_ref,
              ext_ref, state_ref, carry_ref):
    c = pl.program_id(1)

    @pl.when(c == 0)
    def _():
        ext_ref[0:CONV_HALO, :] = jnp.zeros((CONV_HALO, SSD_CONV_CH), F32)
        state_ref[...] = jnp.zeros_like(state_ref)
        carry_ref[...] = jnp.zeros_like(carry_ref)

    row = lax.broadcasted_iota(jnp.int32, (CHUNK, 1), 0)
    valid = (c * CHUNK + row) >= PADL

    ext_ref[CONV_HALO:CONV_HALO + CHUNK, :] = jnp.where(valid, xbc_ref[...], 0.0)
    conv = convb_ref[...]
    for k in range(SSD_CONV):
        start = CONV_HALO - (SSD_CONV - 1) + k
        conv = conv + convw_ref[k:k + 1, :] * ext_ref[start:start + CHUNK, :]
    ext_ref[0:CONV_HALO, :] = ext_ref[CHUNK:CHUNK + CONV_HALO, :]
    xc = jnp.where(valid, _silu(conv), 0.0)
    xs = xc[:, :SSD_D_INNER]
    bm = xc[:, SSD_D_INNER:SSD_D_INNER + SSD_GROUPS * SSD_STATE].astype(BF16)
    cm = xc[:, SSD_D_INNER + SSD_GROUPS * SSD_STATE:].astype(BF16)

    lane = lax.broadcasted_iota(jnp.int32, (CHUNK, SMALL_W), 1)
    is_dt = lane < SSD_HEADS
    is_f = (lane >= SSD_HEADS) & (lane < SSD_HEADS + FOX_HEADS)
    v = small_ref[...] + sbias_ref[...]
    dt = jnp.where(valid & is_dt, _softplus(v), 0.0)
    log_f = jnp.where(valid & is_f, -_softplus(-v), 0.0)
    neg_a = -jnp.exp(alog_ref[...])
    steps = jnp.where(is_dt, dt * neg_a, log_f)
    r_i = lax.broadcasted_iota(jnp.int32, (CHUNK, CHUNK), 0)
    c_i = lax.broadcasted_iota(jnp.int32, (CHUNK, CHUNK), 1)
    causal = r_i >= c_i
    tril = jnp.where(causal, 1.0, 0.0).astype(F32)
    cum = jnp.dot(tril, steps, preferred_element_type=F32,
                  precision=lax.Precision.HIGHEST)
    c_total = jnp.where(is_f, cum + carry_ref[...], 0.0)
    c_ref[...] = c_total
    carry_ref[...] = c_total[CHUNK - 1:CHUNK, :]

    a_last = cum[CHUNK - 1:CHUNK, :]
    dt_x = _split_dot(dt, expand_ref)
    ea_x = _split_dot(jnp.exp(cum), expand_ref)
    de_x = _split_dot(jnp.exp(a_last - cum), expand_ref)
    x_dt = xs * dt_x
    xb = x_dt.astype(BF16)
    xe = (x_dt * de_x).astype(BF16)
    cum_t = cum.T
    lane_p = lax.broadcasted_iota(jnp.int32, (CHUNK, 2 * SSD_HEAD_DIM), 1)
    first_half = lane_p < SSD_HEAD_DIM

    y_groups = []
    for g in range(SSD_GROUPS):
        gs = slice(g * SSD_GROUP_W, (g + 1) * SSD_GROUP_W)
        bg = bm[:, g * SSD_STATE:(g + 1) * SSD_STATE]
        cg = cm[:, g * SSD_STATE:(g + 1) * SSD_STATE]
        cb = lax.dot_general(cg, bg, (((1,), (1,)), ((), ())),
                             preferred_element_type=F32)
        pair_out = []
        for pr in range(SSD_GROUP_W // (2 * SSD_HEAD_DIM)):
            col0 = g * SSD_GROUP_W + pr * 2 * SSD_HEAD_DIM
            x_pair = xb[:, col0:col0 + 2 * SSD_HEAD_DIM]
            ys = []
            for j in range(2):
                hd = col0 // SSD_HEAD_DIM + j
                diff = cum[:, hd:hd + 1] - cum_t[hd:hd + 1, :]
                decay = jnp.exp(jnp.where(causal, diff, -jnp.inf))
                mat = (cb * decay).astype(BF16)
                ys.append(jnp.dot(mat, x_pair, preferred_element_type=F32))
            pair_out.append(jnp.where(first_half, ys[0], ys[1]))
        y_diag = jnp.concatenate(pair_out, axis=1)
        st = state_ref[g]
        y_off = jnp.dot(cg, st.astype(BF16), preferred_element_type=F32) * ea_x[:, gs]
        new = lax.dot_general(bg, xe[:, gs], (((0,), (0,)), ((), ())),
                              preferred_element_type=F32)
        state_ref[g] = st * ea_x[CHUNK - 1:CHUNK, gs] + new
        yg = y_diag + y_off + dskip_ref[:, gs] * xs[:, gs]
        yg = yg * _silu(z_ref[:, gs])
        ms = jnp.mean(yg * yg, axis=-1, keepdims=True)
        y_groups.append(yg * lax.rsqrt(ms + RMS_EPS) * normw_ref[:, gs])
    y_ref[...] = jnp.concatenate(y_groups, axis=1).astype(y_ref.dtype)


def _ssd(xbc, z, small, convw, convb, sbias, alog, dskip, normw, expand):
    nb = xbc.shape[0]
    row_spec = lambda w: pl.BlockSpec((None, CHUNK, w), lambda bi, ci: (bi, ci, 0))
    return pl.pallas_call(
        _ssd_body,
        grid=(nb, LP // CHUNK),
        in_specs=[row_spec(SSD_CONV_CH), row_spec(SSD_D_INNER), row_spec(SMALL_W),
                  _const_spec(convw.shape), _const_spec(convb.shape),
                  _const_spec(sbias.shape), _const_spec(alog.shape),
                  _const_spec(dskip.shape), _const_spec(normw.shape),
                  _const_spec(expand.shape)],
        out_specs=[row_spec(SSD_D_INNER), row_spec(SMALL_W)],
        out_shape=[jax.ShapeDtypeStruct((nb, LP, SSD_D_INNER), BF16),
                   jax.ShapeDtypeStruct((nb, LP, SMALL_W), F32)],
        scratch_shapes=[pltpu.VMEM((CHUNK + CONV_HALO, SSD_CONV_CH), F32),
                        pltpu.VMEM((SSD_GROUPS, SSD_STATE, SSD_GROUP_W), F32),
                        pltpu.VMEM((1, SMALL_W), F32)],
        compiler_params=pltpu.CompilerParams(
            dimension_semantics=("arbitrary", "arbitrary"), vmem_limit_bytes=VMEM_LIMIT),
        name="ssd",
    )(xbc, z, small, convw, convb, sbias, alog, dskip, normw, expand)


def _online_softmax_step(s, v, m_ref, l_ref, acc_ref, idx):
    m_old = m_ref[idx][:, 0:1]
    m_new = jnp.maximum(m_old, jnp.max(s, axis=-1, keepdims=True))
    alpha = jnp.exp(m_old - m_new)
    p = jnp.exp(s - m_new)
    l_new = alpha * l_ref[idx][:, 0:1] + jnp.sum(p, axis=-1, keepdims=True)
    acc_ref[idx] = alpha * acc_ref[idx] + jnp.dot(p.astype(BF16), v,
                                                  preferred_element_type=F32)
    m_ref[idx] = jnp.broadcast_to(m_new, m_ref.shape[1:])
    l_ref[idx] = jnp.broadcast_to(l_new, l_ref.shape[1:])


def _visible(qb, kb):
    qpos = qb * ATT_TILE + lax.broadcasted_iota(jnp.int32, (ATT_TILE, ATT_TILE), 0)
    kpos = kb * ATT_TILE + lax.broadcasted_iota(jnp.int32, (ATT_TILE, ATT_TILE), 1)
    return (kpos <= qpos) & (kpos >= PADL)


def _init_softmax_state(m_ref, l_ref, acc_ref):
    m_ref[...] = jnp.full(m_ref.shape, NEG, F32)
    l_ref[...] = jnp.zeros_like(l_ref)
    acc_ref[...] = jnp.zeros_like(acc_ref)


N_PAD_TILES = -(-PADL // ATT_TILE)


def _fox_body(q_ref, k_ref, v_ref, ck_ref, o_ref, m_ref, l_ref, acc_ref):
    qb = pl.program_id(2)
    q = q_ref[...]
    lane = lax.broadcasted_iota(jnp.int32, q.shape, 1)
    first_half = lane < HEAD_DIM
    zero = jnp.zeros_like(q)
    q_heads = (jnp.where(first_half, q, zero), jnp.where(first_half, zero, q))
    _init_softmax_state(m_ref, l_ref, acc_ref)

    def step(kb, masked):
        start = pl.multiple_of(kb * ATT_TILE, ATT_TILE)
        k = k_ref[pl.ds(start, ATT_TILE), :]
        v = v_ref[pl.ds(start, ATT_TILE), :]
        ck = ck_ref[:, pl.ds(start, ATT_TILE)]
        if masked:
            vis = _visible(qb, kb)
        for j in range(2):
            s = lax.dot_general(q_heads[j], k, (((1,), (1,)), ((), ())),
                                preferred_element_type=F32) - ck[j:j + 1, :]
            if masked:
                s = jnp.where(vis, s, NEG)
            _online_softmax_step(s, v, m_ref, l_ref, acc_ref, j)

    n_head = jnp.minimum(N_PAD_TILES, qb)

    def masked_loop(kb, carry):
        step(kb, True)
        return carry

    def plain_loop(kb, carry):
        step(kb, False)
        return carry

    lax.fori_loop(0, n_head, masked_loop, 0)
    lax.fori_loop(n_head, qb, plain_loop, 0)
    step(qb, True)

    o0 = acc_ref[0] / l_ref[0][:, 0:1]
    o1 = acc_ref[1] / l_ref[1][:, 0:1]
    o_ref[...] = jnp.where(first_half, o0, o1).astype(o_ref.dtype)


def _fox_attention(q, k, v, ck):
    nb = q.shape[0]
    pairs = FOX_HEADS // 2
    return pl.pallas_call(
        _fox_body,
        grid=(nb, pairs, LP // ATT_TILE),
        in_specs=[
            pl.BlockSpec((None, ATT_TILE, LANES), lambda bi, hp, qi: (bi, qi, hp)),
            pl.BlockSpec((None, LP, LANES), lambda bi, hp, qi: (bi, 0, hp)),
            pl.BlockSpec((None, LP, LANES), lambda bi, hp, qi: (bi, 0, hp)),
            pl.BlockSpec((None, None, 2, LP), lambda bi, hp, qi: (bi, hp, 0, 0)),
        ],
        out_specs=pl.BlockSpec((None, ATT_TILE, LANES), lambda bi, hp, qi: (bi, qi, hp)),
        out_shape=jax.ShapeDtypeStruct((nb, LP, FOX_WIDTH), BF16),
        scratch_shapes=[pltpu.VMEM((2, ATT_TILE, LANES), F32),
                        pltpu.VMEM((2, ATT_TILE, LANES), F32),
                        pltpu.VMEM((2, ATT_TILE, LANES), F32)],
        compiler_params=pltpu.CompilerParams(
            dimension_semantics=("parallel", "parallel", "arbitrary"),
            vmem_limit_bytes=VMEM_LIMIT),
        name="fox_attention",
    )(q, k, v, ck)


def _bias_tile(bias_ref, offset):
    n = ATT_TILE // CHUNK
    zeros = jnp.zeros((CHUNK, CHUNK), F32)
    rows = []
    for a in range(n):
        tiles = []
        for b in range(n):
            sub = a - b + offset * n
            tiles.append(bias_ref[0] if sub == 0 else bias_ref[1] if sub == 1 else zeros)
        rows.append(jnp.concatenate(tiles, axis=1))
    return jnp.concatenate(rows, axis=0)


def _diff_body(lambda_init, q_ref, k_ref, v_ref, bias_ref, lam_ref, subln_ref, o_ref,
               m_ref, l_ref, acc_ref):
    qb = pl.program_id(2)
    q = q_ref[...]
    lane = lax.broadcasted_iota(jnp.int32, q.shape, 1)
    first_half = lane < HEAD_DIM
    zero = jnp.zeros_like(q)
    q_parts = (jnp.where(first_half, q, zero), jnp.where(first_half, zero, q))
    _init_softmax_state(m_ref, l_ref, acc_ref)

    def step(kb, masked, bias):
        start = pl.multiple_of(kb * ATT_TILE, ATT_TILE)
        k = k_ref[pl.ds(start, ATT_TILE), :]
        v = v_ref[pl.ds(start, ATT_TILE), :]
        if masked:
            vis = _visible(qb, kb)
        for j in range(2):
            s = lax.dot_general(q_parts[j], k, (((1,), (1,)), ((), ())),
                                preferred_element_type=F32)
            if bias is not None:
                s = s + bias
            if masked:
                s = jnp.where(vis, s, NEG)
            _online_softmax_step(s, v, m_ref, l_ref, acc_ref, j)

    n_far = jnp.maximum(qb - 1, 0)
    n_head = jnp.minimum(N_PAD_TILES, n_far)

    def masked_loop(kb, carry):
        step(kb, True, None)
        return carry

    def plain_loop(kb, carry):
        step(kb, False, None)
        return carry

    lax.fori_loop(0, n_head, masked_loop, 0)
    lax.fori_loop(n_head, n_far, plain_loop, 0)

    @pl.when(qb > 0)
    def _():
        step(qb - 1, True, _bias_tile(bias_ref, 1))

    step(qb, True, _bias_tile(bias_ref, 0))

    lam1 = jnp.exp(jnp.sum(lam_ref[0:1, :] * lam_ref[1:2, :], axis=-1, keepdims=True))
    lam2 = jnp.exp(jnp.sum(lam_ref[2:3, :] * lam_ref[3:4, :], axis=-1, keepdims=True))
    lam = lam1 - lam2 + lambda_init
    o = acc_ref[0] / l_ref[0][:, 0:1] - lam * (acc_ref[1] / l_ref[1][:, 0:1])
    ms = jnp.mean(o * o, axis=-1, keepdims=True)
    o = o * lax.rsqrt(ms + RMS_EPS) * subln_ref[...] * (1.0 - lambda_init)
    o_ref[...] = o.astype(o_ref.dtype)


def _diff_attention(q, k, v, bias_tiles, lam_rows, subln, lambda_init):
    nb = q.shape[0]
    return pl.pallas_call(
        functools.partial(_diff_body, lambda_init),
        grid=(nb, DIFF_HEADS, LP // ATT_TILE),
        in_specs=[
            pl.BlockSpec((None, ATT_TILE, LANES), lambda bi, hd, qi: (bi, qi, hd)),
            pl.BlockSpec((None, LP, LANES), lambda bi, hd, qi: (bi, 0, hd)),
            pl.BlockSpec((None, LP, LANES), lambda bi, hd, qi: (bi, 0, hd)),
            pl.BlockSpec((None, 2, CHUNK, CHUNK), lambda bi, hd, qi: (hd, 0, 0, 0)),
            _const_spec(lam_rows.shape),
            _const_spec(subln.shape),
        ],
        out_specs=pl.BlockSpec((None, ATT_TILE, LANES), lambda bi, hd, qi: (bi, qi, hd)),
        out_shape=jax.ShapeDtypeStruct((nb, LP, DIFF_HEADS * LANES), BF16),
        scratch_shapes=[pltpu.VMEM((2, ATT_TILE, LANES), F32),
                        pltpu.VMEM((2, ATT_TILE, LANES), F32),
                        pltpu.VMEM((2, ATT_TILE, LANES), F32)],
        compiler_params=pltpu.CompilerParams(
            dimension_semantics=("parallel", "parallel", "arbitrary"),
            vmem_limit_bytes=VMEM_LIMIT),
        name="diff_attention",
    )(q, k, v, bias_tiles, lam_rows, subln)


def _t5_bucket(n):
    max_exact = N_BUCKETS // 2
    nf = jnp.maximum(n, 1).astype(F32)
    large = max_exact + (jnp.log(nf / max_exact) / math.log(128 / max_exact)
                         * (N_BUCKETS - max_exact)).astype(jnp.int32)
    large = jnp.minimum(large, N_BUCKETS - 1)
    return jnp.where(n < max_exact, n, large)


def _relative_bias_tiles(rel_table):
    dist = jnp.arange(2 * CHUNK)
    by_dist = rel_table[_t5_bucket(dist)] - rel_table[N_BUCKETS - 1]
    i = np.arange(CHUNK)[:, None]
    j = np.arange(CHUNK)[None, :]
    idx = np.stack([np.maximum(i - j, 0), CHUNK + i - j])
    return by_dist[idx].transpose(3, 0, 1, 2).astype(F32)


def _pad_lanes(vec, width=LANES):
    return jnp.pad(vec, (0, width - vec.shape[0]))[None, :].astype(F32)


def kernel(x, meta_tokens, ln_gain, ln_bias, ffn1_w_gate, ffn1_w_up, ffn1_w_down, ffn2_w_gate, ffn2_w_up, ffn2_w_down, even_w_in, even_conv_w, even_conv_b, ssd_dt_bias, ssd_a_log, ssd_d_skip, ssd_norm_w, fox_f_bias, even_w_out, diff_w_qkv, diff_lambda_q1, diff_lambda_k1, diff_lambda_q2, diff_lambda_k2, diff_subln_w, diff_w_o, rel_bias_table):
    nb = x.shape[0]
    meta = jnp.broadcast_to(meta_tokens[None].astype(x.dtype), (nb, N_META, D_MODEL))
    h = jnp.concatenate([jnp.zeros((nb, PADL, D_MODEL), x.dtype), meta, x], axis=1)
    hf = h.reshape(nb * LP, D_MODEL)

    def ln_params(l, i):
        return ln_gain[l, i][None, :], ln_bias[l, i][None, :]

    def ffn_weights(wg, wu, wd, l):
        return wg[l].astype(BF16), wu[l].astype(BF16), wd[l].astype(BF16)

    hf = _ffn_ln(hf, *ffn_weights(ffn1_w_gate, ffn1_w_up, ffn1_w_down, 0), *ln_params(0, 0))

    w_in = even_w_in[0]
    o_z, o_xbc = 0, SSD_D_INNER
    o_dt = o_xbc + SSD_CONV_CH
    o_q = o_dt + SSD_HEADS
    o_k, o_v = o_q + FOX_WIDTH, o_q + 2 * FOX_WIDTH
    o_f = o_q + 3 * FOX_WIDTH
    w_small = jnp.concatenate(
        [w_in[:, o_dt:o_q], w_in[:, o_f:],
         jnp.zeros((D_MODEL, SMALL_W - SSD_HEADS - FOX_HEADS), w_in.dtype)], axis=1)
    w_even = jnp.concatenate(
        [w_in[:, o_z:o_dt], w_in[:, o_q:o_k] * HEAD_DIM ** -0.5, w_in[:, o_k:o_f], w_small],
        axis=1).astype(BF16)
    z, xbc, q, k, v, small = _proj(
        hf, w_even,
        (SSD_D_INNER, SSD_CONV_CH, FOX_WIDTH, FOX_WIDTH, FOX_WIDTH, SMALL_W),
        (F32, F32, BF16, BF16, BF16, F32), "even_in_proj")

    sbias = _pad_lanes(jnp.concatenate([ssd_dt_bias[0], fox_f_bias[0]]))
    alog = _pad_lanes(ssd_a_log[0])
    dskip = jnp.repeat(ssd_d_skip[0], SSD_HEAD_DIM)[None, :].astype(F32)
    expand = np.zeros((SMALL_W, SSD_D_INNER), np.float32)
    expand[np.arange(SSD_D_INNER) // SSD_HEAD_DIM, np.arange(SSD_D_INNER)] = 1.0
    y, cfull = _ssd(xbc.reshape(nb, LP, SSD_CONV_CH), z.reshape(nb, LP, SSD_D_INNER),
                    small.reshape(nb, LP, SMALL_W), even_conv_w[0], even_conv_b[0][None, :],
                    sbias, alog, dskip, ssd_norm_w[0][None, :], jnp.asarray(expand, BF16))
    ck = cfull[:, :, SSD_HEADS:SSD_HEADS + FOX_HEADS].transpose(0, 2, 1)
    ck = ck.reshape(nb, FOX_HEADS // 2, 2, LP)
    o = _fox_attention(q.reshape(nb, LP, FOX_WIDTH), k.reshape(nb, LP, FOX_WIDTH),
                       v.reshape(nb, LP, FOX_WIDTH), ck)
    w_out = even_w_out[0].astype(BF16)
    hf = _outproj_ln(hf, [y.reshape(nb * LP, SSD_D_INNER), o.reshape(nb * LP, FOX_WIDTH)],
                     [w_out[:SSD_D_INNER], w_out[SSD_D_INNER:]], *ln_params(0, 1),
                     name="even_out_proj_ln")
    hf = _ffn_ln(hf, *ffn_weights(ffn2_w_gate, ffn2_w_up, ffn2_w_down, 0), *ln_params(0, 2))

    hf = _ffn_ln(hf, *ffn_weights(ffn1_w_gate, ffn1_w_up, ffn1_w_down, 1), *ln_params(1, 0))
    qw = DIFF_HEADS * 2 * HEAD_DIM
    w_qkv = diff_w_qkv[0]
    w_qkv = jnp.concatenate([w_qkv[:, :qw] * HEAD_DIM ** -0.5, w_qkv[:, qw:]], axis=1).astype(BF16)
    q, k, v = _proj(hf, w_qkv, (qw, qw, DIFF_HEADS * LANES), (BF16, BF16, BF16), "diff_qkv_proj")
    lambda_init = 0.8 - 0.6 * math.exp(-0.3 * 1)
    lam_rows = jnp.concatenate(
        [_pad_lanes(diff_lambda_q1[0]), _pad_lanes(diff_lambda_k1[0]),
         _pad_lanes(diff_lambda_q2[0]), _pad_lanes(diff_lambda_k2[0]),
         jnp.zeros((4, LANES), F32)], axis=0)
    o = _diff_attention(q.reshape(nb, LP, qw), k.reshape(nb, LP, qw),
                        v.reshape(nb, LP, DIFF_HEADS * LANES),
                        _relative_bias_tiles(rel_bias_table), lam_rows,
                        diff_subln_w[0][None, :], lambda_init)
    hf = _outproj_ln(hf, [o.reshape(nb * LP, DIFF_HEADS * LANES)], [diff_w_o[0].astype(BF16)],
                     *ln_params(1, 1), name="diff_out_proj_ln")
    return _ffn_ln_final(hf.reshape(nb, LP, D_MODEL),
                         *ffn_weights(ffn2_w_gate, ffn2_w_up, ffn2_w_down, 1), *ln_params(1, 2))
```

```python
import functools
import math

import numpy as np
import jax
import jax.numpy as jnp
from jax import lax
from jax.experimental import pallas as pl
from jax.experimental.pallas import tpu as pltpu

F32 = jnp.float32
BF16 = jnp.bfloat16

D_MODEL = 1024
SEQ = 8192
DEPTH = 2
N_META = 16
CHUNK = 128
SSD_D_INNER = 2048
SSD_HEAD_DIM = 64
SSD_HEADS = 32
SSD_GROUPS = 4
SSD_GROUP_W = SSD_D_INNER // SSD_GROUPS
SSD_STATE = 128
SSD_CONV = 4
SSD_CONV_CH = SSD_D_INNER + 2 * SSD_GROUPS * SSD_STATE
FOX_HEADS = 16
FOX_WIDTH = 1024
HEAD_DIM = 64
DIFF_HEADS = 8
N_BUCKETS = 32
D_FF = 2816
ALPHA = (2 * DEPTH) ** 0.25
LN_EPS = 1e-5
RMS_EPS = 1e-5
NEG = -1e30

LANES = 128
LP = 8448
PADL = LP - SEQ - N_META
ROW_TILE = 512
OUT_TILE = 256
ATT_TQ = 768
ATT_TK = 256
CONV_HALO = 8
SMALL_W = LANES
VMEM_LIMIT = 56 * 1024 * 1024

assert PADL % CHUNK == CHUNK - N_META
assert LP % ATT_TQ == 0 and ATT_TQ % ATT_TK == 0 and ATT_TK % LANES == 0
assert LP % CHUNK == 0 and (2 * LP) % ROW_TILE == 0
assert PADL <= OUT_TILE and (LP - OUT_TILE) == SEQ


def _const_spec(shape):
    nd = len(shape)
    return pl.BlockSpec(shape, lambda *_: (0,) * nd, pipeline_mode=pl.Buffered(1))


def _layer_norm(r, g, b):
    mu = jnp.mean(r, axis=-1, keepdims=True)
    d = r - mu
    var = jnp.mean(d * d, axis=-1, keepdims=True)
    return d * lax.rsqrt(var + LN_EPS) * g + b


def _silu(x):
    return x / (1.0 + jnp.exp(-x))


def _softplus(x):
    return jnp.maximum(x, 0.0) + jnp.log(1.0 + jnp.exp(-jnp.abs(x)))


def _ffn_ln_body(h_ref, wg_ref, wu_ref, wd_ref, g_ref, b_ref, o_ref):
    h = h_ref[...]
    hb = h.astype(BF16)
    g = jnp.dot(hb, wg_ref[...], preferred_element_type=F32)
    u = jnp.dot(hb, wu_ref[...], preferred_element_type=F32)
    a = (_silu(g) * u).astype(BF16)
    y = jnp.dot(a, wd_ref[...], preferred_element_type=F32)
    o_ref[...] = _layer_norm(ALPHA * h + 0.5 * y, g_ref[...], b_ref[...])


def _ffn_ln(hf, wg, wu, wd, g, b):
    rows = hf.shape[0]
    return pl.pallas_call(
        _ffn_ln_body,
        grid=(rows // ROW_TILE,),
        in_specs=[
            pl.BlockSpec((ROW_TILE, D_MODEL), lambda i: (i, 0)),
            _const_spec((D_MODEL, D_FF)),
            _const_spec((D_MODEL, D_FF)),
            _const_spec((D_FF, D_MODEL)),
            _const_spec((1, D_MODEL)),
            _const_spec((1, D_MODEL)),
        ],
        out_specs=pl.BlockSpec((ROW_TILE, D_MODEL), lambda i: (i, 0)),
        out_shape=jax.ShapeDtypeStruct((rows, D_MODEL), F32),
        compiler_params=pltpu.CompilerParams(
            dimension_semantics=("parallel",), vmem_limit_bytes=VMEM_LIMIT),
        name="ffn_ln",
    )(hf, wg, wu, wd, g, b)


def _ffn_ln_final_body(h_ref, wg_ref, wu_ref, wd_ref, g_ref, b_ref, o_ref):
    @pl.when(pl.program_id(1) > 0)
    def _():
        _ffn_ln_body(h_ref, wg_ref, wu_ref, wd_ref, g_ref, b_ref, o_ref)


def _ffn_ln_final(h3, wg, wu, wd, g, b):
    nb = h3.shape[0]
    return pl.pallas_call(
        _ffn_ln_final_body,
        grid=(nb, LP // OUT_TILE),
        in_specs=[
            pl.BlockSpec((None, OUT_TILE, D_MODEL), lambda bi, i: (bi, i, 0)),
            _const_spec((D_MODEL, D_FF)),
            _const_spec((D_MODEL, D_FF)),
            _const_spec((D_FF, D_MODEL)),
            _const_spec((1, D_MODEL)),
            _const_spec((1, D_MODEL)),
        ],
        out_specs=pl.BlockSpec((None, OUT_TILE, D_MODEL),
                               lambda bi, i: (bi, jnp.maximum(i - 1, 0), 0)),
        out_shape=jax.ShapeDtypeStruct((nb, SEQ, D_MODEL), F32),
        compiler_params=pltpu.CompilerParams(
            dimension_semantics=("arbitrary", "arbitrary"), vmem_limit_bytes=VMEM_LIMIT),
        name="ffn_ln_final",
    )(h3, wg, wu, wd, g, b)


def _proj_body(h_ref, w_ref, *o_refs):
    hb = h_ref[...].astype(BF16)
    off = 0
    for o_ref in o_refs:
        n = o_ref.shape[-1]
        o_ref[...] = jnp.dot(hb, w_ref[:, off:off + n],
                             preferred_element_type=F32).astype(o_ref.dtype)
        off += n


def _proj(hf, w, widths, dtypes, name):
    rows = hf.shape[0]
    tile = ROW_TILE // 2
    return pl.pallas_call(
        _proj_body,
        grid=(rows // tile,),
        in_specs=[pl.BlockSpec((tile, D_MODEL), lambda i: (i, 0)),
                  _const_spec(w.shape)],
        out_specs=[pl.BlockSpec((tile, n), lambda i: (i, 0)) for n in widths],
        out_shape=[jax.ShapeDtypeStruct((rows, n), dt) for n, dt in zip(widths, dtypes)],
        compiler_params=pltpu.CompilerParams(
            dimension_semantics=("parallel",), vmem_limit_bytes=VMEM_LIMIT),
        name=name,
    )(hf, w)


def _outproj_ln_body(n_in, h_ref, *refs):
    a_refs = refs[:n_in]
    w_refs = refs[n_in:2 * n_in]
    g_ref, b_ref, o_ref = refs[2 * n_in:]
    m = jnp.dot(a_refs[0][...], w_refs[0][...], preferred_element_type=F32)
    for a_ref, w_ref in zip(a_refs[1:], w_refs[1:]):
        m = m + jnp.dot(a_ref[...], w_ref[...], preferred_element_type=F32)
    o_ref[...] = _layer_norm(ALPHA * h_ref[...] + m, g_ref[...], b_ref[...])


def _outproj_ln(hf, acts, ws, g, b, name):
    rows = hf.shape[0]
    n_in = len(acts)
    in_specs = [pl.BlockSpec((ROW_TILE, D_MODEL), lambda i: (i, 0))]
    in_specs += [pl.BlockSpec((ROW_TILE, a.shape[1]), lambda i: (i, 0)) for a in acts]
    in_specs += [_const_spec(w.shape) for w in ws]
    in_specs += [_const_spec((1, D_MODEL)), _const_spec((1, D_MODEL))]
    return pl.pallas_call(
        functools.partial(_outproj_ln_body, n_in),
        grid=(rows // ROW_TILE,),
        in_specs=in_specs,
        out_specs=pl.BlockSpec((ROW_TILE, D_MODEL), lambda i: (i, 0)),
        out_shape=jax.ShapeDtypeStruct((rows, D_MODEL), F32),
        compiler_params=pltpu.CompilerParams(
            dimension_semantics=("parallel",), vmem_limit_bytes=VMEM_LIMIT),
        name=name,
    )(hf, *acts, *ws, g, b)


def _split_dot(x, e_ref):
    hi = x.astype(BF16)
    lo = (x - hi.astype(F32)).astype(BF16)
    e = e_ref[...]
    return (jnp.dot(hi, e, preferred_element_type=F32)
            + jnp.dot(lo, e, preferred_element_type=F32))


def _ssd_body(xbc_ref, z_ref, small_ref, convw_ref, convb_ref, sbias_ref, alog_ref,
              dskip_ref, normw_ref, expand_ref, y_ref, c_ref,
              ext_ref, state_ref, carry_ref):
    c = pl.program_id(1)

    @pl.when(c == 0)
    def _():
        ext_ref[0:CONV_HALO, :] = jnp.zeros((CONV_HALO, SSD_CONV_CH), F32)
        state_ref[...] = jnp.zeros_like(state_ref)
        carry_ref[...] = jnp.zeros_like(carry_ref)

    row = lax.broadcasted_iota(jnp.int32, (CHUNK, 1), 0)
    valid = (c * CHUNK + row) >= PADL

    ext_ref[CONV_HALO:CONV_HALO + CHUNK, :] = jnp.where(valid, xbc_ref[...], 0.0)
    conv = convb_ref[...]
    for k in range(SSD_CONV):
        start = CONV_HALO - (SSD_CONV - 1) + k
        conv = conv + convw_ref[k:k + 1, :] * ext_ref[start:start + CHUNK, :]
    ext_ref[0:CONV_HALO, :] = ext_ref[CHUNK:CHUNK + CONV_HALO, :]
    xc = jnp.where(valid, _silu(conv), 0.0)
    xs = xc[:, :SSD_D_INNER]
    bm = xc[:, SSD_D_INNER:SSD_D_INNER + SSD_GROUPS * SSD_STATE].astype(BF16)
    cm = xc[:, SSD_D_INNER + SSD_GROUPS * SSD_STATE:].astype(BF16)

    lane = lax.broadcasted_iota(jnp.int32, (CHUNK, SMALL_W), 1)
    is_dt = lane < SSD_HEADS
    is_f = (lane >= SSD_HEADS) & (lane < SSD_HEADS + FOX_HEADS)
    v = small_ref[...] + sbias_ref[...]
    dt = jnp.where(valid & is_dt, _softplus(v), 0.0)
    log_f = jnp.where(valid & is_f, -_softplus(-v), 0.0)
    neg_a = -jnp.exp(alog_ref[...])
    steps = jnp.where(is_dt, dt * neg_a, log_f)
    r_i = lax.broadcasted_iota(jnp.int32, (CHUNK, CHUNK), 0)
    c_i = lax.broadcasted_iota(jnp.int32, (CHUNK, CHUNK), 1)
    causal = r_i >= c_i
    tril = jnp.where(causal, 1.0, 0.0).astype(F32)
    cum = jnp.dot(tril, steps, preferred_element_type=F32,
                  precision=lax.Precision.HIGHEST)
    c_total = jnp.where(is_f, cum + carry_ref[...], 0.0)
    c_ref[...] = c_total
    carry_ref[...] = c_total[CHUNK - 1:CHUNK, :]

    a_last = cum[CHUNK - 1:CHUNK, :]
    dt_x = _split_dot(dt, expand_ref)
    ea_x = _split_dot(jnp.exp(cum), expand_ref)
    de_x = _split_dot(jnp.exp(a_last - cum), expand_ref)
    x_dt = xs * dt_x
    xb = x_dt.astype(BF16)
    xe = (x_dt * de_x).astype(BF16)
    cum_t = cum.T
    lane_p = lax.broadcasted_iota(jnp.int32, (CHUNK, 2 * SSD_HEAD_DIM), 1)
    first_half = lane_p < SSD_HEAD_DIM

    y_groups = []
    for g in range(SSD_GROUPS):
        gs = slice(g * SSD_GROUP_W, (g + 1) * SSD_GROUP_W)
        bg = bm[:, g * SSD_STATE:(g + 1) * SSD_STATE]
        cg = cm[:, g * SSD_STATE:(g + 1) * SSD_STATE]
        cb = lax.dot_general(cg, bg, (((1,), (1,)), ((), ())),
                             preferred_element_type=F32)
        pair_out = []
        for pr in range(SSD_GROUP_W // (2 * SSD_HEAD_DIM)):
            col0 = g * SSD_GROUP_W + pr * 2 * SSD_HEAD_DIM
            x_pair = xb[:, col0:col0 + 2 * SSD_HEAD_DIM]
            ys = []
            for j in range(2):
                hd = col0 // SSD_HEAD_DIM + j
                diff = cum[:, hd:hd + 1] - cum_t[hd:hd + 1, :]
                decay = jnp.exp(jnp.where(causal, diff, -jnp.inf))
                mat = (cb * decay).astype(BF16)
                ys.append(jnp.dot(mat, x_pair, preferred_element_type=F32))
            pair_out.append(jnp.where(first_half, ys[0], ys[1]))
        y_diag = jnp.concatenate(pair_out, axis=1)
        st = state_ref[g]
        y_off = jnp.dot(cg, st.astype(BF16), preferred_element_type=F32) * ea_x[:, gs]
        new = lax.dot_general(bg, xe[:, gs], (((0,), (0,)), ((), ())),
                              preferred_element_type=F32)
        state_ref[g] = st * ea_x[CHUNK - 1:CHUNK, gs] + new
        yg = y_diag + y_off + dskip_ref[:, gs] * xs[:, gs]
        yg = yg * _silu(z_ref[:, gs])
        ms = jnp.mean(yg * yg, axis=-1, keepdims=True)
        y_groups.append(yg * lax.rsqrt(ms + RMS_EPS) * normw_ref[:, gs])
    y_ref[...] = jnp.concatenate(y_groups, axis=1).astype(y_ref.dtype)


def _ssd(xbc, z, small, convw, convb, sbias, alog, dskip, normw, expand):
    nb = xbc.shape[0]
    row_spec = lambda w: pl.BlockSpec((None, CHUNK, w), lambda bi, ci: (bi, ci, 0))
    return pl.pallas_call(
        _ssd_body,
        grid=(nb, LP // CHUNK),
        in_specs=[row_spec(SSD_CONV_CH), row_spec(SSD_D_INNER), row_spec(SMALL_W),
                  _const_spec(convw.shape), _const_spec(convb.shape),
                  _const_spec(sbias.shape), _const_spec(alog.shape),
                  _const_spec(dskip.shape), _const_spec(normw.shape),
                  _const_spec(expand.shape)],
        out_specs=[row_spec(SSD_D_INNER), row_spec(SMALL_W)],
        out_shape=[jax.ShapeDtypeStruct((nb, LP, SSD_D_INNER), BF16),
                   jax.ShapeDtypeStruct((nb, LP, SMALL_W), F32)],
        scratch_shapes=[pltpu.VMEM((CHUNK + CONV_HALO, SSD_CONV_CH), F32),
                        pltpu.VMEM((SSD_GROUPS, SSD_STATE, SSD_GROUP_W), F32),
                        pltpu.VMEM((1, SMALL_W), F32)],
        compiler_params=pltpu.CompilerParams(
            dimension_semantics=("arbitrary", "arbitrary"), vmem_limit_bytes=VMEM_LIMIT),
        name="ssd",
    )(xbc, z, small, convw, convb, sbias, alog, dskip, normw, expand)


N_TK_CHUNKS = ATT_TK // LANES
N_TQ_CHUNKS = ATT_TQ // LANES
TILE_RATIO = ATT_TQ // ATT_TK
N_PAD_TILES = -(-PADL // ATT_TK)


def _softmax_update(chunks, v_aug, m_ref, acc_ref, idx):
    m_old = m_ref[idx]
    part = chunks[0]
    for sc in chunks[1:]:
        part = jnp.maximum(part, sc)
    m_new = jnp.maximum(m_old, jnp.max(part, axis=-1, keepdims=True))
    alpha = jnp.exp(m_old - m_new)
    p = jnp.concatenate([jnp.exp(sc - m_new).astype(BF16) for sc in chunks], axis=1)
    pv = jnp.dot(p, v_aug, preferred_element_type=F32)
    for w in range(acc_ref.shape[-1] // LANES):
        ws = slice(w * LANES, (w + 1) * LANES)
        acc_ref[idx, :, ws] = alpha * acc_ref[idx, :, ws] + pv[:, ws]
    m_ref[idx] = m_new


def _visible_chunk(qb, key_start):
    qpos = qb * ATT_TQ + lax.broadcasted_iota(jnp.int32, (ATT_TQ, LANES), 0)
    kpos = key_start + lax.broadcasted_iota(jnp.int32, (ATT_TQ, LANES), 1)
    return (kpos <= qpos) & (kpos >= PADL)


def _init_softmax_state(m_ref, acc_ref):
    m_ref[...] = jnp.full(m_ref.shape, NEG, F32)
    acc_ref[...] = jnp.zeros_like(acc_ref)


def _split_halves(q):
    lane = lax.broadcasted_iota(jnp.int32, q.shape, 1)
    first_half = lane < HEAD_DIM
    zero = jnp.zeros_like(q)
    return first_half, (jnp.where(first_half, q, zero), jnp.where(first_half, zero, q))


def _fox_body(q_ref, kt_ref, v_ref, ck_ref, o_ref, m_ref, acc_ref):
    qb = pl.program_id(2)
    first_half, q_heads = _split_halves(q_ref[...])
    _init_softmax_state(m_ref, acc_ref)

    def step(kb, masked):
        start = pl.multiple_of(kb * ATT_TK, ATT_TK)
        kt = kt_ref[:, pl.ds(start, ATT_TK)]
        for j in range(2):
            s = jnp.dot(q_heads[j], kt, preferred_element_type=F32)
            chunks = []
            for c in range(N_TK_CHUNKS):
                key0 = pl.multiple_of(start + c * LANES, LANES)
                sc = s[:, c * LANES:(c + 1) * LANES] - ck_ref[j:j + 1, pl.ds(key0, LANES)]
                if masked:
                    sc = jnp.where(_visible_chunk(qb, key0), sc, NEG)
                chunks.append(sc)
            v_aug = v_ref[pl.ds(start, ATT_TK), j * LANES:(j + 1) * LANES]
            _softmax_update(chunks, v_aug, m_ref, acc_ref, j)

    def masked_loop(kb, carry):
        step(kb, True)
        return carry

    def plain_loop(kb, carry):
        step(kb, False)
        return carry

    n_below = qb * TILE_RATIO
    n_head = jnp.minimum(N_PAD_TILES, n_below)
    lax.fori_loop(0, n_head, masked_loop, 0)
    lax.fori_loop(n_head, n_below, plain_loop, 0)
    lax.fori_loop(n_below, n_below + TILE_RATIO, masked_loop, 0)

    acc0, acc1 = acc_ref[0], acc_ref[1]
    num = jnp.where(first_half, acc0, acc1)
    den = pltpu.roll(jnp.where(first_half, acc1, acc0), HEAD_DIM, axis=1)
    o_ref[...] = (num / den).astype(o_ref.dtype)


def _fox_attention(q, kt, v_aug, ck):
    nb = q.shape[0]
    pairs = FOX_HEADS // 2
    return pl.pallas_call(
        _fox_body,
        grid=(nb, pairs, LP // ATT_TQ),
        in_specs=[
            pl.BlockSpec((None, ATT_TQ, LANES), lambda bi, hp, qi: (bi, qi, hp)),
            pl.BlockSpec((None, LANES, LP), lambda bi, hp, qi: (bi, hp, 0)),
            pl.BlockSpec((None, LP, 2 * LANES), lambda bi, hp, qi: (bi, 0, hp)),
            pl.BlockSpec((None, None, 2, LP), lambda bi, hp, qi: (bi, hp, 0, 0)),
        ],
        out_specs=pl.BlockSpec((None, ATT_TQ, LANES), lambda bi, hp, qi: (bi, qi, hp)),
        out_shape=jax.ShapeDtypeStruct((nb, LP, FOX_WIDTH), BF16),
        scratch_shapes=[pltpu.VMEM((2, ATT_TQ, LANES), F32),
                        pltpu.VMEM((2, ATT_TQ, LANES), F32)],
        compiler_params=pltpu.CompilerParams(
            dimension_semantics=("parallel", "parallel", "arbitrary"),
            vmem_limit_bytes=VMEM_LIMIT),
        name="fox_attention",
    )(q, kt, v_aug, ck)


def _bias_chunk(bias_ref, key_chunk):
    tiles = []
    for a in range(N_TQ_CHUNKS):
        sub = a - key_chunk
        tiles.append(bias_ref[sub] if sub in (0, 1) else None)
    if all(t is None for t in tiles):
        return None
    zeros = jnp.zeros((CHUNK, CHUNK), F32)
    return jnp.concatenate([zeros if t is None else t for t in tiles], axis=0)


def _diff_body(lambda_init, q_ref, kt_ref, v_ref, bias_ref, lam_ref, subln_ref, o_ref,
               m_ref, acc_ref):
    qb = pl.program_id(2)
    _, q_parts = _split_halves(q_ref[...])
    _init_softmax_state(m_ref, acc_ref)

    def step(kb, masked, first_key_chunk):
        start = pl.multiple_of(kb * ATT_TK, ATT_TK)
        kt = kt_ref[:, pl.ds(start, ATT_TK)]
        v_aug = v_ref[pl.ds(start, ATT_TK), :]
        biases = [None] * N_TK_CHUNKS
        if first_key_chunk is not None:
            biases = [_bias_chunk(bias_ref, first_key_chunk + c) for c in range(N_TK_CHUNKS)]
        for j in range(2):
            s = jnp.dot(q_parts[j], kt, preferred_element_type=F32)
            chunks = []
            for c in range(N_TK_CHUNKS):
                sc = s[:, c * LANES:(c + 1) * LANES]
                if biases[c] is not None:
                    sc = sc + biases[c]
                if masked:
                    sc = jnp.where(_visible_chunk(qb, start + c * LANES), sc, NEG)
                chunks.append(sc)
            _softmax_update(chunks, v_aug, m_ref, acc_ref, j)

    def masked_loop(kb, carry):
        step(kb, True, None)
        return carry

    def plain_loop(kb, carry):
        step(kb, False, None)
        return carry

    n_below = qb * TILE_RATIO
    n_far = jnp.maximum(n_below - 1, 0)
    n_head = jnp.minimum(N_PAD_TILES, n_far)
    lax.fori_loop(0, n_head, masked_loop, 0)
    lax.fori_loop(n_head, n_far, plain_loop, 0)

    @pl.when(qb > 0)
    def _():
        step(n_below - 1, True, -N_TK_CHUNKS)

    for jj in range(TILE_RATIO):
        step(n_below + jj, True, jj * N_TK_CHUNKS)

    lam1 = jnp.exp(jnp.sum(lam_ref[0:1, :] * lam_ref[1:2, :], axis=-1, keepdims=True))
    lam2 = jnp.exp(jnp.sum(lam_ref[2:3, :] * lam_ref[3:4, :], axis=-1, keepdims=True))
    lam = lam1 - lam2 + lambda_init
    o = (acc_ref[0, :, :LANES] / acc_ref[0, :, LANES:]
         - lam * (acc_ref[1, :, :LANES] / acc_ref[1, :, LANES:]))
    ms = jnp.mean(o * o, axis=-1, keepdims=True)
    o = o * lax.rsqrt(ms + RMS_EPS) * subln_ref[...] * (1.0 - lambda_init)
    o_ref[...] = o.astype(o_ref.dtype)


def _diff_attention(q, kt, v_aug, bias_tiles, lam_rows, subln, lambda_init):
    nb = q.shape[0]
    return pl.pallas_call(
        functools.partial(_diff_body, lambda_init),
        grid=(nb, DIFF_HEADS, LP // ATT_TQ),
        in_specs=[
            pl.BlockSpec((None, ATT_TQ, LANES), lambda bi, hd, qi: (bi, qi, hd)),
            pl.BlockSpec((None, LANES, LP), lambda bi, hd, qi: (bi, hd, 0)),
            pl.BlockSpec((None, LP, 2 * LANES), lambda bi, hd, qi: (bi, 0, hd)),
            pl.BlockSpec((None, 2, CHUNK, CHUNK), lambda bi, hd, qi: (hd, 0, 0, 0)),
            _const_spec(lam_rows.shape),
            _const_spec(subln.shape),
        ],
        out_specs=pl.BlockSpec((None, ATT_TQ, LANES), lambda bi, hd, qi: (bi, qi, hd)),
        out_shape=jax.ShapeDtypeStruct((nb, LP, DIFF_HEADS * LANES), BF16),
        scratch_shapes=[pltpu.VMEM((2, ATT_TQ, LANES), F32),
                        pltpu.VMEM((2, ATT_TQ, 2 * LANES), F32)],
        compiler_params=pltpu.CompilerParams(
            dimension_semantics=("parallel", "parallel", "arbitrary"),
            vmem_limit_bytes=VMEM_LIMIT),
        name="diff_attention",
    )(q, kt, v_aug, bias_tiles, lam_rows, subln)


def _t5_bucket(n):
    max_exact = N_BUCKETS // 2
    nf = jnp.maximum(n, 1).astype(F32)
    large = max_exact + (jnp.log(nf / max_exact) / math.log(128 / max_exact)
                         * (N_BUCKETS - max_exact)).astype(jnp.int32)
    large = jnp.minimum(large, N_BUCKETS - 1)
    return jnp.where(n < max_exact, n, large)


def _relative_bias_tiles(rel_table):
    dist = jnp.arange(2 * CHUNK)
    by_dist = rel_table[_t5_bucket(dist)] - rel_table[N_BUCKETS - 1]
    i = np.arange(CHUNK)[:, None]
    j = np.arange(CHUNK)[None, :]
    idx = np.stack([np.maximum(i - j, 0), CHUNK + i - j])
    return by_dist[idx].transpose(3, 0, 1, 2).astype(F32)


def _pad_lanes(vec, width=LANES):
    return jnp.pad(vec, (0, width - vec.shape[0]))[None, :].astype(F32)


def kernel(x, meta_tokens, ln_gain, ln_bias, ffn1_w_gate, ffn1_w_up, ffn1_w_down, ffn2_w_gate, ffn2_w_up, ffn2_w_down, even_w_in, even_conv_w, even_conv_b, ssd_dt_bias, ssd_a_log, ssd_d_skip, ssd_norm_w, fox_f_bias, even_w_out, diff_w_qkv, diff_lambda_q1, diff_lambda_k1, diff_lambda_q2, diff_lambda_k2, diff_subln_w, diff_w_o, rel_bias_table):
    nb = x.shape[0]
    meta = jnp.broadcast_to(meta_tokens[None].astype(x.dtype), (nb, N_META, D_MODEL))
    h = jnp.concatenate([jnp.zeros((nb, PADL, D_MODEL), x.dtype), meta, x], axis=1)
    hf = h.reshape(nb * LP, D_MODEL)

    def ln_params(l, i):
        return ln_gain[l, i][None, :], ln_bias[l, i][None, :]

    def ffn_weights(wg, wu, wd, l):
        return wg[l].astype(BF16), wu[l].astype(BF16), wd[l].astype(BF16)

    hf = _ffn_ln(hf, *ffn_weights(ffn1_w_gate, ffn1_w_up, ffn1_w_down, 0), *ln_params(0, 0))

    w_in = even_w_in[0]
    o_z, o_xbc = 0, SSD_D_INNER
    o_dt = o_xbc + SSD_CONV_CH
    o_q = o_dt + SSD_HEADS
    o_k, o_v = o_q + FOX_WIDTH, o_q + 2 * FOX_WIDTH
    o_f = o_q + 3 * FOX_WIDTH
    w_small = jnp.concatenate(
        [w_in[:, o_dt:o_q], w_in[:, o_f:],
         jnp.zeros((D_MODEL, SMALL_W - SSD_HEADS - FOX_HEADS), w_in.dtype)], axis=1)
    w_even = jnp.concatenate(
        [w_in[:, o_z:o_dt], w_in[:, o_q:o_k] * HEAD_DIM ** -0.5, w_in[:, o_k:o_f], w_small],
        axis=1).astype(BF16)
    z, xbc, q, k, v, small = _proj(
        hf, w_even,
        (SSD_D_INNER, SSD_CONV_CH, FOX_WIDTH, FOX_WIDTH, FOX_WIDTH, SMALL_W),
        (F32, F32, BF16, BF16, BF16, F32), "even_in_proj")

    sbias = _pad_lanes(jnp.concatenate([ssd_dt_bias[0], fox_f_bias[0]]))
    alog = _pad_lanes(ssd_a_log[0])
    dskip = jnp.repeat(ssd_d_skip[0], SSD_HEAD_DIM)[None, :].astype(F32)
    expand = np.zeros((SMALL_W, SSD_D_INNER), np.float32)
    expand[np.arange(SSD_D_INNER) // SSD_HEAD_DIM, np.arange(SSD_D_INNER)] = 1.0
    y, cfull = _ssd(xbc.reshape(nb, LP, SSD_CONV_CH), z.reshape(nb, LP, SSD_D_INNER),
                    small.reshape(nb, LP, SMALL_W), even_conv_w[0], even_conv_b[0][None, :],
                    sbias, alog, dskip, ssd_norm_w[0][None, :], jnp.asarray(expand, BF16))
    ck = cfull[:, :, SSD_HEADS:SSD_HEADS + FOX_HEADS].transpose(0, 2, 1)
    ck = ck.reshape(nb, FOX_HEADS // 2, 2, LP)
    kt = k.reshape(nb, LP, FOX_WIDTH).transpose(0, 2, 1)
    v4 = v.reshape(nb, LP, FOX_HEADS // 2, 2, HEAD_DIM)
    ones = jnp.ones((nb, LP, FOX_HEADS // 2, HEAD_DIM), BF16)
    v_aug = jnp.concatenate([v4[:, :, :, 0], ones, ones, v4[:, :, :, 1]], axis=-1)
    o = _fox_attention(q.reshape(nb, LP, FOX_WIDTH), kt,
                       v_aug.reshape(nb, LP, FOX_HEADS * LANES), ck)
    w_out = even_w_out[0].astype(BF16)
    hf = _outproj_ln(hf, [y.reshape(nb * LP, SSD_D_INNER), o.reshape(nb * LP, FOX_WIDTH)],
                     [w_out[:SSD_D_INNER], w_out[SSD_D_INNER:]], *ln_params(0, 1),
                     name="even_out_proj_ln")
    hf = _ffn_ln(hf, *ffn_weights(ffn2_w_gate, ffn2_w_up, ffn2_w_down, 0), *ln_params(0, 2))

    hf = _ffn_ln(hf, *ffn_weights(ffn1_w_gate, ffn1_w_up, ffn1_w_down, 1), *ln_params(1, 0))
    qw = DIFF_HEADS * 2 * HEAD_DIM
    w_qkv = diff_w_qkv[0]
    w_qkv = jnp.concatenate([w_qkv[:, :qw] * HEAD_DIM ** -0.5, w_qkv[:, qw:]], axis=1).astype(BF16)
    q, k, v = _proj(hf, w_qkv, (qw, qw, DIFF_HEADS * LANES), (BF16, BF16, BF16), "diff_qkv_proj")
    lambda_init = 0.8 - 0.6 * math.exp(-0.3 * 1)
    lam_rows = jnp.concatenate(
        [_pad_lanes(diff_lambda_q1[0]), _pad_lanes(diff_lambda_k1[0]),
         _pad_lanes(diff_lambda_q2[0]), _pad_lanes(diff_lambda_k2[0]),
         jnp.zeros((4, LANES), F32)], axis=0)
    kt = k.reshape(nb, LP, qw).transpose(0, 2, 1)
    v3 = v.reshape(nb, LP, DIFF_HEADS, LANES)
    v_aug = jnp.concatenate([v3, jnp.ones_like(v3)], axis=-1)
    o = _diff_attention(q.reshape(nb, LP, qw), kt,
                        v_aug.reshape(nb, LP, DIFF_HEADS * 2 * LANES),
                        _relative_bias_tiles(rel_bias_table), lam_rows,
                        diff_subln_w[0][None, :], lambda_init)
    hf = _outproj_ln(hf, [o.reshape(nb * LP, DIFF_HEADS * LANES)], [diff_w_o[0].astype(BF16)],
                     *ln_params(1, 1), name="diff_out_proj_ln")
    return _ffn_ln_final(hf.reshape(nb, LP, D_MODEL),
                         *ffn_weights(ffn2_w_gate, ffn2_w_up, ffn2_w_down, 1), *ln_params(1, 2))
```

```python
import functools
import math

import numpy as np
import jax
import jax.numpy as jnp
from jax import lax
from jax.experimental import pallas as pl
from jax.experimental.pallas import tpu as pltpu

F32 = jnp.float32
BF16 = jnp.bfloat16

D_MODEL = 1024
SEQ = 8192
DEPTH = 2
N_META = 16
CHUNK = 128
SSD_D_INNER = 2048
SSD_HEAD_DIM = 64
SSD_HEADS = 32
SSD_GROUPS = 4
SSD_GROUP_W = SSD_D_INNER // SSD_GROUPS
SSD_STATE = 128
SSD_CONV = 4
SSD_CONV_CH = SSD_D_INNER + 2 * SSD_GROUPS * SSD_STATE
FOX_HEADS = 16
FOX_WIDTH = 1024
HEAD_DIM = 64
DIFF_HEADS = 8
N_BUCKETS = 32
D_FF = 2816
ALPHA = (2 * DEPTH) ** 0.25
LN_EPS = 1e-5
RMS_EPS = 1e-5
NEG = -1e30

LANES = 128
LP = 8448
PADL = LP - SEQ - N_META
ROW_TILE = 512
OUT_TILE = 256
ATT_TQ = 768
ATT_TK = 768
CONV_HALO = 8
SMALL_W = LANES
VMEM_LIMIT = 56 * 1024 * 1024

assert PADL % CHUNK == CHUNK - N_META
assert LP % ATT_TQ == 0 and ATT_TQ % ATT_TK == 0 and ATT_TK % LANES == 0
assert LP % CHUNK == 0 and (2 * LP) % ROW_TILE == 0
assert PADL <= OUT_TILE and (LP - OUT_TILE) == SEQ


def _const_spec(shape):
    nd = len(shape)
    return pl.BlockSpec(shape, lambda *_: (0,) * nd, pipeline_mode=pl.Buffered(1))


def _layer_norm(r, g, b):
    mu = jnp.mean(r, axis=-1, keepdims=True)
    d = r - mu
    var = jnp.mean(d * d, axis=-1, keepdims=True)
    return d * lax.rsqrt(var + LN_EPS) * g + b


def _silu(x):
    return x / (1.0 + jnp.exp(-x))


def _softplus(x):
    return jnp.maximum(x, 0.0) + jnp.log(1.0 + jnp.exp(-jnp.abs(x)))


def _ffn_ln_body(h_ref, wg_ref, wu_ref, wd_ref, g_ref, b_ref, o_ref):
    h = h_ref[...]
    hb = h.astype(BF16)
    g = jnp.dot(hb, wg_ref[...], preferred_element_type=F32)
    u = jnp.dot(hb, wu_ref[...], preferred_element_type=F32)
    a = (_silu(g) * u).astype(BF16)
    y = jnp.dot(a, wd_ref[...], preferred_element_type=F32)
    o_ref[...] = _layer_norm(ALPHA * h + 0.5 * y, g_ref[...], b_ref[...])


def _ffn_ln(hf, wg, wu, wd, g, b):
    rows = hf.shape[0]
    return pl.pallas_call(
        _ffn_ln_body,
        grid=(rows // ROW_TILE,),
        in_specs=[
            pl.BlockSpec((ROW_TILE, D_MODEL), lambda i: (i, 0)),
            _const_spec((D_MODEL, D_FF)),
            _const_spec((D_MODEL, D_FF)),
            _const_spec((D_FF, D_MODEL)),
            _const_spec((1, D_MODEL)),
            _const_spec((1, D_MODEL)),
        ],
        out_specs=pl.BlockSpec((ROW_TILE, D_MODEL), lambda i: (i, 0)),
        out_shape=jax.ShapeDtypeStruct((rows, D_MODEL), F32),
        compiler_params=pltpu.CompilerParams(
            dimension_semantics=("parallel",), vmem_limit_bytes=VMEM_LIMIT),
        name="ffn_ln",
    )(hf, wg, wu, wd, g, b)


def _ffn_ln_final_body(h_ref, wg_ref, wu_ref, wd_ref, g_ref, b_ref, o_ref):
    @pl.when(pl.program_id(1) > 0)
    def _():
        _ffn_ln_body(h_ref, wg_ref, wu_ref, wd_ref, g_ref, b_ref, o_ref)


def _ffn_ln_final(h3, wg, wu, wd, g, b):
    nb = h3.shape[0]
    return pl.pallas_call(
        _ffn_ln_final_body,
        grid=(nb, LP // OUT_TILE),
        in_specs=[
            pl.BlockSpec((None, OUT_TILE, D_MODEL), lambda bi, i: (bi, i, 0)),
            _const_spec((D_MODEL, D_FF)),
            _const_spec((D_MODEL, D_FF)),
            _const_spec((D_FF, D_MODEL)),
            _const_spec((1, D_MODEL)),
            _const_spec((1, D_MODEL)),
        ],
        out_specs=pl.BlockSpec((None, OUT_TILE, D_MODEL),
                               lambda bi, i: (bi, jnp.maximum(i - 1, 0), 0)),
        out_shape=jax.ShapeDtypeStruct((nb, SEQ, D_MODEL), F32),
        compiler_params=pltpu.CompilerParams(
            dimension_semantics=("arbitrary", "arbitrary"), vmem_limit_bytes=VMEM_LIMIT),
        name="ffn_ln_final",
    )(h3, wg, wu, wd, g, b)


PROJ_TILE = 256
assert LP % PROJ_TILE == 0


def _proj_body(h_ref, w_ref, wkt_ref, vones_ref, *o_refs):
    hb = h_ref[...].astype(BF16)
    off = 0
    for o_ref in o_refs[:-2]:
        n = o_ref.shape[-1]
        o_ref[...] = jnp.dot(hb, w_ref[:, off:off + n],
                             preferred_element_type=F32).astype(o_ref.dtype)
        off += n
    kt_ref, va_ref = o_refs[-2:]
    kt_ref[...] = lax.dot_general(wkt_ref[...], hb, (((1,), (1,)), ((), ())),
                                  preferred_element_type=F32).astype(kt_ref.dtype)
    va_ref[...] = (jnp.dot(hb, w_ref[:, off:], preferred_element_type=F32)
                   + vones_ref[...]).astype(va_ref.dtype)


def _proj(hf, w, wkt, vones, widths, dtypes, nb, name):
    rows = hf.shape[0]
    tiles_per_batch = LP // PROJ_TILE
    k_width = wkt.shape[0]
    va_width = vones.shape[1]
    out_specs = [pl.BlockSpec((PROJ_TILE, n), lambda i: (i, 0)) for n in widths]
    out_specs += [
        pl.BlockSpec((None, k_width, PROJ_TILE),
                     lambda i: (i // tiles_per_batch, 0, i % tiles_per_batch)),
        pl.BlockSpec((PROJ_TILE, va_width), lambda i: (i, 0)),
    ]
    out_shape = [jax.ShapeDtypeStruct((rows, n), dt) for n, dt in zip(widths, dtypes)]
    out_shape += [jax.ShapeDtypeStruct((nb, k_width, LP), BF16),
                  jax.ShapeDtypeStruct((rows, va_width), BF16)]
    return pl.pallas_call(
        _proj_body,
        grid=(rows // PROJ_TILE,),
        in_specs=[pl.BlockSpec((PROJ_TILE, D_MODEL), lambda i: (i, 0)),
                  _const_spec(w.shape), _const_spec(wkt.shape), _const_spec(vones.shape)],
        out_specs=out_specs,
        out_shape=out_shape,
        compiler_params=pltpu.CompilerParams(
            dimension_semantics=("parallel",), vmem_limit_bytes=VMEM_LIMIT),
        name=name,
    )(hf, w, wkt, vones)


def _outproj_ln_body(n_in, h_ref, *refs):
    a_refs = refs[:n_in]
    w_refs = refs[n_in:2 * n_in]
    g_ref, b_ref, o_ref = refs[2 * n_in:]
    m = jnp.dot(a_refs[0][...], w_refs[0][...], preferred_element_type=F32)
    for a_ref, w_ref in zip(a_refs[1:], w_refs[1:]):
        m = m + jnp.dot(a_ref[...], w_ref[...], preferred_element_type=F32)
    o_ref[...] = _layer_norm(ALPHA * h_ref[...] + m, g_ref[...], b_ref[...])


def _outproj_ln(hf, acts, ws, g, b, name):
    rows = hf.shape[0]
    n_in = len(acts)
    in_specs = [pl.BlockSpec((ROW_TILE, D_MODEL), lambda i: (i, 0))]
    in_specs += [pl.BlockSpec((ROW_TILE, a.shape[1]), lambda i: (i, 0)) for a in acts]
    in_specs += [_const_spec(w.shape) for w in ws]
    in_specs += [_const_spec((1, D_MODEL)), _const_spec((1, D_MODEL))]
    return pl.pallas_call(
        functools.partial(_outproj_ln_body, n_in),
        grid=(rows // ROW_TILE,),
        in_specs=in_specs,
        out_specs=pl.BlockSpec((ROW_TILE, D_MODEL), lambda i: (i, 0)),
        out_shape=jax.ShapeDtypeStruct((rows, D_MODEL), F32),
        compiler_params=pltpu.CompilerParams(
            dimension_semantics=("parallel",), vmem_limit_bytes=VMEM_LIMIT),
        name=name,
    )(hf, *acts, *ws, g, b)


def _split_dot(x, e_ref):
    hi = x.astype(BF16)
    lo = (x - hi.astype(F32)).astype(BF16)
    e = e_ref[...]
    return (jnp.dot(hi, e, preferred_element_type=F32)
            + jnp.dot(lo, e, preferred_element_type=F32))


def _ssd_body(xbc_ref, z_ref, small_ref, convw_ref, convb_ref, sbias_ref, alog_ref,
              dskip_ref, normw_ref, expand_ref, y_ref, c_ref,
              ext_ref, state_ref, carry_ref):
    c = pl.program_id(1)

    @pl.when(c == 0)
    def _():
        ext_ref[0:CONV_HALO, :] = jnp.zeros((CONV_HALO, SSD_CONV_CH), F32)
        state_ref[...] = jnp.zeros_like(state_ref)
        carry_ref[...] = jnp.zeros_like(carry_ref)

    row = lax.broadcasted_iota(jnp.int32, (CHUNK, 1), 0)
    valid = (c * CHUNK + row) >= PADL

    ext_ref[CONV_HALO:CONV_HALO + CHUNK, :] = jnp.where(valid, xbc_ref[...], 0.0)
    conv = convb_ref[...]
    for k in range(SSD_CONV):
        start = CONV_HALO - (SSD_CONV - 1) + k
        conv = conv + convw_ref[k:k + 1, :] * ext_ref[start:start + CHUNK, :]
    ext_ref[0:CONV_HALO, :] = ext_ref[CHUNK:CHUNK + CONV_HALO, :]
    xc = jnp.where(valid, _silu(conv), 0.0)
    xs = xc[:, :SSD_D_INNER]
    bm = xc[:, SSD_D_INNER:SSD_D_INNER + SSD_GROUPS * SSD_STATE].astype(BF16)
    cm = xc[:, SSD_D_INNER + SSD_GROUPS * SSD_STATE:].astype(BF16)

    lane = lax.broadcasted_iota(jnp.int32, (CHUNK, SMALL_W), 1)
    is_dt = lane < SSD_HEADS
    is_f = (lane >= SSD_HEADS) & (lane < SSD_HEADS + FOX_HEADS)
    v = small_ref[...] + sbias_ref[...]
    dt = jnp.where(valid & is_dt, _softplus(v), 0.0)
    log_f = jnp.where(valid & is_f, -_softplus(-v), 0.0)
    neg_a = -jnp.exp(alog_ref[...])
    steps = jnp.where(is_dt, dt * neg_a, log_f)
    r_i = lax.broadcasted_iota(jnp.int32, (CHUNK, CHUNK), 0)
    c_i = lax.broadcasted_iota(jnp.int32, (CHUNK, CHUNK), 1)
    causal = r_i >= c_i
    tril = jnp.where(causal, 1.0, 0.0).astype(F32)
    cum = jnp.dot(tril, steps, preferred_element_type=F32,
                  precision=lax.Precision.HIGHEST)
    c_total = jnp.where(is_f, cum + carry_ref[...], 0.0)
    c_ref[...] = c_total
    carry_ref[...] = c_total[CHUNK - 1:CHUNK, :]

    a_last = cum[CHUNK - 1:CHUNK, :]
    dt_x = _split_dot(dt, expand_ref)
    ea_x = _split_dot(jnp.exp(cum), expand_ref)
    de_x = _split_dot(jnp.exp(a_last - cum), expand_ref)
    x_dt = xs * dt_x
    xb = x_dt.astype(BF16)
    xe = (x_dt * de_x).astype(BF16)
    cum_t = cum.T
    lane_p = lax.broadcasted_iota(jnp.int32, (CHUNK, 2 * SSD_HEAD_DIM), 1)
    first_half = lane_p < SSD_HEAD_DIM

    y_groups = []
    for g in range(SSD_GROUPS):
        gs = slice(g * SSD_GROUP_W, (g + 1) * SSD_GROUP_W)
        bg = bm[:, g * SSD_STATE:(g + 1) * SSD_STATE]
        cg = cm[:, g * SSD_STATE:(g + 1) * SSD_STATE]
        cb = lax.dot_general(cg, bg, (((1,), (1,)), ((), ())),
                             preferred_element_type=F32)
        pair_out = []
        for pr in range(SSD_GROUP_W // (2 * SSD_HEAD_DIM)):
            col0 = g * SSD_GROUP_W + pr * 2 * SSD_HEAD_DIM
            x_pair = xb[:, col0:col0 + 2 * SSD_HEAD_DIM]
            ys = []
            for j in range(2):
                hd = col0 // SSD_HEAD_DIM + j
                diff = cum[:, hd:hd + 1] - cum_t[hd:hd + 1, :]
                decay = jnp.exp(jnp.where(causal, diff, -jnp.inf))
                mat = (cb * decay).astype(BF16)
                ys.append(jnp.dot(mat, x_pair, preferred_element_type=F32))
            pair_out.append(jnp.where(first_half, ys[0], ys[1]))
        y_diag = jnp.concatenate(pair_out, axis=1)
        st = state_ref[g]
        y_off = jnp.dot(cg, st.astype(BF16), preferred_element_type=F32) * ea_x[:, gs]
        new = lax.dot_general(bg, xe[:, gs], (((0,), (0,)), ((), ())),
                              preferred_element_type=F32)
        state_ref[g] = st * ea_x[CHUNK - 1:CHUNK, gs] + new
        yg = y_diag + y_off + dskip_ref[:, gs] * xs[:, gs]
        yg = yg * _silu(z_ref[:, gs])
        ms = jnp.mean(yg * yg, axis=-1, keepdims=True)
        y_groups.append(yg * lax.rsqrt(ms + RMS_EPS) * normw_ref[:, gs])
    y_ref[...] = jnp.concatenate(y_groups, axis=1).astype(y_ref.dtype)


def _ssd(xbc, z, small, convw, convb, sbias, alog, dskip, normw, expand):
    nb = xbc.shape[0]
    row_spec = lambda w: pl.BlockSpec((None, CHUNK, w), lambda bi, ci: (bi, ci, 0))
    return pl.pallas_call(
        _ssd_body,
        grid=(nb, LP // CHUNK),
        in_specs=[row_spec(SSD_CONV_CH), row_spec(SSD_D_INNER), row_spec(SMALL_W),
                  _const_spec(convw.shape), _const_spec(convb.shape),
                  _const_spec(sbias.shape), _const_spec(alog.shape),
                  _const_spec(dskip.shape), _const_spec(normw.shape),
                  _const_spec(expand.shape)],
        out_specs=[row_spec(SSD_D_INNER), row_spec(SMALL_W)],
        out_shape=[jax.ShapeDtypeStruct((nb, LP, SSD_D_INNER), BF16),
                   jax.ShapeDtypeStruct((nb, LP, SMALL_W), F32)],
        scratch_shapes=[pltpu.VMEM((CHUNK + CONV_HALO, SSD_CONV_CH), F32),
                        pltpu.VMEM((SSD_GROUPS, SSD_STATE, SSD_GROUP_W), F32),
                        pltpu.VMEM((1, SMALL_W), F32)],
        compiler_params=pltpu.CompilerParams(
            dimension_semantics=("arbitrary", "arbitrary"), vmem_limit_bytes=VMEM_LIMIT),
        name="ssd",
    )(xbc, z, small, convw, convb, sbias, alog, dskip, normw, expand)


N_TK_CHUNKS = ATT_TK // LANES
N_TQ_CHUNKS = ATT_TQ // LANES
TILE_RATIO = ATT_TQ // ATT_TK
N_PAD_TILES = -(-PADL // ATT_TK)
LOG2E = math.log2(math.e)
Q_SCALE = LOG2E * HEAD_DIM ** -0.5


def _softmax_update(chunks, v_aug, m_ref, acc_ref, idx):
    m_old = m_ref[idx]
    part = chunks[0]
    for sc in chunks[1:]:
        part = jnp.maximum(part, sc)
    m_new = jnp.maximum(m_old, jnp.max(part, axis=-1, keepdims=True))
    alpha = jnp.exp2(m_old - m_new)
    p = jnp.concatenate([jnp.exp2(sc - m_new).astype(BF16) for sc in chunks], axis=1)
    pv = jnp.dot(p, v_aug, preferred_element_type=F32)
    for w in range(acc_ref.shape[-1] // LANES):
        ws = slice(w * LANES, (w + 1) * LANES)
        acc_ref[idx, :, ws] = alpha * acc_ref[idx, :, ws] + pv[:, ws]
    m_ref[idx] = m_new


def _visible_chunk(qb, key_start):
    qpos = qb * ATT_TQ + lax.broadcasted_iota(jnp.int32, (ATT_TQ, LANES), 0)
    kpos = key_start + lax.broadcasted_iota(jnp.int32, (ATT_TQ, LANES), 1)
    return (kpos <= qpos) & (kpos >= PADL)


def _init_softmax_state(m_ref, acc_ref):
    m_ref[...] = jnp.full(m_ref.shape, NEG, F32)
    acc_ref[...] = jnp.zeros_like(acc_ref)


def _split_halves(q):
    lane = lax.broadcasted_iota(jnp.int32, q.shape, 1)
    first_half = lane < HEAD_DIM
    zero = jnp.zeros_like(q)
    return first_half, (jnp.where(first_half, q, zero), jnp.where(first_half, zero, q))


def _fox_body(q_ref, kt_ref, v_ref, ck_ref, o_ref, m_ref, acc_ref):
    qb = pl.program_id(2)
    first_half, q_heads = _split_halves(q_ref[...])
    _init_softmax_state(m_ref, acc_ref)

    def step(kb, masked):
        start = pl.multiple_of(kb * ATT_TK, ATT_TK)
        kt = kt_ref[:, pl.ds(start, ATT_TK)]
        for j in range(2):
            s = jnp.dot(q_heads[j], kt, preferred_element_type=F32)
            chunks = []
            for c in range(N_TK_CHUNKS):
                key0 = pl.multiple_of(start + c * LANES, LANES)
                sc = s[:, c * LANES:(c + 1) * LANES] - ck_ref[j:j + 1, pl.ds(key0, LANES)]
                if masked:
                    sc = jnp.where(_visible_chunk(qb, key0), sc, NEG)
                chunks.append(sc)
            v_aug = v_ref[pl.ds(start, ATT_TK), j * LANES:(j + 1) * LANES]
            _softmax_update(chunks, v_aug, m_ref, acc_ref, j)

    def masked_loop(kb, carry):
        step(kb, True)
        return carry

    def plain_loop(kb, carry):
        step(kb, False)
        return carry

    n_below = qb * TILE_RATIO
    n_head = jnp.minimum(N_PAD_TILES, n_below)
    lax.fori_loop(0, n_head, masked_loop, 0)
    lax.fori_loop(n_head, n_below, plain_loop, 0)
    lax.fori_loop(n_below, n_below + TILE_RATIO, masked_loop, 0)

    acc0, acc1 = acc_ref[0], acc_ref[1]
    num = jnp.where(first_half, acc0, acc1)
    den = pltpu.roll(jnp.where(first_half, acc1, acc0), HEAD_DIM, axis=1)
    o_ref[...] = (num / den).astype(o_ref.dtype)


def _fox_attention(q, kt, v_aug, ck):
    nb = q.shape[0]
    pairs = FOX_HEADS // 2
    return pl.pallas_call(
        _fox_body,
        grid=(nb, pairs, LP // ATT_TQ),
        in_specs=[
            pl.BlockSpec((None, ATT_TQ, LANES), lambda bi, hp, qi: (bi, qi, hp)),
            pl.BlockSpec((None, LANES, LP), lambda bi, hp, qi: (bi, hp, 0)),
            pl.BlockSpec((None, LP, 2 * LANES), lambda bi, hp, qi: (bi, 0, hp)),
            pl.BlockSpec((None, None, 2, LP), lambda bi, hp, qi: (bi, hp, 0, 0)),
        ],
        out_specs=pl.BlockSpec((None, ATT_TQ, LANES), lambda bi, hp, qi: (bi, qi, hp)),
        out_shape=jax.ShapeDtypeStruct((nb, LP, FOX_WIDTH), BF16),
        scratch_shapes=[pltpu.VMEM((2, ATT_TQ, LANES), F32),
                        pltpu.VMEM((2, ATT_TQ, LANES), F32)],
        compiler_params=pltpu.CompilerParams(
            dimension_semantics=("parallel", "parallel", "arbitrary"),
            vmem_limit_bytes=VMEM_LIMIT),
        name="fox_attention",
    )(q, kt, v_aug, ck)


def _bias_tiles(bias_ref):
    rows = jnp.broadcast_to(bias_ref[...], (CHUNK, 2 * CHUNK))
    rolled = pltpu.roll(rows, 0, 1, stride=1, stride_axis=0)
    return rolled[:, :CHUNK], rolled[:, CHUNK:]


def _bias_chunk(tiles, key_chunk):
    picked = []
    for a in range(N_TQ_CHUNKS):
        sub = a - key_chunk
        picked.append(tiles[sub] if sub in (0, 1) else None)
    if all(t is None for t in picked):
        return None
    zeros = jnp.zeros((CHUNK, CHUNK), F32)
    return jnp.concatenate([zeros if t is None else t for t in picked], axis=0)


def _diff_body(lambda_init, q_ref, kt_ref, v_ref, bias_ref, lam_ref, subln_ref, o_ref,
               m_ref, acc_ref):
    qb = pl.program_id(2)
    _, q_parts = _split_halves(q_ref[...])
    _init_softmax_state(m_ref, acc_ref)

    def step(kb, masked, first_key_chunk):
        start = pl.multiple_of(kb * ATT_TK, ATT_TK)
        kt = kt_ref[:, pl.ds(start, ATT_TK)]
        v_aug = v_ref[pl.ds(start, ATT_TK), :]
        biases = [None] * N_TK_CHUNKS
        if first_key_chunk is not None:
            tiles = _bias_tiles(bias_ref)
            biases = [_bias_chunk(tiles, first_key_chunk + c) for c in range(N_TK_CHUNKS)]
        for j in range(2):
            s = jnp.dot(q_parts[j], kt, preferred_element_type=F32)
            chunks = []
            for c in range(N_TK_CHUNKS):
                sc = s[:, c * LANES:(c + 1) * LANES]
                if biases[c] is not None:
                    sc = sc + biases[c]
                if masked:
                    sc = jnp.where(_visible_chunk(qb, start + c * LANES), sc, NEG)
                chunks.append(sc)
            _softmax_update(chunks, v_aug, m_ref, acc_ref, j)

    def masked_loop(kb, carry):
        step(kb, True, None)
        return carry

    def plain_loop(kb, carry):
        step(kb, False, None)
        return carry

    n_below = qb * TILE_RATIO
    n_far = jnp.maximum(n_below - 1, 0)
    n_head = jnp.minimum(N_PAD_TILES, n_far)
    lax.fori_loop(0, n_head, masked_loop, 0)
    lax.fori_loop(n_head, n_far, plain_loop, 0)

    @pl.when(qb > 0)
    def _():
        step(n_below - 1, True, -N_TK_CHUNKS)

    for jj in range(TILE_RATIO):
        step(n_below + jj, True, jj * N_TK_CHUNKS)

    lam1 = jnp.exp(jnp.sum(lam_ref[0:1, :] * lam_ref[1:2, :], axis=-1, keepdims=True))
    lam2 = jnp.exp(jnp.sum(lam_ref[2:3, :] * lam_ref[3:4, :], axis=-1, keepdims=True))
    lam = lam1 - lam2 + lambda_init
    o = (acc_ref[0, :, :LANES] / acc_ref[0, :, LANES:]
         - lam * (acc_ref[1, :, :LANES] / acc_ref[1, :, LANES:]))
    ms = jnp.mean(o * o, axis=-1, keepdims=True)
    o = o * lax.rsqrt(ms + RMS_EPS) * subln_ref[...] * (1.0 - lambda_init)
    o_ref[...] = o.astype(o_ref.dtype)


def _diff_attention(q, kt, v_aug, bias_rows, lam_rows, subln, lambda_init):
    nb = q.shape[0]
    return pl.pallas_call(
        functools.partial(_diff_body, lambda_init),
        grid=(nb, DIFF_HEADS, LP // ATT_TQ),
        in_specs=[
            pl.BlockSpec((None, ATT_TQ, LANES), lambda bi, hd, qi: (bi, qi, hd)),
            pl.BlockSpec((None, LANES, LP), lambda bi, hd, qi: (bi, hd, 0)),
            pl.BlockSpec((None, LP, 2 * LANES), lambda bi, hd, qi: (bi, 0, hd)),
            pl.BlockSpec((None, 1, 2 * CHUNK), lambda bi, hd, qi: (hd, 0, 0)),
            _const_spec(lam_rows.shape),
            _const_spec(subln.shape),
        ],
        out_specs=pl.BlockSpec((None, ATT_TQ, LANES), lambda bi, hd, qi: (bi, qi, hd)),
        out_shape=jax.ShapeDtypeStruct((nb, LP, DIFF_HEADS * LANES), BF16),
        scratch_shapes=[pltpu.VMEM((2, ATT_TQ, LANES), F32),
                        pltpu.VMEM((2, ATT_TQ, 2 * LANES), F32)],
        compiler_params=pltpu.CompilerParams(
            dimension_semantics=("parallel", "parallel", "arbitrary"),
            vmem_limit_bytes=VMEM_LIMIT),
        name="diff_attention",
    )(q, kt, v_aug, bias_rows, lam_rows, subln)


def _t5_bucket(n):
    max_exact = N_BUCKETS // 2
    nf = jnp.maximum(n, 1).astype(F32)
    large = max_exact + (jnp.log(nf / max_exact) / math.log(128 / max_exact)
                         * (N_BUCKETS - max_exact)).astype(jnp.int32)
    large = jnp.minimum(large, N_BUCKETS - 1)
    return jnp.where(n < max_exact, n, large)


def _relative_bias_rows(rel_table):
    dist = jnp.arange(2 * CHUNK)
    by_dist = LOG2E * (rel_table[_t5_bucket(dist)] - rel_table[N_BUCKETS - 1])
    reversed_rows = jnp.concatenate([by_dist[:1], by_dist[:0:-1]], axis=0)
    return reversed_rows.T[:, None, :].astype(F32)


def _pad_lanes(vec, width=LANES):
    return jnp.pad(vec, (0, width - vec.shape[0]))[None, :].astype(F32)


def kernel(x, meta_tokens, ln_gain, ln_bias, ffn1_w_gate, ffn1_w_up, ffn1_w_down, ffn2_w_gate, ffn2_w_up, ffn2_w_down, even_w_in, even_conv_w, even_conv_b, ssd_dt_bias, ssd_a_log, ssd_d_skip, ssd_norm_w, fox_f_bias, even_w_out, diff_w_qkv, diff_lambda_q1, diff_lambda_k1, diff_lambda_q2, diff_lambda_k2, diff_subln_w, diff_w_o, rel_bias_table):
    nb = x.shape[0]
    meta = jnp.broadcast_to(meta_tokens[None].astype(x.dtype), (nb, N_META, D_MODEL))
    h = jnp.concatenate([jnp.zeros((nb, PADL, D_MODEL), x.dtype), meta, x], axis=1)
    hf = h.reshape(nb * LP, D_MODEL)

    def ln_params(l, i):
        return ln_gain[l, i][None, :], ln_bias[l, i][None, :]

    def ffn_weights(wg, wu, wd, l):
        return wg[l].astype(BF16), wu[l].astype(BF16), wd[l].astype(BF16)

    hf = _ffn_ln(hf, *ffn_weights(ffn1_w_gate, ffn1_w_up, ffn1_w_down, 0), *ln_params(0, 0))

    w_in = even_w_in[0]
    o_z, o_xbc = 0, SSD_D_INNER
    o_dt = o_xbc + SSD_CONV_CH
    o_q = o_dt + SSD_HEADS
    o_k, o_v = o_q + FOX_WIDTH, o_q + 2 * FOX_WIDTH
    o_f = o_q + 3 * FOX_WIDTH
    w_small = jnp.concatenate(
        [w_in[:, o_dt:o_q], w_in[:, o_f:],
         jnp.zeros((D_MODEL, SMALL_W - SSD_HEADS - FOX_HEADS), w_in.dtype)], axis=1)
    pairs = FOX_HEADS // 2
    wv4 = w_in[:, o_v:o_f].reshape(D_MODEL, pairs, 2, HEAD_DIM)
    wz = jnp.zeros((D_MODEL, pairs, HEAD_DIM), w_in.dtype)
    w_vaug = jnp.concatenate([wv4[:, :, 0], wz, wz, wv4[:, :, 1]], axis=-1)
    ones_fox = np.tile(np.repeat(np.array([0.0, 1.0, 1.0, 0.0], np.float32), HEAD_DIM), pairs)
    w_even = jnp.concatenate(
        [w_in[:, o_z:o_dt], w_in[:, o_q:o_k] * Q_SCALE, w_small,
         w_vaug.reshape(D_MODEL, 2 * FOX_WIDTH)], axis=1).astype(BF16)
    z, xbc, q, small, kt, v_aug = _proj(
        hf, w_even, w_in[:, o_k:o_v].T.astype(BF16), jnp.asarray(ones_fox)[None, :],
        (SSD_D_INNER, SSD_CONV_CH, FOX_WIDTH, SMALL_W), (F32, F32, BF16, F32), nb,
        "even_in_proj")

    sbias = _pad_lanes(jnp.concatenate([ssd_dt_bias[0], fox_f_bias[0]]))
    alog = _pad_lanes(ssd_a_log[0])
    dskip = jnp.repeat(ssd_d_skip[0], SSD_HEAD_DIM)[None, :].astype(F32)
    expand = np.zeros((SMALL_W, SSD_D_INNER), np.float32)
    expand[np.arange(SSD_D_INNER) // SSD_HEAD_DIM, np.arange(SSD_D_INNER)] = 1.0
    y, cfull = _ssd(xbc.reshape(nb, LP, SSD_CONV_CH), z.reshape(nb, LP, SSD_D_INNER),
                    small.reshape(nb, LP, SMALL_W), even_conv_w[0], even_conv_b[0][None, :],
                    sbias, alog, dskip, ssd_norm_w[0][None, :], jnp.asarray(expand, BF16))
    ck = LOG2E * cfull[:, :, SSD_HEADS:SSD_HEADS + FOX_HEADS].transpose(0, 2, 1)
    ck = ck.reshape(nb, FOX_HEADS // 2, 2, LP)
    o = _fox_attention(q.reshape(nb, LP, FOX_WIDTH), kt,
                       v_aug.reshape(nb, LP, FOX_HEADS * LANES), ck)
    w_out = even_w_out[0].astype(BF16)
    hf = _outproj_ln(hf, [y.reshape(nb * LP, SSD_D_INNER), o.reshape(nb * LP, FOX_WIDTH)],
                     [w_out[:SSD_D_INNER], w_out[SSD_D_INNER:]], *ln_params(0, 1),
                     name="even_out_proj_ln")
    hf = _ffn_ln(hf, *ffn_weights(ffn2_w_gate, ffn2_w_up, ffn2_w_down, 0), *ln_params(0, 2))

    hf = _ffn_ln(hf, *ffn_weights(ffn1_w_gate, ffn1_w_up, ffn1_w_down, 1), *ln_params(1, 0))
    qw = DIFF_HEADS * 2 * HEAD_DIM
    w_qkv = diff_w_qkv[0]
    wv3 = w_qkv[:, 2 * qw:].reshape(D_MODEL, DIFF_HEADS, LANES)
    w_q_vaug = jnp.concatenate(
        [w_qkv[:, :qw] * Q_SCALE,
         jnp.concatenate([wv3, jnp.zeros_like(wv3)], axis=-1).reshape(D_MODEL, 2 * qw)],
        axis=1).astype(BF16)
    ones_diff = np.tile(np.repeat(np.array([0.0, 1.0], np.float32), LANES), DIFF_HEADS)
    q, kt, v_aug = _proj(hf, w_q_vaug, w_qkv[:, qw:2 * qw].T.astype(BF16),
                         jnp.asarray(ones_diff)[None, :], (qw,), (BF16,), nb, "diff_qkv_proj")
    lambda_init = 0.8 - 0.6 * math.exp(-0.3 * 1)
    lam_rows = jnp.concatenate(
        [_pad_lanes(diff_lambda_q1[0]), _pad_lanes(diff_lambda_k1[0]),
         _pad_lanes(diff_lambda_q2[0]), _pad_lanes(diff_lambda_k2[0]),
         jnp.zeros((4, LANES), F32)], axis=0)
    o = _diff_attention(q.reshape(nb, LP, qw), kt,
                        v_aug.reshape(nb, LP, DIFF_HEADS * 2 * LANES),
                        _relative_bias_rows(rel_bias_table), lam_rows,
                        diff_subln_w[0][None, :], lambda_init)
    hf = _outproj_ln(hf, [o.reshape(nb * LP, DIFF_HEADS * LANES)], [diff_w_o[0].astype(BF16)],
                     *ln_params(1, 1), name="diff_out_proj_ln")
    return _ffn_ln_final(hf.reshape(nb, LP, D_MODEL),
                         *ffn_weights(ffn2_w_gate, ffn2_w_up, ffn2_w_down, 1), *ln_params(1, 2))
```

```python
import functools
import math

import numpy as np
import jax
import jax.numpy as jnp
from jax import lax
from jax.experimental import pallas as pl
from jax.experimental.pallas import tpu as pltpu

F32 = jnp.float32
BF16 = jnp.bfloat16

D_MODEL = 1024
SEQ = 8192
DEPTH = 2
N_META = 16
CHUNK = 128
SSD_D_INNER = 2048
SSD_HEAD_DIM = 64
SSD_HEADS = 32
SSD_GROUPS = 4
SSD_GROUP_W = SSD_D_INNER // SSD_GROUPS
SSD_STATE = 128
SSD_CONV = 4
SSD_CONV_CH = SSD_D_INNER + 2 * SSD_GROUPS * SSD_STATE
FOX_HEADS = 16
FOX_WIDTH = 1024
HEAD_DIM = 64
DIFF_HEADS = 8
N_BUCKETS = 32
D_FF = 2816
ALPHA = (2 * DEPTH) ** 0.25
LN_EPS = 1e-5
RMS_EPS = 1e-5
NEG = -1e30

LANES = 128
LP = 8448
PADL = LP - SEQ - N_META
ROW_TILE = 512
OUT_TILE = 256
ATT_TQ = 768
ATT_TK = 768
CONV_HALO = 8
SMALL_W = LANES
VMEM_LIMIT = 56 * 1024 * 1024

assert PADL % CHUNK == CHUNK - N_META
assert LP % ATT_TQ == 0 and ATT_TQ % ATT_TK == 0 and ATT_TK % LANES == 0
assert LP % CHUNK == 0 and (2 * LP) % ROW_TILE == 0
assert PADL <= OUT_TILE and (LP - OUT_TILE) == SEQ


def _const_spec(shape):
    nd = len(shape)
    return pl.BlockSpec(shape, lambda *_: (0,) * nd, pipeline_mode=pl.Buffered(1))


def _layer_norm(r, g, b):
    mu = jnp.mean(r, axis=-1, keepdims=True)
    d = r - mu
    var = jnp.mean(d * d, axis=-1, keepdims=True)
    return d * lax.rsqrt(var + LN_EPS) * g + b


def _silu(x):
    return x / (1.0 + jnp.exp(-x))


def _softplus(x):
    return jnp.maximum(x, 0.0) + jnp.log(1.0 + jnp.exp(-jnp.abs(x)))


def _ffn_ln_body(h_ref, wg_ref, wu_ref, wd_ref, g_ref, b_ref, o_ref):
    h = h_ref[...]
    hb = h.astype(BF16)
    g = jnp.dot(hb, wg_ref[...], preferred_element_type=F32)
    u = jnp.dot(hb, wu_ref[...], preferred_element_type=F32)
    a = (_silu(g) * u).astype(BF16)
    y = jnp.dot(a, wd_ref[...], preferred_element_type=F32)
    o_ref[...] = _layer_norm(ALPHA * h + 0.5 * y, g_ref[...], b_ref[...])


def _ffn_ln(hf, wg, wu, wd, g, b):
    rows = hf.shape[0]
    return pl.pallas_call(
        _ffn_ln_body,
        grid=(rows // ROW_TILE,),
        in_specs=[
            pl.BlockSpec((ROW_TILE, D_MODEL), lambda i: (i, 0)),
            _const_spec((D_MODEL, D_FF)),
            _const_spec((D_MODEL, D_FF)),
            _const_spec((D_FF, D_MODEL)),
            _const_spec((1, D_MODEL)),
            _const_spec((1, D_MODEL)),
        ],
        out_specs=pl.BlockSpec((ROW_TILE, D_MODEL), lambda i: (i, 0)),
        out_shape=jax.ShapeDtypeStruct((rows, D_MODEL), F32),
        compiler_params=pltpu.CompilerParams(
            dimension_semantics=("parallel",), vmem_limit_bytes=VMEM_LIMIT),
        name="ffn_ln",
    )(hf, wg, wu, wd, g, b)


def _ffn_ln_final_body(h_ref, wg_ref, wu_ref, wd_ref, g_ref, b_ref, o_ref):
    @pl.when(pl.program_id(1) > 0)
    def _():
        _ffn_ln_body(h_ref, wg_ref, wu_ref, wd_ref, g_ref, b_ref, o_ref)


def _ffn_ln_final(h3, wg, wu, wd, g, b):
    nb = h3.shape[0]
    return pl.pallas_call(
        _ffn_ln_final_body,
        grid=(nb, LP // OUT_TILE),
        in_specs=[
            pl.BlockSpec((None, OUT_TILE, D_MODEL), lambda bi, i: (bi, i, 0)),
            _const_spec((D_MODEL, D_FF)),
            _const_spec((D_MODEL, D_FF)),
            _const_spec((D_FF, D_MODEL)),
            _const_spec((1, D_MODEL)),
            _const_spec((1, D_MODEL)),
        ],
        out_specs=pl.BlockSpec((None, OUT_TILE, D_MODEL),
                               lambda bi, i: (bi, jnp.maximum(i - 1, 0), 0)),
        out_shape=jax.ShapeDtypeStruct((nb, SEQ, D_MODEL), F32),
        compiler_params=pltpu.CompilerParams(
            dimension_semantics=("arbitrary", "arbitrary"), vmem_limit_bytes=VMEM_LIMIT),
        name="ffn_ln_final",
    )(h3, wg, wu, wd, g, b)


PROJ_TILE = 256
assert LP % PROJ_TILE == 0


def _proj_body(h_ref, w_ref, wkt_ref, vones_ref, *o_refs):
    hb = h_ref[...].astype(BF16)
    off = 0
    for o_ref in o_refs[:-2]:
        n = o_ref.shape[-1]
        o_ref[...] = jnp.dot(hb, w_ref[:, off:off + n],
                             preferred_element_type=F32).astype(o_ref.dtype)
        off += n
    kt_ref, va_ref = o_refs[-2:]
    kt_ref[...] = lax.dot_general(wkt_ref[...], hb, (((1,), (1,)), ((), ())),
                                  preferred_element_type=F32).astype(kt_ref.dtype)
    va_ref[...] = (jnp.dot(hb, w_ref[:, off:], preferred_element_type=F32)
                   + vones_ref[...]).astype(va_ref.dtype)


def _proj(hf, w, wkt, vones, widths, dtypes, nb, name):
    rows = hf.shape[0]
    tiles_per_batch = LP // PROJ_TILE
    k_width = wkt.shape[0]
    va_width = vones.shape[1]
    out_specs = [pl.BlockSpec((PROJ_TILE, n), lambda i: (i, 0)) for n in widths]
    out_specs += [
        pl.BlockSpec((None, k_width, PROJ_TILE),
                     lambda i: (i // tiles_per_batch, 0, i % tiles_per_batch)),
        pl.BlockSpec((PROJ_TILE, va_width), lambda i: (i, 0)),
    ]
    out_shape = [jax.ShapeDtypeStruct((rows, n), dt) for n, dt in zip(widths, dtypes)]
    out_shape += [jax.ShapeDtypeStruct((nb, k_width, LP), BF16),
                  jax.ShapeDtypeStruct((rows, va_width), BF16)]
    return pl.pallas_call(
        _proj_body,
        grid=(rows // PROJ_TILE,),
        in_specs=[pl.BlockSpec((PROJ_TILE, D_MODEL), lambda i: (i, 0)),
                  _const_spec(w.shape), _const_spec(wkt.shape), _const_spec(vones.shape)],
        out_specs=out_specs,
        out_shape=out_shape,
        compiler_params=pltpu.CompilerParams(
            dimension_semantics=("parallel",), vmem_limit_bytes=VMEM_LIMIT),
        name=name,
    )(hf, w, wkt, vones)


def _outproj_ln_body(n_in, h_ref, *refs):
    a_refs = refs[:n_in]
    w_refs = refs[n_in:2 * n_in]
    g_ref, b_ref, o_ref = refs[2 * n_in:]
    m = jnp.dot(a_refs[0][...], w_refs[0][...], preferred_element_type=F32)
    for a_ref, w_ref in zip(a_refs[1:], w_refs[1:]):
        m = m + jnp.dot(a_ref[...], w_ref[...], preferred_element_type=F32)
    o_ref[...] = _layer_norm(ALPHA * h_ref[...] + m, g_ref[...], b_ref[...])


def _outproj_ln(hf, acts, ws, g, b, name):
    rows = hf.shape[0]
    n_in = len(acts)
    in_specs = [pl.BlockSpec((ROW_TILE, D_MODEL), lambda i: (i, 0))]
    in_specs += [pl.BlockSpec((ROW_TILE, a.shape[1]), lambda i: (i, 0)) for a in acts]
    in_specs += [_const_spec(w.shape) for w in ws]
    in_specs += [_const_spec((1, D_MODEL)), _const_spec((1, D_MODEL))]
    return pl.pallas_call(
        functools.partial(_outproj_ln_body, n_in),
        grid=(rows // ROW_TILE,),
        in_specs=in_specs,
        out_specs=pl.BlockSpec((ROW_TILE, D_MODEL), lambda i: (i, 0)),
        out_shape=jax.ShapeDtypeStruct((rows, D_MODEL), F32),
        compiler_params=pltpu.CompilerParams(
            dimension_semantics=("parallel",), vmem_limit_bytes=VMEM_LIMIT),
        name=name,
    )(hf, *acts, *ws, g, b)


def _split_dot(x, e_ref):
    hi = x.astype(BF16)
    lo = (x - hi.astype(F32)).astype(BF16)
    e = e_ref[...]
    return (jnp.dot(hi, e, preferred_element_type=F32)
            + jnp.dot(lo, e, preferred_element_type=F32))


def _ssd_body(xbc_ref, z_ref, small_ref, convw_ref, convb_ref, sbias_ref, alog_ref,
              dskip_ref, normw_ref, expand_ref, y_ref, c_ref,
              ext_ref, state_ref, carry_ref):
    c = pl.program_id(1)

    @pl.when(c == 0)
    def _():
        ext_ref[0:CONV_HALO, :] = jnp.zeros((CONV_HALO, SSD_CONV_CH), F32)
        state_ref[...] = jnp.zeros_like(state_ref)
        carry_ref[...] = jnp.zeros_like(carry_ref)

    row = lax.broadcasted_iota(jnp.int32, (CHUNK, 1), 0)
    valid = (c * CHUNK + row) >= PADL

    ext_ref[CONV_HALO:CONV_HALO + CHUNK, :] = jnp.where(valid, xbc_ref[...], 0.0)
    conv = convb_ref[...]
    for k in range(SSD_CONV):
        start = CONV_HALO - (SSD_CONV - 1) + k
        conv = conv + convw_ref[k:k + 1, :] * ext_ref[start:start + CHUNK, :]
    ext_ref[0:CONV_HALO, :] = ext_ref[CHUNK:CHUNK + CONV_HALO, :]
    xc = jnp.where(valid, _silu(conv), 0.0)
    xs = xc[:, :SSD_D_INNER]
    bm = xc[:, SSD_D_INNER:SSD_D_INNER + SSD_GROUPS * SSD_STATE].astype(BF16)
    cm = xc[:, SSD_D_INNER + SSD_GROUPS * SSD_STATE:].astype(BF16)

    lane = lax.broadcasted_iota(jnp.int32, (CHUNK, SMALL_W), 1)
    is_dt = lane < SSD_HEADS
    is_f = (lane >= SSD_HEADS) & (lane < SSD_HEADS + FOX_HEADS)
    v = small_ref[...] + sbias_ref[...]
    dt = jnp.where(valid & is_dt, _softplus(v), 0.0)
    log_f = jnp.where(valid & is_f, -_softplus(-v), 0.0)
    neg_a = -jnp.exp(alog_ref[...])
    steps = jnp.where(is_dt, dt * neg_a, log_f)
    r_i = lax.broadcasted_iota(jnp.int32, (CHUNK, CHUNK), 0)
    c_i = lax.broadcasted_iota(jnp.int32, (CHUNK, CHUNK), 1)
    causal = r_i >= c_i
    tril = jnp.where(causal, 1.0, 0.0).astype(F32)
    cum = jnp.dot(tril, steps, preferred_element_type=F32,
                  precision=lax.Precision.HIGHEST)
    c_total = jnp.where(is_f, cum + carry_ref[...], 0.0)
    c_ref[...] = c_total
    carry_ref[...] = c_total[CHUNK - 1:CHUNK, :]

    a_last = cum[CHUNK - 1:CHUNK, :]
    dt_x = _split_dot(dt, expand_ref)
    ea_x = _split_dot(jnp.exp(cum), expand_ref)
    de_x = _split_dot(jnp.exp(a_last - cum), expand_ref)
    x_dt = xs * dt_x
    xb = x_dt.astype(BF16)
    xe = (x_dt * de_x).astype(BF16)
    cum_t = cum.T
    lane_p = lax.broadcasted_iota(jnp.int32, (CHUNK, 2 * SSD_HEAD_DIM), 1)
    first_half = lane_p < SSD_HEAD_DIM

    y_groups = []
    for g in range(SSD_GROUPS):
        gs = slice(g * SSD_GROUP_W, (g + 1) * SSD_GROUP_W)
        bg = bm[:, g * SSD_STATE:(g + 1) * SSD_STATE]
        cg = cm[:, g * SSD_STATE:(g + 1) * SSD_STATE]
        cb = lax.dot_general(cg, bg, (((1,), (1,)), ((), ())),
                             preferred_element_type=F32)
        pair_out = []
        for pr in range(SSD_GROUP_W // (2 * SSD_HEAD_DIM)):
            col0 = g * SSD_GROUP_W + pr * 2 * SSD_HEAD_DIM
            x_pair = xb[:, col0:col0 + 2 * SSD_HEAD_DIM]
            ys = []
            for j in range(2):
                hd = col0 // SSD_HEAD_DIM + j
                diff = cum[:, hd:hd + 1] - cum_t[hd:hd + 1, :]
                decay = jnp.exp(jnp.where(causal, diff, -jnp.inf))
                mat = (cb * decay).astype(BF16)
                ys.append(jnp.dot(mat, x_pair, preferred_element_type=F32))
            pair_out.append(jnp.where(first_half, ys[0], ys[1]))
        y_diag = jnp.concatenate(pair_out, axis=1)
        st = state_ref[g]
        y_off = jnp.dot(cg, st.astype(BF16), preferred_element_type=F32) * ea_x[:, gs]
        new = lax.dot_general(bg, xe[:, gs], (((0,), (0,)), ((), ())),
                              preferred_element_type=F32)
        state_ref[g] = st * ea_x[CHUNK - 1:CHUNK, gs] + new
        yg = y_diag + y_off + dskip_ref[:, gs] * xs[:, gs]
        yg = yg * _silu(z_ref[:, gs])
        ms = jnp.mean(yg * yg, axis=-1, keepdims=True)
        y_groups.append(yg * lax.rsqrt(ms + RMS_EPS) * normw_ref[:, gs])
    y_ref[...] = jnp.concatenate(y_groups, axis=1).astype(y_ref.dtype)


def _ssd(xbc, z, small, convw, convb, sbias, alog, dskip, normw, expand):
    nb = xbc.shape[0]
    row_spec = lambda w: pl.BlockSpec((None, CHUNK, w), lambda bi, ci: (bi, ci, 0))
    return pl.pallas_call(
        _ssd_body,
        grid=(nb, LP // CHUNK),
        in_specs=[row_spec(SSD_CONV_CH), row_spec(SSD_D_INNER), row_spec(SMALL_W),
                  _const_spec(convw.shape), _const_spec(convb.shape),
                  _const_spec(sbias.shape), _const_spec(alog.shape),
                  _const_spec(dskip.shape), _const_spec(normw.shape),
                  _const_spec(expand.shape)],
        out_specs=[row_spec(SSD_D_INNER), row_spec(SMALL_W)],
        out_shape=[jax.ShapeDtypeStruct((nb, LP, SSD_D_INNER), BF16),
                   jax.ShapeDtypeStruct((nb, LP, SMALL_W), F32)],
        scratch_shapes=[pltpu.VMEM((CHUNK + CONV_HALO, SSD_CONV_CH), F32),
                        pltpu.VMEM((SSD_GROUPS, SSD_STATE, SSD_GROUP_W), F32),
                        pltpu.VMEM((1, SMALL_W), F32)],
        compiler_params=pltpu.CompilerParams(
            dimension_semantics=("arbitrary", "arbitrary"), vmem_limit_bytes=VMEM_LIMIT),
        name="ssd",
    )(xbc, z, small, convw, convb, sbias, alog, dskip, normw, expand)


N_TK_CHUNKS = ATT_TK // LANES
N_TQ_CHUNKS = ATT_TQ // LANES
assert ATT_TQ == ATT_TK and PADL <= ATT_TK
LOG2E = math.log2(math.e)
Q_SCALE = LOG2E * HEAD_DIM ** -0.5


def _softmax_update(chunk, v_aug, m_ref, acc_ref, idx):
    m_old = m_ref[idx]
    part = chunk(0)
    for c in range(1, N_TK_CHUNKS):
        part = jnp.maximum(part, chunk(c))
    m_new = jnp.maximum(m_old, jnp.max(part, axis=-1, keepdims=True))
    alpha = jnp.exp2(m_old - m_new)
    p = jnp.concatenate([jnp.exp2(chunk(c) - m_new).astype(BF16)
                         for c in range(N_TK_CHUNKS)], axis=1)
    pv = jnp.dot(p, v_aug, preferred_element_type=F32)
    for w in range(acc_ref.shape[-1] // LANES):
        ws = slice(w * LANES, (w + 1) * LANES)
        acc_ref[idx, :, ws] = alpha * acc_ref[idx, :, ws] + pv[:, ws]
    m_ref[idx] = m_new


def _causal_chunk(c):
    qpos = lax.broadcasted_iota(jnp.int32, (ATT_TQ, LANES), 0)
    kpos = c * LANES + lax.broadcasted_iota(jnp.int32, (ATT_TQ, LANES), 1)
    return kpos <= qpos


def _not_padding_chunk(kb, c):
    kpos = c * LANES + lax.broadcasted_iota(jnp.int32, (ATT_TQ, LANES), 1)
    return (kpos >= PADL) | (kb > 0)


def _init_softmax_state(m_ref, acc_ref):
    m_ref[...] = jnp.full(m_ref.shape, NEG, F32)
    acc_ref[...] = jnp.zeros_like(acc_ref)


def _split_halves(q):
    lane = lax.broadcasted_iota(jnp.int32, q.shape, 1)
    first_half = lane < HEAD_DIM
    zero = jnp.zeros_like(q)
    return first_half, (jnp.where(first_half, q, zero), jnp.where(first_half, zero, q))


def _fox_body(q_ref, kt_ref, v_ref, ck_ref, o_ref, m_ref, acc_ref, sa_ref, sb_ref):
    qb = pl.program_id(2)
    first_half, q_heads = _split_halves(q_ref[...])
    _init_softmax_state(m_ref, acc_ref)

    def scores(kb, s_dst):
        start = pl.multiple_of(kb * ATT_TK, ATT_TK)
        kt = kt_ref[:, pl.ds(start, ATT_TK)]
        for j in range(2):
            s_dst[j] = jnp.dot(q_heads[j], kt, preferred_element_type=F32)

    def step(kb, s_src, s_next):
        start = pl.multiple_of(kb * ATT_TK, ATT_TK)
        if s_next is not None:
            scores(kb + 1, s_next)
        for j in range(2):
            def chunk(c, j=j):
                key0 = pl.multiple_of(start + c * LANES, LANES)
                sc = s_src[j, :, c * LANES:(c + 1) * LANES] - ck_ref[j:j + 1, pl.ds(key0, LANES)]
                if s_next is None:
                    sc = jnp.where(_causal_chunk(c), sc, NEG)
                return sc
            v_aug = v_ref[pl.ds(start, ATT_TK), j * LANES:(j + 1) * LANES]
            _softmax_update(chunk, v_aug, m_ref, acc_ref, j)

    odd = qb % 2

    @pl.when(odd == 0)
    def _():
        scores(0, sa_ref)

    @pl.when(odd == 1)
    def _():
        scores(0, sb_ref)
        step(0, sb_ref, sa_ref)

    def two_below_diagonal(i, carry):
        kb = odd + 2 * i
        step(kb, sa_ref, sb_ref)
        step(kb + 1, sb_ref, sa_ref)
        return carry

    lax.fori_loop(0, qb // 2, two_below_diagonal, 0)
    step(qb, sa_ref, None)

    acc0, acc1 = acc_ref[0], acc_ref[1]
    num = jnp.where(first_half, acc0, acc1)
    den = pltpu.roll(jnp.where(first_half, acc1, acc0), HEAD_DIM, axis=1)
    o_ref[...] = (num / den).astype(o_ref.dtype)


def _fox_attention(q, kt, v_aug, ck):
    nb = q.shape[0]
    pairs = FOX_HEADS // 2
    return pl.pallas_call(
        _fox_body,
        grid=(nb, pairs, LP // ATT_TQ),
        in_specs=[
            pl.BlockSpec((None, ATT_TQ, LANES), lambda bi, hp, qi: (bi, qi, hp)),
            pl.BlockSpec((None, LANES, LP), lambda bi, hp, qi: (bi, hp, 0)),
            pl.BlockSpec((None, LP, 2 * LANES), lambda bi, hp, qi: (bi, 0, hp)),
            pl.BlockSpec((None, None, 2, LP), lambda bi, hp, qi: (bi, hp, 0, 0)),
        ],
        out_specs=pl.BlockSpec((None, ATT_TQ, LANES), lambda bi, hp, qi: (bi, qi, hp)),
        out_shape=jax.ShapeDtypeStruct((nb, LP, FOX_WIDTH), BF16),
        scratch_shapes=[pltpu.VMEM((2, ATT_TQ, LANES), F32),
                        pltpu.VMEM((2, ATT_TQ, LANES), F32),
                        pltpu.VMEM((2, ATT_TQ, ATT_TK), F32),
                        pltpu.VMEM((2, ATT_TQ, ATT_TK), F32)],
        compiler_params=pltpu.CompilerParams(
            dimension_semantics=("parallel", "parallel", "arbitrary"),
            vmem_limit_bytes=VMEM_LIMIT),
        name="fox_attention",
    )(q, kt, v_aug, ck)


def _bias_tiles(bias_ref):
    rows = jnp.broadcast_to(bias_ref[...], (CHUNK, 2 * CHUNK))
    rolled = pltpu.roll(rows, 0, 1, stride=1, stride_axis=0)
    return rolled[:, :CHUNK], rolled[:, CHUNK:]


def _bias_chunk(tiles, key_chunk):
    picked = []
    for a in range(N_TQ_CHUNKS):
        sub = a - key_chunk
        picked.append(tiles[sub] if sub in (0, 1) else None)
    if all(t is None for t in picked):
        return None
    zeros = jnp.zeros((CHUNK, CHUNK), F32)
    return jnp.concatenate([zeros if t is None else t for t in picked], axis=0)


def _diff_body(lambda_init, q_ref, kt_ref, v_ref, bias_ref, lam_ref, subln_ref, o_ref,
               m_ref, acc_ref, sa_ref, sb_ref):
    qb = pl.program_id(2)
    _, q_parts = _split_halves(q_ref[...])
    _init_softmax_state(m_ref, acc_ref)

    def scores(kb, s_dst):
        start = pl.multiple_of(kb * ATT_TK, ATT_TK)
        kt = kt_ref[:, pl.ds(start, ATT_TK)]
        for j in range(2):
            s_dst[j] = jnp.dot(q_parts[j], kt, preferred_element_type=F32)

    def step(kb, s_src, s_next, first_key_chunk):
        start = pl.multiple_of(kb * ATT_TK, ATT_TK)
        diagonal = first_key_chunk == 0
        if not diagonal:
            scores(kb + 1, s_next)
        v_aug = v_ref[pl.ds(start, ATT_TK), :]
        biases = [None] * N_TK_CHUNKS
        if first_key_chunk is not None:
            tiles = _bias_tiles(bias_ref)
            biases = [_bias_chunk(tiles, first_key_chunk + c) for c in range(N_TK_CHUNKS)]
        for j in range(2):
            def chunk(c, j=j):
                sc = s_src[j, :, c * LANES:(c + 1) * LANES]
                if biases[c] is not None:
                    sc = sc + biases[c]
                if c * LANES < PADL:
                    sc = jnp.where(_not_padding_chunk(kb, c), sc, NEG)
                if diagonal:
                    sc = jnp.where(_causal_chunk(c), sc, NEG)
                return sc
            _softmax_update(chunk, v_aug, m_ref, acc_ref, j)

    n_far = jnp.maximum(qb - 1, 0)
    odd = n_far % 2

    @pl.when((odd == 1) | (qb == 0))
    def _():
        scores(0, sa_ref)

    @pl.when((odd == 0) & (qb > 0))
    def _():
        scores(0, sb_ref)

    @pl.when(odd == 1)
    def _():
        step(0, sa_ref, sb_ref, None)

    def two_far_below_diagonal(i, carry):
        kb = odd + 2 * i
        step(kb, sb_ref, sa_ref, None)
        step(kb + 1, sa_ref, sb_ref, None)
        return carry

    lax.fori_loop(0, n_far // 2, two_far_below_diagonal, 0)

    @pl.when(qb > 0)
    def _():
        step(qb - 1, sb_ref, sa_ref, -N_TK_CHUNKS)

    step(qb, sa_ref, None, 0)

    lam1 = jnp.exp(jnp.sum(lam_ref[0:1, :] * lam_ref[1:2, :], axis=-1, keepdims=True))
    lam2 = jnp.exp(jnp.sum(lam_ref[2:3, :] * lam_ref[3:4, :], axis=-1, keepdims=True))
    lam = lam1 - lam2 + lambda_init
    o = (acc_ref[0, :, :LANES] / acc_ref[0, :, LANES:]
         - lam * (acc_ref[1, :, :LANES] / acc_ref[1, :, LANES:]))
    ms = jnp.mean(o * o, axis=-1, keepdims=True)
    o = o * lax.rsqrt(ms + RMS_EPS) * subln_ref[...] * (1.0 - lambda_init)
    o_ref[...] = o.astype(o_ref.dtype)


def _diff_attention(q, kt, v_aug, bias_rows, lam_rows, subln, lambda_init):
    nb = q.shape[0]
    return pl.pallas_call(
        functools.partial(_diff_body, lambda_init),
        grid=(nb, DIFF_HEADS, LP // ATT_TQ),
        in_specs=[
            pl.BlockSpec((None, ATT_TQ, LANES), lambda bi, hd, qi: (bi, qi, hd)),
            pl.BlockSpec((None, LANES, LP), lambda bi, hd, qi: (bi, hd, 0)),
            pl.BlockSpec((None, LP, 2 * LANES), lambda bi, hd, qi: (bi, 0, hd)),
            pl.BlockSpec((None, 1, 2 * CHUNK), lambda bi, hd, qi: (hd, 0, 0)),
            _const_spec(lam_rows.shape),
            _const_spec(subln.shape),
        ],
        out_specs=pl.BlockSpec((None, ATT_TQ, LANES), lambda bi, hd, qi: (bi, qi, hd)),
        out_shape=jax.ShapeDtypeStruct((nb, LP, DIFF_HEADS * LANES), BF16),
        scratch_shapes=[pltpu.VMEM((2, ATT_TQ, LANES), F32),
                        pltpu.VMEM((2, ATT_TQ, 2 * LANES), F32),
                        pltpu.VMEM((2, ATT_TQ, ATT_TK), F32),
                        pltpu.VMEM((2, ATT_TQ, ATT_TK), F32)],
        compiler_params=pltpu.CompilerParams(
            dimension_semantics=("parallel", "parallel", "arbitrary"),
            vmem_limit_bytes=VMEM_LIMIT),
        name="diff_attention",
    )(q, kt, v_aug, bias_rows, lam_rows, subln)


def _t5_bucket(n):
    max_exact = N_BUCKETS // 2
    nf = jnp.maximum(n, 1).astype(F32)
    large = max_exact + (jnp.log(nf / max_exact) / math.log(128 / max_exact)
                         * (N_BUCKETS - max_exact)).astype(jnp.int32)
    large = jnp.minimum(large, N_BUCKETS - 1)
    return jnp.where(n < max_exact, n, large)


def _relative_bias_rows(rel_table):
    dist = jnp.arange(2 * CHUNK)
    by_dist = LOG2E * (rel_table[_t5_bucket(dist)] - rel_table[N_BUCKETS - 1])
    reversed_rows = jnp.concatenate([by_dist[:1], by_dist[:0:-1]], axis=0)
    return reversed_rows.T[:, None, :].astype(F32)


def _pad_lanes(vec, width=LANES):
    return jnp.pad(vec, (0, width - vec.shape[0]))[None, :].astype(F32)


def kernel(x, meta_tokens, ln_gain, ln_bias, ffn1_w_gate, ffn1_w_up, ffn1_w_down, ffn2_w_gate, ffn2_w_up, ffn2_w_down, even_w_in, even_conv_w, even_conv_b, ssd_dt_bias, ssd_a_log, ssd_d_skip, ssd_norm_w, fox_f_bias, even_w_out, diff_w_qkv, diff_lambda_q1, diff_lambda_k1, diff_lambda_q2, diff_lambda_k2, diff_subln_w, diff_w_o, rel_bias_table):
    nb = x.shape[0]
    meta = jnp.broadcast_to(meta_tokens[None].astype(x.dtype), (nb, N_META, D_MODEL))
    h = jnp.concatenate([jnp.zeros((nb, PADL, D_MODEL), x.dtype), meta, x], axis=1)
    hf = h.reshape(nb * LP, D_MODEL)

    def ln_params(l, i):
        return ln_gain[l, i][None, :], ln_bias[l, i][None, :]

    def ffn_weights(wg, wu, wd, l):
        return wg[l].astype(BF16), wu[l].astype(BF16), wd[l].astype(BF16)

    hf = _ffn_ln(hf, *ffn_weights(ffn1_w_gate, ffn1_w_up, ffn1_w_down, 0), *ln_params(0, 0))

    w_in = even_w_in[0]
    o_z, o_xbc = 0, SSD_D_INNER
    o_dt = o_xbc + SSD_CONV_CH
    o_q = o_dt + SSD_HEADS
    o_k, o_v = o_q + FOX_WIDTH, o_q + 2 * FOX_WIDTH
    o_f = o_q + 3 * FOX_WIDTH
    w_small = jnp.concatenate(
        [w_in[:, o_dt:o_q], w_in[:, o_f:],
         jnp.zeros((D_MODEL, SMALL_W - SSD_HEADS - FOX_HEADS), w_in.dtype)], axis=1)
    pairs = FOX_HEADS // 2
    wv4 = w_in[:, o_v:o_f].reshape(D_MODEL, pairs, 2, HEAD_DIM)
    wz = jnp.zeros((D_MODEL, pairs, HEAD_DIM), w_in.dtype)
    w_vaug = jnp.concatenate([wv4[:, :, 0], wz, wz, wv4[:, :, 1]], axis=-1)
    ones_fox = np.tile(np.repeat(np.array([0.0, 1.0, 1.0, 0.0], np.float32), HEAD_DIM), pairs)
    w_even = jnp.concatenate(
        [w_in[:, o_z:o_dt], w_in[:, o_q:o_k] * Q_SCALE, w_small,
         w_vaug.reshape(D_MODEL, 2 * FOX_WIDTH)], axis=1).astype(BF16)
    z, xbc, q, small, kt, v_aug = _proj(
        hf, w_even, w_in[:, o_k:o_v].T.astype(BF16), jnp.asarray(ones_fox)[None, :],
        (SSD_D_INNER, SSD_CONV_CH, FOX_WIDTH, SMALL_W), (F32, F32, BF16, F32), nb,
        "even_in_proj")

    sbias = _pad_lanes(jnp.concatenate([ssd_dt_bias[0], fox_f_bias[0]]))
    alog = _pad_lanes(ssd_a_log[0])
    dskip = jnp.repeat(ssd_d_skip[0], SSD_HEAD_DIM)[None, :].astype(F32)
    expand = np.zeros((SMALL_W, SSD_D_INNER), np.float32)
    expand[np.arange(SSD_D_INNER) // SSD_HEAD_DIM, np.arange(SSD_D_INNER)] = 1.0
    y, cfull = _ssd(xbc.reshape(nb, LP, SSD_CONV_CH), z.reshape(nb, LP, SSD_D_INNER),
                    small.reshape(nb, LP, SMALL_W), even_conv_w[0], even_conv_b[0][None, :],
                    sbias, alog, dskip, ssd_norm_w[0][None, :], jnp.asarray(expand, BF16))
    ck = LOG2E * cfull[:, :, SSD_HEADS:SSD_HEADS + FOX_HEADS].transpose(0, 2, 1)
    ck = jnp.where(jnp.arange(LP) < PADL, -NEG, ck)
    ck = ck.reshape(nb, FOX_HEADS // 2, 2, LP)
    o = _fox_attention(q.reshape(nb, LP, FOX_WIDTH), kt,
                       v_aug.reshape(nb, LP, FOX_HEADS * LANES), ck)
    w_out = even_w_out[0].astype(BF16)
    hf = _outproj_ln(hf, [y.reshape(nb * LP, SSD_D_INNER), o.reshape(nb * LP, FOX_WIDTH)],
                     [w_out[:SSD_D_INNER], w_out[SSD_D_INNER:]], *ln_params(0, 1),
                     name="even_out_proj_ln")
    hf = _ffn_ln(hf, *ffn_weights(ffn2_w_gate, ffn2_w_up, ffn2_w_down, 0), *ln_params(0, 2))

    hf = _ffn_ln(hf, *ffn_weights(ffn1_w_gate, ffn1_w_up, ffn1_w_down, 1), *ln_params(1, 0))
    qw = DIFF_HEADS * 2 * HEAD_DIM
    w_qkv = diff_w_qkv[0]
    wv3 = w_qkv[:, 2 * qw:].reshape(D_MODEL, DIFF_HEADS, LANES)
    w_q_vaug = jnp.concatenate(
        [w_qkv[:, :qw] * Q_SCALE,
         jnp.concatenate([wv3, jnp.zeros_like(wv3)], axis=-1).reshape(D_MODEL, 2 * qw)],
        axis=1).astype(BF16)
    ones_diff = np.tile(np.repeat(np.array([0.0, 1.0], np.float32), LANES), DIFF_HEADS)
    q, kt, v_aug = _proj(hf, w_q_vaug, w_qkv[:, qw:2 * qw].T.astype(BF16),
                         jnp.asarray(ones_diff)[None, :], (qw,), (BF16,), nb, "diff_qkv_proj")
    lambda_init = 0.8 - 0.6 * math.exp(-0.3 * 1)
    lam_rows = jnp.concatenate(
        [_pad_lanes(diff_lambda_q1[0]), _pad_lanes(diff_lambda_k1[0]),
         _pad_lanes(diff_lambda_q2[0]), _pad_lanes(diff_lambda_k2[0]),
         jnp.zeros((4, LANES), F32)], axis=0)
    o = _diff_attention(q.reshape(nb, LP, qw), kt,
                        v_aug.reshape(nb, LP, DIFF_HEADS * 2 * LANES),
                        _relative_bias_rows(rel_bias_table), lam_rows,
                        diff_subln_w[0][None, :], lambda_init)
    hf = _outproj_ln(hf, [o.reshape(nb * LP, DIFF_HEADS * LANES)], [diff_w_o[0].astype(BF16)],
                     *ln_params(1, 1), name="diff_out_proj_ln")
    return _ffn_ln_final(hf.reshape(nb, LP, D_MODEL),
                         *ffn_weights(ffn2_w_gate, ffn2_w_up, ffn2_w_down, 1), *ln_params(1, 2))
```

```python
import functools
import math

import numpy as np
import jax
import jax.numpy as jnp
from jax import lax
from jax.experimental import pallas as pl
from jax.experimental.pallas import tpu as pltpu

F32 = jnp.float32
BF16 = jnp.bfloat16

D_MODEL = 1024
SEQ = 8192
DEPTH = 2
N_META = 16
CHUNK = 128
SSD_D_INNER = 2048
SSD_HEAD_DIM = 64
SSD_HEADS = 32
SSD_GROUPS = 4
SSD_GROUP_W = SSD_D_INNER // SSD_GROUPS
SSD_STATE = 128
SSD_CONV = 4
SSD_CONV_CH = SSD_D_INNER + 2 * SSD_GROUPS * SSD_STATE
FOX_HEADS = 16
FOX_WIDTH = 1024
HEAD_DIM = 64
DIFF_HEADS = 8
N_BUCKETS = 32
D_FF = 2816
ALPHA = (2 * DEPTH) ** 0.25
LN_EPS = 1e-5
RMS_EPS = 1e-5
NEG = -1e30

LANES = 128
LP = 8448
PADL = LP - SEQ - N_META
ROW_TILE = 512
OUT_TILE = 256
ATT_TQ = 768
ATT_TK = 768
CONV_HALO = 8
SMALL_W = LANES
VMEM_LIMIT = 56 * 1024 * 1024

assert PADL % CHUNK == CHUNK - N_META
assert LP % ATT_TQ == 0 and ATT_TQ % ATT_TK == 0 and ATT_TK % LANES == 0
assert LP % CHUNK == 0 and (2 * LP) % ROW_TILE == 0
assert PADL <= OUT_TILE and (LP - OUT_TILE) == SEQ


def _const_spec(shape):
    nd = len(shape)
    return pl.BlockSpec(shape, lambda *_: (0,) * nd, pipeline_mode=pl.Buffered(1))


def _layer_norm(r, g, b):
    mu = jnp.mean(r, axis=-1, keepdims=True)
    d = r - mu
    var = jnp.mean(d * d, axis=-1, keepdims=True)
    return d * lax.rsqrt(var + LN_EPS) * g + b


def _silu(x):
    return x / (1.0 + jnp.exp(-x))


def _softplus(x):
    return jnp.maximum(x, 0.0) + jnp.log(1.0 + jnp.exp(-jnp.abs(x)))


def _ffn_ln_body(h_ref, wg_ref, wu_ref, wd_ref, g_ref, b_ref, o_ref):
    h = h_ref[...]
    hb = h.astype(BF16)
    g = jnp.dot(hb, wg_ref[...], preferred_element_type=F32)
    u = jnp.dot(hb, wu_ref[...], preferred_element_type=F32)
    a = (_silu(g) * u).astype(BF16)
    y = jnp.dot(a, wd_ref[...], preferred_element_type=F32)
    o_ref[...] = _layer_norm(ALPHA * h + 0.5 * y, g_ref[...], b_ref[...])


def _ffn_ln(hf, wg, wu, wd, g, b):
    rows = hf.shape[0]
    return pl.pallas_call(
        _ffn_ln_body,
        grid=(rows // ROW_TILE,),
        in_specs=[
            pl.BlockSpec((ROW_TILE, D_MODEL), lambda i: (i, 0)),
            _const_spec((D_MODEL, D_FF)),
            _const_spec((D_MODEL, D_FF)),
            _const_spec((D_FF, D_MODEL)),
            _const_spec((1, D_MODEL)),
            _const_spec((1, D_MODEL)),
        ],
        out_specs=pl.BlockSpec((ROW_TILE, D_MODEL), lambda i: (i, 0)),
        out_shape=jax.ShapeDtypeStruct((rows, D_MODEL), F32),
        compiler_params=pltpu.CompilerParams(
            dimension_semantics=("parallel",), vmem_limit_bytes=VMEM_LIMIT),
        name="ffn_ln",
    )(hf, wg, wu, wd, g, b)


def _ffn_ln_final_body(h_ref, wg_ref, wu_ref, wd_ref, g_ref, b_ref, o_ref):
    @pl.when(pl.program_id(1) > 0)
    def _():
        _ffn_ln_body(h_ref, wg_ref, wu_ref, wd_ref, g_ref, b_ref, o_ref)


def _ffn_ln_final(h3, wg, wu, wd, g, b):
    nb = h3.shape[0]
    return pl.pallas_call(
        _ffn_ln_final_body,
        grid=(nb, LP // OUT_TILE),
        in_specs=[
            pl.BlockSpec((None, OUT_TILE, D_MODEL), lambda bi, i: (bi, i, 0)),
            _const_spec((D_MODEL, D_FF)),
            _const_spec((D_MODEL, D_FF)),
            _const_spec((D_FF, D_MODEL)),
            _const_spec((1, D_MODEL)),
            _const_spec((1, D_MODEL)),
        ],
        out_specs=pl.BlockSpec((None, OUT_TILE, D_MODEL),
                               lambda bi, i: (bi, jnp.maximum(i - 1, 0), 0)),
        out_shape=jax.ShapeDtypeStruct((nb, SEQ, D_MODEL), F32),
        compiler_params=pltpu.CompilerParams(
            dimension_semantics=("arbitrary", "arbitrary"), vmem_limit_bytes=VMEM_LIMIT),
        name="ffn_ln_final",
    )(h3, wg, wu, wd, g, b)


PROJ_TILE = 256
assert LP % PROJ_TILE == 0


def _proj_body(h_ref, w_ref, wkt_ref, vones_ref, *o_refs):
    hb = h_ref[...].astype(BF16)
    off = 0
    for o_ref in o_refs[:-2]:
        n = o_ref.shape[-1]
        o_ref[...] = jnp.dot(hb, w_ref[:, off:off + n],
                             preferred_element_type=F32).astype(o_ref.dtype)
        off += n
    kt_ref, va_ref = o_refs[-2:]
    kt_ref[...] = lax.dot_general(wkt_ref[...], hb, (((1,), (1,)), ((), ())),
                                  preferred_element_type=F32).astype(kt_ref.dtype)
    va_ref[...] = (jnp.dot(hb, w_ref[:, off:], preferred_element_type=F32)
                   + vones_ref[...]).astype(va_ref.dtype)


def _proj(hf, w, wkt, vones, widths, dtypes, nb, name):
    rows = hf.shape[0]
    tiles_per_batch = LP // PROJ_TILE
    k_width = wkt.shape[0]
    va_width = vones.shape[1]
    out_specs = [pl.BlockSpec((PROJ_TILE, n), lambda i: (i, 0)) for n in widths]
    out_specs += [
        pl.BlockSpec((None, k_width, PROJ_TILE),
                     lambda i: (i // tiles_per_batch, 0, i % tiles_per_batch)),
        pl.BlockSpec((PROJ_TILE, va_width), lambda i: (i, 0)),
    ]
    out_shape = [jax.ShapeDtypeStruct((rows, n), dt) for n, dt in zip(widths, dtypes)]
    out_shape += [jax.ShapeDtypeStruct((nb, k_width, LP), BF16),
                  jax.ShapeDtypeStruct((rows, va_width), BF16)]
    return pl.pallas_call(
        _proj_body,
        grid=(rows // PROJ_TILE,),
        in_specs=[pl.BlockSpec((PROJ_TILE, D_MODEL), lambda i: (i, 0)),
                  _const_spec(w.shape), _const_spec(wkt.shape), _const_spec(vones.shape)],
        out_specs=out_specs,
        out_shape=out_shape,
        compiler_params=pltpu.CompilerParams(
            dimension_semantics=("parallel",), vmem_limit_bytes=VMEM_LIMIT),
        name=name,
    )(hf, w, wkt, vones)


def _outproj_ln_body(n_in, h_ref, *refs):
    a_refs = refs[:n_in]
    w_refs = refs[n_in:2 * n_in]
    g_ref, b_ref, o_ref = refs[2 * n_in:]
    m = jnp.dot(a_refs[0][...], w_refs[0][...], preferred_element_type=F32)
    for a_ref, w_ref in zip(a_refs[1:], w_refs[1:]):
        m = m + jnp.dot(a_ref[...], w_ref[...], preferred_element_type=F32)
    o_ref[...] = _layer_norm(ALPHA * h_ref[...] + m, g_ref[...], b_ref[...])


def _outproj_ln(hf, acts, ws, g, b, name):
    rows = hf.shape[0]
    n_in = len(acts)
    in_specs = [pl.BlockSpec((ROW_TILE, D_MODEL), lambda i: (i, 0))]
    in_specs += [pl.BlockSpec((ROW_TILE, a.shape[1]), lambda i: (i, 0)) for a in acts]
    in_specs += [_const_spec(w.shape) for w in ws]
    in_specs += [_const_spec((1, D_MODEL)), _const_spec((1, D_MODEL))]
    return pl.pallas_call(
        functools.partial(_outproj_ln_body, n_in),
        grid=(rows // ROW_TILE,),
        in_specs=in_specs,
        out_specs=pl.BlockSpec((ROW_TILE, D_MODEL), lambda i: (i, 0)),
        out_shape=jax.ShapeDtypeStruct((rows, D_MODEL), F32),
        compiler_params=pltpu.CompilerParams(
            dimension_semantics=("parallel",), vmem_limit_bytes=VMEM_LIMIT),
        name=name,
    )(hf, *acts, *ws, g, b)


def _split_dot(x, e_ref):
    hi = x.astype(BF16)
    lo = (x - hi.astype(F32)).astype(BF16)
    e = e_ref[...]
    return (jnp.dot(hi, e, preferred_element_type=F32)
            + jnp.dot(lo, e, preferred_element_type=F32))


def _ssd_body(xbc_ref, z_ref, small_ref, convw_ref, convb_ref, sbias_ref, alog_ref,
              dskip_ref, normw_ref, expand_ref, y_ref, c_ref,
              ext_ref, state_ref, carry_ref):
    c = pl.program_id(1)

    @pl.when(c == 0)
    def _():
        ext_ref[0:CONV_HALO, :] = jnp.zeros((CONV_HALO, SSD_CONV_CH), F32)
        state_ref[...] = jnp.zeros_like(state_ref)
        carry_ref[...] = jnp.zeros_like(carry_ref)

    row = lax.broadcasted_iota(jnp.int32, (CHUNK, 1), 0)
    valid = (c * CHUNK + row) >= PADL

    ext_ref[CONV_HALO:CONV_HALO + CHUNK, :] = jnp.where(valid, xbc_ref[...], 0.0)
    conv = convb_ref[...]
    for k in range(SSD_CONV):
        start = CONV_HALO - (SSD_CONV - 1) + k
        conv = conv + convw_ref[k:k + 1, :] * ext_ref[start:start + CHUNK, :]
    ext_ref[0:CONV_HALO, :] = ext_ref[CHUNK:CHUNK + CONV_HALO, :]
    xc = jnp.where(valid, _silu(conv), 0.0)
    xs = xc[:, :SSD_D_INNER]
    bm = xc[:, SSD_D_INNER:SSD_D_INNER + SSD_GROUPS * SSD_STATE].astype(BF16)
    cm = xc[:, SSD_D_INNER + SSD_GROUPS * SSD_STATE:].astype(BF16)

    lane = lax.broadcasted_iota(jnp.int32, (CHUNK, SMALL_W), 1)
    is_dt = lane < SSD_HEADS
    is_f = (lane >= SSD_HEADS) & (lane < SSD_HEADS + FOX_HEADS)
    v = small_ref[...] + sbias_ref[...]
    dt = jnp.where(valid & is_dt, _softplus(v), 0.0)
    log_f = jnp.where(valid & is_f, -_softplus(-v), 0.0)
    neg_a = -jnp.exp(alog_ref[...])
    steps = jnp.where(is_dt, dt * neg_a, log_f)
    r_i = lax.broadcasted_iota(jnp.int32, (CHUNK, CHUNK), 0)
    c_i = lax.broadcasted_iota(jnp.int32, (CHUNK, CHUNK), 1)
    causal = r_i >= c_i
    tril = jnp.where(causal, 1.0, 0.0).astype(F32)
    cum = jnp.dot(tril, steps, preferred_element_type=F32,
                  precision=lax.Precision.HIGHEST)
    c_total = jnp.where(is_f, cum + carry_ref[...], 0.0)
    c_ref[...] = c_total
    carry_ref[...] = c_total[CHUNK - 1:CHUNK, :]

    a_last = cum[CHUNK - 1:CHUNK, :]
    dt_x = _split_dot(dt, expand_ref)
    ea_x = _split_dot(jnp.exp(cum), expand_ref)
    de_x = _split_dot(jnp.exp(a_last - cum), expand_ref)
    x_dt = xs * dt_x
    xb = x_dt.astype(BF16)
    xe = (x_dt * de_x).astype(BF16)
    cum_t = cum.T
    lane_p = lax.broadcasted_iota(jnp.int32, (CHUNK, 2 * SSD_HEAD_DIM), 1)
    first_half = lane_p < SSD_HEAD_DIM

    y_groups = []
    for g in range(SSD_GROUPS):
        gs = slice(g * SSD_GROUP_W, (g + 1) * SSD_GROUP_W)
        bg = bm[:, g * SSD_STATE:(g + 1) * SSD_STATE]
        cg = cm[:, g * SSD_STATE:(g + 1) * SSD_STATE]
        cb = lax.dot_general(cg, bg, (((1,), (1,)), ((), ())),
                             preferred_element_type=F32)
        pair_out = []
        for pr in range(SSD_GROUP_W // (2 * SSD_HEAD_DIM)):
            col0 = g * SSD_GROUP_W + pr * 2 * SSD_HEAD_DIM
            x_pair = xb[:, col0:col0 + 2 * SSD_HEAD_DIM]
            ys = []
            for j in range(2):
                hd = col0 // SSD_HEAD_DIM + j
                diff = cum[:, hd:hd + 1] - cum_t[hd:hd + 1, :]
                decay = jnp.exp(jnp.where(causal, diff, -jnp.inf))
                mat = (cb * decay).astype(BF16)
                ys.append(jnp.dot(mat, x_pair, preferred_element_type=F32))
            pair_out.append(jnp.where(first_half, ys[0], ys[1]))
        y_diag = jnp.concatenate(pair_out, axis=1)
        st = state_ref[g]
        y_off = jnp.dot(cg, st.astype(BF16), preferred_element_type=F32) * ea_x[:, gs]
        new = lax.dot_general(bg, xe[:, gs], (((0,), (0,)), ((), ())),
                              preferred_element_type=F32)
        state_ref[g] = st * ea_x[CHUNK - 1:CHUNK, gs] + new
        yg = y_diag + y_off + dskip_ref[:, gs] * xs[:, gs]
        yg = yg * _silu(z_ref[:, gs])
        ms = jnp.mean(yg * yg, axis=-1, keepdims=True)
        y_groups.append(yg * lax.rsqrt(ms + RMS_EPS) * normw_ref[:, gs])
    y_ref[...] = jnp.concatenate(y_groups, axis=1).astype(y_ref.dtype)


def _ssd(xbc, z, small, convw, convb, sbias, alog, dskip, normw, expand):
    nb = xbc.shape[0]
    row_spec = lambda w: pl.BlockSpec((None, CHUNK, w), lambda bi, ci: (bi, ci, 0))
    return pl.pallas_call(
        _ssd_body,
        grid=(nb, LP // CHUNK),
        in_specs=[row_spec(SSD_CONV_CH), row_spec(SSD_D_INNER), row_spec(SMALL_W),
                  _const_spec(convw.shape), _const_spec(convb.shape),
                  _const_spec(sbias.shape), _const_spec(alog.shape),
                  _const_spec(dskip.shape), _const_spec(normw.shape),
                  _const_spec(expand.shape)],
        out_specs=[row_spec(SSD_D_INNER), row_spec(SMALL_W)],
        out_shape=[jax.ShapeDtypeStruct((nb, LP, SSD_D_INNER), BF16),
                   jax.ShapeDtypeStruct((nb, LP, SMALL_W), F32)],
        scratch_shapes=[pltpu.VMEM((CHUNK + CONV_HALO, SSD_CONV_CH), F32),
                        pltpu.VMEM((SSD_GROUPS, SSD_STATE, SSD_GROUP_W), F32),
                        pltpu.VMEM((1, SMALL_W), F32)],
        compiler_params=pltpu.CompilerParams(
            dimension_semantics=("arbitrary", "arbitrary"), vmem_limit_bytes=VMEM_LIMIT),
        name="ssd",
    )(xbc, z, small, convw, convb, sbias, alog, dskip, normw, expand)


N_TK_CHUNKS = ATT_TK // LANES
N_TQ_CHUNKS = ATT_TQ // LANES
assert ATT_TQ == ATT_TK and PADL <= ATT_TK
LOG2E = math.log2(math.e)
Q_SCALE = LOG2E * HEAD_DIM ** -0.5


def _chunk_max(chunk):
    part = chunk(0)
    for c in range(1, N_TK_CHUNKS):
        part = jnp.maximum(part, chunk(c))
    return part


def _softmax_update(chunk, part, v_aug, m_ref, acc_ref, idx):
    m_old = m_ref[idx]
    m_new = jnp.maximum(m_old, jnp.max(part, axis=-1, keepdims=True))
    alpha = jnp.exp2(m_old - m_new)
    p = jnp.concatenate([jnp.exp2(chunk(c) - m_new).astype(BF16)
                         for c in range(N_TK_CHUNKS)], axis=1)
    pv = jnp.dot(p, v_aug, preferred_element_type=F32)
    for w in range(acc_ref.shape[-1] // LANES):
        ws = slice(w * LANES, (w + 1) * LANES)
        acc_ref[idx, :, ws] = alpha * acc_ref[idx, :, ws] + pv[:, ws]
    m_ref[idx] = m_new


def _causal_chunk(c):
    qpos = lax.broadcasted_iota(jnp.int32, (ATT_TQ, LANES), 0)
    kpos = c * LANES + lax.broadcasted_iota(jnp.int32, (ATT_TQ, LANES), 1)
    return kpos <= qpos


def _not_padding_chunk(kb, c):
    kpos = c * LANES + lax.broadcasted_iota(jnp.int32, (ATT_TQ, LANES), 1)
    return (kpos >= PADL) | (kb > 0)


def _init_softmax_state(m_ref, acc_ref):
    m_ref[...] = jnp.full(m_ref.shape, NEG, F32)
    acc_ref[...] = jnp.zeros_like(acc_ref)


def _split_halves(q):
    lane = lax.broadcasted_iota(jnp.int32, q.shape, 1)
    first_half = lane < HEAD_DIM
    zero = jnp.zeros_like(q)
    return first_half, (jnp.where(first_half, q, zero), jnp.where(first_half, zero, q))


def _fox_body(q_ref, kt_ref, v_ref, ck_ref, o_ref, m_ref, acc_ref,
              sa_ref, pa_ref, sb_ref, pb_ref):
    qb = pl.program_id(2)
    first_half, q_heads = _split_halves(q_ref[...])
    _init_softmax_state(m_ref, acc_ref)
    buf_a, buf_b = (sa_ref, pa_ref), (sb_ref, pb_ref)

    def scores(kb, buf):
        s_dst, part_dst = buf
        start = pl.multiple_of(kb * ATT_TK, ATT_TK)
        kt = kt_ref[:, pl.ds(start, ATT_TK)]
        for j in range(2):
            s = jnp.dot(q_heads[j], kt, preferred_element_type=F32)
            part = None
            for c in range(N_TK_CHUNKS):
                key0 = pl.multiple_of(start + c * LANES, LANES)
                sc = s[:, c * LANES:(c + 1) * LANES] - ck_ref[j:j + 1, pl.ds(key0, LANES)]
                s_dst[j, :, c * LANES:(c + 1) * LANES] = sc
                part = sc if part is None else jnp.maximum(part, sc)
            part_dst[j] = part

    def step(kb, src, nxt):
        s_src, part_src = src
        start = pl.multiple_of(kb * ATT_TK, ATT_TK)
        if nxt is not None:
            scores(kb + 1, nxt)
        for j in range(2):
            def chunk(c, j=j):
                sc = s_src[j, :, c * LANES:(c + 1) * LANES]
                if nxt is None:
                    sc = jnp.where(_causal_chunk(c), sc, NEG)
                return sc
            part = _chunk_max(chunk) if nxt is None else part_src[j]
            v_aug = v_ref[pl.ds(start, ATT_TK), j * LANES:(j + 1) * LANES]
            _softmax_update(chunk, part, v_aug, m_ref, acc_ref, j)

    odd = qb % 2

    @pl.when(odd == 0)
    def _():
        scores(0, buf_a)

    @pl.when(odd == 1)
    def _():
        scores(0, buf_b)
        step(0, buf_b, buf_a)

    def two_below_diagonal(i, carry):
        kb = odd + 2 * i
        step(kb, buf_a, buf_b)
        step(kb + 1, buf_b, buf_a)
        return carry

    lax.fori_loop(0, qb // 2, two_below_diagonal, 0)
    step(qb, buf_a, None)

    acc0, acc1 = acc_ref[0], acc_ref[1]
    num = jnp.where(first_half, acc0, acc1)
    den = pltpu.roll(jnp.where(first_half, acc1, acc0), HEAD_DIM, axis=1)
    o_ref[...] = (num / den).astype(o_ref.dtype)


def _fox_attention(q, kt, v_aug, ck):
    nb = q.shape[0]
    pairs = FOX_HEADS // 2
    return pl.pallas_call(
        _fox_body,
        grid=(nb, pairs, LP // ATT_TQ),
        in_specs=[
            pl.BlockSpec((None, ATT_TQ, LANES), lambda bi, hp, qi: (bi, qi, hp)),
            pl.BlockSpec((None, LANES, LP), lambda bi, hp, qi: (bi, hp, 0)),
            pl.BlockSpec((None, LP, 2 * LANES), lambda bi, hp, qi: (bi, 0, hp)),
            pl.BlockSpec((None, None, 2, LP), lambda bi, hp, qi: (bi, hp, 0, 0)),
        ],
        out_specs=pl.BlockSpec((None, ATT_TQ, LANES), lambda bi, hp, qi: (bi, qi, hp)),
        out_shape=jax.ShapeDtypeStruct((nb, LP, FOX_WIDTH), BF16),
        scratch_shapes=[pltpu.VMEM((2, ATT_TQ, LANES), F32),
                        pltpu.VMEM((2, ATT_TQ, LANES), F32),
                        pltpu.VMEM((2, ATT_TQ, ATT_TK), F32),
                        pltpu.VMEM((2, ATT_TQ, LANES), F32),
                        pltpu.VMEM((2, ATT_TQ, ATT_TK), F32),
                        pltpu.VMEM((2, ATT_TQ, LANES), F32)],
        compiler_params=pltpu.CompilerParams(
            dimension_semantics=("parallel", "parallel", "arbitrary"),
            vmem_limit_bytes=VMEM_LIMIT),
        name="fox_attention",
    )(q, kt, v_aug, ck)


def _bias_tiles(bias_ref):
    rows = jnp.broadcast_to(bias_ref[...], (CHUNK, 2 * CHUNK))
    rolled = pltpu.roll(rows, 0, 1, stride=1, stride_axis=0)
    return rolled[:, :CHUNK], rolled[:, CHUNK:]


def _bias_chunk(tiles, key_chunk):
    picked = []
    for a in range(N_TQ_CHUNKS):
        sub = a - key_chunk
        picked.append(tiles[sub] if sub in (0, 1) else None)
    if all(t is None for t in picked):
        return None
    zeros = jnp.zeros((CHUNK, CHUNK), F32)
    return jnp.concatenate([zeros if t is None else t for t in picked], axis=0)


def _diff_body(lambda_init, q_ref, kt_ref, v_ref, bias_ref, lam_ref, subln_ref, o_ref,
               m_ref, acc_ref, sa_ref, pa_ref, sb_ref, pb_ref):
    qb = pl.program_id(2)
    _, q_parts = _split_halves(q_ref[...])
    _init_softmax_state(m_ref, acc_ref)
    buf_a, buf_b = (sa_ref, pa_ref), (sb_ref, pb_ref)

    def scores(kb, buf):
        s_dst, part_dst = buf
        start = pl.multiple_of(kb * ATT_TK, ATT_TK)
        kt = kt_ref[:, pl.ds(start, ATT_TK)]
        for j in range(2):
            s = jnp.dot(q_parts[j], kt, preferred_element_type=F32)
            part = None
            for c in range(N_TK_CHUNKS):
                sc = s[:, c * LANES:(c + 1) * LANES]
                if c * LANES < PADL:
                    sc = jnp.where(_not_padding_chunk(kb, c), sc, NEG)
                s_dst[j, :, c * LANES:(c + 1) * LANES] = sc
                part = sc if part is None else jnp.maximum(part, sc)
            part_dst[j] = part

    def step(kb, src, nxt, first_key_chunk):
        s_src, part_src = src
        start = pl.multiple_of(kb * ATT_TK, ATT_TK)
        diagonal = first_key_chunk == 0
        if not diagonal:
            scores(kb + 1, nxt)
        v_aug = v_ref[pl.ds(start, ATT_TK), :]
        biases = [None] * N_TK_CHUNKS
        if first_key_chunk is not None:
            tiles = _bias_tiles(bias_ref)
            biases = [_bias_chunk(tiles, first_key_chunk + c) for c in range(N_TK_CHUNKS)]
        for j in range(2):
            def chunk(c, j=j):
                sc = s_src[j, :, c * LANES:(c + 1) * LANES]
                if biases[c] is not None:
                    sc = sc + biases[c]
                if diagonal:
                    sc = jnp.where(_causal_chunk(c), sc, NEG)
                return sc
            part = part_src[j] if first_key_chunk is None else _chunk_max(chunk)
            _softmax_update(chunk, part, v_aug, m_ref, acc_ref, j)

    n_far = jnp.maximum(qb - 1, 0)
    odd = n_far % 2

    @pl.when((odd == 1) | (qb == 0))
    def _():
        scores(0, buf_a)

    @pl.when((odd == 0) & (qb > 0))
    def _():
        scores(0, buf_b)

    @pl.when(odd == 1)
    def _():
        step(0, buf_a, buf_b, None)

    def two_far_below_diagonal(i, carry):
        kb = odd + 2 * i
        step(kb, buf_b, buf_a, None)
        step(kb + 1, buf_a, buf_b, None)
        return carry

    lax.fori_loop(0, n_far // 2, two_far_below_diagonal, 0)

    @pl.when(qb > 0)
    def _():
        step(qb - 1, buf_b, buf_a, -N_TK_CHUNKS)

    step(qb, buf_a, None, 0)

    lam1 = jnp.exp(jnp.sum(lam_ref[0:1, :] * lam_ref[1:2, :], axis=-1, keepdims=True))
    lam2 = jnp.exp(jnp.sum(lam_ref[2:3, :] * lam_ref[3:4, :], axis=-1, keepdims=True))
    lam = lam1 - lam2 + lambda_init
    o = (acc_ref[0, :, :LANES] / acc_ref[0, :, LANES:]
         - lam * (acc_ref[1, :, :LANES] / acc_ref[1, :, LANES:]))
    ms = jnp.mean(o * o, axis=-1, keepdims=True)
    o = o * lax.rsqrt(ms + RMS_EPS) * subln_ref[...] * (1.0 - lambda_init)
    o_ref[...] = o.astype(o_ref.dtype)


def _diff_attention(q, kt, v_aug, bias_rows, lam_rows, subln, lambda_init):
    nb = q.shape[0]
    return pl.pallas_call(
        functools.partial(_diff_body, lambda_init),
        grid=(nb, DIFF_HEADS, LP // ATT_TQ),
        in_specs=[
            pl.BlockSpec((None, ATT_TQ, LANES), lambda bi, hd, qi: (bi, qi, hd)),
            pl.BlockSpec((None, LANES, LP), lambda bi, hd, qi: (bi, hd, 0)),
            pl.BlockSpec((None, LP, 2 * LANES), lambda bi, hd, qi: (bi, 0, hd)),
            pl.BlockSpec((None, 1, 2 * CHUNK), lambda bi, hd, qi: (hd, 0, 0)),
            _const_spec(lam_rows.shape),
            _const_spec(subln.shape),
        ],
        out_specs=pl.BlockSpec((None, ATT_TQ, LANES), lambda bi, hd, qi: (bi, qi, hd)),
        out_shape=jax.ShapeDtypeStruct((nb, LP, DIFF_HEADS * LANES), BF16),
        scratch_shapes=[pltpu.VMEM((2, ATT_TQ, LANES), F32),
                        pltpu.VMEM((2, ATT_TQ, 2 * LANES), F32),
                        pltpu.VMEM((2, ATT_TQ, ATT_TK), F32),
                        pltpu.VMEM((2, ATT_TQ, LANES), F32),
                        pltpu.VMEM((2, ATT_TQ, ATT_TK), F32),
                        pltpu.VMEM((2, ATT_TQ, LANES), F32)],
        compiler_params=pltpu.CompilerParams(
            dimension_semantics=("parallel", "parallel", "arbitrary"),
            vmem_limit_bytes=VMEM_LIMIT),
        name="diff_attention",
    )(q, kt, v_aug, bias_rows, lam_rows, subln)


def _t5_bucket(n):
    max_exact = N_BUCKETS // 2
    nf = jnp.maximum(n, 1).astype(F32)
    large = max_exact + (jnp.log(nf / max_exact) / math.log(128 / max_exact)
                         * (N_BUCKETS - max_exact)).astype(jnp.int32)
    large = jnp.minimum(large, N_BUCKETS - 1)
    return jnp.where(n < max_exact, n, large)


def _relative_bias_rows(rel_table):
    dist = jnp.arange(2 * CHUNK)
    by_dist = LOG2E * (rel_table[_t5_bucket(dist)] - rel_table[N_BUCKETS - 1])
    reversed_rows = jnp.concatenate([by_dist[:1], by_dist[:0:-1]], axis=0)
    return reversed_rows.T[:, None, :].astype(F32)


def _pad_lanes(vec, width=LANES):
    return jnp.pad(vec, (0, width - vec.shape[0]))[None, :].astype(F32)


def kernel(x, meta_tokens, ln_gain, ln_bias, ffn1_w_gate, ffn1_w_up, ffn1_w_down, ffn2_w_gate, ffn2_w_up, ffn2_w_down, even_w_in, even_conv_w, even_conv_b, ssd_dt_bias, ssd_a_log, ssd_d_skip, ssd_norm_w, fox_f_bias, even_w_out, diff_w_qkv, diff_lambda_q1, diff_lambda_k1, diff_lambda_q2, diff_lambda_k2, diff_subln_w, diff_w_o, rel_bias_table):
    nb = x.shape[0]
    meta = jnp.broadcast_to(meta_tokens[None].astype(x.dtype), (nb, N_META, D_MODEL))
    h = jnp.concatenate([jnp.zeros((nb, PADL, D_MODEL), x.dtype), meta, x], axis=1)
    hf = h.reshape(nb * LP, D_MODEL)

    def ln_params(l, i):
        return ln_gain[l, i][None, :], ln_bias[l, i][None, :]

    def ffn_weights(wg, wu, wd, l):
        return wg[l].astype(BF16), wu[l].astype(BF16), wd[l].astype(BF16)

    hf = _ffn_ln(hf, *ffn_weights(ffn1_w_gate, ffn1_w_up, ffn1_w_down, 0), *ln_params(0, 0))

    w_in = even_w_in[0]
    o_z, o_xbc = 0, SSD_D_INNER
    o_dt = o_xbc + SSD_CONV_CH
    o_q = o_dt + SSD_HEADS
    o_k, o_v = o_q + FOX_WIDTH, o_q + 2 * FOX_WIDTH
    o_f = o_q + 3 * FOX_WIDTH
    w_small = jnp.concatenate(
        [w_in[:, o_dt:o_q], w_in[:, o_f:],
         jnp.zeros((D_MODEL, SMALL_W - SSD_HEADS - FOX_HEADS), w_in.dtype)], axis=1)
    pairs = FOX_HEADS // 2
    wv4 = w_in[:, o_v:o_f].reshape(D_MODEL, pairs, 2, HEAD_DIM)
    wz = jnp.zeros((D_MODEL, pairs, HEAD_DIM), w_in.dtype)
    w_vaug = jnp.concatenate([wv4[:, :, 0], wz, wz, wv4[:, :, 1]], axis=-1)
    ones_fox = np.tile(np.repeat(np.array([0.0, 1.0, 1.0, 0.0], np.float32), HEAD_DIM), pairs)
    w_even = jnp.concatenate(
        [w_in[:, o_z:o_dt], w_in[:, o_q:o_k] * Q_SCALE, w_small,
         w_vaug.reshape(D_MODEL, 2 * FOX_WIDTH)], axis=1).astype(BF16)
    z, xbc, q, small, kt, v_aug = _proj(
        hf, w_even, w_in[:, o_k:o_v].T.astype(BF16), jnp.asarray(ones_fox)[None, :],
        (SSD_D_INNER, SSD_CONV_CH, FOX_WIDTH, SMALL_W), (F32, F32, BF16, F32), nb,
        "even_in_proj")

    sbias = _pad_lanes(jnp.concatenate([ssd_dt_bias[0], fox_f_bias[0]]))
    alog = _pad_lanes(ssd_a_log[0])
    dskip = jnp.repeat(ssd_d_skip[0], SSD_HEAD_DIM)[None, :].astype(F32)
    expand = np.zeros((SMALL_W, SSD_D_INNER), np.float32)
    expand[np.arange(SSD_D_INNER) // SSD_HEAD_DIM, np.arange(SSD_D_INNER)] = 1.0
    y, cfull = _ssd(xbc.reshape(nb, LP, SSD_CONV_CH), z.reshape(nb, LP, SSD_D_INNER),
                    small.reshape(nb, LP, SMALL_W), even_conv_w[0], even_conv_b[0][None, :],
                    sbias, alog, dskip, ssd_norm_w[0][None, :], jnp.asarray(expand, BF16))
    ck = LOG2E * cfull[:, :, SSD_HEADS:SSD_HEADS + FOX_HEADS].transpose(0, 2, 1)
    ck = jnp.where(jnp.arange(LP) < PADL, -NEG, ck)
    ck = ck.reshape(nb, FOX_HEADS // 2, 2, LP)
    o = _fox_attention(q.reshape(nb, LP, FOX_WIDTH), kt,
                       v_aug.reshape(nb, LP, FOX_HEADS * LANES), ck)
    w_out = even_w_out[0].astype(BF16)
    hf = _outproj_ln(hf, [y.reshape(nb * LP, SSD_D_INNER), o.reshape(nb * LP, FOX_WIDTH)],
                     [w_out[:SSD_D_INNER], w_out[SSD_D_INNER:]], *ln_params(0, 1),
                     name="even_out_proj_ln")
    hf = _ffn_ln(hf, *ffn_weights(ffn2_w_gate, ffn2_w_up, ffn2_w_down, 0), *ln_params(0, 2))

    hf = _ffn_ln(hf, *ffn_weights(ffn1_w_gate, ffn1_w_up, ffn1_w_down, 1), *ln_params(1, 0))
    qw = DIFF_HEADS * 2 * HEAD_DIM
    w_qkv = diff_w_qkv[0]
    wv3 = w_qkv[:, 2 * qw:].reshape(D_MODEL, DIFF_HEADS, LANES)
    w_q_vaug = jnp.concatenate(
        [w_qkv[:, :qw] * Q_SCALE,
         jnp.concatenate([wv3, jnp.zeros_like(wv3)], axis=-1).reshape(D_MODEL, 2 * qw)],
        axis=1).astype(BF16)
    ones_diff = np.tile(np.repeat(np.array([0.0, 1.0], np.float32), LANES), DIFF_HEADS)
    q, kt, v_aug = _proj(hf, w_q_vaug, w_qkv[:, qw:2 * qw].T.astype(BF16),
                         jnp.asarray(ones_diff)[None, :], (qw,), (BF16,), nb, "diff_qkv_proj")
    lambda_init = 0.8 - 0.6 * math.exp(-0.3 * 1)
    lam_rows = jnp.concatenate(
        [_pad_lanes(diff_lambda_q1[0]), _pad_lanes(diff_lambda_k1[0]),
         _pad_lanes(diff_lambda_q2[0]), _pad_lanes(diff_lambda_k2[0]),
         jnp.zeros((4, LANES), F32)], axis=0)
    o = _diff_attention(q.reshape(nb, LP, qw), kt,
                        v_aug.reshape(nb, LP, DIFF_HEADS * 2 * LANES),
                        _relative_bias_rows(rel_bias_table), lam_rows,
                        diff_subln_w[0][None, :], lambda_init)
    hf = _outproj_ln(hf, [o.reshape(nb * LP, DIFF_HEADS * LANES)], [diff_w_o[0].astype(BF16)],
                     *ln_params(1, 1), name="diff_out_proj_ln")
    return _ffn_ln_final(hf.reshape(nb, LP, D_MODEL),
                         *ffn_weights(ffn2_w_gate, ffn2_w_up, ffn2_w_down, 1), *ln_params(1, 2))
```

```python
import functools
import math

import numpy as np
import jax
import jax.numpy as jnp
from jax import lax
from jax.experimental import pallas as pl
from jax.experimental.pallas import tpu as pltpu

F32 = jnp.float32
BF16 = jnp.bfloat16

D_MODEL = 1024
SEQ = 8192
DEPTH = 2
N_META = 16
CHUNK = 128
SSD_D_INNER = 2048
SSD_HEAD_DIM = 64
SSD_HEADS = 32
SSD_GROUPS = 4
SSD_GROUP_W = SSD_D_INNER // SSD_GROUPS
SSD_STATE = 128
SSD_CONV = 4
SSD_CONV_CH = SSD_D_INNER + 2 * SSD_GROUPS * SSD_STATE
FOX_HEADS = 16
FOX_WIDTH = 1024
HEAD_DIM = 64
DIFF_HEADS = 8
N_BUCKETS = 32
D_FF = 2816
ALPHA = (2 * DEPTH) ** 0.25
LN_EPS = 1e-5
RMS_EPS = 1e-5
NEG = -1e30

LANES = 128
LP = 8448
PADL = LP - SEQ - N_META
ROW_TILE = 512
OUT_TILE = 256
ATT_TQ = 768
ATT_TK = 768
CONV_HALO = 8
SMALL_W = LANES
VMEM_LIMIT = 56 * 1024 * 1024

assert PADL % CHUNK == CHUNK - N_META
assert LP % ATT_TQ == 0 and ATT_TQ % ATT_TK == 0 and ATT_TK % LANES == 0
assert LP % CHUNK == 0 and (2 * LP) % ROW_TILE == 0
assert PADL <= OUT_TILE and (LP - OUT_TILE) == SEQ


def _const_spec(shape):
    nd = len(shape)
    return pl.BlockSpec(shape, lambda *_: (0,) * nd, pipeline_mode=pl.Buffered(1))


def _layer_norm(r, g, b):
    mu = jnp.mean(r, axis=-1, keepdims=True)
    d = r - mu
    var = jnp.mean(d * d, axis=-1, keepdims=True)
    return d * lax.rsqrt(var + LN_EPS) * g + b


def _silu(x):
    return x / (1.0 + jnp.exp(-x))


def _softplus(x):
    return jnp.maximum(x, 0.0) + jnp.log(1.0 + jnp.exp(-jnp.abs(x)))


def _ffn_ln_body(h_ref, wg_ref, wu_ref, wd_ref, g_ref, b_ref, o_ref):
    h = h_ref[...]
    hb = h.astype(BF16)
    g = jnp.dot(hb, wg_ref[...], preferred_element_type=F32)
    u = jnp.dot(hb, wu_ref[...], preferred_element_type=F32)
    a = (_silu(g) * u).astype(BF16)
    y = jnp.dot(a, wd_ref[...], preferred_element_type=F32)
    o_ref[...] = _layer_norm(ALPHA * h + 0.5 * y, g_ref[...], b_ref[...])


def _ffn_ln(hf, wg, wu, wd, g, b):
    rows = hf.shape[0]
    return pl.pallas_call(
        _ffn_ln_body,
        grid=(rows // ROW_TILE,),
        in_specs=[
            pl.BlockSpec((ROW_TILE, D_MODEL), lambda i: (i, 0)),
            _const_spec((D_MODEL, D_FF)),
            _const_spec((D_MODEL, D_FF)),
            _const_spec((D_FF, D_MODEL)),
            _const_spec((1, D_MODEL)),
            _const_spec((1, D_MODEL)),
        ],
        out_specs=pl.BlockSpec((ROW_TILE, D_MODEL), lambda i: (i, 0)),
        out_shape=jax.ShapeDtypeStruct((rows, D_MODEL), F32),
        compiler_params=pltpu.CompilerParams(
            dimension_semantics=("parallel",), vmem_limit_bytes=VMEM_LIMIT),
        name="ffn_ln",
    )(hf, wg, wu, wd, g, b)


def _ffn_ln_first_body(lead_ref, x_ref, wg_ref, wu_ref, wd_ref, g_ref, b_ref, o_ref, h_ref):
    h_ref[...] = jnp.where(pl.program_id(1) == 0, lead_ref[...], x_ref[...])
    _ffn_ln_body(h_ref, wg_ref, wu_ref, wd_ref, g_ref, b_ref, o_ref)


def _ffn_ln_first(x, lead, wg, wu, wd, g, b):
    nb = x.shape[0]
    tiles = LP // OUT_TILE
    return pl.pallas_call(
        _ffn_ln_first_body,
        grid=(nb, tiles),
        in_specs=[
            _const_spec((OUT_TILE, D_MODEL)),
            pl.BlockSpec((None, OUT_TILE, D_MODEL), lambda bi, i: (bi, jnp.maximum(i - 1, 0), 0)),
            _const_spec((D_MODEL, D_FF)),
            _const_spec((D_MODEL, D_FF)),
            _const_spec((D_FF, D_MODEL)),
            _const_spec((1, D_MODEL)),
            _const_spec((1, D_MODEL)),
        ],
        out_specs=pl.BlockSpec((OUT_TILE, D_MODEL), lambda bi, i: (bi * tiles + i, 0)),
        out_shape=jax.ShapeDtypeStruct((nb * LP, D_MODEL), F32),
        scratch_shapes=[pltpu.VMEM((OUT_TILE, D_MODEL), F32)],
        compiler_params=pltpu.CompilerParams(
            dimension_semantics=("parallel", "arbitrary"), vmem_limit_bytes=VMEM_LIMIT),
        name="ffn_ln_first",
    )(lead, x, wg, wu, wd, g, b)


def _ffn_ln_final_body(h_ref, wg_ref, wu_ref, wd_ref, g_ref, b_ref, o_ref):
    @pl.when(pl.program_id(1) > 0)
    def _():
        _ffn_ln_body(h_ref, wg_ref, wu_ref, wd_ref, g_ref, b_ref, o_ref)


def _ffn_ln_final(h3, wg, wu, wd, g, b):
    nb = h3.shape[0]
    return pl.pallas_call(
        _ffn_ln_final_body,
        grid=(nb, LP // OUT_TILE),
        in_specs=[
            pl.BlockSpec((None, OUT_TILE, D_MODEL), lambda bi, i: (bi, i, 0)),
            _const_spec((D_MODEL, D_FF)),
            _const_spec((D_MODEL, D_FF)),
            _const_spec((D_FF, D_MODEL)),
            _const_spec((1, D_MODEL)),
            _const_spec((1, D_MODEL)),
        ],
        out_specs=pl.BlockSpec((None, OUT_TILE, D_MODEL),
                               lambda bi, i: (bi, jnp.maximum(i - 1, 0), 0)),
        out_shape=jax.ShapeDtypeStruct((nb, SEQ, D_MODEL), F32),
        compiler_params=pltpu.CompilerParams(
            dimension_semantics=("arbitrary", "arbitrary"), vmem_limit_bytes=VMEM_LIMIT),
        name="ffn_ln_final",
    )(h3, wg, wu, wd, g, b)


PROJ_TILE = 256
assert LP % PROJ_TILE == 0


def _proj_body(h_ref, w_ref, wkt_ref, vones_ref, *o_refs):
    hb = h_ref[...].astype(BF16)
    off = 0
    for o_ref in o_refs[:-2]:
        n = o_ref.shape[-1]
        o_ref[...] = jnp.dot(hb, w_ref[:, off:off + n],
                             preferred_element_type=F32).astype(o_ref.dtype)
        off += n
    kt_ref, va_ref = o_refs[-2:]
    kt_ref[...] = lax.dot_general(wkt_ref[...], hb, (((1,), (1,)), ((), ())),
                                  preferred_element_type=F32).astype(kt_ref.dtype)
    va_ref[...] = (jnp.dot(hb, w_ref[:, off:], preferred_element_type=F32)
                   + vones_ref[...]).astype(va_ref.dtype)


def _proj(hf, w, wkt, vones, widths, dtypes, nb, name):
    rows = hf.shape[0]
    tiles_per_batch = LP // PROJ_TILE
    k_width = wkt.shape[0]
    va_width = vones.shape[1]
    out_specs = [pl.BlockSpec((PROJ_TILE, n), lambda i: (i, 0)) for n in widths]
    out_specs += [
        pl.BlockSpec((None, k_width, PROJ_TILE),
                     lambda i: (i // tiles_per_batch, 0, i % tiles_per_batch)),
        pl.BlockSpec((PROJ_TILE, va_width), lambda i: (i, 0)),
    ]
    out_shape = [jax.ShapeDtypeStruct((rows, n), dt) for n, dt in zip(widths, dtypes)]
    out_shape += [jax.ShapeDtypeStruct((nb, k_width, LP), BF16),
                  jax.ShapeDtypeStruct((rows, va_width), BF16)]
    return pl.pallas_call(
        _proj_body,
        grid=(rows // PROJ_TILE,),
        in_specs=[pl.BlockSpec((PROJ_TILE, D_MODEL), lambda i: (i, 0)),
                  _const_spec(w.shape), _const_spec(wkt.shape), _const_spec(vones.shape)],
        out_specs=out_specs,
        out_shape=out_shape,
        compiler_params=pltpu.CompilerParams(
            dimension_semantics=("parallel",), vmem_limit_bytes=VMEM_LIMIT),
        name=name,
    )(hf, w, wkt, vones)


def _outproj_ln_body(n_in, h_ref, *refs):
    a_refs = refs[:n_in]
    w_refs = refs[n_in:2 * n_in]
    g_ref, b_ref, o_ref = refs[2 * n_in:]
    m = jnp.dot(a_refs[0][...], w_refs[0][...], preferred_element_type=F32)
    for a_ref, w_ref in zip(a_refs[1:], w_refs[1:]):
        m = m + jnp.dot(a_ref[...], w_ref[...], preferred_element_type=F32)
    o_ref[...] = _layer_norm(ALPHA * h_ref[...] + m, g_ref[...], b_ref[...])


def _outproj_ln(hf, acts, ws, g, b, name):
    rows = hf.shape[0]
    n_in = len(acts)
    in_specs = [pl.BlockSpec((ROW_TILE, D_MODEL), lambda i: (i, 0))]
    in_specs += [pl.BlockSpec((ROW_TILE, a.shape[1]), lambda i: (i, 0)) for a in acts]
    in_specs += [_const_spec(w.shape) for w in ws]
    in_specs += [_const_spec((1, D_MODEL)), _const_spec((1, D_MODEL))]
    return pl.pallas_call(
        functools.partial(_outproj_ln_body, n_in),
        grid=(rows // ROW_TILE,),
        in_specs=in_specs,
        out_specs=pl.BlockSpec((ROW_TILE, D_MODEL), lambda i: (i, 0)),
        out_shape=jax.ShapeDtypeStruct((rows, D_MODEL), F32),
        compiler_params=pltpu.CompilerParams(
            dimension_semantics=("parallel",), vmem_limit_bytes=VMEM_LIMIT),
        name=name,
    )(hf, *acts, *ws, g, b)


def _split_dot(x, e_ref):
    hi = x.astype(BF16)
    lo = (x - hi.astype(F32)).astype(BF16)
    e = e_ref[...]
    return (jnp.dot(hi, e, preferred_element_type=F32)
            + jnp.dot(lo, e, preferred_element_type=F32))


def _ssd_body(xbc_ref, z_ref, small_ref, convw_ref, convb_ref, sbias_ref, alog_ref,
              dskip_ref, normw_ref, expand_ref, y_ref, c_ref,
              ext_ref, state_ref, carry_ref):
    c = pl.program_id(1)

    @pl.when(c == 0)
    def _():
        ext_ref[0:CONV_HALO, :] = jnp.zeros((CONV_HALO, SSD_CONV_CH), F32)
        state_ref[...] = jnp.zeros_like(state_ref)
        carry_ref[...] = jnp.zeros_like(carry_ref)

    row = lax.broadcasted_iota(jnp.int32, (CHUNK, 1), 0)
    valid = (c * CHUNK + row) >= PADL

    ext_ref[CONV_HALO:CONV_HALO + CHUNK, :] = jnp.where(valid, xbc_ref[...], 0.0)
    ext = ext_ref[...]
    conv = convb_ref[...] + convw_ref[SSD_CONV - 1:SSD_CONV, :] * ext[CONV_HALO:, :]
    for k in range(SSD_CONV - 1):
        shifted = pltpu.roll(ext, SSD_CONV - 1 - k, 0)[CONV_HALO:, :]
        conv = conv + convw_ref[k:k + 1, :] * shifted
    ext_ref[0:CONV_HALO, :] = ext_ref[CHUNK:CHUNK + CONV_HALO, :]
    xc = jnp.where(valid, _silu(conv), 0.0)
    xs = xc[:, :SSD_D_INNER]
    bm = xc[:, SSD_D_INNER:SSD_D_INNER + SSD_GROUPS * SSD_STATE].astype(BF16)
    cm = xc[:, SSD_D_INNER + SSD_GROUPS * SSD_STATE:].astype(BF16)

    lane = lax.broadcasted_iota(jnp.int32, (CHUNK, SMALL_W), 1)
    is_dt = lane < SSD_HEADS
    is_f = (lane >= SSD_HEADS) & (lane < SSD_HEADS + FOX_HEADS)
    v = small_ref[...] + sbias_ref[...]
    dt = jnp.where(valid & is_dt, _softplus(v), 0.0)
    log_f = jnp.where(valid & is_f, -_softplus(-v), 0.0)
    neg_a = -jnp.exp(alog_ref[...])
    steps = jnp.where(is_dt, dt * neg_a, log_f)
    r_i = lax.broadcasted_iota(jnp.int32, (CHUNK, CHUNK), 0)
    c_i = lax.broadcasted_iota(jnp.int32, (CHUNK, CHUNK), 1)
    causal = r_i >= c_i
    tril = jnp.where(causal, 1.0, 0.0).astype(F32)
    cum = jnp.dot(tril, steps, preferred_element_type=F32,
                  precision=lax.Precision.HIGHEST)
    c_total = jnp.where(is_f, cum + carry_ref[...], 0.0)
    c_ref[...] = c_total
    carry_ref[...] = c_total[CHUNK - 1:CHUNK, :]

    a_last = cum[CHUNK - 1:CHUNK, :]
    dt_x = _split_dot(dt, expand_ref)
    ea_x = _split_dot(jnp.exp(cum), expand_ref)
    de_x = _split_dot(jnp.exp(a_last - cum), expand_ref)
    x_dt = xs * dt_x
    xb = x_dt.astype(BF16)
    xe = (x_dt * de_x).astype(BF16)
    cum_t = cum.T
    lane_p = lax.broadcasted_iota(jnp.int32, (CHUNK, 2 * SSD_HEAD_DIM), 1)
    first_half = lane_p < SSD_HEAD_DIM

    y_groups = []
    for g in range(SSD_GROUPS):
        gs = slice(g * SSD_GROUP_W, (g + 1) * SSD_GROUP_W)
        bg = bm[:, g * SSD_STATE:(g + 1) * SSD_STATE]
        cg = cm[:, g * SSD_STATE:(g + 1) * SSD_STATE]
        cb = lax.dot_general(cg, bg, (((1,), (1,)), ((), ())),
                             preferred_element_type=F32)
        pair_out = []
        for pr in range(SSD_GROUP_W // (2 * SSD_HEAD_DIM)):
            col0 = g * SSD_GROUP_W + pr * 2 * SSD_HEAD_DIM
            x_pair = xb[:, col0:col0 + 2 * SSD_HEAD_DIM]
            ys = []
            for j in range(2):
                hd = col0 // SSD_HEAD_DIM + j
                diff = cum[:, hd:hd + 1] - cum_t[hd:hd + 1, :]
                decay = jnp.exp(jnp.where(causal, diff, -jnp.inf))
                mat = (cb * decay).astype(BF16)
                ys.append(jnp.dot(mat, x_pair, preferred_element_type=F32))
            pair_out.append(jnp.where(first_half, ys[0], ys[1]))
        y_diag = jnp.concatenate(pair_out, axis=1)
        st = state_ref[g]
        y_off = jnp.dot(cg, st.astype(BF16), preferred_element_type=F32) * ea_x[:, gs]
        new = lax.dot_general(bg, xe[:, gs], (((0,), (0,)), ((), ())),
                              preferred_element_type=F32)
        state_ref[g] = st * ea_x[CHUNK - 1:CHUNK, gs] + new
        yg = y_diag + y_off + dskip_ref[:, gs] * xs[:, gs]
        yg = yg * _silu(z_ref[:, gs])
        ms = jnp.mean(yg * yg, axis=-1, keepdims=True)
        y_groups.append(yg * lax.rsqrt(ms + RMS_EPS) * normw_ref[:, gs])
    y_ref[...] = jnp.concatenate(y_groups, axis=1).astype(y_ref.dtype)


def _ssd(xbc, z, small, convw, convb, sbias, alog, dskip, normw, expand):
    nb = xbc.shape[0]
    row_spec = lambda w: pl.BlockSpec((None, CHUNK, w), lambda bi, ci: (bi, ci, 0))
    return pl.pallas_call(
        _ssd_body,
        grid=(nb, LP // CHUNK),
        in_specs=[row_spec(SSD_CONV_CH), row_spec(SSD_D_INNER), row_spec(SMALL_W),
                  _const_spec(convw.shape), _const_spec(convb.shape),
                  _const_spec(sbias.shape), _const_spec(alog.shape),
                  _const_spec(dskip.shape), _const_spec(normw.shape),
                  _const_spec(expand.shape)],
        out_specs=[row_spec(SSD_D_INNER), row_spec(SMALL_W)],
        out_shape=[jax.ShapeDtypeStruct((nb, LP, SSD_D_INNER), BF16),
                   jax.ShapeDtypeStruct((nb, LP, SMALL_W), F32)],
        scratch_shapes=[pltpu.VMEM((CHUNK + CONV_HALO, SSD_CONV_CH), F32),
                        pltpu.VMEM((SSD_GROUPS, SSD_STATE, SSD_GROUP_W), F32),
                        pltpu.VMEM((1, SMALL_W), F32)],
        compiler_params=pltpu.CompilerParams(
            dimension_semantics=("arbitrary", "arbitrary"), vmem_limit_bytes=VMEM_LIMIT),
        name="ssd",
    )(xbc, z, small, convw, convb, sbias, alog, dskip, normw, expand)


N_TK_CHUNKS = ATT_TK // LANES
N_TQ_CHUNKS = ATT_TQ // LANES
assert ATT_TQ == ATT_TK and PADL <= ATT_TK
LOG2E = math.log2(math.e)
Q_SCALE = LOG2E * HEAD_DIM ** -0.5


def _chunk_max(chunk):
    part = chunk(0)
    for c in range(1, N_TK_CHUNKS):
        part = jnp.maximum(part, chunk(c))
    return part


def _softmax_update(chunk, part, v_aug, m_ref, acc_ref, idx):
    m_old = m_ref[idx]
    m_new = jnp.maximum(m_old, jnp.max(part, axis=-1, keepdims=True))
    alpha = jnp.exp2(m_old - m_new)
    p = jnp.concatenate([jnp.exp2(chunk(c) - m_new).astype(BF16)
                         for c in range(N_TK_CHUNKS)], axis=1)
    pv = jnp.dot(p, v_aug, preferred_element_type=F32)
    for w in range(acc_ref.shape[-1] // LANES):
        ws = slice(w * LANES, (w + 1) * LANES)
        acc_ref[idx, :, ws] = alpha * acc_ref[idx, :, ws] + pv[:, ws]
    m_ref[idx] = m_new


def _causal_chunk(c):
    qpos = lax.broadcasted_iota(jnp.int32, (ATT_TQ, LANES), 0)
    kpos = c * LANES + lax.broadcasted_iota(jnp.int32, (ATT_TQ, LANES), 1)
    return kpos <= qpos


def _not_padding_chunk(kb, c):
    kpos = c * LANES + lax.broadcasted_iota(jnp.int32, (ATT_TQ, LANES), 1)
    return (kpos >= PADL) | (kb > 0)


def _init_softmax_state(m_ref, acc_ref):
    m_ref[...] = jnp.full(m_ref.shape, NEG, F32)
    acc_ref[...] = jnp.zeros_like(acc_ref)


def _split_halves(q):
    lane = lax.broadcasted_iota(jnp.int32, q.shape, 1)
    first_half = lane < HEAD_DIM
    zero = jnp.zeros_like(q)
    return first_half, (jnp.where(first_half, q, zero), jnp.where(first_half, zero, q))


def _run_key_tiles(qb, scores, step, buf_a, buf_b):
    odd = qb % 2

    @pl.when(odd == 0)
    def _():
        scores(0, buf_a)

    @pl.when(odd == 1)
    def _():
        scores(0, buf_b)
        step(0, buf_b, buf_a)

    def two_below_diagonal(i, carry):
        kb = odd + 2 * i
        step(kb, buf_a, buf_b)
        step(kb + 1, buf_b, buf_a)
        return carry

    lax.fori_loop(0, qb // 2, two_below_diagonal, 0)
    step(qb, buf_a, None)


def _fox_body(q_ref, kt_ref, v_ref, ck_ref, o_ref, m_ref, acc_ref,
              sa_ref, pa_ref, sb_ref, pb_ref):
    qb = pl.program_id(2)
    first_half, q_heads = _split_halves(q_ref[...])
    _init_softmax_state(m_ref, acc_ref)
    buf_a, buf_b = (sa_ref, pa_ref), (sb_ref, pb_ref)

    def scores(kb, buf):
        s_dst, part_dst = buf
        start = pl.multiple_of(kb * ATT_TK, ATT_TK)
        kt = kt_ref[:, pl.ds(start, ATT_TK)]
        for j in range(2):
            s = jnp.dot(q_heads[j], kt, preferred_element_type=F32)
            part = None
            for c in range(N_TK_CHUNKS):
                key0 = pl.multiple_of(start + c * LANES, LANES)
                sc = s[:, c * LANES:(c + 1) * LANES] - ck_ref[j:j + 1, pl.ds(key0, LANES)]
                s_dst[j, :, c * LANES:(c + 1) * LANES] = sc
                part = sc if part is None else jnp.maximum(part, sc)
            part_dst[j] = part

    def step(kb, src, nxt):
        s_src, part_src = src
        start = pl.multiple_of(kb * ATT_TK, ATT_TK)
        if nxt is not None:
            scores(kb + 1, nxt)
        for j in range(2):
            def chunk(c, j=j):
                sc = s_src[j, :, c * LANES:(c + 1) * LANES]
                if nxt is None:
                    sc = jnp.where(_causal_chunk(c), sc, NEG)
                return sc
            part = _chunk_max(chunk) if nxt is None else part_src[j]
            v_aug = v_ref[pl.ds(start, ATT_TK), j * LANES:(j + 1) * LANES]
            _softmax_update(chunk, part, v_aug, m_ref, acc_ref, j)

    _run_key_tiles(qb, scores, step, buf_a, buf_b)

    acc0, acc1 = acc_ref[0], acc_ref[1]
    num = jnp.where(first_half, acc0, acc1)
    den = pltpu.roll(jnp.where(first_half, acc1, acc0), HEAD_DIM, axis=1)
    o_ref[...] = (num / den).astype(o_ref.dtype)


def _fox_attention(q, kt, v_aug, ck):
    nb = q.shape[0]
    pairs = FOX_HEADS // 2
    return pl.pallas_call(
        _fox_body,
        grid=(nb, pairs, LP // ATT_TQ),
        in_specs=[
            pl.BlockSpec((None, ATT_TQ, LANES), lambda bi, hp, qi: (bi, qi, hp)),
            pl.BlockSpec((None, LANES, LP), lambda bi, hp, qi: (bi, hp, 0)),
            pl.BlockSpec((None, LP, 2 * LANES), lambda bi, hp, qi: (bi, 0, hp)),
            pl.BlockSpec((None, None, 2, LP), lambda bi, hp, qi: (bi, hp, 0, 0)),
        ],
        out_specs=pl.BlockSpec((None, ATT_TQ, LANES), lambda bi, hp, qi: (bi, qi, hp)),
        out_shape=jax.ShapeDtypeStruct((nb, LP, FOX_WIDTH), BF16),
        scratch_shapes=[pltpu.VMEM((2, ATT_TQ, LANES), F32),
                        pltpu.VMEM((2, ATT_TQ, LANES), F32)]
                       + [pltpu.VMEM((2, ATT_TQ, ATT_TK), F32),
                          pltpu.VMEM((2, ATT_TQ, LANES), F32)] * 2,
        compiler_params=pltpu.CompilerParams(
            dimension_semantics=("parallel", "parallel", "arbitrary"),
            vmem_limit_bytes=VMEM_LIMIT),
        name="fox_attention",
    )(q, kt, v_aug, ck)


def _bias_tiles(bias_ref):
    rows = jnp.broadcast_to(bias_ref[...], (CHUNK, 2 * CHUNK))
    rolled = pltpu.roll(rows, 0, 1, stride=1, stride_axis=0)
    return rolled[:, :CHUNK], rolled[:, CHUNK:]


def _bias_chunk(tiles, key_chunk):
    picked = []
    for a in range(N_TQ_CHUNKS):
        sub = a - key_chunk
        picked.append(tiles[sub] if sub in (0, 1) else None)
    if all(t is None for t in picked):
        return None
    zeros = jnp.zeros((CHUNK, CHUNK), F32)
    return jnp.concatenate([zeros if t is None else t for t in picked], axis=0)


def _diff_body(lambda_init, q_ref, kt_ref, v_ref, bias_ref, lam_ref, subln_ref, o_ref,
               m_ref, acc_ref, tiles_ref, sa_ref, pa_ref, sb_ref, pb_ref):
    qb = pl.program_id(2)
    _, q_parts = _split_halves(q_ref[...])
    _init_softmax_state(m_ref, acc_ref)
    buf_a, buf_b = (sa_ref, pa_ref), (sb_ref, pb_ref)
    tile0, tile1 = _bias_tiles(bias_ref)
    tiles_ref[0] = tile0
    tiles_ref[1] = tile1
    last = N_TK_CHUNKS - 1

    def scores(kb, buf):
        s_dst, part_dst = buf
        start = pl.multiple_of(kb * ATT_TK, ATT_TK)
        kt = kt_ref[:, pl.ds(start, ATT_TK)]
        for j in range(2):
            s = jnp.dot(q_parts[j], kt, preferred_element_type=F32)
            part = None
            for c in range(N_TK_CHUNKS):
                sc = s[:, c * LANES:(c + 1) * LANES]
                if c * LANES < PADL:
                    sc = jnp.where(_not_padding_chunk(kb, c), sc, NEG)
                s_dst[j, :, c * LANES:(c + 1) * LANES] = sc
                part = sc if part is None else jnp.maximum(part, sc)
            part_dst[j] = part

    def step(kb, src, nxt):
        s_src, part_src = src
        start = pl.multiple_of(kb * ATT_TK, ATT_TK)
        diagonal = nxt is None
        if diagonal:
            tiles = (tiles_ref[0], tiles_ref[1])
            biases = [_bias_chunk(tiles, c) for c in range(N_TK_CHUNKS)]
        else:
            scores(kb + 1, nxt)
            corner = jnp.where(kb == qb - 1, 1.0, 0.0) * tiles_ref[1]
        v_aug = v_ref[pl.ds(start, ATT_TK), :]
        for j in range(2):
            def chunk(c, j=j):
                sc = s_src[j, :, c * LANES:(c + 1) * LANES]
                if diagonal:
                    if biases[c] is not None:
                        sc = sc + biases[c]
                    sc = jnp.where(_causal_chunk(c), sc, NEG)
                elif c == last:
                    sc = jnp.concatenate([sc[:CHUNK] + corner, sc[CHUNK:]], axis=0)
                return sc
            if diagonal:
                part = _chunk_max(chunk)
            else:
                top = s_src[j, :CHUNK, last * LANES:] + corner
                for c in range(last):
                    top = jnp.maximum(top, s_src[j, :CHUNK, c * LANES:(c + 1) * LANES])
                part = jnp.concatenate([top, part_src[j, CHUNK:, :]], axis=0)
            _softmax_update(chunk, part, v_aug, m_ref, acc_ref, j)

    _run_key_tiles(qb, scores, step, buf_a, buf_b)

    lam1 = jnp.exp(jnp.sum(lam_ref[0:1, :] * lam_ref[1:2, :], axis=-1, keepdims=True))
    lam2 = jnp.exp(jnp.sum(lam_ref[2:3, :] * lam_ref[3:4, :], axis=-1, keepdims=True))
    lam = lam1 - lam2 + lambda_init
    o = (acc_ref[0, :, :LANES] / acc_ref[0, :, LANES:]
         - lam * (acc_ref[1, :, :LANES] / acc_ref[1, :, LANES:]))
    ms = jnp.mean(o * o, axis=-1, keepdims=True)
    o = o * lax.rsqrt(ms + RMS_EPS) * subln_ref[...] * (1.0 - lambda_init)
    o_ref[...] = o.astype(o_ref.dtype)


def _diff_attention(q, kt, v_aug, bias_rows, lam_rows, subln, lambda_init):
    nb = q.shape[0]
    return pl.pallas_call(
        functools.partial(_diff_body, lambda_init),
        grid=(nb, DIFF_HEADS, LP // ATT_TQ),
        in_specs=[
            pl.BlockSpec((None, ATT_TQ, LANES), lambda bi, hd, qi: (bi, qi, hd)),
            pl.BlockSpec((None, LANES, LP), lambda bi, hd, qi: (bi, hd, 0)),
            pl.BlockSpec((None, LP, 2 * LANES), lambda bi, hd, qi: (bi, 0, hd)),
            pl.BlockSpec((None, 1, 2 * CHUNK), lambda bi, hd, qi: (hd, 0, 0)),
            _const_spec(lam_rows.shape),
            _const_spec(subln.shape),
        ],
        out_specs=pl.BlockSpec((None, ATT_TQ, LANES), lambda bi, hd, qi: (bi, qi, hd)),
        out_shape=jax.ShapeDtypeStruct((nb, LP, DIFF_HEADS * LANES), BF16),
        scratch_shapes=[pltpu.VMEM((2, ATT_TQ, LANES), F32),
                        pltpu.VMEM((2, ATT_TQ, 2 * LANES), F32),
                        pltpu.VMEM((2, CHUNK, CHUNK), F32)]
                       + [pltpu.VMEM((2, ATT_TQ, ATT_TK), F32),
                          pltpu.VMEM((2, ATT_TQ, LANES), F32)] * 2,
        compiler_params=pltpu.CompilerParams(
            dimension_semantics=("parallel", "parallel", "arbitrary"),
            vmem_limit_bytes=VMEM_LIMIT),
        name="diff_attention",
    )(q, kt, v_aug, bias_rows, lam_rows, subln)


def _t5_bucket(n):
    max_exact = N_BUCKETS // 2
    nf = jnp.maximum(n, 1).astype(F32)
    large = max_exact + (jnp.log(nf / max_exact) / math.log(128 / max_exact)
                         * (N_BUCKETS - max_exact)).astype(jnp.int32)
    large = jnp.minimum(large, N_BUCKETS - 1)
    return jnp.where(n < max_exact, n, large)


def _relative_bias_rows(rel_table):
    dist = jnp.arange(2 * CHUNK)
    by_dist = LOG2E * (rel_table[_t5_bucket(dist)] - rel_table[N_BUCKETS - 1])
    reversed_rows = jnp.concatenate([by_dist[:1], by_dist[:0:-1]], axis=0)
    return reversed_rows.T[:, None, :].astype(F32)


def _pad_lanes(vec, width=LANES):
    return jnp.pad(vec, (0, width - vec.shape[0]))[None, :].astype(F32)


def kernel(x, meta_tokens, ln_gain, ln_bias, ffn1_w_gate, ffn1_w_up, ffn1_w_down, ffn2_w_gate, ffn2_w_up, ffn2_w_down, even_w_in, even_conv_w, even_conv_b, ssd_dt_bias, ssd_a_log, ssd_d_skip, ssd_norm_w, fox_f_bias, even_w_out, diff_w_qkv, diff_lambda_q1, diff_lambda_k1, diff_lambda_q2, diff_lambda_k2, diff_subln_w, diff_w_o, rel_bias_table):
    nb = x.shape[0]
    lead = jnp.concatenate([jnp.zeros((PADL, D_MODEL), x.dtype), meta_tokens.astype(x.dtype)], axis=0)

    def ln_params(l, i):
        return ln_gain[l, i][None, :], ln_bias[l, i][None, :]

    def ffn_weights(wg, wu, wd, l):
        return wg[l].astype(BF16), wu[l].astype(BF16), wd[l].astype(BF16)

    hf = _ffn_ln_first(x, lead, *ffn_weights(ffn1_w_gate, ffn1_w_up, ffn1_w_down, 0),
                       *ln_params(0, 0))

    w_in = even_w_in[0]
    o_z, o_xbc = 0, SSD_D_INNER
    o_dt = o_xbc + SSD_CONV_CH
    o_q = o_dt + SSD_HEADS
    o_k, o_v = o_q + FOX_WIDTH, o_q + 2 * FOX_WIDTH
    o_f = o_q + 3 * FOX_WIDTH
    w_small = jnp.concatenate(
        [w_in[:, o_dt:o_q], w_in[:, o_f:],
         jnp.zeros((D_MODEL, SMALL_W - SSD_HEADS - FOX_HEADS), w_in.dtype)], axis=1)
    pairs = FOX_HEADS // 2
    wv4 = w_in[:, o_v:o_f].reshape(D_MODEL, pairs, 2, HEAD_DIM)
    wz = jnp.zeros((D_MODEL, pairs, HEAD_DIM), w_in.dtype)
    w_vaug = jnp.concatenate([wv4[:, :, 0], wz, wz, wv4[:, :, 1]], axis=-1)
    ones_fox = np.tile(np.repeat(np.array([0.0, 1.0, 1.0, 0.0], np.float32), HEAD_DIM), pairs)
    w_even = jnp.concatenate(
        [w_in[:, o_z:o_dt], w_in[:, o_q:o_k] * Q_SCALE, w_small,
         w_vaug.reshape(D_MODEL, 2 * FOX_WIDTH)], axis=1).astype(BF16)
    z, xbc, q, small, kt, v_aug = _proj(
        hf, w_even, w_in[:, o_k:o_v].T.astype(BF16), jnp.asarray(ones_fox)[None, :],
        (SSD_D_INNER, SSD_CONV_CH, FOX_WIDTH, SMALL_W), (F32, F32, BF16, F32), nb,
        "even_in_proj")

    sbias = _pad_lanes(jnp.concatenate([ssd_dt_bias[0], fox_f_bias[0]]))
    alog = _pad_lanes(ssd_a_log[0])
    dskip = jnp.repeat(ssd_d_skip[0], SSD_HEAD_DIM)[None, :].astype(F32)
    expand = np.zeros((SMALL_W, SSD_D_INNER), np.float32)
    expand[np.arange(SSD_D_INNER) // SSD_HEAD_DIM, np.arange(SSD_D_INNER)] = 1.0
    y, cfull = _ssd(xbc.reshape(nb, LP, SSD_CONV_CH), z.reshape(nb, LP, SSD_D_INNER),
                    small.reshape(nb, LP, SMALL_W), even_conv_w[0], even_conv_b[0][None, :],
                    sbias, alog, dskip, ssd_norm_w[0][None, :], jnp.asarray(expand, BF16))
    ck = LOG2E * cfull[:, :, SSD_HEADS:SSD_HEADS + FOX_HEADS].transpose(0, 2, 1)
    ck = jnp.where(jnp.arange(LP) < PADL, -NEG, ck)
    ck = ck.reshape(nb, FOX_HEADS // 2, 2, LP)
    o = _fox_attention(q.reshape(nb, LP, FOX_WIDTH), kt,
                       v_aug.reshape(nb, LP, FOX_HEADS * LANES), ck)
    w_out = even_w_out[0].astype(BF16)
    hf = _outproj_ln(hf, [y.reshape(nb * LP, SSD_D_INNER), o.reshape(nb * LP, FOX_WIDTH)],
                     [w_out[:SSD_D_INNER], w_out[SSD_D_INNER:]], *ln_params(0, 1),
                     name="even_out_proj_ln")
    hf = _ffn_ln(hf, *ffn_weights(ffn2_w_gate, ffn2_w_up, ffn2_w_down, 0), *ln_params(0, 2))

    hf = _ffn_ln(hf, *ffn_weights(ffn1_w_gate, ffn1_w_up, ffn1_w_down, 1), *ln_params(1, 0))
    qw = DIFF_HEADS * 2 * HEAD_DIM
    w_qkv = diff_w_qkv[0]
    wv3 = w_qkv[:, 2 * qw:].reshape(D_MODEL, DIFF_HEADS, LANES)
    w_q_vaug = jnp.concatenate(
        [w_qkv[:, :qw] * Q_SCALE,
         jnp.concatenate([wv3, jnp.zeros_like(wv3)], axis=-1).reshape(D_MODEL, 2 * qw)],
        axis=1).astype(BF16)
    ones_diff = np.tile(np.repeat(np.array([0.0, 1.0], np.float32), LANES), DIFF_HEADS)
    q, kt, v_aug = _proj(hf, w_q_vaug, w_qkv[:, qw:2 * qw].T.astype(BF16),
                         jnp.asarray(ones_diff)[None, :], (qw,), (BF16,), nb, "diff_qkv_proj")
    lambda_init = 0.8 - 0.6 * math.exp(-0.3 * 1)
    lam_rows = jnp.concatenate(
        [_pad_lanes(diff_lambda_q1[0]), _pad_lanes(diff_lambda_k1[0]),
         _pad_lanes(diff_lambda_q2[0]), _pad_lanes(diff_lambda_k2[0]),
         jnp.zeros((4, LANES), F32)], axis=0)
    o = _diff_attention(q.reshape(nb, LP, qw), kt,
                        v_aug.reshape(nb, LP, DIFF_HEADS * 2 * LANES),
                        _relative_bias_rows(rel_bias_table), lam_rows,
                        diff_subln_w[0][None, :], lambda_init)
    hf = _outproj_ln(hf, [o.reshape(nb * LP, DIFF_HEADS * LANES)], [diff_w_o[0].astype(BF16)],
                     *ln_params(1, 1), name="diff_out_proj_ln")
    return _ffn_ln_final(hf.reshape(nb, LP, D_MODEL),
                         *ffn_weights(ffn2_w_gate, ffn2_w_up, ffn2_w_down, 1), *ln_params(1, 2))
```

```python
import functools
import math

import numpy as np
import jax
import jax.numpy as jnp
from jax import lax
from jax.experimental import pallas as pl
from jax.experimental.pallas import tpu as pltpu

F32 = jnp.float32
BF16 = jnp.bfloat16

D_MODEL = 1024
SEQ = 8192
DEPTH = 2
N_META = 16
CHUNK = 128
SSD_D_INNER = 2048
SSD_HEAD_DIM = 64
SSD_HEADS = 32
SSD_GROUPS = 4
SSD_GROUP_W = SSD_D_INNER // SSD_GROUPS
SSD_STATE = 128
SSD_CONV = 4
SSD_CONV_CH = SSD_D_INNER + 2 * SSD_GROUPS * SSD_STATE
FOX_HEADS = 16
FOX_WIDTH = 1024
HEAD_DIM = 64
DIFF_HEADS = 8
N_BUCKETS = 32
D_FF = 2816
ALPHA = (2 * DEPTH) ** 0.25
LN_EPS = 1e-5
RMS_EPS = 1e-5
NEG = -1e30

LANES = 128
LP = 8448
PADL = LP - SEQ - N_META
ROW_TILE = 512
OUT_TILE = 256
ATT_TQ = 768
ATT_TK = 768
CONV_HALO = 8
SMALL_W = LANES
VMEM_LIMIT = 56 * 1024 * 1024

assert PADL % CHUNK == CHUNK - N_META
assert LP % ATT_TQ == 0 and ATT_TQ % ATT_TK == 0 and ATT_TK % LANES == 0
assert LP % CHUNK == 0 and (2 * LP) % ROW_TILE == 0
assert PADL <= OUT_TILE and (LP - OUT_TILE) == SEQ


def _const_spec(shape):
    nd = len(shape)
    return pl.BlockSpec(shape, lambda *_: (0,) * nd, pipeline_mode=pl.Buffered(1))


def _layer_norm(r, g, b):
    mu = jnp.mean(r, axis=-1, keepdims=True)
    d = r - mu
    var = jnp.mean(d * d, axis=-1, keepdims=True)
    return d * lax.rsqrt(var + LN_EPS) * g + b


def _silu(x):
    return x / (1.0 + jnp.exp(-x))


def _softplus(x):
    return jnp.maximum(x, 0.0) + jnp.log(1.0 + jnp.exp(-jnp.abs(x)))


def _ffn_ln_body(h_ref, wg_ref, wu_ref, wd_ref, g_ref, b_ref, o_ref):
    h = h_ref[...]
    hb = h.astype(BF16)
    g = jnp.dot(hb, wg_ref[...], preferred_element_type=F32)
    u = jnp.dot(hb, wu_ref[...], preferred_element_type=F32)
    a = (_silu(g) * u).astype(BF16)
    y = jnp.dot(a, wd_ref[...], preferred_element_type=F32)
    o_ref[...] = _layer_norm(ALPHA * h + 0.5 * y, g_ref[...], b_ref[...])


def _ffn_ln(hf, wg, wu, wd, g, b):
    rows = hf.shape[0]
    return pl.pallas_call(
        _ffn_ln_body,
        grid=(rows // ROW_TILE,),
        in_specs=[
            pl.BlockSpec((ROW_TILE, D_MODEL), lambda i: (i, 0)),
            _const_spec((D_MODEL, D_FF)),
            _const_spec((D_MODEL, D_FF)),
            _const_spec((D_FF, D_MODEL)),
            _const_spec((1, D_MODEL)),
            _const_spec((1, D_MODEL)),
        ],
        out_specs=pl.BlockSpec((ROW_TILE, D_MODEL), lambda i: (i, 0)),
        out_shape=jax.ShapeDtypeStruct((rows, D_MODEL), F32),
        compiler_params=pltpu.CompilerParams(
            dimension_semantics=("parallel",), vmem_limit_bytes=VMEM_LIMIT),
        name="ffn_ln",
    )(hf, wg, wu, wd, g, b)


def _ffn_ln_first_body(lead_ref, x_ref, wg_ref, wu_ref, wd_ref, g_ref, b_ref, o_ref, h_ref):
    h_ref[...] = jnp.where(pl.program_id(1) == 0, lead_ref[...], x_ref[...])
    _ffn_ln_body(h_ref, wg_ref, wu_ref, wd_ref, g_ref, b_ref, o_ref)


def _ffn_ln_first(x, lead, wg, wu, wd, g, b):
    nb = x.shape[0]
    tiles = LP // OUT_TILE
    return pl.pallas_call(
        _ffn_ln_first_body,
        grid=(nb, tiles),
        in_specs=[
            _const_spec((OUT_TILE, D_MODEL)),
            pl.BlockSpec((None, OUT_TILE, D_MODEL), lambda bi, i: (bi, jnp.maximum(i - 1, 0), 0)),
            _const_spec((D_MODEL, D_FF)),
            _const_spec((D_MODEL, D_FF)),
            _const_spec((D_FF, D_MODEL)),
            _const_spec((1, D_MODEL)),
            _const_spec((1, D_MODEL)),
        ],
        out_specs=pl.BlockSpec((OUT_TILE, D_MODEL), lambda bi, i: (bi * tiles + i, 0)),
        out_shape=jax.ShapeDtypeStruct((nb * LP, D_MODEL), F32),
        scratch_shapes=[pltpu.VMEM((OUT_TILE, D_MODEL), F32)],
        compiler_params=pltpu.CompilerParams(
            dimension_semantics=("parallel", "arbitrary"), vmem_limit_bytes=VMEM_LIMIT),
        name="ffn_ln_first",
    )(lead, x, wg, wu, wd, g, b)


def _ffn_ln_final_body(h_ref, wg_ref, wu_ref, wd_ref, g_ref, b_ref, o_ref):
    @pl.when(pl.program_id(1) > 0)
    def _():
        _ffn_ln_body(h_ref, wg_ref, wu_ref, wd_ref, g_ref, b_ref, o_ref)


def _ffn_ln_final(h3, wg, wu, wd, g, b):
    nb = h3.shape[0]
    return pl.pallas_call(
        _ffn_ln_final_body,
        grid=(nb, LP // OUT_TILE),
        in_specs=[
            pl.BlockSpec((None, OUT_TILE, D_MODEL), lambda bi, i: (bi, i, 0)),
            _const_spec((D_MODEL, D_FF)),
            _const_spec((D_MODEL, D_FF)),
            _const_spec((D_FF, D_MODEL)),
            _const_spec((1, D_MODEL)),
            _const_spec((1, D_MODEL)),
        ],
        out_specs=pl.BlockSpec((None, OUT_TILE, D_MODEL),
                               lambda bi, i: (bi, jnp.maximum(i - 1, 0), 0)),
        out_shape=jax.ShapeDtypeStruct((nb, SEQ, D_MODEL), F32),
        compiler_params=pltpu.CompilerParams(
            dimension_semantics=("arbitrary", "arbitrary"), vmem_limit_bytes=VMEM_LIMIT),
        name="ffn_ln_final",
    )(h3, wg, wu, wd, g, b)


PROJ_TILE = 256
assert LP % PROJ_TILE == 0


def _proj_body(n_t, h_ref, w_ref, wt_ref, *o_refs):
    hb = h_ref[...].astype(BF16)
    off = 0
    for o_ref in o_refs[:-n_t]:
        n = o_ref.shape[-1]
        o_ref[...] = jnp.dot(hb, w_ref[:, off:off + n],
                             preferred_element_type=F32).astype(o_ref.dtype)
        off += n
    off = 0
    for o_ref in o_refs[-n_t:]:
        n = o_ref.shape[0]
        o_ref[...] = lax.dot_general(wt_ref[off:off + n, :], hb, (((1,), (1,)), ((), ())),
                                     preferred_element_type=F32).astype(o_ref.dtype)
        off += n


def _proj(hf, w, wt, widths, dtypes, t_widths, nb, name):
    rows = hf.shape[0]
    tiles_per_batch = LP // PROJ_TILE
    out_specs = [pl.BlockSpec((PROJ_TILE, n), lambda i: (i, 0)) for n in widths]
    out_specs += [pl.BlockSpec((None, n, PROJ_TILE),
                               lambda i: (i // tiles_per_batch, 0, i % tiles_per_batch))
                  for n in t_widths]
    out_shape = [jax.ShapeDtypeStruct((rows, n), dt) for n, dt in zip(widths, dtypes)]
    out_shape += [jax.ShapeDtypeStruct((nb, n, LP), BF16) for n in t_widths]
    return pl.pallas_call(
        functools.partial(_proj_body, len(t_widths)),
        grid=(rows // PROJ_TILE,),
        in_specs=[pl.BlockSpec((PROJ_TILE, D_MODEL), lambda i: (i, 0)),
                  _const_spec(w.shape), _const_spec(wt.shape)],
        out_specs=out_specs,
        out_shape=out_shape,
        compiler_params=pltpu.CompilerParams(
            dimension_semantics=("parallel",), vmem_limit_bytes=VMEM_LIMIT),
        name=name,
    )(hf, w, wt)


def _outproj_ln_body(n_in, h_ref, *refs):
    a_refs = refs[:n_in]
    w_refs = refs[n_in:2 * n_in]
    g_ref, b_ref, o_ref = refs[2 * n_in:]
    m = jnp.dot(a_refs[0][...], w_refs[0][...], preferred_element_type=F32)
    for a_ref, w_ref in zip(a_refs[1:], w_refs[1:]):
        m = m + jnp.dot(a_ref[...], w_ref[...], preferred_element_type=F32)
    o_ref[...] = _layer_norm(ALPHA * h_ref[...] + m, g_ref[...], b_ref[...])


def _outproj_ln(hf, acts, ws, g, b, name):
    rows = hf.shape[0]
    n_in = len(acts)
    in_specs = [pl.BlockSpec((ROW_TILE, D_MODEL), lambda i: (i, 0))]
    in_specs += [pl.BlockSpec((ROW_TILE, a.shape[1]), lambda i: (i, 0)) for a in acts]
    in_specs += [_const_spec(w.shape) for w in ws]
    in_specs += [_const_spec((1, D_MODEL)), _const_spec((1, D_MODEL))]
    return pl.pallas_call(
        functools.partial(_outproj_ln_body, n_in),
        grid=(rows // ROW_TILE,),
        in_specs=in_specs,
        out_specs=pl.BlockSpec((ROW_TILE, D_MODEL), lambda i: (i, 0)),
        out_shape=jax.ShapeDtypeStruct((rows, D_MODEL), F32),
        compiler_params=pltpu.CompilerParams(
            dimension_semantics=("parallel",), vmem_limit_bytes=VMEM_LIMIT),
        name=name,
    )(hf, *acts, *ws, g, b)


def _split_dot(x, e_ref):
    hi = x.astype(BF16)
    lo = (x - hi.astype(F32)).astype(BF16)
    e = e_ref[...]
    return (jnp.dot(hi, e, preferred_element_type=F32)
            + jnp.dot(lo, e, preferred_element_type=F32))


def _ssd_body(xbc_ref, z_ref, small_ref, convw_ref, convb_ref, sbias_ref, alog_ref,
              dskip_ref, normw_ref, expand_ref, y_ref, c_ref,
              ext_ref, state_ref, carry_ref):
    c = pl.program_id(1)

    @pl.when(c == 0)
    def _():
        ext_ref[0:CONV_HALO, :] = jnp.zeros((CONV_HALO, SSD_CONV_CH), F32)
        state_ref[...] = jnp.zeros_like(state_ref)
        carry_ref[...] = jnp.zeros_like(carry_ref)

    row = lax.broadcasted_iota(jnp.int32, (CHUNK, 1), 0)
    valid = (c * CHUNK + row) >= PADL

    ext_ref[CONV_HALO:CONV_HALO + CHUNK, :] = jnp.where(valid, xbc_ref[...], 0.0)
    ext = ext_ref[...]
    conv = convb_ref[...] + convw_ref[SSD_CONV - 1:SSD_CONV, :] * ext[CONV_HALO:, :]
    for k in range(SSD_CONV - 1):
        shifted = pltpu.roll(ext, SSD_CONV - 1 - k, 0)[CONV_HALO:, :]
        conv = conv + convw_ref[k:k + 1, :] * shifted
    ext_ref[0:CONV_HALO, :] = ext_ref[CHUNK:CHUNK + CONV_HALO, :]
    xc = jnp.where(valid, _silu(conv), 0.0)
    xs = xc[:, :SSD_D_INNER]
    bm = xc[:, SSD_D_INNER:SSD_D_INNER + SSD_GROUPS * SSD_STATE].astype(BF16)
    cm = xc[:, SSD_D_INNER + SSD_GROUPS * SSD_STATE:].astype(BF16)

    lane = lax.broadcasted_iota(jnp.int32, (CHUNK, SMALL_W), 1)
    is_dt = lane < SSD_HEADS
    is_f = (lane >= SSD_HEADS) & (lane < SSD_HEADS + FOX_HEADS)
    v = small_ref[...] + sbias_ref[...]
    dt = jnp.where(valid & is_dt, _softplus(v), 0.0)
    log_f = jnp.where(valid & is_f, -_softplus(-v), 0.0)
    neg_a = -jnp.exp(alog_ref[...])
    steps = jnp.where(is_dt, dt * neg_a, log_f)
    r_i = lax.broadcasted_iota(jnp.int32, (CHUNK, CHUNK), 0)
    c_i = lax.broadcasted_iota(jnp.int32, (CHUNK, CHUNK), 1)
    causal = r_i >= c_i
    tril = jnp.where(causal, 1.0, 0.0).astype(F32)
    cum = jnp.dot(tril, steps, preferred_element_type=F32,
                  precision=lax.Precision.HIGHEST)
    c_total = jnp.where(is_f, cum + carry_ref[...], 0.0)
    c_ref[...] = c_total
    carry_ref[...] = c_total[CHUNK - 1:CHUNK, :]

    a_last = cum[CHUNK - 1:CHUNK, :]
    dt_x = _split_dot(dt, expand_ref)
    ea_x = _split_dot(jnp.exp(cum), expand_ref)
    de_x = _split_dot(jnp.exp(a_last - cum), expand_ref)
    x_dt = xs * dt_x
    xb = x_dt.astype(BF16)
    xe = (x_dt * de_x).astype(BF16)
    cum_t = cum.T
    lane_p = lax.broadcasted_iota(jnp.int32, (CHUNK, 2 * SSD_HEAD_DIM), 1)
    first_half = lane_p < SSD_HEAD_DIM

    y_groups = []
    for g in range(SSD_GROUPS):
        gs = slice(g * SSD_GROUP_W, (g + 1) * SSD_GROUP_W)
        bg = bm[:, g * SSD_STATE:(g + 1) * SSD_STATE]
        cg = cm[:, g * SSD_STATE:(g + 1) * SSD_STATE]
        cb = lax.dot_general(cg, bg, (((1,), (1,)), ((), ())),
                             preferred_element_type=F32)
        pair_out = []
        for pr in range(SSD_GROUP_W // (2 * SSD_HEAD_DIM)):
            col0 = g * SSD_GROUP_W + pr * 2 * SSD_HEAD_DIM
            x_pair = xb[:, col0:col0 + 2 * SSD_HEAD_DIM]
            ys = []
            for j in range(2):
                hd = col0 // SSD_HEAD_DIM + j
                diff = cum[:, hd:hd + 1] - cum_t[hd:hd + 1, :]
                decay = jnp.exp(jnp.where(causal, diff, -jnp.inf))
                mat = (cb * decay).astype(BF16)
                ys.append(jnp.dot(mat, x_pair, preferred_element_type=F32))
            pair_out.append(jnp.where(first_half, ys[0], ys[1]))
        y_diag = jnp.concatenate(pair_out, axis=1)
        st = state_ref[g]
        y_off = jnp.dot(cg, st.astype(BF16), preferred_element_type=F32) * ea_x[:, gs]
        new = lax.dot_general(bg, xe[:, gs], (((0,), (0,)), ((), ())),
                              preferred_element_type=F32)
        state_ref[g] = st * ea_x[CHUNK - 1:CHUNK, gs] + new
        yg = y_diag + y_off + dskip_ref[:, gs] * xs[:, gs]
        yg = yg * _silu(z_ref[:, gs])
        ms = jnp.mean(yg * yg, axis=-1, keepdims=True)
        y_groups.append(yg * lax.rsqrt(ms + RMS_EPS) * normw_ref[:, gs])
    y_ref[...] = jnp.concatenate(y_groups, axis=1).astype(y_ref.dtype)


def _ssd(xbc, z, small, convw, convb, sbias, alog, dskip, normw, expand):
    nb = xbc.shape[0]
    row_spec = lambda w: pl.BlockSpec((None, CHUNK, w), lambda bi, ci: (bi, ci, 0))
    return pl.pallas_call(
        _ssd_body,
        grid=(nb, LP // CHUNK),
        in_specs=[row_spec(SSD_CONV_CH), row_spec(SSD_D_INNER), row_spec(SMALL_W),
                  _const_spec(convw.shape), _const_spec(convb.shape),
                  _const_spec(sbias.shape), _const_spec(alog.shape),
                  _const_spec(dskip.shape), _const_spec(normw.shape),
                  _const_spec(expand.shape)],
        out_specs=[row_spec(SSD_D_INNER), row_spec(SMALL_W)],
        out_shape=[jax.ShapeDtypeStruct((nb, LP, SSD_D_INNER), BF16),
                   jax.ShapeDtypeStruct((nb, LP, SMALL_W), F32)],
        scratch_shapes=[pltpu.VMEM((CHUNK + CONV_HALO, SSD_CONV_CH), F32),
                        pltpu.VMEM((SSD_GROUPS, SSD_STATE, SSD_GROUP_W), F32),
                        pltpu.VMEM((1, SMALL_W), F32)],
        compiler_params=pltpu.CompilerParams(
            dimension_semantics=("arbitrary", "arbitrary"), vmem_limit_bytes=VMEM_LIMIT),
        name="ssd",
    )(xbc, z, small, convw, convb, sbias, alog, dskip, normw, expand)


N_TK_CHUNKS = ATT_TK // LANES
N_TQ_CHUNKS = ATT_TQ // LANES
assert ATT_TQ == ATT_TK and PADL <= ATT_TK
LOG2E = math.log2(math.e)
Q_SCALE = LOG2E * HEAD_DIM ** -0.5


def _chunk_max(chunk):
    part = chunk(0)
    for c in range(1, N_TK_CHUNKS):
        part = jnp.maximum(part, chunk(c))
    return part


def _softmax_update(chunk, part, v_aug, m_ref, acc_ref, idx):
    m_old = m_ref[idx]
    m_new = jnp.maximum(m_old, jnp.max(part, axis=-1, keepdims=True))
    alpha = jnp.exp2(m_old - m_new)
    p = jnp.concatenate([jnp.exp2(chunk(c) - m_new).astype(BF16)
                         for c in range(N_TK_CHUNKS)], axis=1)
    pv = jnp.dot(p, v_aug, preferred_element_type=F32)
    for w in range(acc_ref.shape[-1] // LANES):
        ws = slice(w * LANES, (w + 1) * LANES)
        acc_ref[idx, :, ws] = alpha * acc_ref[idx, :, ws] + pv[:, ws]
    m_ref[idx] = m_new


def _causal_chunk(c):
    qpos = lax.broadcasted_iota(jnp.int32, (ATT_TQ, LANES), 0)
    kpos = c * LANES + lax.broadcasted_iota(jnp.int32, (ATT_TQ, LANES), 1)
    return kpos <= qpos


def _not_padding_chunk(kb, c):
    kpos = c * LANES + lax.broadcasted_iota(jnp.int32, (ATT_TQ, LANES), 1)
    return (kpos >= PADL) | (kb > 0)


def _init_softmax_state(m_ref, acc_ref):
    m_ref[...] = jnp.full(m_ref.shape, NEG, F32)
    acc_ref[...] = jnp.zeros_like(acc_ref)


def _split_halves(q):
    lane = lax.broadcasted_iota(jnp.int32, q.shape, 1)
    first_half = lane < HEAD_DIM
    zero = jnp.zeros_like(q)
    return first_half, (jnp.where(first_half, q, zero), jnp.where(first_half, zero, q))


def _run_key_tiles(qb, scores, step, buf_a, buf_b):
    odd = qb % 2

    @pl.when(odd == 0)
    def _():
        scores(0, buf_a)

    @pl.when(odd == 1)
    def _():
        scores(0, buf_b)
        step(0, buf_b, buf_a)

    def two_below_diagonal(i, carry):
        kb = odd + 2 * i
        step(kb, buf_a, buf_b)
        step(kb + 1, buf_b, buf_a)
        return carry

    lax.fori_loop(0, qb // 2, two_below_diagonal, 0)
    step(qb, buf_a, None)


def _fox_body(q_ref, kt_ref, v_ref, ck_ref, o_ref, m_ref, acc_ref,
              sa_ref, pa_ref, sb_ref, pb_ref):
    qb = pl.program_id(2)
    first_half, q_heads = _split_halves(q_ref[...])
    _init_softmax_state(m_ref, acc_ref)
    buf_a, buf_b = (sa_ref, pa_ref), (sb_ref, pb_ref)

    def scores(kb, buf):
        s_dst, part_dst = buf
        start = pl.multiple_of(kb * ATT_TK, ATT_TK)
        kt = kt_ref[:, pl.ds(start, ATT_TK)]
        for j in range(2):
            s = jnp.dot(q_heads[j], kt, preferred_element_type=F32)
            part = None
            for c in range(N_TK_CHUNKS):
                key0 = pl.multiple_of(start + c * LANES, LANES)
                sc = s[:, c * LANES:(c + 1) * LANES] - ck_ref[j:j + 1, pl.ds(key0, LANES)]
                s_dst[j, :, c * LANES:(c + 1) * LANES] = sc
                part = sc if part is None else jnp.maximum(part, sc)
            part_dst[j] = part

    def step(kb, src, nxt):
        s_src, part_src = src
        start = pl.multiple_of(kb * ATT_TK, ATT_TK)
        if nxt is not None:
            scores(kb + 1, nxt)
        for j in range(2):
            def chunk(c, j=j):
                sc = s_src[j, :, c * LANES:(c + 1) * LANES]
                if nxt is None:
                    sc = jnp.where(_causal_chunk(c), sc, NEG)
                return sc
            part = _chunk_max(chunk) if nxt is None else part_src[j]
            v_aug = v_ref[pl.ds(start, ATT_TK), j * LANES:(j + 1) * LANES]
            _softmax_update(chunk, part, v_aug, m_ref, acc_ref, j)

    _run_key_tiles(qb, scores, step, buf_a, buf_b)

    acc0, acc1 = acc_ref[0], acc_ref[1]
    num = jnp.where(first_half, acc0, acc1)
    den = pltpu.roll(jnp.where(first_half, acc1, acc0), HEAD_DIM, axis=1)
    o_ref[...] = (num / den).astype(o_ref.dtype)


def _fox_attention(q, kt, v_aug, ck):
    nb = q.shape[0]
    pairs = FOX_HEADS // 2
    return pl.pallas_call(
        _fox_body,
        grid=(nb, pairs, LP // ATT_TQ),
        in_specs=[
            pl.BlockSpec((None, ATT_TQ, LANES), lambda bi, hp, qi: (bi, qi, hp)),
            pl.BlockSpec((None, LANES, LP), lambda bi, hp, qi: (bi, hp, 0)),
            pl.BlockSpec((None, LP, 2 * LANES), lambda bi, hp, qi: (bi, 0, hp)),
            pl.BlockSpec((None, None, 2, LP), lambda bi, hp, qi: (bi, hp, 0, 0)),
        ],
        out_specs=pl.BlockSpec((None, ATT_TQ, LANES), lambda bi, hp, qi: (bi, qi, hp)),
        out_shape=jax.ShapeDtypeStruct((nb, LP, FOX_WIDTH), BF16),
        scratch_shapes=[pltpu.VMEM((2, ATT_TQ, LANES), F32),
                        pltpu.VMEM((2, ATT_TQ, LANES), F32)]
                       + [pltpu.VMEM((2, ATT_TQ, ATT_TK), F32),
                          pltpu.VMEM((2, ATT_TQ, LANES), F32)] * 2,
        compiler_params=pltpu.CompilerParams(
            dimension_semantics=("parallel", "parallel", "arbitrary"),
            vmem_limit_bytes=VMEM_LIMIT),
        name="fox_attention",
    )(q, kt, v_aug, ck)


def _bias_tiles(bias_ref):
    rows = jnp.broadcast_to(bias_ref[...], (CHUNK, 2 * CHUNK))
    rolled = pltpu.roll(rows, 0, 1, stride=1, stride_axis=0)
    return rolled[:, :CHUNK], rolled[:, CHUNK:]


def _bias_chunk(tiles, key_chunk):
    picked = []
    for a in range(N_TQ_CHUNKS):
        sub = a - key_chunk
        picked.append(tiles[sub] if sub in (0, 1) else None)
    if all(t is None for t in picked):
        return None
    zeros = jnp.zeros((CHUNK, CHUNK), F32)
    return jnp.concatenate([zeros if t is None else t for t in picked], axis=0)


def _diff_body(lambda_init, q_ref, kt_ref, v_ref, bias_ref, lam_ref, subln_ref, o_ref,
               m_ref, acc_ref, tiles_ref, sa_ref, pa_ref, sb_ref, pb_ref):
    qb = pl.program_id(2)
    _, q_parts = _split_halves(q_ref[...])
    _init_softmax_state(m_ref, acc_ref)
    buf_a, buf_b = (sa_ref, pa_ref), (sb_ref, pb_ref)
    tile0, tile1 = _bias_tiles(bias_ref)
    tiles_ref[0] = tile0
    tiles_ref[1] = tile1
    last = N_TK_CHUNKS - 1

    def scores(kb, buf):
        s_dst, part_dst = buf
        start = pl.multiple_of(kb * ATT_TK, ATT_TK)
        kt = kt_ref[:, pl.ds(start, ATT_TK)]
        for j in range(2):
            s = jnp.dot(q_parts[j], kt, preferred_element_type=F32)
            part = None
            for c in range(N_TK_CHUNKS):
                sc = s[:, c * LANES:(c + 1) * LANES]
                if c * LANES < PADL:
                    sc = jnp.where(_not_padding_chunk(kb, c), sc, NEG)
                s_dst[j, :, c * LANES:(c + 1) * LANES] = sc
                part = sc if part is None else jnp.maximum(part, sc)
            part_dst[j] = part

    def step(kb, src, nxt):
        s_src, part_src = src
        start = pl.multiple_of(kb * ATT_TK, ATT_TK)
        diagonal = nxt is None
        if diagonal:
            tiles = (tiles_ref[0], tiles_ref[1])
            biases = [_bias_chunk(tiles, c) for c in range(N_TK_CHUNKS)]
        else:
            scores(kb + 1, nxt)
            corner = jnp.where(kb == qb - 1, 1.0, 0.0) * tiles_ref[1]
        v_aug = v_ref[pl.ds(start, ATT_TK), :]
        for j in range(2):
            def chunk(c, j=j):
                sc = s_src[j, :, c * LANES:(c + 1) * LANES]
                if diagonal:
                    if biases[c] is not None:
                        sc = sc + biases[c]
                    sc = jnp.where(_causal_chunk(c), sc, NEG)
                elif c == last:
                    sc = jnp.concatenate([sc[:CHUNK] + corner, sc[CHUNK:]], axis=0)
                return sc
            if diagonal:
                part = _chunk_max(chunk)
            else:
                top = s_src[j, :CHUNK, last * LANES:] + corner
                for c in range(last):
                    top = jnp.maximum(top, s_src[j, :CHUNK, c * LANES:(c + 1) * LANES])
                part = jnp.concatenate([top, part_src[j, CHUNK:, :]], axis=0)
            _softmax_update(chunk, part, v_aug, m_ref, acc_ref, j)

    _run_key_tiles(qb, scores, step, buf_a, buf_b)

    lam1 = jnp.exp(jnp.sum(lam_ref[0:1, :] * lam_ref[1:2, :], axis=-1, keepdims=True))
    lam2 = jnp.exp(jnp.sum(lam_ref[2:3, :] * lam_ref[3:4, :], axis=-1, keepdims=True))
    lam = lam1 - lam2 + lambda_init
    o = (acc_ref[0, :, :LANES] / acc_ref[0, :, LANES:]
         - lam * (acc_ref[1, :, :LANES] / acc_ref[1, :, LANES:]))
    ms = jnp.mean(o * o, axis=-1, keepdims=True)
    o = o * lax.rsqrt(ms + RMS_EPS) * subln_ref[...] * (1.0 - lambda_init)
    o_ref[...] = o.astype(o_ref.dtype)


def _diff_attention(q, kt, v_aug, bias_rows, lam_rows, subln, lambda_init):
    nb = q.shape[0]
    return pl.pallas_call(
        functools.partial(_diff_body, lambda_init),
        grid=(nb, DIFF_HEADS, LP // ATT_TQ),
        in_specs=[
            pl.BlockSpec((None, ATT_TQ, LANES), lambda bi, hd, qi: (bi, qi, hd)),
            pl.BlockSpec((None, LANES, LP), lambda bi, hd, qi: (bi, hd, 0)),
            pl.BlockSpec((None, LP, 2 * LANES), lambda bi, hd, qi: (bi, 0, hd)),
            pl.BlockSpec((None, 1, 2 * CHUNK), lambda bi, hd, qi: (hd, 0, 0)),
            _const_spec(lam_rows.shape),
            _const_spec(subln.shape),
        ],
        out_specs=pl.BlockSpec((None, ATT_TQ, LANES), lambda bi, hd, qi: (bi, qi, hd)),
        out_shape=jax.ShapeDtypeStruct((nb, LP, DIFF_HEADS * LANES), BF16),
        scratch_shapes=[pltpu.VMEM((2, ATT_TQ, LANES), F32),
                        pltpu.VMEM((2, ATT_TQ, 2 * LANES), F32),
                        pltpu.VMEM((2, CHUNK, CHUNK), F32)]
                       + [pltpu.VMEM((2, ATT_TQ, ATT_TK), F32),
                          pltpu.VMEM((2, ATT_TQ, LANES), F32)] * 2,
        compiler_params=pltpu.CompilerParams(
            dimension_semantics=("parallel", "parallel", "arbitrary"),
            vmem_limit_bytes=VMEM_LIMIT),
        name="diff_attention",
    )(q, kt, v_aug, bias_rows, lam_rows, subln)


ONES_ROWS = 16


def _value_rows(vt_ref, row0, n_rows, start):
    vt = vt_ref[row0:row0 + n_rows, pl.ds(start, ATT_TK)]
    return jnp.concatenate([vt, jnp.ones((ONES_ROWS, ATT_TK), BF16)], axis=0)


def _softmax_update_t(s_t, col_max, vt_aug, m_ref, acc_ref, idx):
    m_old = m_ref[idx]
    m_new = jnp.maximum(m_old, col_max)
    alpha = jnp.exp2(m_old - m_new)
    p_t = jnp.exp2(s_t - m_new).astype(BF16)
    pv = jnp.dot(vt_aug, p_t, preferred_element_type=F32)
    acc_ref[idx] = alpha * acc_ref[idx] + pv
    m_ref[idx] = m_new


def _causal_t():
    kpos = lax.broadcasted_iota(jnp.int32, (ATT_TK, ATT_TQ), 0)
    qpos = lax.broadcasted_iota(jnp.int32, (ATT_TK, ATT_TQ), 1)
    return kpos <= qpos


FOX_NEG_ROWS = 16
FOX_CK_TERMS = 3


def _fox_body_t(qt_ref, ka_ref, vt_ref, o_ref, m_ref, acc_ref, sa_ref, pa_ref, sb_ref, pb_ref):
    qb = pl.program_id(2)
    _init_softmax_state(m_ref, acc_ref)
    buf_a, buf_b = (sa_ref, pa_ref), (sb_ref, pb_ref)
    row = lax.broadcasted_iota(jnp.int32, (FOX_NEG_ROWS, ATT_TQ), 0)
    neg_rows = jnp.where(row < FOX_CK_TERMS, -1.0, 0.0).astype(BF16)
    zero_rows = jnp.zeros((LANES - HEAD_DIM - FOX_NEG_ROWS, ATT_TQ), BF16)
    q_heads = [jnp.concatenate([qt_ref[j * HEAD_DIM:(j + 1) * HEAD_DIM, :], neg_rows, zero_rows],
                               axis=0) for j in range(2)]

    def scores(kb, buf):
        s_dst, max_dst = buf
        start = pl.multiple_of(kb * ATT_TK, ATT_TK)
        for j in range(2):
            k_aug = ka_ref[pl.ds(start, ATT_TK), j * LANES:(j + 1) * LANES]
            s_t = jnp.dot(k_aug, q_heads[j], preferred_element_type=F32)
            s_dst[j] = s_t
            max_dst[j] = jnp.max(s_t, axis=0, keepdims=True)

    def step(kb, src, nxt):
        s_src, max_src = src
        start = pl.multiple_of(kb * ATT_TK, ATT_TK)
        if nxt is not None:
            scores(kb + 1, nxt)
        for j in range(2):
            s_t = s_src[j]
            col_max = max_src[j]
            if nxt is None:
                s_t = jnp.where(_causal_t(), s_t, NEG)
                col_max = jnp.max(s_t, axis=0, keepdims=True)
            vt_aug = _value_rows(vt_ref, j * HEAD_DIM, HEAD_DIM, start)
            _softmax_update_t(s_t, col_max, vt_aug, m_ref, acc_ref, j)

    _run_key_tiles(qb, scores, step, buf_a, buf_b)

    o_t = jnp.concatenate(
        [acc_ref[j, :HEAD_DIM, :] / acc_ref[j, HEAD_DIM:HEAD_DIM + 1, :] for j in range(2)], axis=0)
    o_ref[...] = o_t.T.astype(o_ref.dtype)


def _fox_attention_t(qt, k_aug, vt):
    nb = qt.shape[0]
    pairs = FOX_HEADS // 2
    acc_rows = HEAD_DIM + ONES_ROWS
    return pl.pallas_call(
        _fox_body_t,
        grid=(nb, pairs, LP // ATT_TQ),
        in_specs=[
            pl.BlockSpec((None, LANES, ATT_TQ), lambda bi, hp, qi: (bi, hp, qi)),
            pl.BlockSpec((None, LP, 2 * LANES), lambda bi, hp, qi: (bi, 0, hp)),
            pl.BlockSpec((None, LANES, LP), lambda bi, hp, qi: (bi, hp, 0)),
        ],
        out_specs=pl.BlockSpec((None, ATT_TQ, LANES), lambda bi, hp, qi: (bi, qi, hp)),
        out_shape=jax.ShapeDtypeStruct((nb, LP, FOX_WIDTH), BF16),
        scratch_shapes=[pltpu.VMEM((2, 1, ATT_TQ), F32),
                        pltpu.VMEM((2, acc_rows, ATT_TQ), F32)]
                       + [pltpu.VMEM((2, ATT_TK, ATT_TQ), F32),
                          pltpu.VMEM((2, 1, ATT_TQ), F32)] * 2,
        compiler_params=pltpu.CompilerParams(
            dimension_semantics=("parallel", "parallel", "arbitrary"),
            vmem_limit_bytes=VMEM_LIMIT),
        name="fox_attention",
    )(qt, k_aug, vt)


def _bias_tiles_t(bias_ref):
    rows = jnp.broadcast_to(bias_ref[...], (CHUNK, 2 * CHUNK))
    rolled = pltpu.roll(rows, 0, 1, stride=1, stride_axis=0)
    return rolled[:, :CHUNK], rolled[:, CHUNK:]


def _diagonal_bias_t(tiles):
    zeros = jnp.zeros((CHUNK, CHUNK), F32)
    rows = []
    for a in range(N_TK_CHUNKS):
        rows.append(jnp.concatenate(
            [tiles[b - a] if b - a in (0, 1) else zeros for b in range(N_TQ_CHUNKS)], axis=1))
    return jnp.concatenate(rows, axis=0)


def _diff_body_t(lambda_init, qt_ref, k_ref, vt_ref, bias_ref, lam_ref, subln_ref, o_ref,
                 m_ref, acc_ref, tiles_ref, sa_ref, pa_ref, sb_ref, pb_ref):
    qb = pl.program_id(2)
    _init_softmax_state(m_ref, acc_ref)
    buf_a, buf_b = (sa_ref, pa_ref), (sb_ref, pb_ref)
    tile0, tile1 = _bias_tiles_t(bias_ref)
    tiles_ref[0] = tile0
    tiles_ref[1] = tile1
    zero_rows = jnp.zeros((HEAD_DIM, ATT_TQ), BF16)
    q_parts = [jnp.concatenate([qt_ref[:HEAD_DIM, :], zero_rows], axis=0),
               jnp.concatenate([zero_rows, qt_ref[HEAD_DIM:, :]], axis=0)]
    pad_rows = -(-PADL // CHUNK) * CHUNK
    corner0 = ATT_TK - CHUNK

    def scores(kb, buf):
        s_dst, max_dst = buf
        start = pl.multiple_of(kb * ATT_TK, ATT_TK)
        k_rows = k_ref[pl.ds(start, ATT_TK), :]
        kpos = lax.broadcasted_iota(jnp.int32, (pad_rows, ATT_TQ), 0)
        not_padding = (kpos >= PADL) | (kb > 0)
        for j in range(2):
            s_t = jnp.dot(k_rows, q_parts[j], preferred_element_type=F32)
            s_t = jnp.concatenate(
                [jnp.where(not_padding, s_t[:pad_rows], NEG), s_t[pad_rows:]], axis=0)
            s_dst[j] = s_t
            max_dst[j] = jnp.max(s_t, axis=0, keepdims=True)

    def step(kb, src, nxt):
        s_src, max_src = src
        start = pl.multiple_of(kb * ATT_TK, ATT_TK)
        diagonal = nxt is None
        if diagonal:
            bias = _diagonal_bias_t((tiles_ref[0], tiles_ref[1]))
            visible = _causal_t()
        else:
            scores(kb + 1, nxt)
            corner = jnp.where(kb == qb - 1, 1.0, 0.0) * tiles_ref[1]
        vt_aug = _value_rows(vt_ref, 0, LANES, start)
        for j in range(2):
            s_t = s_src[j]
            if diagonal:
                s_t = jnp.where(visible, s_t + bias, NEG)
                col_max = jnp.max(s_t, axis=0, keepdims=True)
            else:
                near = s_t[corner0:, :CHUNK] + corner
                s_t = jnp.concatenate(
                    [s_t[:corner0],
                     jnp.concatenate([near, s_t[corner0:, CHUNK:]], axis=1)], axis=0)
                first = jnp.maximum(jnp.max(s_t[:corner0, :CHUNK], axis=0, keepdims=True),
                                    jnp.max(near, axis=0, keepdims=True))
                col_max = jnp.concatenate([first, max_src[j][:, CHUNK:]], axis=1)
            _softmax_update_t(s_t, col_max, vt_aug, m_ref, acc_ref, j)

    _run_key_tiles(qb, scores, step, buf_a, buf_b)

    lam1 = jnp.exp(jnp.sum(lam_ref[0:1, :] * lam_ref[1:2, :], axis=-1, keepdims=True))
    lam2 = jnp.exp(jnp.sum(lam_ref[2:3, :] * lam_ref[3:4, :], axis=-1, keepdims=True))
    lam = lam1 - lam2 + lambda_init
    o_t = (acc_ref[0, :LANES, :] / acc_ref[0, LANES:LANES + 1, :]
           - lam * (acc_ref[1, :LANES, :] / acc_ref[1, LANES:LANES + 1, :]))
    o = o_t.T
    ms = jnp.mean(o * o, axis=-1, keepdims=True)
    o = o * lax.rsqrt(ms + RMS_EPS) * subln_ref[...] * (1.0 - lambda_init)
    o_ref[...] = o.astype(o_ref.dtype)


def _diff_attention_t(qt, k, vt, bias_rows, lam_rows, subln, lambda_init):
    nb = qt.shape[0]
    acc_rows = LANES + ONES_ROWS
    return pl.pallas_call(
        functools.partial(_diff_body_t, lambda_init),
        grid=(nb, DIFF_HEADS, LP // ATT_TQ),
        in_specs=[
            pl.BlockSpec((None, LANES, ATT_TQ), lambda bi, hd, qi: (bi, hd, qi)),
            pl.BlockSpec((None, LP, LANES), lambda bi, hd, qi: (bi, 0, hd)),
            pl.BlockSpec((None, LANES, LP), lambda bi, hd, qi: (bi, hd, 0)),
            pl.BlockSpec((None, 1, 2 * CHUNK), lambda bi, hd, qi: (hd, 0, 0)),
            _const_spec(lam_rows.shape),
            _const_spec(subln.shape),
        ],
        out_specs=pl.BlockSpec((None, ATT_TQ, LANES), lambda bi, hd, qi: (bi, qi, hd)),
        out_shape=jax.ShapeDtypeStruct((nb, LP, DIFF_HEADS * LANES), BF16),
        scratch_shapes=[pltpu.VMEM((2, 1, ATT_TQ), F32),
                        pltpu.VMEM((2, acc_rows, ATT_TQ), F32),
                        pltpu.VMEM((2, CHUNK, CHUNK), F32)]
                       + [pltpu.VMEM((2, ATT_TK, ATT_TQ), F32),
                          pltpu.VMEM((2, 1, ATT_TQ), F32)] * 2,
        compiler_params=pltpu.CompilerParams(
            dimension_semantics=("parallel", "parallel", "arbitrary"),
            vmem_limit_bytes=VMEM_LIMIT),
        name="diff_attention",
    )(qt, k, vt, bias_rows, lam_rows, subln)


def _t5_bucket(n):
    max_exact = N_BUCKETS // 2
    nf = jnp.maximum(n, 1).astype(F32)
    large = max_exact + (jnp.log(nf / max_exact) / math.log(128 / max_exact)
                         * (N_BUCKETS - max_exact)).astype(jnp.int32)
    large = jnp.minimum(large, N_BUCKETS - 1)
    return jnp.where(n < max_exact, n, large)


def _relative_bias_rows(rel_table):
    dist = jnp.arange(2 * CHUNK)
    by_dist = LOG2E * (rel_table[_t5_bucket(dist)] - rel_table[N_BUCKETS - 1])
    return by_dist.T[:, None, :].astype(F32)


def _pad_lanes(vec, width=LANES):
    return jnp.pad(vec, (0, width - vec.shape[0]))[None, :].astype(F32)


def kernel(x, meta_tokens, ln_gain, ln_bias, ffn1_w_gate, ffn1_w_up, ffn1_w_down, ffn2_w_gate, ffn2_w_up, ffn2_w_down, even_w_in, even_conv_w, even_conv_b, ssd_dt_bias, ssd_a_log, ssd_d_skip, ssd_norm_w, fox_f_bias, even_w_out, diff_w_qkv, diff_lambda_q1, diff_lambda_k1, diff_lambda_q2, diff_lambda_k2, diff_subln_w, diff_w_o, rel_bias_table):
    nb = x.shape[0]
    lead = jnp.concatenate([jnp.zeros((PADL, D_MODEL), x.dtype), meta_tokens.astype(x.dtype)], axis=0)

    def ln_params(l, i):
        return ln_gain[l, i][None, :], ln_bias[l, i][None, :]

    def ffn_weights(wg, wu, wd, l):
        return wg[l].astype(BF16), wu[l].astype(BF16), wd[l].astype(BF16)

    hf = _ffn_ln_first(x, lead, *ffn_weights(ffn1_w_gate, ffn1_w_up, ffn1_w_down, 0),
                       *ln_params(0, 0))

    w_in = even_w_in[0]
    o_z, o_xbc = 0, SSD_D_INNER
    o_dt = o_xbc + SSD_CONV_CH
    o_q = o_dt + SSD_HEADS
    o_k, o_v = o_q + FOX_WIDTH, o_q + 2 * FOX_WIDTH
    o_f = o_q + 3 * FOX_WIDTH
    w_small = jnp.concatenate(
        [w_in[:, o_dt:o_q], w_in[:, o_f:],
         jnp.zeros((D_MODEL, SMALL_W - SSD_HEADS - FOX_HEADS), w_in.dtype)], axis=1)
    w_even = jnp.concatenate([w_in[:, o_z:o_dt], w_in[:, o_k:o_v], w_small], axis=1).astype(BF16)
    wt_even = jnp.concatenate([w_in[:, o_q:o_k] * Q_SCALE, w_in[:, o_v:o_f]], axis=1).T.astype(BF16)
    z, xbc, k, small, qt, vt = _proj(
        hf, w_even, wt_even, (SSD_D_INNER, SSD_CONV_CH, FOX_WIDTH, SMALL_W),
        (F32, F32, BF16, F32), (FOX_WIDTH, FOX_WIDTH), nb, "even_in_proj")

    sbias = _pad_lanes(jnp.concatenate([ssd_dt_bias[0], fox_f_bias[0]]))
    alog = _pad_lanes(ssd_a_log[0])
    dskip = jnp.repeat(ssd_d_skip[0], SSD_HEAD_DIM)[None, :].astype(F32)
    expand = np.zeros((SMALL_W, SSD_D_INNER), np.float32)
    expand[np.arange(SSD_D_INNER) // SSD_HEAD_DIM, np.arange(SSD_D_INNER)] = 1.0
    y, cfull = _ssd(xbc.reshape(nb, LP, SSD_CONV_CH), z.reshape(nb, LP, SSD_D_INNER),
                    small.reshape(nb, LP, SMALL_W), even_conv_w[0], even_conv_b[0][None, :],
                    sbias, alog, dskip, ssd_norm_w[0][None, :], jnp.asarray(expand, BF16))
    ck = LOG2E * cfull[:, :, SSD_HEADS:SSD_HEADS + FOX_HEADS]
    ck = jnp.where(jnp.arange(LP)[None, :, None] < PADL, -NEG, ck)
    terms, rest = [], ck
    for _ in range(FOX_CK_TERMS):
        top = lax.bitcast_convert_type(
            lax.bitcast_convert_type(rest, jnp.uint32) & jnp.uint32(0xFFFF0000), F32)
        terms.append(top.astype(BF16))
        rest = rest - top
    k_aug = jnp.concatenate(
        [k.reshape(nb, LP, FOX_HEADS, HEAD_DIM), jnp.stack(terms, axis=-1),
         jnp.zeros((nb, LP, FOX_HEADS, LANES - HEAD_DIM - FOX_CK_TERMS), BF16)], axis=-1)
    o = _fox_attention_t(qt, k_aug.reshape(nb, LP, FOX_HEADS * LANES), vt)
    w_out = even_w_out[0].astype(BF16)
    hf = _outproj_ln(hf, [y.reshape(nb * LP, SSD_D_INNER), o.reshape(nb * LP, FOX_WIDTH)],
                     [w_out[:SSD_D_INNER], w_out[SSD_D_INNER:]], *ln_params(0, 1),
                     name="even_out_proj_ln")
    hf = _ffn_ln(hf, *ffn_weights(ffn2_w_gate, ffn2_w_up, ffn2_w_down, 0), *ln_params(0, 2))

    hf = _ffn_ln(hf, *ffn_weights(ffn1_w_gate, ffn1_w_up, ffn1_w_down, 1), *ln_params(1, 0))
    qw = DIFF_HEADS * 2 * HEAD_DIM
    w_qkv = diff_w_qkv[0]
    wt_diff = jnp.concatenate([w_qkv[:, :qw] * Q_SCALE, w_qkv[:, 2 * qw:]], axis=1).T.astype(BF16)
    k, qt, vt = _proj(hf, w_qkv[:, qw:2 * qw].astype(BF16), wt_diff, (qw,), (BF16,),
                      (qw, DIFF_HEADS * LANES), nb, "diff_qkv_proj")
    lambda_init = 0.8 - 0.6 * math.exp(-0.3 * 1)
    lam_rows = jnp.concatenate(
        [_pad_lanes(diff_lambda_q1[0]), _pad_lanes(diff_lambda_k1[0]),
         _pad_lanes(diff_lambda_q2[0]), _pad_lanes(diff_lambda_k2[0]),
         jnp.zeros((4, LANES), F32)], axis=0)
    o = _diff_attention_t(qt, k.reshape(nb, LP, qw), vt,
                          _relative_bias_rows(rel_bias_table), lam_rows,
                          diff_subln_w[0][None, :], lambda_init)
    hf = _outproj_ln(hf, [o.reshape(nb * LP, DIFF_HEADS * LANES)], [diff_w_o[0].astype(BF16)],
                     *ln_params(1, 1), name="diff_out_proj_ln")
    return _ffn_ln_final(hf.reshape(nb, LP, D_MODEL),
                         *ffn_weights(ffn2_w_gate, ffn2_w_up, ffn2_w_down, 1), *ln_params(1, 2))
```

```python
import functools
import math

import numpy as np
import jax
import jax.numpy as jnp
from jax import lax
from jax.experimental import pallas as pl
from jax.experimental.pallas import tpu as pltpu

F32 = jnp.float32
BF16 = jnp.bfloat16

D_MODEL = 1024
SEQ = 8192
DEPTH = 2
N_META = 16
CHUNK = 128
SSD_D_INNER = 2048
SSD_HEAD_DIM = 64
SSD_HEADS = 32
SSD_GROUPS = 4
SSD_GROUP_W = SSD_D_INNER // SSD_GROUPS
SSD_STATE = 128
SSD_CONV = 4
SSD_CONV_CH = SSD_D_INNER + 2 * SSD_GROUPS * SSD_STATE
FOX_HEADS = 16
FOX_WIDTH = 1024
HEAD_DIM = 64
DIFF_HEADS = 8
N_BUCKETS = 32
D_FF = 2816
ALPHA = (2 * DEPTH) ** 0.25
LN_EPS = 1e-5
RMS_EPS = 1e-5
NEG = -1e30

LANES = 128
LP = 8448
PADL = LP - SEQ - N_META
ROW_TILE = 512
OUT_TILE = 256
ATT_TQ = 768
ATT_TK = 768
CONV_HALO = 8
SMALL_W = LANES
VMEM_LIMIT = 56 * 1024 * 1024

assert PADL % CHUNK == CHUNK - N_META
assert LP % ATT_TQ == 0 and ATT_TQ % ATT_TK == 0 and ATT_TK % LANES == 0
assert LP % CHUNK == 0 and (2 * LP) % ROW_TILE == 0
assert PADL <= OUT_TILE and (LP - OUT_TILE) == SEQ


def _const_spec(shape):
    nd = len(shape)
    return pl.BlockSpec(shape, lambda *_: (0,) * nd, pipeline_mode=pl.Buffered(1))


def _layer_norm(r, g, b):
    mu = jnp.mean(r, axis=-1, keepdims=True)
    d = r - mu
    var = jnp.mean(d * d, axis=-1, keepdims=True)
    return d * lax.rsqrt(var + LN_EPS) * g + b


def _silu(x):
    return x / (1.0 + jnp.exp(-x))


def _softplus(x):
    return jnp.maximum(x, 0.0) + jnp.log(1.0 + jnp.exp(-jnp.abs(x)))


def _ffn_ln_body(h_ref, wg_ref, wu_ref, wd_ref, g_ref, b_ref, o_ref):
    h = h_ref[...]
    hb = h.astype(BF16)
    g = jnp.dot(hb, wg_ref[...], preferred_element_type=F32)
    u = jnp.dot(hb, wu_ref[...], preferred_element_type=F32)
    a = (_silu(g) * u).astype(BF16)
    y = jnp.dot(a, wd_ref[...], preferred_element_type=F32)
    o_ref[...] = _layer_norm(ALPHA * h + 0.5 * y, g_ref[...], b_ref[...])


def _ffn_ln(hf, wg, wu, wd, g, b):
    rows = hf.shape[0]
    return pl.pallas_call(
        _ffn_ln_body,
        grid=(rows // ROW_TILE,),
        in_specs=[
            pl.BlockSpec((ROW_TILE, D_MODEL), lambda i: (i, 0)),
            _const_spec((D_MODEL, D_FF)),
            _const_spec((D_MODEL, D_FF)),
            _const_spec((D_FF, D_MODEL)),
            _const_spec((1, D_MODEL)),
            _const_spec((1, D_MODEL)),
        ],
        out_specs=pl.BlockSpec((ROW_TILE, D_MODEL), lambda i: (i, 0)),
        out_shape=jax.ShapeDtypeStruct((rows, D_MODEL), F32),
        compiler_params=pltpu.CompilerParams(
            dimension_semantics=("parallel",), vmem_limit_bytes=VMEM_LIMIT),
        name="ffn_ln",
    )(hf, wg, wu, wd, g, b)


def _ffn_ln_first_body(lead_ref, x_ref, wg_ref, wu_ref, wd_ref, g_ref, b_ref, o_ref, h_ref):
    h_ref[...] = jnp.where(pl.program_id(1) == 0, lead_ref[...], x_ref[...])
    _ffn_ln_body(h_ref, wg_ref, wu_ref, wd_ref, g_ref, b_ref, o_ref)


def _ffn_ln_first(x, lead, wg, wu, wd, g, b):
    nb = x.shape[0]
    tiles = LP // OUT_TILE
    return pl.pallas_call(
        _ffn_ln_first_body,
        grid=(nb, tiles),
        in_specs=[
            _const_spec((OUT_TILE, D_MODEL)),
            pl.BlockSpec((None, OUT_TILE, D_MODEL), lambda bi, i: (bi, jnp.maximum(i - 1, 0), 0)),
            _const_spec((D_MODEL, D_FF)),
            _const_spec((D_MODEL, D_FF)),
            _const_spec((D_FF, D_MODEL)),
            _const_spec((1, D_MODEL)),
            _const_spec((1, D_MODEL)),
        ],
        out_specs=pl.BlockSpec((OUT_TILE, D_MODEL), lambda bi, i: (bi * tiles + i, 0)),
        out_shape=jax.ShapeDtypeStruct((nb * LP, D_MODEL), F32),
        scratch_shapes=[pltpu.VMEM((OUT_TILE, D_MODEL), F32)],
        compiler_params=pltpu.CompilerParams(
            dimension_semantics=("parallel", "arbitrary"), vmem_limit_bytes=VMEM_LIMIT),
        name="ffn_ln_first",
    )(lead, x, wg, wu, wd, g, b)


def _ffn_ln_final_body(h_ref, wg_ref, wu_ref, wd_ref, g_ref, b_ref, o_ref):
    @pl.when(pl.program_id(1) > 0)
    def _():
        _ffn_ln_body(h_ref, wg_ref, wu_ref, wd_ref, g_ref, b_ref, o_ref)


def _ffn_ln_final(h3, wg, wu, wd, g, b):
    nb = h3.shape[0]
    return pl.pallas_call(
        _ffn_ln_final_body,
        grid=(nb, LP // OUT_TILE),
        in_specs=[
            pl.BlockSpec((None, OUT_TILE, D_MODEL), lambda bi, i: (bi, i, 0)),
            _const_spec((D_MODEL, D_FF)),
            _const_spec((D_MODEL, D_FF)),
            _const_spec((D_FF, D_MODEL)),
            _const_spec((1, D_MODEL)),
            _const_spec((1, D_MODEL)),
        ],
        out_specs=pl.BlockSpec((None, OUT_TILE, D_MODEL),
                               lambda bi, i: (bi, jnp.maximum(i - 1, 0), 0)),
        out_shape=jax.ShapeDtypeStruct((nb, SEQ, D_MODEL), F32),
        compiler_params=pltpu.CompilerParams(
            dimension_semantics=("arbitrary", "arbitrary"), vmem_limit_bytes=VMEM_LIMIT),
        name="ffn_ln_final",
    )(h3, wg, wu, wd, g, b)


PROJ_TILE = 256
assert LP % PROJ_TILE == 0


def _proj_body(n_t, h_ref, w_ref, wt_ref, *o_refs):
    hb = h_ref[...].astype(BF16)
    off = 0
    for o_ref in o_refs[:-n_t]:
        n = o_ref.shape[-1]
        o_ref[...] = jnp.dot(hb, w_ref[:, off:off + n],
                             preferred_element_type=F32).astype(o_ref.dtype)
        off += n
    off = 0
    for o_ref in o_refs[-n_t:]:
        n = o_ref.shape[0]
        o_ref[...] = lax.dot_general(wt_ref[off:off + n, :], hb, (((1,), (1,)), ((), ())),
                                     preferred_element_type=F32).astype(o_ref.dtype)
        off += n


def _proj(hf, w, wt, widths, dtypes, t_widths, nb, name):
    rows = hf.shape[0]
    tiles_per_batch = LP // PROJ_TILE
    out_specs = [pl.BlockSpec((PROJ_TILE, n), lambda i: (i, 0)) for n in widths]
    out_specs += [pl.BlockSpec((None, n, PROJ_TILE),
                               lambda i: (i // tiles_per_batch, 0, i % tiles_per_batch))
                  for n in t_widths]
    out_shape = [jax.ShapeDtypeStruct((rows, n), dt) for n, dt in zip(widths, dtypes)]
    out_shape += [jax.ShapeDtypeStruct((nb, n, LP), BF16) for n in t_widths]
    return pl.pallas_call(
        functools.partial(_proj_body, len(t_widths)),
        grid=(rows // PROJ_TILE,),
        in_specs=[pl.BlockSpec((PROJ_TILE, D_MODEL), lambda i: (i, 0)),
                  _const_spec(w.shape), _const_spec(wt.shape)],
        out_specs=out_specs,
        out_shape=out_shape,
        compiler_params=pltpu.CompilerParams(
            dimension_semantics=("parallel",), vmem_limit_bytes=VMEM_LIMIT),
        name=name,
    )(hf, w, wt)


def _outproj_ln_body(n_in, h_ref, *refs):
    a_refs = refs[:n_in]
    w_refs = refs[n_in:2 * n_in]
    g_ref, b_ref, o_ref = refs[2 * n_in:]
    m = jnp.dot(a_refs[0][...], w_refs[0][...], preferred_element_type=F32)
    for a_ref, w_ref in zip(a_refs[1:], w_refs[1:]):
        m = m + jnp.dot(a_ref[...], w_ref[...], preferred_element_type=F32)
    o_ref[...] = _layer_norm(ALPHA * h_ref[...] + m, g_ref[...], b_ref[...])


def _outproj_ln(hf, acts, ws, g, b, name):
    rows = hf.shape[0]
    n_in = len(acts)
    in_specs = [pl.BlockSpec((ROW_TILE, D_MODEL), lambda i: (i, 0))]
    in_specs += [pl.BlockSpec((ROW_TILE, a.shape[1]), lambda i: (i, 0)) for a in acts]
    in_specs += [_const_spec(w.shape) for w in ws]
    in_specs += [_const_spec((1, D_MODEL)), _const_spec((1, D_MODEL))]
    return pl.pallas_call(
        functools.partial(_outproj_ln_body, n_in),
        grid=(rows // ROW_TILE,),
        in_specs=in_specs,
        out_specs=pl.BlockSpec((ROW_TILE, D_MODEL), lambda i: (i, 0)),
        out_shape=jax.ShapeDtypeStruct((rows, D_MODEL), F32),
        compiler_params=pltpu.CompilerParams(
            dimension_semantics=("parallel",), vmem_limit_bytes=VMEM_LIMIT),
        name=name,
    )(hf, *acts, *ws, g, b)


def _split_dot(x, e_ref):
    hi = x.astype(BF16)
    lo = (x - hi.astype(F32)).astype(BF16)
    e = e_ref[...]
    return (jnp.dot(hi, e, preferred_element_type=F32)
            + jnp.dot(lo, e, preferred_element_type=F32))


def _ssd_body(xbc_ref, z_ref, small_ref, convw_ref, convb_ref, sbias_ref, alog_ref,
              dskip_ref, normw_ref, expand_ref, y_ref, c_ref,
              ext_ref, state_ref, carry_ref):
    c = pl.program_id(1)

    @pl.when(c == 0)
    def _():
        ext_ref[0:CONV_HALO, :] = jnp.zeros((CONV_HALO, SSD_CONV_CH), F32)
        state_ref[...] = jnp.zeros_like(state_ref)
        carry_ref[...] = jnp.zeros_like(carry_ref)

    row = lax.broadcasted_iota(jnp.int32, (CHUNK, 1), 0)
    valid = (c * CHUNK + row) >= PADL

    ext_ref[CONV_HALO:CONV_HALO + CHUNK, :] = jnp.where(valid, xbc_ref[...], 0.0)
    ext = ext_ref[...]
    conv = convb_ref[...] + convw_ref[SSD_CONV - 1:SSD_CONV, :] * ext[CONV_HALO:, :]
    for k in range(SSD_CONV - 1):
        shifted = pltpu.roll(ext, SSD_CONV - 1 - k, 0)[CONV_HALO:, :]
        conv = conv + convw_ref[k:k + 1, :] * shifted
    ext_ref[0:CONV_HALO, :] = ext_ref[CHUNK:CHUNK + CONV_HALO, :]
    xc = jnp.where(valid, _silu(conv), 0.0)
    xs = xc[:, :SSD_D_INNER]
    bm = xc[:, SSD_D_INNER:SSD_D_INNER + SSD_GROUPS * SSD_STATE].astype(BF16)
    cm = xc[:, SSD_D_INNER + SSD_GROUPS * SSD_STATE:].astype(BF16)

    lane = lax.broadcasted_iota(jnp.int32, (CHUNK, SMALL_W), 1)
    is_dt = lane < SSD_HEADS
    is_f = (lane >= SSD_HEADS) & (lane < SSD_HEADS + FOX_HEADS)
    v = small_ref[...] + sbias_ref[...]
    dt = jnp.where(valid & is_dt, _softplus(v), 0.0)
    log_f = jnp.where(valid & is_f, -_softplus(-v), 0.0)
    neg_a = -jnp.exp(alog_ref[...])
    steps = jnp.where(is_dt, dt * neg_a, log_f)
    r_i = lax.broadcasted_iota(jnp.int32, (CHUNK, CHUNK), 0)
    c_i = lax.broadcasted_iota(jnp.int32, (CHUNK, CHUNK), 1)
    causal = r_i >= c_i
    tril = jnp.where(causal, 1.0, 0.0).astype(F32)
    cum = jnp.dot(tril, steps, preferred_element_type=F32,
                  precision=lax.Precision.HIGHEST)
    c_total = jnp.where(is_f, cum + carry_ref[...], 0.0)
    c_ref[...] = c_total
    carry_ref[...] = c_total[CHUNK - 1:CHUNK, :]

    a_last = cum[CHUNK - 1:CHUNK, :]
    dt_x = _split_dot(dt, expand_ref)
    ea_x = _split_dot(jnp.exp(cum), expand_ref)
    de_x = _split_dot(jnp.exp(a_last - cum), expand_ref)
    x_dt = xs * dt_x
    xb = x_dt.astype(BF16)
    xe = (x_dt * de_x).astype(BF16)
    cum_t = cum.T
    lane_p = lax.broadcasted_iota(jnp.int32, (CHUNK, 2 * SSD_HEAD_DIM), 1)
    first_half = lane_p < SSD_HEAD_DIM

    y_groups = []
    for g in range(SSD_GROUPS):
        gs = slice(g * SSD_GROUP_W, (g + 1) * SSD_GROUP_W)
        bg = bm[:, g * SSD_STATE:(g + 1) * SSD_STATE]
        cg = cm[:, g * SSD_STATE:(g + 1) * SSD_STATE]
        cb = lax.dot_general(cg, bg, (((1,), (1,)), ((), ())),
                             preferred_element_type=F32)
        pair_out = []
        for pr in range(SSD_GROUP_W // (2 * SSD_HEAD_DIM)):
            col0 = g * SSD_GROUP_W + pr * 2 * SSD_HEAD_DIM
            x_pair = xb[:, col0:col0 + 2 * SSD_HEAD_DIM]
            ys = []
            for j in range(2):
                hd = col0 // SSD_HEAD_DIM + j
                diff = cum[:, hd:hd + 1] - cum_t[hd:hd + 1, :]
                decay = jnp.exp(jnp.where(causal, diff, -jnp.inf))
                mat = (cb * decay).astype(BF16)
                ys.append(jnp.dot(mat, x_pair, preferred_element_type=F32))
            pair_out.append(jnp.where(first_half, ys[0], ys[1]))
        y_diag = jnp.concatenate(pair_out, axis=1)
        st = state_ref[g]
        y_off = jnp.dot(cg, st.astype(BF16), preferred_element_type=F32) * ea_x[:, gs]
        new = lax.dot_general(bg, xe[:, gs], (((0,), (0,)), ((), ())),
                              preferred_element_type=F32)
        state_ref[g] = st * ea_x[CHUNK - 1:CHUNK, gs] + new
        yg = y_diag + y_off + dskip_ref[:, gs] * xs[:, gs]
        yg = yg * _silu(z_ref[:, gs])
        ms = jnp.mean(yg * yg, axis=-1, keepdims=True)
        y_groups.append(yg * lax.rsqrt(ms + RMS_EPS) * normw_ref[:, gs])
    y_ref[...] = jnp.concatenate(y_groups, axis=1).astype(y_ref.dtype)


def _ssd(xbc, z, small, convw, convb, sbias, alog, dskip, normw, expand):
    nb = xbc.shape[0]
    row_spec = lambda w: pl.BlockSpec((None, CHUNK, w), lambda bi, ci: (bi, ci, 0))
    return pl.pallas_call(
        _ssd_body,
        grid=(nb, LP // CHUNK),
        in_specs=[row_spec(SSD_CONV_CH), row_spec(SSD_D_INNER), row_spec(SMALL_W),
                  _const_spec(convw.shape), _const_spec(convb.shape),
                  _const_spec(sbias.shape), _const_spec(alog.shape),
                  _const_spec(dskip.shape), _const_spec(normw.shape),
                  _const_spec(expand.shape)],
        out_specs=[row_spec(SSD_D_INNER), row_spec(SMALL_W)],
        out_shape=[jax.ShapeDtypeStruct((nb, LP, SSD_D_INNER), BF16),
                   jax.ShapeDtypeStruct((nb, LP, SMALL_W), F32)],
        scratch_shapes=[pltpu.VMEM((CHUNK + CONV_HALO, SSD_CONV_CH), F32),
                        pltpu.VMEM((SSD_GROUPS, SSD_STATE, SSD_GROUP_W), F32),
                        pltpu.VMEM((1, SMALL_W), F32)],
        compiler_params=pltpu.CompilerParams(
            dimension_semantics=("arbitrary", "arbitrary"), vmem_limit_bytes=VMEM_LIMIT),
        name="ssd",
    )(xbc, z, small, convw, convb, sbias, alog, dskip, normw, expand)


N_TK_CHUNKS = ATT_TK // LANES
N_TQ_CHUNKS = ATT_TQ // LANES
assert ATT_TQ == ATT_TK and PADL <= ATT_TK
LOG2E = math.log2(math.e)
Q_SCALE = LOG2E * HEAD_DIM ** -0.5


def _chunk_max(chunk):
    part = chunk(0)
    for c in range(1, N_TK_CHUNKS):
        part = jnp.maximum(part, chunk(c))
    return part


def _softmax_update(chunk, part, v_aug, m_ref, acc_ref, idx):
    m_old = m_ref[idx]
    m_new = jnp.maximum(m_old, jnp.max(part, axis=-1, keepdims=True))
    alpha = jnp.exp2(m_old - m_new)
    p = jnp.concatenate([jnp.exp2(chunk(c) - m_new).astype(BF16)
                         for c in range(N_TK_CHUNKS)], axis=1)
    pv = jnp.dot(p, v_aug, preferred_element_type=F32)
    for w in range(acc_ref.shape[-1] // LANES):
        ws = slice(w * LANES, (w + 1) * LANES)
        acc_ref[idx, :, ws] = alpha * acc_ref[idx, :, ws] + pv[:, ws]
    m_ref[idx] = m_new


def _causal_chunk(c):
    qpos = lax.broadcasted_iota(jnp.int32, (ATT_TQ, LANES), 0)
    kpos = c * LANES + lax.broadcasted_iota(jnp.int32, (ATT_TQ, LANES), 1)
    return kpos <= qpos


def _not_padding_chunk(kb, c):
    kpos = c * LANES + lax.broadcasted_iota(jnp.int32, (ATT_TQ, LANES), 1)
    return (kpos >= PADL) | (kb > 0)


def _init_softmax_state(m_ref, acc_ref):
    m_ref[...] = jnp.full(m_ref.shape, NEG, F32)
    acc_ref[...] = jnp.zeros_like(acc_ref)


def _split_halves(q):
    lane = lax.broadcasted_iota(jnp.int32, q.shape, 1)
    first_half = lane < HEAD_DIM
    zero = jnp.zeros_like(q)
    return first_half, (jnp.where(first_half, q, zero), jnp.where(first_half, zero, q))


def _run_key_tiles(qb, scores, step, buf_a, buf_b):
    odd = qb % 2

    @pl.when(odd == 0)
    def _():
        scores(0, buf_a)

    @pl.when(odd == 1)
    def _():
        scores(0, buf_b)
        step(0, buf_b, buf_a)

    def two_below_diagonal(i, carry):
        kb = odd + 2 * i
        step(kb, buf_a, buf_b)
        step(kb + 1, buf_b, buf_a)
        return carry

    lax.fori_loop(0, qb // 2, two_below_diagonal, 0)
    step(qb, buf_a, None)


def _fox_body(q_ref, kt_ref, v_ref, ck_ref, o_ref, m_ref, acc_ref,
              sa_ref, pa_ref, sb_ref, pb_ref):
    qb = pl.program_id(2)
    first_half, q_heads = _split_halves(q_ref[...])
    _init_softmax_state(m_ref, acc_ref)
    buf_a, buf_b = (sa_ref, pa_ref), (sb_ref, pb_ref)

    def scores(kb, buf):
        s_dst, part_dst = buf
        start = pl.multiple_of(kb * ATT_TK, ATT_TK)
        kt = kt_ref[:, pl.ds(start, ATT_TK)]
        for j in range(2):
            s = jnp.dot(q_heads[j], kt, preferred_element_type=F32)
            part = None
            for c in range(N_TK_CHUNKS):
                key0 = pl.multiple_of(start + c * LANES, LANES)
                sc = s[:, c * LANES:(c + 1) * LANES] - ck_ref[j:j + 1, pl.ds(key0, LANES)]
                s_dst[j, :, c * LANES:(c + 1) * LANES] = sc
                part = sc if part is None else jnp.maximum(part, sc)
            part_dst[j] = part

    def step(kb, src, nxt):
        s_src, part_src = src
        start = pl.multiple_of(kb * ATT_TK, ATT_TK)
        if nxt is not None:
            scores(kb + 1, nxt)
        for j in range(2):
            def chunk(c, j=j):
                sc = s_src[j, :, c * LANES:(c + 1) * LANES]
                if nxt is None:
                    sc = jnp.where(_causal_chunk(c), sc, NEG)
                return sc
            part = _chunk_max(chunk) if nxt is None else part_src[j]
            v_aug = v_ref[pl.ds(start, ATT_TK), j * LANES:(j + 1) * LANES]
            _softmax_update(chunk, part, v_aug, m_ref, acc_ref, j)

    _run_key_tiles(qb, scores, step, buf_a, buf_b)

    acc0, acc1 = acc_ref[0], acc_ref[1]
    num = jnp.where(first_half, acc0, acc1)
    den = pltpu.roll(jnp.where(first_half, acc1, acc0), HEAD_DIM, axis=1)
    o_ref[...] = (num / den).astype(o_ref.dtype)


def _fox_attention(q, kt, v_aug, ck):
    nb = q.shape[0]
    pairs = FOX_HEADS // 2
    return pl.pallas_call(
        _fox_body,
        grid=(nb, pairs, LP // ATT_TQ),
        in_specs=[
            pl.BlockSpec((None, ATT_TQ, LANES), lambda bi, hp, qi: (bi, qi, hp)),
            pl.BlockSpec((None, LANES, LP), lambda bi, hp, qi: (bi, hp, 0)),
            pl.BlockSpec((None, LP, 2 * LANES), lambda bi, hp, qi: (bi, 0, hp)),
            pl.BlockSpec((None, None, 2, LP), lambda bi, hp, qi: (bi, hp, 0, 0)),
        ],
        out_specs=pl.BlockSpec((None, ATT_TQ, LANES), lambda bi, hp, qi: (bi, qi, hp)),
        out_shape=jax.ShapeDtypeStruct((nb, LP, FOX_WIDTH), BF16),
        scratch_shapes=[pltpu.VMEM((2, ATT_TQ, LANES), F32),
                        pltpu.VMEM((2, ATT_TQ, LANES), F32)]
                       + [pltpu.VMEM((2, ATT_TQ, ATT_TK), F32),
                          pltpu.VMEM((2, ATT_TQ, LANES), F32)] * 2,
        compiler_params=pltpu.CompilerParams(
            dimension_semantics=("parallel", "parallel", "arbitrary"),
            vmem_limit_bytes=VMEM_LIMIT),
        name="fox_attention",
    )(q, kt, v_aug, ck)


def _bias_tiles(bias_ref):
    rows = jnp.broadcast_to(bias_ref[...], (CHUNK, 2 * CHUNK))
    rolled = pltpu.roll(rows, 0, 1, stride=1, stride_axis=0)
    return rolled[:, :CHUNK], rolled[:, CHUNK:]


def _bias_chunk(tiles, key_chunk):
    picked = []
    for a in range(N_TQ_CHUNKS):
        sub = a - key_chunk
        picked.append(tiles[sub] if sub in (0, 1) else None)
    if all(t is None for t in picked):
        return None
    zeros = jnp.zeros((CHUNK, CHUNK), F32)
    return jnp.concatenate([zeros if t is None else t for t in picked], axis=0)


def _diff_body(lambda_init, q_ref, kt_ref, v_ref, bias_ref, lam_ref, subln_ref, o_ref,
               m_ref, acc_ref, tiles_ref, sa_ref, pa_ref, sb_ref, pb_ref):
    qb = pl.program_id(2)
    _, q_parts = _split_halves(q_ref[...])
    _init_softmax_state(m_ref, acc_ref)
    buf_a, buf_b = (sa_ref, pa_ref), (sb_ref, pb_ref)
    tile0, tile1 = _bias_tiles(bias_ref)
    tiles_ref[0] = tile0
    tiles_ref[1] = tile1
    last = N_TK_CHUNKS - 1

    def scores(kb, buf):
        s_dst, part_dst = buf
        start = pl.multiple_of(kb * ATT_TK, ATT_TK)
        kt = kt_ref[:, pl.ds(start, ATT_TK)]
        for j in range(2):
            s = jnp.dot(q_parts[j], kt, preferred_element_type=F32)
            part = None
            for c in range(N_TK_CHUNKS):
                sc = s[:, c * LANES:(c + 1) * LANES]
                if c * LANES < PADL:
                    sc = jnp.where(_not_padding_chunk(kb, c), sc, NEG)
                s_dst[j, :, c * LANES:(c + 1) * LANES] = sc
                part = sc if part is None else jnp.maximum(part, sc)
            part_dst[j] = part

    def step(kb, src, nxt):
        s_src, part_src = src
        start = pl.multiple_of(kb * ATT_TK, ATT_TK)
        diagonal = nxt is None
        if diagonal:
            tiles = (tiles_ref[0], tiles_ref[1])
            biases = [_bias_chunk(tiles, c) for c in range(N_TK_CHUNKS)]
        else:
            scores(kb + 1, nxt)
            corner = jnp.where(kb == qb - 1, 1.0, 0.0) * tiles_ref[1]
        v_aug = v_ref[pl.ds(start, ATT_TK), :]
        for j in range(2):
            def chunk(c, j=j):
                sc = s_src[j, :, c * LANES:(c + 1) * LANES]
                if diagonal:
                    if biases[c] is not None:
                        sc = sc + biases[c]
                    sc = jnp.where(_causal_chunk(c), sc, NEG)
                elif c == last:
                    sc = jnp.concatenate([sc[:CHUNK] + corner, sc[CHUNK:]], axis=0)
                return sc
            if diagonal:
                part = _chunk_max(chunk)
            else:
                top = s_src[j, :CHUNK, last * LANES:] + corner
                for c in range(last):
                    top = jnp.maximum(top, s_src[j, :CHUNK, c * LANES:(c + 1) * LANES])
                part = jnp.concatenate([top, part_src[j, CHUNK:, :]], axis=0)
            _softmax_update(chunk, part, v_aug, m_ref, acc_ref, j)

    _run_key_tiles(qb, scores, step, buf_a, buf_b)

    lam1 = jnp.exp(jnp.sum(lam_ref[0:1, :] * lam_ref[1:2, :], axis=-1, keepdims=True))
    lam2 = jnp.exp(jnp.sum(lam_ref[2:3, :] * lam_ref[3:4, :], axis=-1, keepdims=True))
    lam = lam1 - lam2 + lambda_init
    o = (acc_ref[0, :, :LANES] / acc_ref[0, :, LANES:]
         - lam * (acc_ref[1, :, :LANES] / acc_ref[1, :, LANES:]))
    ms = jnp.mean(o * o, axis=-1, keepdims=True)
    o = o * lax.rsqrt(ms + RMS_EPS) * subln_ref[...] * (1.0 - lambda_init)
    o_ref[...] = o.astype(o_ref.dtype)


def _diff_attention(q, kt, v_aug, bias_rows, lam_rows, subln, lambda_init):
    nb = q.shape[0]
    return pl.pallas_call(
        functools.partial(_diff_body, lambda_init),
        grid=(nb, DIFF_HEADS, LP // ATT_TQ),
        in_specs=[
            pl.BlockSpec((None, ATT_TQ, LANES), lambda bi, hd, qi: (bi, qi, hd)),
            pl.BlockSpec((None, LANES, LP), lambda bi, hd, qi: (bi, hd, 0)),
            pl.BlockSpec((None, LP, 2 * LANES), lambda bi, hd, qi: (bi, 0, hd)),
            pl.BlockSpec((None, 1, 2 * CHUNK), lambda bi, hd, qi: (hd, 0, 0)),
            _const_spec(lam_rows.shape),
            _const_spec(subln.shape),
        ],
        out_specs=pl.BlockSpec((None, ATT_TQ, LANES), lambda bi, hd, qi: (bi, qi, hd)),
        out_shape=jax.ShapeDtypeStruct((nb, LP, DIFF_HEADS * LANES), BF16),
        scratch_shapes=[pltpu.VMEM((2, ATT_TQ, LANES), F32),
                        pltpu.VMEM((2, ATT_TQ, 2 * LANES), F32),
                        pltpu.VMEM((2, CHUNK, CHUNK), F32)]
                       + [pltpu.VMEM((2, ATT_TQ, ATT_TK), F32),
                          pltpu.VMEM((2, ATT_TQ, LANES), F32)] * 2,
        compiler_params=pltpu.CompilerParams(
            dimension_semantics=("parallel", "parallel", "arbitrary"),
            vmem_limit_bytes=VMEM_LIMIT),
        name="diff_attention",
    )(q, kt, v_aug, bias_rows, lam_rows, subln)


ONES_ROWS = 16


def _value_rows(vt_ref, row0, n_rows, start):
    vt = vt_ref[row0:row0 + n_rows, pl.ds(start, ATT_TK)]
    return jnp.concatenate([vt, jnp.ones((ONES_ROWS, ATT_TK), BF16)], axis=0)


def _softmax_update_t(s_t, col_max, vt_aug, m_ref, acc_ref, idx):
    m_old = m_ref[idx]
    m_new = jnp.maximum(m_old, col_max)
    alpha = jnp.exp2(m_old - m_new)
    p_t = jnp.exp2(s_t - m_new).astype(BF16)
    pv = jnp.dot(vt_aug, p_t, preferred_element_type=F32)
    acc_ref[idx] = alpha * acc_ref[idx] + pv
    m_ref[idx] = m_new


def _causal_t():
    kpos = lax.broadcasted_iota(jnp.int32, (ATT_TK, ATT_TQ), 0)
    qpos = lax.broadcasted_iota(jnp.int32, (ATT_TK, ATT_TQ), 1)
    return kpos <= qpos


def _fox_body_t(qt_ref, k_ref, vt_ref, c_ref, o_ref, m_ref, acc_ref,
                sa_ref, pa_ref, sb_ref, pb_ref):
    qb = pl.program_id(2)
    _init_softmax_state(m_ref, acc_ref)
    buf_a, buf_b = (sa_ref, pa_ref), (sb_ref, pb_ref)
    zero_rows = jnp.zeros((HEAD_DIM, ATT_TQ), BF16)
    q_heads = [jnp.concatenate([qt_ref[:HEAD_DIM, :], zero_rows], axis=0),
               jnp.concatenate([zero_rows, qt_ref[HEAD_DIM:, :]], axis=0)]
    lane = lax.broadcasted_iota(jnp.int32, (ATT_TK, LANES), 1)
    first_lane = SSD_HEADS + 2 * pl.program_id(1)

    def scores(kb, buf):
        s_dst, max_dst = buf
        start = pl.multiple_of(kb * ATT_TK, ATT_TK)
        k_rows = k_ref[pl.ds(start, ATT_TK), :]
        c_rows = c_ref[pl.ds(start, ATT_TK), :]
        for j in range(2):
            ck = jnp.sum(jnp.where(lane == first_lane + j, c_rows, 0.0), axis=1, keepdims=True)
            s_t = jnp.dot(k_rows, q_heads[j], preferred_element_type=F32) - ck
            s_dst[j] = s_t
            max_dst[j] = jnp.max(s_t, axis=0, keepdims=True)

    def step(kb, src, nxt):
        s_src, max_src = src
        start = pl.multiple_of(kb * ATT_TK, ATT_TK)
        if nxt is not None:
            scores(kb + 1, nxt)
        for j in range(2):
            s_t = s_src[j]
            col_max = max_src[j]
            if nxt is None:
                s_t = jnp.where(_causal_t(), s_t, NEG)
                col_max = jnp.max(s_t, axis=0, keepdims=True)
            vt_aug = _value_rows(vt_ref, j * HEAD_DIM, HEAD_DIM, start)
            _softmax_update_t(s_t, col_max, vt_aug, m_ref, acc_ref, j)

    _run_key_tiles(qb, scores, step, buf_a, buf_b)

    o_t = jnp.concatenate(
        [acc_ref[j, :HEAD_DIM, :] / acc_ref[j, HEAD_DIM:HEAD_DIM + 1, :] for j in range(2)], axis=0)
    o_ref[...] = o_t.T.astype(o_ref.dtype)


def _fox_attention_t(qt, k, vt, c):
    nb = qt.shape[0]
    pairs = FOX_HEADS // 2
    acc_rows = HEAD_DIM + ONES_ROWS
    return pl.pallas_call(
        _fox_body_t,
        grid=(nb, pairs, LP // ATT_TQ),
        in_specs=[
            pl.BlockSpec((None, LANES, ATT_TQ), lambda bi, hp, qi: (bi, hp, qi)),
            pl.BlockSpec((None, LP, LANES), lambda bi, hp, qi: (bi, 0, hp)),
            pl.BlockSpec((None, LANES, LP), lambda bi, hp, qi: (bi, hp, 0)),
            pl.BlockSpec((None, LP, SMALL_W), lambda bi, hp, qi: (bi, 0, 0)),
        ],
        out_specs=pl.BlockSpec((None, ATT_TQ, LANES), lambda bi, hp, qi: (bi, qi, hp)),
        out_shape=jax.ShapeDtypeStruct((nb, LP, FOX_WIDTH), BF16),
        scratch_shapes=[pltpu.VMEM((2, 1, ATT_TQ), F32),
                        pltpu.VMEM((2, acc_rows, ATT_TQ), F32)]
                       + [pltpu.VMEM((2, ATT_TK, ATT_TQ), F32),
                          pltpu.VMEM((2, 1, ATT_TQ), F32)] * 2,
        compiler_params=pltpu.CompilerParams(
            dimension_semantics=("parallel", "parallel", "arbitrary"),
            vmem_limit_bytes=VMEM_LIMIT),
        name="fox_attention",
    )(qt, k, vt, c)


def _bias_tiles_t(bias_ref):
    rows = jnp.broadcast_to(bias_ref[...], (CHUNK, 2 * CHUNK))
    rolled = pltpu.roll(rows, 0, 1, stride=1, stride_axis=0)
    return rolled[:, :CHUNK], rolled[:, CHUNK:]


def _diagonal_bias_t(tiles):
    zeros = jnp.zeros((CHUNK, CHUNK), F32)
    rows = []
    for a in range(N_TK_CHUNKS):
        rows.append(jnp.concatenate(
            [tiles[b - a] if b - a in (0, 1) else zeros for b in range(N_TQ_CHUNKS)], axis=1))
    return jnp.concatenate(rows, axis=0)


def _diff_body_t(lambda_init, qt_ref, k_ref, vt_ref, bias_ref, lam_ref, subln_ref, o_ref,
                 m_ref, acc_ref, tiles_ref, sa_ref, pa_ref, sb_ref, pb_ref):
    qb = pl.program_id(2)
    _init_softmax_state(m_ref, acc_ref)
    buf_a, buf_b = (sa_ref, pa_ref), (sb_ref, pb_ref)
    tile0, tile1 = _bias_tiles_t(bias_ref)
    tiles_ref[0] = tile0
    tiles_ref[1] = tile1
    zero_rows = jnp.zeros((HEAD_DIM, ATT_TQ), BF16)
    q_parts = [jnp.concatenate([qt_ref[:HEAD_DIM, :], zero_rows], axis=0),
               jnp.concatenate([zero_rows, qt_ref[HEAD_DIM:, :]], axis=0)]
    pad_rows = -(-PADL // CHUNK) * CHUNK
    corner0 = ATT_TK - CHUNK

    def scores(kb, buf):
        s_dst, max_dst = buf
        start = pl.multiple_of(kb * ATT_TK, ATT_TK)
        k_rows = k_ref[pl.ds(start, ATT_TK), :]
        kpos = lax.broadcasted_iota(jnp.int32, (pad_rows, ATT_TQ), 0)
        not_padding = (kpos >= PADL) | (kb > 0)
        for j in range(2):
            s_t = jnp.dot(k_rows, q_parts[j], preferred_element_type=F32)
            s_t = jnp.concatenate(
                [jnp.where(not_padding, s_t[:pad_rows], NEG), s_t[pad_rows:]], axis=0)
            s_dst[j] = s_t
            max_dst[j] = jnp.max(s_t, axis=0, keepdims=True)

    def step(kb, src, nxt):
        s_src, max_src = src
        start = pl.multiple_of(kb * ATT_TK, ATT_TK)
        diagonal = nxt is None
        if diagonal:
            bias = _diagonal_bias_t((tiles_ref[0], tiles_ref[1]))
            visible = _causal_t()
        else:
            scores(kb + 1, nxt)
            corner = jnp.where(kb == qb - 1, 1.0, 0.0) * tiles_ref[1]
        vt_aug = _value_rows(vt_ref, 0, LANES, start)
        for j in range(2):
            s_t = s_src[j]
            if diagonal:
                s_t = jnp.where(visible, s_t + bias, NEG)
                col_max = jnp.max(s_t, axis=0, keepdims=True)
            else:
                near = s_t[corner0:, :CHUNK] + corner
                s_t = jnp.concatenate(
                    [s_t[:corner0],
                     jnp.concatenate([near, s_t[corner0:, CHUNK:]], axis=1)], axis=0)
                first = jnp.maximum(jnp.max(s_t[:corner0, :CHUNK], axis=0, keepdims=True),
                                    jnp.max(near, axis=0, keepdims=True))
                col_max = jnp.concatenate([first, max_src[j][:, CHUNK:]], axis=1)
            _softmax_update_t(s_t, col_max, vt_aug, m_ref, acc_ref, j)

    _run_key_tiles(qb, scores, step, buf_a, buf_b)

    lam1 = jnp.exp(jnp.sum(lam_ref[0:1, :] * lam_ref[1:2, :], axis=-1, keepdims=True))
    lam2 = jnp.exp(jnp.sum(lam_ref[2:3, :] * lam_ref[3:4, :], axis=-1, keepdims=True))
    lam = lam1 - lam2 + lambda_init
    o_t = (acc_ref[0, :LANES, :] / acc_ref[0, LANES:LANES + 1, :]
           - lam * (acc_ref[1, :LANES, :] / acc_ref[1, LANES:LANES + 1, :]))
    o = o_t.T
    ms = jnp.mean(o * o, axis=-1, keepdims=True)
    o = o * lax.rsqrt(ms + RMS_EPS) * subln_ref[...] * (1.0 - lambda_init)
    o_ref[...] = o.astype(o_ref.dtype)


def _diff_attention_t(qt, k, vt, bias_rows, lam_rows, subln, lambda_init):
    nb = qt.shape[0]
    acc_rows = LANES + ONES_ROWS
    return pl.pallas_call(
        functools.partial(_diff_body_t, lambda_init),
        grid=(nb, DIFF_HEADS, LP // ATT_TQ),
        in_specs=[
            pl.BlockSpec((None, LANES, ATT_TQ), lambda bi, hd, qi: (bi, hd, qi)),
            pl.BlockSpec((None, LP, LANES), lambda bi, hd, qi: (bi, 0, hd)),
            pl.BlockSpec((None, LANES, LP), lambda bi, hd, qi: (bi, hd, 0)),
            pl.BlockSpec((None, 1, 2 * CHUNK), lambda bi, hd, qi: (hd, 0, 0)),
            _const_spec(lam_rows.shape),
            _const_spec(subln.shape),
        ],
        out_specs=pl.BlockSpec((None, ATT_TQ, LANES), lambda bi, hd, qi: (bi, qi, hd)),
        out_shape=jax.ShapeDtypeStruct((nb, LP, DIFF_HEADS * LANES), BF16),
        scratch_shapes=[pltpu.VMEM((2, 1, ATT_TQ), F32),
                        pltpu.VMEM((2, acc_rows, ATT_TQ), F32),
                        pltpu.VMEM((2, CHUNK, CHUNK), F32)]
                       + [pltpu.VMEM((2, ATT_TK, ATT_TQ), F32),
                          pltpu.VMEM((2, 1, ATT_TQ), F32)] * 2,
        compiler_params=pltpu.CompilerParams(
            dimension_semantics=("parallel", "parallel", "arbitrary"),
            vmem_limit_bytes=VMEM_LIMIT),
        name="diff_attention",
    )(qt, k, vt, bias_rows, lam_rows, subln)


def _t5_bucket(n):
    max_exact = N_BUCKETS // 2
    nf = jnp.maximum(n, 1).astype(F32)
    large = max_exact + (jnp.log(nf / max_exact) / math.log(128 / max_exact)
                         * (N_BUCKETS - max_exact)).astype(jnp.int32)
    large = jnp.minimum(large, N_BUCKETS - 1)
    return jnp.where(n < max_exact, n, large)


def _relative_bias_rows(rel_table):
    dist = jnp.arange(2 * CHUNK)
    by_dist = LOG2E * (rel_table[_t5_bucket(dist)] - rel_table[N_BUCKETS - 1])
    return by_dist.T[:, None, :].astype(F32)


def _pad_lanes(vec, width=LANES):
    return jnp.pad(vec, (0, width - vec.shape[0]))[None, :].astype(F32)


def kernel(x, meta_tokens, ln_gain, ln_bias, ffn1_w_gate, ffn1_w_up, ffn1_w_down, ffn2_w_gate, ffn2_w_up, ffn2_w_down, even_w_in, even_conv_w, even_conv_b, ssd_dt_bias, ssd_a_log, ssd_d_skip, ssd_norm_w, fox_f_bias, even_w_out, diff_w_qkv, diff_lambda_q1, diff_lambda_k1, diff_lambda_q2, diff_lambda_k2, diff_subln_w, diff_w_o, rel_bias_table):
    nb = x.shape[0]
    lead = jnp.concatenate([jnp.zeros((PADL, D_MODEL), x.dtype), meta_tokens.astype(x.dtype)], axis=0)

    def ln_params(l, i):
        return ln_gain[l, i][None, :], ln_bias[l, i][None, :]

    def ffn_weights(wg, wu, wd, l):
        return wg[l].astype(BF16), wu[l].astype(BF16), wd[l].astype(BF16)

    hf = _ffn_ln_first(x, lead, *ffn_weights(ffn1_w_gate, ffn1_w_up, ffn1_w_down, 0),
                       *ln_params(0, 0))

    w_in = even_w_in[0]
    o_z, o_xbc = 0, SSD_D_INNER
    o_dt = o_xbc + SSD_CONV_CH
    o_q = o_dt + SSD_HEADS
    o_k, o_v = o_q + FOX_WIDTH, o_q + 2 * FOX_WIDTH
    o_f = o_q + 3 * FOX_WIDTH
    w_small = jnp.concatenate(
        [w_in[:, o_dt:o_q], w_in[:, o_f:],
         jnp.zeros((D_MODEL, SMALL_W - SSD_HEADS - FOX_HEADS), w_in.dtype)], axis=1)
    w_even = jnp.concatenate([w_in[:, o_z:o_dt], w_in[:, o_k:o_v], w_small], axis=1).astype(BF16)
    wt_even = jnp.concatenate([w_in[:, o_q:o_k] * Q_SCALE, w_in[:, o_v:o_f]], axis=1).T.astype(BF16)
    z, xbc, k, small, qt, vt = _proj(
        hf, w_even, wt_even, (SSD_D_INNER, SSD_CONV_CH, FOX_WIDTH, SMALL_W),
        (F32, F32, BF16, F32), (FOX_WIDTH, FOX_WIDTH), nb, "even_in_proj")

    sbias = _pad_lanes(jnp.concatenate([ssd_dt_bias[0], fox_f_bias[0]]))
    alog = _pad_lanes(ssd_a_log[0])
    dskip = jnp.repeat(ssd_d_skip[0], SSD_HEAD_DIM)[None, :].astype(F32)
    expand = np.zeros((SMALL_W, SSD_D_INNER), np.float32)
    expand[np.arange(SSD_D_INNER) // SSD_HEAD_DIM, np.arange(SSD_D_INNER)] = 1.0
    y, cfull = _ssd(xbc.reshape(nb, LP, SSD_CONV_CH), z.reshape(nb, LP, SSD_D_INNER),
                    small.reshape(nb, LP, SMALL_W), even_conv_w[0], even_conv_b[0][None, :],
                    sbias, alog, dskip, ssd_norm_w[0][None, :], jnp.asarray(expand, BF16))
    ck = jnp.where(jnp.arange(LP)[None, :, None] < PADL, -NEG, LOG2E * cfull)
    o = _fox_attention_t(qt, k.reshape(nb, LP, FOX_WIDTH), vt, ck)
    w_out = even_w_out[0].astype(BF16)
    hf = _outproj_ln(hf, [y.reshape(nb * LP, SSD_D_INNER), o.reshape(nb * LP, FOX_WIDTH)],
                     [w_out[:SSD_D_INNER], w_out[SSD_D_INNER:]], *ln_params(0, 1),
                     name="even_out_proj_ln")
    hf = _ffn_ln(hf, *ffn_weights(ffn2_w_gate, ffn2_w_up, ffn2_w_down, 0), *ln_params(0, 2))

    hf = _ffn_ln(hf, *ffn_weights(ffn1_w_gate, ffn1_w_up, ffn1_w_down, 1), *ln_params(1, 0))
    qw = DIFF_HEADS * 2 * HEAD_DIM
    w_qkv = diff_w_qkv[0]
    wt_diff = jnp.concatenate([w_qkv[:, :qw] * Q_SCALE, w_qkv[:, 2 * qw:]], axis=1).T.astype(BF16)
    k, qt, vt = _proj(hf, w_qkv[:, qw:2 * qw].astype(BF16), wt_diff, (qw,), (BF16,),
                      (qw, DIFF_HEADS * LANES), nb, "diff_qkv_proj")
    lambda_init = 0.8 - 0.6 * math.exp(-0.3 * 1)
    lam_rows = jnp.concatenate(
        [_pad_lanes(diff_lambda_q1[0]), _pad_lanes(diff_lambda_k1[0]),
         _pad_lanes(diff_lambda_q2[0]), _pad_lanes(diff_lambda_k2[0]),
         jnp.zeros((4, LANES), F32)], axis=0)
    o = _diff_attention_t(qt, k.reshape(nb, LP, qw), vt,
                          _relative_bias_rows(rel_bias_table), lam_rows,
                          diff_subln_w[0][None, :], lambda_init)
    hf = _outproj_ln(hf, [o.reshape(nb * LP, DIFF_HEADS * LANES)], [diff_w_o[0].astype(BF16)],
                     *ln_params(1, 1), name="diff_out_proj_ln")
    return _ffn_ln_final(hf.reshape(nb, LP, D_MODEL),
                         *ffn_weights(ffn2_w_gate, ffn2_w_up, ffn2_w_down, 1), *ln_params(1, 2))
```

```python
import functools
import math

import numpy as np
import jax
import jax.numpy as jnp
from jax import lax
from jax.experimental import pallas as pl
from jax.experimental.pallas import tpu as pltpu

F32 = jnp.float32
BF16 = jnp.bfloat16

D_MODEL = 1024
SEQ = 8192
DEPTH = 2
N_META = 16
CHUNK = 128
SSD_D_INNER = 2048
SSD_HEAD_DIM = 64
SSD_HEADS = 32
SSD_GROUPS = 4
SSD_GROUP_W = SSD_D_INNER // SSD_GROUPS
SSD_STATE = 128
SSD_CONV = 4
SSD_CONV_CH = SSD_D_INNER + 2 * SSD_GROUPS * SSD_STATE
FOX_HEADS = 16
FOX_WIDTH = 1024
HEAD_DIM = 64
DIFF_HEADS = 8
N_BUCKETS = 32
D_FF = 2816
ALPHA = (2 * DEPTH) ** 0.25
LN_EPS = 1e-5
RMS_EPS = 1e-5
NEG = -1e30

LANES = 128
LP = 8448
PADL = LP - SEQ - N_META
ROW_TILE = 512
OUT_TILE = 256
ATT_TQ = 768
ATT_TK = 768
CONV_HALO = 8
SMALL_W = LANES
VMEM_LIMIT = 56 * 1024 * 1024

assert PADL % CHUNK == CHUNK - N_META
assert LP % ATT_TQ == 0 and ATT_TQ % ATT_TK == 0 and ATT_TK % LANES == 0
assert LP % CHUNK == 0 and (2 * LP) % ROW_TILE == 0
assert PADL <= OUT_TILE and (LP - OUT_TILE) == SEQ


def _const_spec(shape):
    nd = len(shape)
    return pl.BlockSpec(shape, lambda *_: (0,) * nd, pipeline_mode=pl.Buffered(1))


def _layer_norm(r, g, b):
    mu = jnp.mean(r, axis=-1, keepdims=True)
    d = r - mu
    var = jnp.mean(d * d, axis=-1, keepdims=True)
    return d * lax.rsqrt(var + LN_EPS) * g + b


def _silu(x):
    return x / (1.0 + jnp.exp(-x))


def _softplus(x):
    return jnp.maximum(x, 0.0) + jnp.log(1.0 + jnp.exp(-jnp.abs(x)))


def _ffn_ln_body(h_ref, wg_ref, wu_ref, wd_ref, g_ref, b_ref, o_ref):
    h = h_ref[...]
    hb = h.astype(BF16)
    g = jnp.dot(hb, wg_ref[...], preferred_element_type=F32)
    u = jnp.dot(hb, wu_ref[...], preferred_element_type=F32)
    a = (_silu(g) * u).astype(BF16)
    y = jnp.dot(a, wd_ref[...], preferred_element_type=F32)
    o_ref[...] = _layer_norm(ALPHA * h + 0.5 * y, g_ref[...], b_ref[...])


def _ffn_ln(hf, wg, wu, wd, g, b):
    rows = hf.shape[0]
    return pl.pallas_call(
        _ffn_ln_body,
        grid=(rows // ROW_TILE,),
        in_specs=[
            pl.BlockSpec((ROW_TILE, D_MODEL), lambda i: (i, 0)),
            _const_spec((D_MODEL, D_FF)),
            _const_spec((D_MODEL, D_FF)),
            _const_spec((D_FF, D_MODEL)),
            _const_spec((1, D_MODEL)),
            _const_spec((1, D_MODEL)),
        ],
        out_specs=pl.BlockSpec((ROW_TILE, D_MODEL), lambda i: (i, 0)),
        out_shape=jax.ShapeDtypeStruct((rows, D_MODEL), F32),
        compiler_params=pltpu.CompilerParams(
            dimension_semantics=("parallel",), vmem_limit_bytes=VMEM_LIMIT),
        name="ffn_ln",
    )(hf, wg, wu, wd, g, b)


def _ffn_ln_first_body(lead_ref, x_ref, wg_ref, wu_ref, wd_ref, g_ref, b_ref, o_ref, h_ref):
    h_ref[...] = jnp.where(pl.program_id(1) == 0, lead_ref[...], x_ref[...])
    _ffn_ln_body(h_ref, wg_ref, wu_ref, wd_ref, g_ref, b_ref, o_ref)


def _ffn_ln_first(x, lead, wg, wu, wd, g, b):
    nb = x.shape[0]
    tiles = LP // OUT_TILE
    return pl.pallas_call(
        _ffn_ln_first_body,
        grid=(nb, tiles),
        in_specs=[
            _const_spec((OUT_TILE, D_MODEL)),
            pl.BlockSpec((None, OUT_TILE, D_MODEL), lambda bi, i: (bi, jnp.maximum(i - 1, 0), 0)),
            _const_spec((D_MODEL, D_FF)),
            _const_spec((D_MODEL, D_FF)),
            _const_spec((D_FF, D_MODEL)),
            _const_spec((1, D_MODEL)),
            _const_spec((1, D_MODEL)),
        ],
        out_specs=pl.BlockSpec((OUT_TILE, D_MODEL), lambda bi, i: (bi * tiles + i, 0)),
        out_shape=jax.ShapeDtypeStruct((nb * LP, D_MODEL), F32),
        scratch_shapes=[pltpu.VMEM((OUT_TILE, D_MODEL), F32)],
        compiler_params=pltpu.CompilerParams(
            dimension_semantics=("parallel", "arbitrary"), vmem_limit_bytes=VMEM_LIMIT),
        name="ffn_ln_first",
    )(lead, x, wg, wu, wd, g, b)


def _ffn_ln_final_body(h_ref, wg_ref, wu_ref, wd_ref, g_ref, b_ref, o_ref):
    @pl.when(pl.program_id(1) > 0)
    def _():
        _ffn_ln_body(h_ref, wg_ref, wu_ref, wd_ref, g_ref, b_ref, o_ref)


def _ffn_ln_final(h3, wg, wu, wd, g, b):
    nb = h3.shape[0]
    return pl.pallas_call(
        _ffn_ln_final_body,
        grid=(nb, LP // OUT_TILE),
        in_specs=[
            pl.BlockSpec((None, OUT_TILE, D_MODEL), lambda bi, i: (bi, i, 0)),
            _const_spec((D_MODEL, D_FF)),
            _const_spec((D_MODEL, D_FF)),
            _const_spec((D_FF, D_MODEL)),
            _const_spec((1, D_MODEL)),
            _const_spec((1, D_MODEL)),
        ],
        out_specs=pl.BlockSpec((None, OUT_TILE, D_MODEL),
                               lambda bi, i: (bi, jnp.maximum(i - 1, 0), 0)),
        out_shape=jax.ShapeDtypeStruct((nb, SEQ, D_MODEL), F32),
        compiler_params=pltpu.CompilerParams(
            dimension_semantics=("arbitrary", "arbitrary"), vmem_limit_bytes=VMEM_LIMIT),
        name="ffn_ln_final",
    )(h3, wg, wu, wd, g, b)


PROJ_TILE = 256
assert LP % PROJ_TILE == 0


def _proj_body(n_t, h_ref, w_ref, wt_ref, *o_refs):
    hb = h_ref[...].astype(BF16)
    off = 0
    for o_ref in o_refs[:-n_t]:
        n = o_ref.shape[-1]
        o_ref[...] = jnp.dot(hb, w_ref[:, off:off + n],
                             preferred_element_type=F32).astype(o_ref.dtype)
        off += n
    off = 0
    for o_ref in o_refs[-n_t:]:
        n = o_ref.shape[0]
        o_ref[...] = lax.dot_general(wt_ref[off:off + n, :], hb, (((1,), (1,)), ((), ())),
                                     preferred_element_type=F32).astype(o_ref.dtype)
        off += n


def _proj(hf, w, wt, widths, dtypes, t_widths, nb, name):
    rows = hf.shape[0]
    tiles_per_batch = LP // PROJ_TILE
    out_specs = [pl.BlockSpec((PROJ_TILE, n), lambda i: (i, 0)) for n in widths]
    out_specs += [pl.BlockSpec((None, n, PROJ_TILE),
                               lambda i: (i // tiles_per_batch, 0, i % tiles_per_batch))
                  for n in t_widths]
    out_shape = [jax.ShapeDtypeStruct((rows, n), dt) for n, dt in zip(widths, dtypes)]
    out_shape += [jax.ShapeDtypeStruct((nb, n, LP), BF16) for n in t_widths]
    return pl.pallas_call(
        functools.partial(_proj_body, len(t_widths)),
        grid=(rows // PROJ_TILE,),
        in_specs=[pl.BlockSpec((PROJ_TILE, D_MODEL), lambda i: (i, 0)),
                  _const_spec(w.shape), _const_spec(wt.shape)],
        out_specs=out_specs,
        out_shape=out_shape,
        compiler_params=pltpu.CompilerParams(
            dimension_semantics=("parallel",), vmem_limit_bytes=VMEM_LIMIT),
        name=name,
    )(hf, w, wt)


def _outproj_ln_body(n_in, h_ref, *refs):
    a_refs = refs[:n_in]
    w_refs = refs[n_in:2 * n_in]
    g_ref, b_ref, o_ref = refs[2 * n_in:]
    m = jnp.dot(a_refs[0][...], w_refs[0][...], preferred_element_type=F32)
    for a_ref, w_ref in zip(a_refs[1:], w_refs[1:]):
        m = m + jnp.dot(a_ref[...], w_ref[...], preferred_element_type=F32)
    o_ref[...] = _layer_norm(ALPHA * h_ref[...] + m, g_ref[...], b_ref[...])


def _outproj_ln(hf, acts, ws, g, b, name):
    rows = hf.shape[0]
    n_in = len(acts)
    in_specs = [pl.BlockSpec((ROW_TILE, D_MODEL), lambda i: (i, 0))]
    in_specs += [pl.BlockSpec((ROW_TILE, a.shape[1]), lambda i: (i, 0)) for a in acts]
    in_specs += [_const_spec(w.shape) for w in ws]
    in_specs += [_const_spec((1, D_MODEL)), _const_spec((1, D_MODEL))]
    return pl.pallas_call(
        functools.partial(_outproj_ln_body, n_in),
        grid=(rows // ROW_TILE,),
        in_specs=in_specs,
        out_specs=pl.BlockSpec((ROW_TILE, D_MODEL), lambda i: (i, 0)),
        out_shape=jax.ShapeDtypeStruct((rows, D_MODEL), F32),
        compiler_params=pltpu.CompilerParams(
            dimension_semantics=("parallel",), vmem_limit_bytes=VMEM_LIMIT),
        name=name,
    )(hf, *acts, *ws, g, b)


def _split_dot(x, e2_ref):
    hi = x.astype(BF16)
    lo = (x - hi.astype(F32)).astype(BF16)
    return jnp.dot(jnp.concatenate([hi, lo], axis=1), e2_ref[...], preferred_element_type=F32)


def _ssd_body(xbc_ref, z_ref, small_ref, convw_ref, convb_ref, sbias_ref, alog_ref,
              dskip_ref, normw_ref, expand_ref, y_ref, c_ref,
              ext_ref, state_ref, carry_ref):
    c = pl.program_id(1)

    @pl.when(c == 0)
    def _():
        ext_ref[0:CONV_HALO, :] = jnp.zeros((CONV_HALO, SSD_CONV_CH), F32)
        state_ref[...] = jnp.zeros_like(state_ref)
        carry_ref[...] = jnp.zeros_like(carry_ref)

    row = lax.broadcasted_iota(jnp.int32, (CHUNK, 1), 0)
    valid = (c * CHUNK + row) >= PADL

    ext_ref[CONV_HALO:, :] = xbc_ref[...]

    @pl.when(c * CHUNK < PADL)
    def _():
        ext_ref[CONV_HALO:, :] = jnp.where(valid, ext_ref[CONV_HALO:, :], 0.0)

    ext = ext_ref[...]
    conv = convb_ref[...] + convw_ref[SSD_CONV - 1:SSD_CONV, :] * ext[CONV_HALO:, :]
    for k in range(SSD_CONV - 1):
        shifted = pltpu.roll(ext, SSD_CONV - 1 - k, 0)[CONV_HALO:, :]
        conv = conv + convw_ref[k:k + 1, :] * shifted
    ext_ref[0:CONV_HALO, :] = ext_ref[CHUNK:CHUNK + CONV_HALO, :]
    xc = _silu(conv)
    xs = xc[:, :SSD_D_INNER]
    bm = xc[:, SSD_D_INNER:SSD_D_INNER + SSD_GROUPS * SSD_STATE].astype(BF16)
    cm = xc[:, SSD_D_INNER + SSD_GROUPS * SSD_STATE:].astype(BF16)

    lane = lax.broadcasted_iota(jnp.int32, (CHUNK, SMALL_W), 1)
    is_dt = lane < SSD_HEADS
    is_f = (lane >= SSD_HEADS) & (lane < SSD_HEADS + FOX_HEADS)
    v = small_ref[...] + sbias_ref[...]
    dt = jnp.where(valid & is_dt, _softplus(v), 0.0)
    log_f = jnp.where(valid & is_f, -_softplus(-v), 0.0)
    neg_a = -jnp.exp(alog_ref[...])
    steps = jnp.where(is_dt, dt * neg_a, log_f)
    r_i = lax.broadcasted_iota(jnp.int32, (CHUNK, CHUNK), 0)
    c_i = lax.broadcasted_iota(jnp.int32, (CHUNK, CHUNK), 1)
    causal = r_i >= c_i
    tril = jnp.where(causal, 1.0, 0.0).astype(F32)
    cum = jnp.dot(tril, steps, preferred_element_type=F32,
                  precision=lax.Precision.HIGHEST)
    c_total = jnp.where(is_f, cum + carry_ref[...], 0.0)
    c_ref[...] = c_total
    carry_ref[...] = c_total[CHUNK - 1:CHUNK, :]

    a_last = cum[CHUNK - 1:CHUNK, :]
    dt_x = _split_dot(dt, expand_ref)
    ea_x = _split_dot(jnp.exp(cum), expand_ref)
    de_x = _split_dot(jnp.exp(a_last - cum), expand_ref)
    x_dt = xs * dt_x
    xb = x_dt.astype(BF16)
    xe = (x_dt * de_x).astype(BF16)
    cum_t = cum.T
    lane_p = lax.broadcasted_iota(jnp.int32, (CHUNK, 2 * SSD_HEAD_DIM), 1)
    first_half = lane_p < SSD_HEAD_DIM

    y_groups = []
    for g in range(SSD_GROUPS):
        gs = slice(g * SSD_GROUP_W, (g + 1) * SSD_GROUP_W)
        bg = bm[:, g * SSD_STATE:(g + 1) * SSD_STATE]
        cg = cm[:, g * SSD_STATE:(g + 1) * SSD_STATE]
        cb = lax.dot_general(cg, bg, (((1,), (1,)), ((), ())),
                             preferred_element_type=F32)
        pair_out = []
        for pr in range(SSD_GROUP_W // (2 * SSD_HEAD_DIM)):
            col0 = g * SSD_GROUP_W + pr * 2 * SSD_HEAD_DIM
            x_pair = xb[:, col0:col0 + 2 * SSD_HEAD_DIM]
            ys = []
            for j in range(2):
                hd = col0 // SSD_HEAD_DIM + j
                diff = cum[:, hd:hd + 1] - cum_t[hd:hd + 1, :]
                decay = jnp.exp(jnp.where(causal, diff, -jnp.inf))
                mat = (cb * decay).astype(BF16)
                ys.append(jnp.dot(mat, x_pair, preferred_element_type=F32))
            pair_out.append(jnp.where(first_half, ys[0], ys[1]))
        y_diag = jnp.concatenate(pair_out, axis=1)
        st = state_ref[g]
        y_off = jnp.dot(cg, st.astype(BF16), preferred_element_type=F32) * ea_x[:, gs]
        new = lax.dot_general(bg, xe[:, gs], (((0,), (0,)), ((), ())),
                              preferred_element_type=F32)
        state_ref[g] = st * ea_x[CHUNK - 1:CHUNK, gs] + new
        yg = y_diag + y_off + dskip_ref[:, gs] * xs[:, gs]
        yg = yg * _silu(z_ref[:, gs])
        ms = jnp.mean(yg * yg, axis=-1, keepdims=True)
        y_groups.append(yg * lax.rsqrt(ms + RMS_EPS) * normw_ref[:, gs])
    y_ref[...] = jnp.concatenate(y_groups, axis=1).astype(y_ref.dtype)


def _ssd(xbc, z, small, convw, convb, sbias, alog, dskip, normw, expand):
    nb = xbc.shape[0]
    row_spec = lambda w: pl.BlockSpec((None, CHUNK, w), lambda bi, ci: (bi, ci, 0))
    return pl.pallas_call(
        _ssd_body,
        grid=(nb, LP // CHUNK),
        in_specs=[row_spec(SSD_CONV_CH), row_spec(SSD_D_INNER), row_spec(SMALL_W),
                  _const_spec(convw.shape), _const_spec(convb.shape),
                  _const_spec(sbias.shape), _const_spec(alog.shape),
                  _const_spec(dskip.shape), _const_spec(normw.shape),
                  _const_spec(expand.shape)],
        out_specs=[row_spec(SSD_D_INNER), row_spec(SMALL_W)],
        out_shape=[jax.ShapeDtypeStruct((nb, LP, SSD_D_INNER), BF16),
                   jax.ShapeDtypeStruct((nb, LP, SMALL_W), F32)],
        scratch_shapes=[pltpu.VMEM((CHUNK + CONV_HALO, SSD_CONV_CH), F32),
                        pltpu.VMEM((SSD_GROUPS, SSD_STATE, SSD_GROUP_W), F32),
                        pltpu.VMEM((1, SMALL_W), F32)],
        compiler_params=pltpu.CompilerParams(
            dimension_semantics=("arbitrary", "arbitrary"), vmem_limit_bytes=VMEM_LIMIT),
        name="ssd",
    )(xbc, z, small, convw, convb, sbias, alog, dskip, normw, expand)


N_TK_CHUNKS = ATT_TK // LANES
N_TQ_CHUNKS = ATT_TQ // LANES
ONES_ROWS = 16
assert ATT_TQ == ATT_TK and PADL <= ATT_TK
LOG2E = math.log2(math.e)
Q_SCALE = LOG2E * HEAD_DIM ** -0.5


def _init_softmax_state(m_ref, acc_ref):
    m_ref[...] = jnp.full(m_ref.shape, NEG, F32)
    acc_ref[...] = jnp.zeros_like(acc_ref)


def _run_key_tiles(qb, scores, step, buf_a, buf_b):
    odd = qb % 2

    @pl.when(odd == 0)
    def _():
        scores(0, buf_a)

    @pl.when(odd == 1)
    def _():
        scores(0, buf_b)
        step(0, buf_b, buf_a)

    def two_below_diagonal(i, carry):
        kb = odd + 2 * i
        step(kb, buf_a, buf_b)
        step(kb + 1, buf_b, buf_a)
        return carry

    lax.fori_loop(0, qb // 2, two_below_diagonal, 0)
    step(qb, buf_a, None)


def _query_operands(qt_ref):
    zero_rows = jnp.zeros((HEAD_DIM, ATT_TQ), BF16)
    return [jnp.concatenate([qt_ref[:HEAD_DIM, :], zero_rows], axis=0),
            jnp.concatenate([zero_rows, qt_ref[HEAD_DIM:, :]], axis=0)]


def _value_rows(vt_ref, row0, n_rows, start):
    vt = vt_ref[row0:row0 + n_rows, pl.ds(start, ATT_TK)]
    return jnp.concatenate([vt, jnp.ones((ONES_ROWS, ATT_TK), BF16)], axis=0)


def _softmax_update(s_t, col_max, vt_aug, m_ref, acc_ref, idx):
    m_old = m_ref[idx]
    m_new = jnp.maximum(m_old, col_max)
    alpha = jnp.exp2(m_old - m_new)
    p_t = jnp.exp2(s_t - m_new).astype(BF16)
    pv = jnp.dot(vt_aug, p_t, preferred_element_type=F32)
    acc_ref[idx] = alpha * acc_ref[idx] + pv
    m_ref[idx] = m_new


def _causal_mask():
    kpos = lax.broadcasted_iota(jnp.int32, (ATT_TK, ATT_TQ), 0)
    qpos = lax.broadcasted_iota(jnp.int32, (ATT_TK, ATT_TQ), 1)
    return kpos <= qpos


def _fox_body(qt_ref, k_ref, vt_ref, c_ref, o_ref, m_ref, acc_ref,
              sa_ref, pa_ref, sb_ref, pb_ref):
    qb = pl.program_id(2)
    _init_softmax_state(m_ref, acc_ref)
    buf_a, buf_b = (sa_ref, pa_ref), (sb_ref, pb_ref)
    q_heads = _query_operands(qt_ref)
    lane = lax.broadcasted_iota(jnp.int32, (ATT_TK, LANES), 1)
    first_lane = SSD_HEADS + 2 * pl.program_id(1)

    def scores(kb, buf):
        s_dst, max_dst = buf
        start = pl.multiple_of(kb * ATT_TK, ATT_TK)
        k_rows = k_ref[pl.ds(start, ATT_TK), :]
        c_rows = c_ref[pl.ds(start, ATT_TK), :]
        for j in range(2):
            ck = jnp.sum(jnp.where(lane == first_lane + j, c_rows, 0.0), axis=1, keepdims=True)
            s_t = jnp.dot(k_rows, q_heads[j], preferred_element_type=F32) - ck
            s_dst[j] = s_t
            max_dst[j] = jnp.max(s_t, axis=0, keepdims=True)

    def step(kb, src, nxt):
        s_src, max_src = src
        start = pl.multiple_of(kb * ATT_TK, ATT_TK)
        if nxt is not None:
            scores(kb + 1, nxt)
        for j in range(2):
            s_t = s_src[j]
            col_max = max_src[j]
            if nxt is None:
                s_t = jnp.where(_causal_mask(), s_t, NEG)
                col_max = jnp.max(s_t, axis=0, keepdims=True)
            vt_aug = _value_rows(vt_ref, j * HEAD_DIM, HEAD_DIM, start)
            _softmax_update(s_t, col_max, vt_aug, m_ref, acc_ref, j)

    _run_key_tiles(qb, scores, step, buf_a, buf_b)

    o_t = jnp.concatenate(
        [acc_ref[j, :HEAD_DIM, :] / acc_ref[j, HEAD_DIM:HEAD_DIM + 1, :] for j in range(2)], axis=0)
    o_ref[...] = o_t.T.astype(o_ref.dtype)


def _fox_attention(qt, k, vt, c):
    nb = qt.shape[0]
    pairs = FOX_HEADS // 2
    acc_rows = HEAD_DIM + ONES_ROWS
    return pl.pallas_call(
        _fox_body,
        grid=(nb, pairs, LP // ATT_TQ),
        in_specs=[
            pl.BlockSpec((None, LANES, ATT_TQ), lambda bi, hp, qi: (bi, hp, qi)),
            pl.BlockSpec((None, LP, LANES), lambda bi, hp, qi: (bi, 0, hp)),
            pl.BlockSpec((None, LANES, LP), lambda bi, hp, qi: (bi, hp, 0)),
            pl.BlockSpec((None, LP, SMALL_W), lambda bi, hp, qi: (bi, 0, 0)),
        ],
        out_specs=pl.BlockSpec((None, ATT_TQ, LANES), lambda bi, hp, qi: (bi, qi, hp)),
        out_shape=jax.ShapeDtypeStruct((nb, LP, FOX_WIDTH), BF16),
        scratch_shapes=[pltpu.VMEM((2, 1, ATT_TQ), F32),
                        pltpu.VMEM((2, acc_rows, ATT_TQ), F32)]
                       + [pltpu.VMEM((2, ATT_TK, ATT_TQ), F32),
                          pltpu.VMEM((2, 1, ATT_TQ), F32)] * 2,
        compiler_params=pltpu.CompilerParams(
            dimension_semantics=("parallel", "parallel", "arbitrary"),
            vmem_limit_bytes=VMEM_LIMIT),
        name="fox_attention",
    )(qt, k, vt, c)


def _bias_tiles(bias_ref):
    rows = jnp.broadcast_to(bias_ref[...], (CHUNK, 2 * CHUNK))
    rolled = pltpu.roll(rows, 0, 1, stride=1, stride_axis=0)
    return rolled[:, :CHUNK], rolled[:, CHUNK:]


def _diagonal_bias(tiles):
    zeros = jnp.zeros((CHUNK, CHUNK), F32)
    rows = []
    for a in range(N_TK_CHUNKS):
        rows.append(jnp.concatenate(
            [tiles[b - a] if b - a in (0, 1) else zeros for b in range(N_TQ_CHUNKS)], axis=1))
    return jnp.concatenate(rows, axis=0)


def _diff_body(lambda_init, qt_ref, k_ref, vt_ref, bias_ref, lam_ref, subln_ref, o_ref,
               m_ref, acc_ref, tiles_ref, sa_ref, pa_ref, sb_ref, pb_ref):
    qb = pl.program_id(2)
    _init_softmax_state(m_ref, acc_ref)
    buf_a, buf_b = (sa_ref, pa_ref), (sb_ref, pb_ref)
    tile0, tile1 = _bias_tiles(bias_ref)
    tiles_ref[0] = tile0
    tiles_ref[1] = tile1
    q_parts = _query_operands(qt_ref)
    pad_rows = -(-PADL // CHUNK) * CHUNK
    corner0 = ATT_TK - CHUNK

    def scores(kb, buf):
        s_dst, max_dst = buf
        start = pl.multiple_of(kb * ATT_TK, ATT_TK)
        k_rows = k_ref[pl.ds(start, ATT_TK), :]
        kpos = lax.broadcasted_iota(jnp.int32, (pad_rows, ATT_TQ), 0)
        not_padding = (kpos >= PADL) | (kb > 0)
        for j in range(2):
            s_t = jnp.dot(k_rows, q_parts[j], preferred_element_type=F32)
            s_t = jnp.concatenate(
                [jnp.where(not_padding, s_t[:pad_rows], NEG), s_t[pad_rows:]], axis=0)
            s_dst[j] = s_t
            max_dst[j] = jnp.max(s_t, axis=0, keepdims=True)

    def step(kb, src, nxt):
        s_src, max_src = src
        start = pl.multiple_of(kb * ATT_TK, ATT_TK)
        diagonal = nxt is None
        if diagonal:
            bias = _diagonal_bias((tiles_ref[0], tiles_ref[1]))
            visible = _causal_mask()
        else:
            scores(kb + 1, nxt)
            corner = jnp.where(kb == qb - 1, 1.0, 0.0) * tiles_ref[1]
        vt_aug = _value_rows(vt_ref, 0, LANES, start)
        for j in range(2):
            s_t = s_src[j]
            if diagonal:
                s_t = jnp.where(visible, s_t + bias, NEG)
                col_max = jnp.max(s_t, axis=0, keepdims=True)
            else:
                near = s_t[corner0:, :CHUNK] + corner
                s_t = jnp.concatenate(
                    [s_t[:corner0],
                     jnp.concatenate([near, s_t[corner0:, CHUNK:]], axis=1)], axis=0)
                first = jnp.maximum(jnp.max(s_t[:corner0, :CHUNK], axis=0, keepdims=True),
                                    jnp.max(near, axis=0, keepdims=True))
                col_max = jnp.concatenate([first, max_src[j][:, CHUNK:]], axis=1)
            _softmax_update(s_t, col_max, vt_aug, m_ref, acc_ref, j)

    _run_key_tiles(qb, scores, step, buf_a, buf_b)

    lam1 = jnp.exp(jnp.sum(lam_ref[0:1, :] * lam_ref[1:2, :], axis=-1, keepdims=True))
    lam2 = jnp.exp(jnp.sum(lam_ref[2:3, :] * lam_ref[3:4, :], axis=-1, keepdims=True))
    lam = lam1 - lam2 + lambda_init
    o_t = (acc_ref[0, :LANES, :] / acc_ref[0, LANES:LANES + 1, :]
           - lam * (acc_ref[1, :LANES, :] / acc_ref[1, LANES:LANES + 1, :]))
    o = o_t.T
    ms = jnp.mean(o * o, axis=-1, keepdims=True)
    o = o * lax.rsqrt(ms + RMS_EPS) * subln_ref[...] * (1.0 - lambda_init)
    o_ref[...] = o.astype(o_ref.dtype)


def _diff_attention(qt, k, vt, bias_rows, lam_rows, subln, lambda_init):
    nb = qt.shape[0]
    acc_rows = LANES + ONES_ROWS
    return pl.pallas_call(
        functools.partial(_diff_body, lambda_init),
        grid=(nb, DIFF_HEADS, LP // ATT_TQ),
        in_specs=[
            pl.BlockSpec((None, LANES, ATT_TQ), lambda bi, hd, qi: (bi, hd, qi)),
            pl.BlockSpec((None, LP, LANES), lambda bi, hd, qi: (bi, 0, hd)),
            pl.BlockSpec((None, LANES, LP), lambda bi, hd, qi: (bi, hd, 0)),
            pl.BlockSpec((None, 1, 2 * CHUNK), lambda bi, hd, qi: (hd, 0, 0)),
            _const_spec(lam_rows.shape),
            _const_spec(subln.shape),
        ],
        out_specs=pl.BlockSpec((None, ATT_TQ, LANES), lambda bi, hd, qi: (bi, qi, hd)),
        out_shape=jax.ShapeDtypeStruct((nb, LP, DIFF_HEADS * LANES), BF16),
        scratch_shapes=[pltpu.VMEM((2, 1, ATT_TQ), F32),
                        pltpu.VMEM((2, acc_rows, ATT_TQ), F32),
                        pltpu.VMEM((2, CHUNK, CHUNK), F32)]
                       + [pltpu.VMEM((2, ATT_TK, ATT_TQ), F32),
                          pltpu.VMEM((2, 1, ATT_TQ), F32)] * 2,
        compiler_params=pltpu.CompilerParams(
            dimension_semantics=("parallel", "parallel", "arbitrary"),
            vmem_limit_bytes=VMEM_LIMIT),
        name="diff_attention",
    )(qt, k, vt, bias_rows, lam_rows, subln)


def _t5_bucket(n):
    max_exact = N_BUCKETS // 2
    nf = jnp.maximum(n, 1).astype(F32)
    large = max_exact + (jnp.log(nf / max_exact) / math.log(128 / max_exact)
                         * (N_BUCKETS - max_exact)).astype(jnp.int32)
    large = jnp.minimum(large, N_BUCKETS - 1)
    return jnp.where(n < max_exact, n, large)


def _relative_bias_rows(rel_table):
    dist = jnp.arange(2 * CHUNK)
    by_dist = LOG2E * (rel_table[_t5_bucket(dist)] - rel_table[N_BUCKETS - 1])
    return by_dist.T[:, None, :].astype(F32)


def _pad_lanes(vec, width=LANES):
    return jnp.pad(vec, (0, width - vec.shape[0]))[None, :].astype(F32)


def kernel(x, meta_tokens, ln_gain, ln_bias, ffn1_w_gate, ffn1_w_up, ffn1_w_down, ffn2_w_gate, ffn2_w_up, ffn2_w_down, even_w_in, even_conv_w, even_conv_b, ssd_dt_bias, ssd_a_log, ssd_d_skip, ssd_norm_w, fox_f_bias, even_w_out, diff_w_qkv, diff_lambda_q1, diff_lambda_k1, diff_lambda_q2, diff_lambda_k2, diff_subln_w, diff_w_o, rel_bias_table):
    nb = x.shape[0]
    lead = jnp.concatenate([jnp.zeros((PADL, D_MODEL), x.dtype), meta_tokens.astype(x.dtype)], axis=0)

    def ln_params(l, i):
        return ln_gain[l, i][None, :], ln_bias[l, i][None, :]

    def ffn_weights(wg, wu, wd, l):
        return wg[l].astype(BF16), wu[l].astype(BF16), wd[l].astype(BF16)

    hf = _ffn_ln_first(x, lead, *ffn_weights(ffn1_w_gate, ffn1_w_up, ffn1_w_down, 0),
                       *ln_params(0, 0))

    w_in = even_w_in[0]
    o_z, o_xbc = 0, SSD_D_INNER
    o_dt = o_xbc + SSD_CONV_CH
    o_q = o_dt + SSD_HEADS
    o_k, o_v = o_q + FOX_WIDTH, o_q + 2 * FOX_WIDTH
    o_f = o_q + 3 * FOX_WIDTH
    w_small = jnp.concatenate(
        [w_in[:, o_dt:o_q], w_in[:, o_f:],
         jnp.zeros((D_MODEL, SMALL_W - SSD_HEADS - FOX_HEADS), w_in.dtype)], axis=1)
    w_even = jnp.concatenate([w_in[:, o_z:o_dt], w_in[:, o_k:o_v], w_small], axis=1).astype(BF16)
    wt_even = jnp.concatenate([w_in[:, o_q:o_k] * Q_SCALE, w_in[:, o_v:o_f]], axis=1).T.astype(BF16)
    z, xbc, k, small, qt, vt = _proj(
        hf, w_even, wt_even, (SSD_D_INNER, SSD_CONV_CH, FOX_WIDTH, SMALL_W),
        (F32, F32, BF16, F32), (FOX_WIDTH, FOX_WIDTH), nb, "even_in_proj")

    sbias = _pad_lanes(jnp.concatenate([ssd_dt_bias[0], fox_f_bias[0]]))
    alog = _pad_lanes(ssd_a_log[0])
    dskip = jnp.repeat(ssd_d_skip[0], SSD_HEAD_DIM)[None, :].astype(F32)
    expand = np.zeros((SMALL_W, SSD_D_INNER), np.float32)
    expand[np.arange(SSD_D_INNER) // SSD_HEAD_DIM, np.arange(SSD_D_INNER)] = 1.0
    y, cfull = _ssd(xbc.reshape(nb, LP, SSD_CONV_CH), z.reshape(nb, LP, SSD_D_INNER),
                    small.reshape(nb, LP, SMALL_W), even_conv_w[0], even_conv_b[0][None, :],
                    sbias, alog, dskip, ssd_norm_w[0][None, :],
                    jnp.asarray(np.concatenate([expand, expand], axis=0), BF16))
    ck = jnp.where(jnp.arange(LP)[None, :, None] < PADL, -NEG, LOG2E * cfull)
    o = _fox_attention(qt, k.reshape(nb, LP, FOX_WIDTH), vt, ck)
    w_out = even_w_out[0].astype(BF16)
    hf = _outproj_ln(hf, [y.reshape(nb * LP, SSD_D_INNER), o.reshape(nb * LP, FOX_WIDTH)],
                     [w_out[:SSD_D_INNER], w_out[SSD_D_INNER:]], *ln_params(0, 1),
                     name="even_out_proj_ln")
    hf = _ffn_ln(hf, *ffn_weights(ffn2_w_gate, ffn2_w_up, ffn2_w_down, 0), *ln_params(0, 2))

    hf = _ffn_ln(hf, *ffn_weights(ffn1_w_gate, ffn1_w_up, ffn1_w_down, 1), *ln_params(1, 0))
    qw = DIFF_HEADS * 2 * HEAD_DIM
    w_qkv = diff_w_qkv[0]
    wt_diff = jnp.concatenate([w_qkv[:, :qw] * Q_SCALE, w_qkv[:, 2 * qw:]], axis=1).T.astype(BF16)
    k, qt, vt = _proj(hf, w_qkv[:, qw:2 * qw].astype(BF16), wt_diff, (qw,), (BF16,),
                      (qw, DIFF_HEADS * LANES), nb, "diff_qkv_proj")
    lambda_init = 0.8 - 0.6 * math.exp(-0.3 * 1)
    lam_rows = jnp.concatenate(
        [_pad_lanes(diff_lambda_q1[0]), _pad_lanes(diff_lambda_k1[0]),
         _pad_lanes(diff_lambda_q2[0]), _pad_lanes(diff_lambda_k2[0]),
         jnp.zeros((4, LANES), F32)], axis=0)
    o = _diff_attention(qt, k.reshape(nb, LP, qw), vt, _relative_bias_rows(rel_bias_table),
                        lam_rows, diff_subln_w[0][None, :], lambda_init)
    hf = _outproj_ln(hf, [o.reshape(nb * LP, DIFF_HEADS * LANES)], [diff_w_o[0].astype(BF16)],
                     *ln_params(1, 1), name="diff_out_proj_ln")
    return _ffn_ln_final(hf.reshape(nb, LP, D_MODEL),
                         *ffn_weights(ffn2_w_gate, ffn2_w_up, ffn2_w_down, 1), *ln_params(1, 2))
```

```python
import functools
import math

import numpy as np
import jax
import jax.numpy as jnp
from jax import lax
from jax.experimental import pallas as pl
from jax.experimental.pallas import tpu as pltpu

F32 = jnp.float32
BF16 = jnp.bfloat16

D_MODEL = 1024
SEQ = 8192
DEPTH = 2
N_META = 16
CHUNK = 128
SSD_D_INNER = 2048
SSD_HEAD_DIM = 64
SSD_HEADS = 32
SSD_GROUPS = 4
SSD_GROUP_W = SSD_D_INNER // SSD_GROUPS
SSD_STATE = 128
SSD_CONV = 4
SSD_CONV_CH = SSD_D_INNER + 2 * SSD_GROUPS * SSD_STATE
FOX_HEADS = 16
FOX_WIDTH = 1024
HEAD_DIM = 64
DIFF_HEADS = 8
N_BUCKETS = 32
D_FF = 2816
ALPHA = (2 * DEPTH) ** 0.25
LN_EPS = 1e-5
RMS_EPS = 1e-5
NEG = -1e30

LANES = 128
LP = 8448
PADL = LP - SEQ - N_META
ROW_TILE = 512
OUT_TILE = 256
ATT_TQ = 768
ATT_TK = 768
CONV_HALO = 8
SMALL_W = LANES
VMEM_LIMIT = 56 * 1024 * 1024

assert PADL % CHUNK == CHUNK - N_META
assert LP % ATT_TQ == 0 and ATT_TQ % ATT_TK == 0 and ATT_TK % LANES == 0
assert LP % CHUNK == 0 and (2 * LP) % ROW_TILE == 0
assert PADL <= OUT_TILE and (LP - OUT_TILE) == SEQ


def _const_spec(shape):
    nd = len(shape)
    return pl.BlockSpec(shape, lambda *_: (0,) * nd, pipeline_mode=pl.Buffered(1))


def _layer_spec(shape, layer):
    nd = len(shape)
    return pl.BlockSpec((None,) + tuple(shape), lambda *_: (layer,) + (0,) * nd,
                        pipeline_mode=pl.Buffered(1))


def _ffn_weight_specs(layer):
    return [_layer_spec((D_MODEL, D_FF), layer), _layer_spec((D_MODEL, D_FF), layer),
            _layer_spec((D_FF, D_MODEL), layer)]


def _layer_norm(r, g, b):
    mu = jnp.mean(r, axis=-1, keepdims=True)
    d = r - mu
    var = jnp.mean(d * d, axis=-1, keepdims=True)
    return d * lax.rsqrt(var + LN_EPS) * g + b


def _silu(x):
    return x / (1.0 + jnp.exp(-x))


def _softplus(x):
    return jnp.maximum(x, 0.0) + jnp.log(1.0 + jnp.exp(-jnp.abs(x)))


def _ffn_ln_body(h_ref, wg_ref, wu_ref, wd_ref, g_ref, b_ref, o_ref):
    h = h_ref[...]
    hb = h.astype(BF16)
    g = jnp.dot(hb, wg_ref[...], preferred_element_type=F32)
    u = jnp.dot(hb, wu_ref[...], preferred_element_type=F32)
    a = (_silu(g) * u).astype(BF16)
    y = jnp.dot(a, wd_ref[...], preferred_element_type=F32)
    o_ref[...] = _layer_norm(ALPHA * h + 0.5 * y, g_ref[...], b_ref[...])


def _ffn_ln(hf, wg, wu, wd, layer, g, b):
    rows = hf.shape[0]
    return pl.pallas_call(
        _ffn_ln_body,
        grid=(rows // ROW_TILE,),
        in_specs=[
            pl.BlockSpec((ROW_TILE, D_MODEL), lambda i: (i, 0)),
            *_ffn_weight_specs(layer),
            _const_spec((1, D_MODEL)),
            _const_spec((1, D_MODEL)),
        ],
        out_specs=pl.BlockSpec((ROW_TILE, D_MODEL), lambda i: (i, 0)),
        out_shape=jax.ShapeDtypeStruct((rows, D_MODEL), F32),
        compiler_params=pltpu.CompilerParams(
            dimension_semantics=("parallel",), vmem_limit_bytes=VMEM_LIMIT),
        name="ffn_ln",
    )(hf, wg, wu, wd, g, b)


def _ffn_ln_first_body(lead_ref, x_ref, wg_ref, wu_ref, wd_ref, g_ref, b_ref, o_ref, h_ref):
    h_ref[...] = jnp.where(pl.program_id(1) == 0, lead_ref[...], x_ref[...])
    _ffn_ln_body(h_ref, wg_ref, wu_ref, wd_ref, g_ref, b_ref, o_ref)


def _ffn_ln_first(x, lead, wg, wu, wd, layer, g, b):
    nb = x.shape[0]
    tiles = LP // OUT_TILE
    return pl.pallas_call(
        _ffn_ln_first_body,
        grid=(nb, tiles),
        in_specs=[
            _const_spec((OUT_TILE, D_MODEL)),
            pl.BlockSpec((None, OUT_TILE, D_MODEL), lambda bi, i: (bi, jnp.maximum(i - 1, 0), 0)),
            *_ffn_weight_specs(layer),
            _const_spec((1, D_MODEL)),
            _const_spec((1, D_MODEL)),
        ],
        out_specs=pl.BlockSpec((OUT_TILE, D_MODEL), lambda bi, i: (bi * tiles + i, 0)),
        out_shape=jax.ShapeDtypeStruct((nb * LP, D_MODEL), F32),
        scratch_shapes=[pltpu.VMEM((OUT_TILE, D_MODEL), F32)],
        compiler_params=pltpu.CompilerParams(
            dimension_semantics=("parallel", "arbitrary"), vmem_limit_bytes=VMEM_LIMIT),
        name="ffn_ln_first",
    )(lead, x, wg, wu, wd, g, b)


def _ffn_ln_final_body(h_ref, wg_ref, wu_ref, wd_ref, g_ref, b_ref, o_ref):
    @pl.when(pl.program_id(1) > 0)
    def _():
        _ffn_ln_body(h_ref, wg_ref, wu_ref, wd_ref, g_ref, b_ref, o_ref)


def _ffn_ln_final(h3, wg, wu, wd, layer, g, b):
    nb = h3.shape[0]
    return pl.pallas_call(
        _ffn_ln_final_body,
        grid=(nb, LP // OUT_TILE),
        in_specs=[
            pl.BlockSpec((None, OUT_TILE, D_MODEL), lambda bi, i: (bi, i, 0)),
            *_ffn_weight_specs(layer),
            _const_spec((1, D_MODEL)),
            _const_spec((1, D_MODEL)),
        ],
        out_specs=pl.BlockSpec((None, OUT_TILE, D_MODEL),
                               lambda bi, i: (bi, jnp.maximum(i - 1, 0), 0)),
        out_shape=jax.ShapeDtypeStruct((nb, SEQ, D_MODEL), F32),
        compiler_params=pltpu.CompilerParams(
            dimension_semantics=("arbitrary", "arbitrary"), vmem_limit_bytes=VMEM_LIMIT),
        name="ffn_ln_final",
    )(h3, wg, wu, wd, g, b)


PROJ_TILE = 256
assert LP % PROJ_TILE == 0


def _proj_body(n_t, h_ref, w_ref, wt_ref, *o_refs):
    hb = h_ref[...].astype(BF16)
    off = 0
    for o_ref in o_refs[:-n_t]:
        n = o_ref.shape[-1]
        o_ref[...] = jnp.dot(hb, w_ref[:, off:off + n],
                             preferred_element_type=F32).astype(o_ref.dtype)
        off += n
    off = 0
    for o_ref in o_refs[-n_t:]:
        n = o_ref.shape[0]
        o_ref[...] = lax.dot_general(wt_ref[off:off + n, :], hb, (((1,), (1,)), ((), ())),
                                     preferred_element_type=F32).astype(o_ref.dtype)
        off += n


def _proj(hf, w, wt, widths, dtypes, t_widths, nb, name):
    rows = hf.shape[0]
    tiles_per_batch = LP // PROJ_TILE
    out_specs = [pl.BlockSpec((PROJ_TILE, n), lambda i: (i, 0)) for n in widths]
    out_specs += [pl.BlockSpec((None, n, PROJ_TILE),
                               lambda i: (i // tiles_per_batch, 0, i % tiles_per_batch))
                  for n in t_widths]
    out_shape = [jax.ShapeDtypeStruct((rows, n), dt) for n, dt in zip(widths, dtypes)]
    out_shape += [jax.ShapeDtypeStruct((nb, n, LP), BF16) for n in t_widths]
    return pl.pallas_call(
        functools.partial(_proj_body, len(t_widths)),
        grid=(rows // PROJ_TILE,),
        in_specs=[pl.BlockSpec((PROJ_TILE, D_MODEL), lambda i: (i, 0)),
                  _const_spec(w.shape), _const_spec(wt.shape)],
        out_specs=out_specs,
        out_shape=out_shape,
        compiler_params=pltpu.CompilerParams(
            dimension_semantics=("parallel",), vmem_limit_bytes=VMEM_LIMIT),
        name=name,
    )(hf, w, wt)


def _outproj_ln_body(n_in, h_ref, *refs):
    a_refs = refs[:n_in]
    w_refs = refs[n_in:2 * n_in]
    g_ref, b_ref, o_ref = refs[2 * n_in:]
    m = jnp.dot(a_refs[0][...], w_refs[0][...], preferred_element_type=F32)
    for a_ref, w_ref in zip(a_refs[1:], w_refs[1:]):
        m = m + jnp.dot(a_ref[...], w_ref[...], preferred_element_type=F32)
    o_ref[...] = _layer_norm(ALPHA * h_ref[...] + m, g_ref[...], b_ref[...])


def _outproj_ln(hf, acts, ws, g, b, name):
    rows = hf.shape[0]
    n_in = len(acts)
    in_specs = [pl.BlockSpec((ROW_TILE, D_MODEL), lambda i: (i, 0))]
    in_specs += [pl.BlockSpec((ROW_TILE, a.shape[1]), lambda i: (i, 0)) for a in acts]
    in_specs += [_const_spec(w.shape) for w in ws]
    in_specs += [_const_spec((1, D_MODEL)), _const_spec((1, D_MODEL))]
    return pl.pallas_call(
        functools.partial(_outproj_ln_body, n_in),
        grid=(rows // ROW_TILE,),
        in_specs=in_specs,
        out_specs=pl.BlockSpec((ROW_TILE, D_MODEL), lambda i: (i, 0)),
        out_shape=jax.ShapeDtypeStruct((rows, D_MODEL), F32),
        compiler_params=pltpu.CompilerParams(
            dimension_semantics=("parallel",), vmem_limit_bytes=VMEM_LIMIT),
        name=name,
    )(hf, *acts, *ws, g, b)


def _split_dot(x, e2_ref):
    hi = x.astype(BF16)
    lo = (x - hi.astype(F32)).astype(BF16)
    return jnp.dot(jnp.concatenate([hi, lo], axis=1), e2_ref[...], preferred_element_type=F32)


def _ssd_body(xbc_ref, z_ref, small_ref, convw_ref, convb_ref, sbias_ref, alog_ref,
              dskip_ref, normw_ref, expand_ref, y_ref, c_ref,
              ext_ref, state_ref, carry_ref):
    c = pl.program_id(1)

    @pl.when(c == 0)
    def _():
        ext_ref[0:CONV_HALO, :] = jnp.zeros((CONV_HALO, SSD_CONV_CH), F32)
        state_ref[...] = jnp.zeros_like(state_ref)
        carry_ref[...] = jnp.zeros_like(carry_ref)

    row = lax.broadcasted_iota(jnp.int32, (CHUNK, 1), 0)
    valid = (c * CHUNK + row) >= PADL

    ext_ref[CONV_HALO:, :] = xbc_ref[...]

    @pl.when(c * CHUNK < PADL)
    def _():
        ext_ref[CONV_HALO:, :] = jnp.where(valid, ext_ref[CONV_HALO:, :], 0.0)

    ext = ext_ref[...]
    conv = convb_ref[...] + convw_ref[SSD_CONV - 1:SSD_CONV, :] * ext[CONV_HALO:, :]
    for k in range(SSD_CONV - 1):
        shifted = pltpu.roll(ext, SSD_CONV - 1 - k, 0)[CONV_HALO:, :]
        conv = conv + convw_ref[k:k + 1, :] * shifted
    ext_ref[0:CONV_HALO, :] = ext_ref[CHUNK:CHUNK + CONV_HALO, :]
    xc = _silu(conv)
    xs = xc[:, :SSD_D_INNER]
    bm = xc[:, SSD_D_INNER:SSD_D_INNER + SSD_GROUPS * SSD_STATE].astype(BF16)
    cm = xc[:, SSD_D_INNER + SSD_GROUPS * SSD_STATE:].astype(BF16)

    lane = lax.broadcasted_iota(jnp.int32, (CHUNK, SMALL_W), 1)
    is_dt = lane < SSD_HEADS
    is_f = (lane >= SSD_HEADS) & (lane < SSD_HEADS + FOX_HEADS)
    v = small_ref[...] + sbias_ref[...]
    dt = jnp.where(valid & is_dt, _softplus(v), 0.0)
    log_f = jnp.where(valid & is_f, -_softplus(-v), 0.0)
    neg_a = -jnp.exp(alog_ref[...])
    steps = jnp.where(is_dt, dt * neg_a, log_f)
    r_i = lax.broadcasted_iota(jnp.int32, (CHUNK, CHUNK), 0)
    c_i = lax.broadcasted_iota(jnp.int32, (CHUNK, CHUNK), 1)
    causal = r_i >= c_i
    tril = jnp.where(causal, 1.0, 0.0).astype(F32)
    cum = jnp.dot(tril, steps, preferred_element_type=F32,
                  precision=lax.Precision.HIGHEST)
    c_total = jnp.where(is_f, cum + carry_ref[...], 0.0)
    c_ref[...] = c_total
    carry_ref[...] = c_total[CHUNK - 1:CHUNK, :]

    a_last = cum[CHUNK - 1:CHUNK, :]
    dt_x = _split_dot(dt, expand_ref)
    ea_x = _split_dot(jnp.exp(cum), expand_ref)
    de_x = _split_dot(jnp.exp(a_last - cum), expand_ref)
    x_dt = xs * dt_x
    xb = x_dt.astype(BF16)
    xe = (x_dt * de_x).astype(BF16)
    cum_t = cum.T
    lane_p = lax.broadcasted_iota(jnp.int32, (CHUNK, 2 * SSD_HEAD_DIM), 1)
    first_half = lane_p < SSD_HEAD_DIM

    y_groups = []
    for g in range(SSD_GROUPS):
        gs = slice(g * SSD_GROUP_W, (g + 1) * SSD_GROUP_W)
        bg = bm[:, g * SSD_STATE:(g + 1) * SSD_STATE]
        cg = cm[:, g * SSD_STATE:(g + 1) * SSD_STATE]
        cb = lax.dot_general(cg, bg, (((1,), (1,)), ((), ())),
                             preferred_element_type=F32)
        pair_out = []
        for pr in range(SSD_GROUP_W // (2 * SSD_HEAD_DIM)):
            col0 = g * SSD_GROUP_W + pr * 2 * SSD_HEAD_DIM
            x_pair = xb[:, col0:col0 + 2 * SSD_HEAD_DIM]
            ys = []
            for j in range(2):
                hd = col0 // SSD_HEAD_DIM + j
                diff = cum[:, hd:hd + 1] - cum_t[hd:hd + 1, :]
                decay = jnp.exp(jnp.where(causal, diff, -jnp.inf))
                mat = (cb * decay).astype(BF16)
                ys.append(jnp.dot(mat, x_pair, preferred_element_type=F32))
            pair_out.append(jnp.where(first_half, ys[0], ys[1]))
        y_diag = jnp.concatenate(pair_out, axis=1)
        st = state_ref[g]
        y_off = jnp.dot(cg, st.astype(BF16), preferred_element_type=F32) * ea_x[:, gs]
        new = lax.dot_general(bg, xe[:, gs], (((0,), (0,)), ((), ())),
                              preferred_element_type=F32)
        state_ref[g] = st * ea_x[CHUNK - 1:CHUNK, gs] + new
        yg = y_diag + y_off + dskip_ref[:, gs] * xs[:, gs]
        yg = yg * _silu(z_ref[:, gs])
        ms = jnp.mean(yg * yg, axis=-1, keepdims=True)
        y_groups.append(yg * lax.rsqrt(ms + RMS_EPS) * normw_ref[:, gs])
    y_ref[...] = jnp.concatenate(y_groups, axis=1).astype(y_ref.dtype)


def _ssd(xbc, z, small, convw, convb, sbias, alog, dskip, normw, expand):
    nb = xbc.shape[0]
    row_spec = lambda w: pl.BlockSpec((None, CHUNK, w), lambda bi, ci: (bi, ci, 0))
    return pl.pallas_call(
        _ssd_body,
        grid=(nb, LP // CHUNK),
        in_specs=[row_spec(SSD_CONV_CH), row_spec(SSD_D_INNER), row_spec(SMALL_W),
                  _const_spec(convw.shape), _const_spec(convb.shape),
                  _const_spec(sbias.shape), _const_spec(alog.shape),
                  _const_spec(dskip.shape), _const_spec(normw.shape),
                  _const_spec(expand.shape)],
        out_specs=[row_spec(SSD_D_INNER), row_spec(SMALL_W)],
        out_shape=[jax.ShapeDtypeStruct((nb, LP, SSD_D_INNER), BF16),
                   jax.ShapeDtypeStruct((nb, LP, SMALL_W), F32)],
        scratch_shapes=[pltpu.VMEM((CHUNK + CONV_HALO, SSD_CONV_CH), F32),
                        pltpu.VMEM((SSD_GROUPS, SSD_STATE, SSD_GROUP_W), F32),
                        pltpu.VMEM((1, SMALL_W), F32)],
        compiler_params=pltpu.CompilerParams(
            dimension_semantics=("arbitrary", "arbitrary"), vmem_limit_bytes=VMEM_LIMIT),
        name="ssd",
    )(xbc, z, small, convw, convb, sbias, alog, dskip, normw, expand)


N_TK_CHUNKS = ATT_TK // LANES
N_TQ_CHUNKS = ATT_TQ // LANES
ONES_ROWS = 16
assert ATT_TQ == ATT_TK and PADL <= ATT_TK
LOG2E = math.log2(math.e)
Q_SCALE = LOG2E * HEAD_DIM ** -0.5


def _init_softmax_state(m_ref, acc_ref):
    m_ref[...] = jnp.full(m_ref.shape, NEG, F32)
    acc_ref[...] = jnp.zeros_like(acc_ref)


def _run_key_tiles(qb, scores, step, buf_a, buf_b):
    odd = qb % 2

    @pl.when(odd == 0)
    def _():
        scores(0, buf_a)

    @pl.when(odd == 1)
    def _():
        scores(0, buf_b)
        step(0, buf_b, buf_a)

    def two_below_diagonal(i, carry):
        kb = odd + 2 * i
        step(kb, buf_a, buf_b)
        step(kb + 1, buf_b, buf_a)
        return carry

    lax.fori_loop(0, qb // 2, two_below_diagonal, 0)
    step(qb, buf_a, None)


def _query_operands(qt_ref):
    zero_rows = jnp.zeros((HEAD_DIM, ATT_TQ), BF16)
    return [jnp.concatenate([qt_ref[:HEAD_DIM, :], zero_rows], axis=0),
            jnp.concatenate([zero_rows, qt_ref[HEAD_DIM:, :]], axis=0)]


def _value_rows(vt_ref, row0, n_rows, start):
    vt = vt_ref[row0:row0 + n_rows, pl.ds(start, ATT_TK)]
    return jnp.concatenate([vt, jnp.ones((ONES_ROWS, ATT_TK), BF16)], axis=0)


def _softmax_update(s_t, col_max, vt_aug, m_ref, acc_ref, idx):
    m_old = m_ref[idx]
    m_new = jnp.maximum(m_old, col_max)
    alpha = jnp.exp2(m_old - m_new)
    p_t = jnp.exp2(s_t - m_new).astype(BF16)
    pv = jnp.dot(vt_aug, p_t, preferred_element_type=F32)
    acc_ref[idx] = alpha * acc_ref[idx] + pv
    m_ref[idx] = m_new


def _causal_mask():
    kpos = lax.broadcasted_iota(jnp.int32, (ATT_TK, ATT_TQ), 0)
    qpos = lax.broadcasted_iota(jnp.int32, (ATT_TK, ATT_TQ), 1)
    return kpos <= qpos


def _fox_body(qt_ref, k_ref, vt_ref, c_ref, o_ref, m_ref, acc_ref,
              sa_ref, pa_ref, sb_ref, pb_ref):
    qb = pl.program_id(2)
    _init_softmax_state(m_ref, acc_ref)
    buf_a, buf_b = (sa_ref, pa_ref), (sb_ref, pb_ref)
    q_heads = _query_operands(qt_ref)
    lane = lax.broadcasted_iota(jnp.int32, (ATT_TK, LANES), 1)
    first_lane = SSD_HEADS + 2 * pl.program_id(1)

    def scores(kb, buf):
        s_dst, max_dst = buf
        start = pl.multiple_of(kb * ATT_TK, ATT_TK)
        k_rows = k_ref[pl.ds(start, ATT_TK), :]
        c_rows = c_ref[pl.ds(start, ATT_TK), :]
        for j in range(2):
            ck = jnp.sum(jnp.where(lane == first_lane + j, c_rows, 0.0), axis=1, keepdims=True)
            s_t = jnp.dot(k_rows, q_heads[j], preferred_element_type=F32) - ck
            s_dst[j] = s_t
            max_dst[j] = jnp.max(s_t, axis=0, keepdims=True)

    def step(kb, src, nxt):
        s_src, max_src = src
        start = pl.multiple_of(kb * ATT_TK, ATT_TK)
        if nxt is not None:
            scores(kb + 1, nxt)
        for j in range(2):
            s_t = s_src[j]
            col_max = max_src[j]
            if nxt is None:
                s_t = jnp.where(_causal_mask(), s_t, NEG)
                col_max = jnp.max(s_t, axis=0, keepdims=True)
            vt_aug = _value_rows(vt_ref, j * HEAD_DIM, HEAD_DIM, start)
            _softmax_update(s_t, col_max, vt_aug, m_ref, acc_ref, j)

    _run_key_tiles(qb, scores, step, buf_a, buf_b)

    o_t = jnp.concatenate(
        [acc_ref[j, :HEAD_DIM, :] / acc_ref[j, HEAD_DIM:HEAD_DIM + 1, :] for j in range(2)], axis=0)
    o_ref[...] = o_t.T.astype(o_ref.dtype)


def _fox_attention(qt, k, vt, c):
    nb = qt.shape[0]
    pairs = FOX_HEADS // 2
    acc_rows = HEAD_DIM + ONES_ROWS
    return pl.pallas_call(
        _fox_body,
        grid=(nb, pairs, LP // ATT_TQ),
        in_specs=[
            pl.BlockSpec((None, LANES, ATT_TQ), lambda bi, hp, qi: (bi, hp, qi)),
            pl.BlockSpec((None, LP, LANES), lambda bi, hp, qi: (bi, 0, hp)),
            pl.BlockSpec((None, LANES, LP), lambda bi, hp, qi: (bi, hp, 0)),
            pl.BlockSpec((None, LP, SMALL_W), lambda bi, hp, qi: (bi, 0, 0)),
        ],
        out_specs=pl.BlockSpec((None, ATT_TQ, LANES), lambda bi, hp, qi: (bi, qi, hp)),
        out_shape=jax.ShapeDtypeStruct((nb, LP, FOX_WIDTH), BF16),
        scratch_shapes=[pltpu.VMEM((2, 1, ATT_TQ), F32),
                        pltpu.VMEM((2, acc_rows, ATT_TQ), F32)]
                       + [pltpu.VMEM((2, ATT_TK, ATT_TQ), F32),
                          pltpu.VMEM((2, 1, ATT_TQ), F32)] * 2,
        compiler_params=pltpu.CompilerParams(
            dimension_semantics=("parallel", "parallel", "arbitrary"),
            vmem_limit_bytes=VMEM_LIMIT),
        name="fox_attention",
    )(qt, k, vt, c)


def _bias_tiles(bias_ref):
    rows = jnp.broadcast_to(bias_ref[...], (CHUNK, 2 * CHUNK))
    rolled = pltpu.roll(rows, 0, 1, stride=1, stride_axis=0)
    return rolled[:, :CHUNK], rolled[:, CHUNK:]


def _diagonal_bias(tiles):
    zeros = jnp.zeros((CHUNK, CHUNK), F32)
    rows = []
    for a in range(N_TK_CHUNKS):
        rows.append(jnp.concatenate(
            [tiles[b - a] if b - a in (0, 1) else zeros for b in range(N_TQ_CHUNKS)], axis=1))
    return jnp.concatenate(rows, axis=0)


def _diff_body(lambda_init, qt_ref, k_ref, vt_ref, bias_ref, lam_ref, subln_ref, o_ref,
               m_ref, acc_ref, tiles_ref, sa_ref, pa_ref, sb_ref, pb_ref):
    qb = pl.program_id(2)
    _init_softmax_state(m_ref, acc_ref)
    buf_a, buf_b = (sa_ref, pa_ref), (sb_ref, pb_ref)
    tile0, tile1 = _bias_tiles(bias_ref)
    tiles_ref[0] = tile0
    tiles_ref[1] = tile1
    q_parts = _query_operands(qt_ref)
    pad_rows = -(-PADL // CHUNK) * CHUNK
    corner0 = ATT_TK - CHUNK

    def scores(kb, buf):
        s_dst, max_dst = buf
        start = pl.multiple_of(kb * ATT_TK, ATT_TK)
        k_rows = k_ref[pl.ds(start, ATT_TK), :]
        kpos = lax.broadcasted_iota(jnp.int32, (pad_rows, ATT_TQ), 0)
        not_padding = (kpos >= PADL) | (kb > 0)
        for j in range(2):
            s_t = jnp.dot(k_rows, q_parts[j], preferred_element_type=F32)
            s_t = jnp.concatenate(
                [jnp.where(not_padding, s_t[:pad_rows], NEG), s_t[pad_rows:]], axis=0)
            s_dst[j] = s_t
            max_dst[j] = jnp.max(s_t, axis=0, keepdims=True)

    def step(kb, src, nxt):
        s_src, max_src = src
        start = pl.multiple_of(kb * ATT_TK, ATT_TK)
        diagonal = nxt is None
        if diagonal:
            bias = _diagonal_bias((tiles_ref[0], tiles_ref[1]))
            visible = _causal_mask()
        else:
            scores(kb + 1, nxt)
            corner = jnp.where(kb == qb - 1, 1.0, 0.0) * tiles_ref[1]
        vt_aug = _value_rows(vt_ref, 0, LANES, start)
        for j in range(2):
            s_t = s_src[j]
            if diagonal:
                s_t = jnp.where(visible, s_t + bias, NEG)
                col_max = jnp.max(s_t, axis=0, keepdims=True)
            else:
                near = s_t[corner0:, :CHUNK] + corner
                s_t = jnp.concatenate(
                    [s_t[:corner0],
                     jnp.concatenate([near, s_t[corner0:, CHUNK:]], axis=1)], axis=0)
                first = jnp.maximum(jnp.max(s_t[:corner0, :CHUNK], axis=0, keepdims=True),
                                    jnp.max(near, axis=0, keepdims=True))
                col_max = jnp.concatenate([first, max_src[j][:, CHUNK:]], axis=1)
            _softmax_update(s_t, col_max, vt_aug, m_ref, acc_ref, j)

    _run_key_tiles(qb, scores, step, buf_a, buf_b)

    lam1 = jnp.exp(jnp.sum(lam_ref[0:1, :] * lam_ref[1:2, :], axis=-1, keepdims=True))
    lam2 = jnp.exp(jnp.sum(lam_ref[2:3, :] * lam_ref[3:4, :], axis=-1, keepdims=True))
    lam = lam1 - lam2 + lambda_init
    o_t = (acc_ref[0, :LANES, :] / acc_ref[0, LANES:LANES + 1, :]
           - lam * (acc_ref[1, :LANES, :] / acc_ref[1, LANES:LANES + 1, :]))
    o = o_t.T
    ms = jnp.mean(o * o, axis=-1, keepdims=True)
    o = o * lax.rsqrt(ms + RMS_EPS) * subln_ref[...] * (1.0 - lambda_init)
    o_ref[...] = o.astype(o_ref.dtype)


def _diff_attention(qt, k, vt, bias_rows, lam_rows, subln, lambda_init):
    nb = qt.shape[0]
    acc_rows = LANES + ONES_ROWS
    return pl.pallas_call(
        functools.partial(_diff_body, lambda_init),
        grid=(nb, DIFF_HEADS, LP // ATT_TQ),
        in_specs=[
            pl.BlockSpec((None, LANES, ATT_TQ), lambda bi, hd, qi: (bi, hd, qi)),
            pl.BlockSpec((None, LP, LANES), lambda bi, hd, qi: (bi, 0, hd)),
            pl.BlockSpec((None, LANES, LP), lambda bi, hd, qi: (bi, hd, 0)),
            pl.BlockSpec((None, 1, 2 * CHUNK), lambda bi, hd, qi: (hd, 0, 0)),
            _const_spec(lam_rows.shape),
            _const_spec(subln.shape),
        ],
        out_specs=pl.BlockSpec((None, ATT_TQ, LANES), lambda bi, hd, qi: (bi, qi, hd)),
        out_shape=jax.ShapeDtypeStruct((nb, LP, DIFF_HEADS * LANES), BF16),
        scratch_shapes=[pltpu.VMEM((2, 1, ATT_TQ), F32),
                        pltpu.VMEM((2, acc_rows, ATT_TQ), F32),
                        pltpu.VMEM((2, CHUNK, CHUNK), F32)]
                       + [pltpu.VMEM((2, ATT_TK, ATT_TQ), F32),
                          pltpu.VMEM((2, 1, ATT_TQ), F32)] * 2,
        compiler_params=pltpu.CompilerParams(
            dimension_semantics=("parallel", "parallel", "arbitrary"),
            vmem_limit_bytes=VMEM_LIMIT),
        name="diff_attention",
    )(qt, k, vt, bias_rows, lam_rows, subln)


def _t5_bucket(n):
    max_exact = N_BUCKETS // 2
    nf = jnp.maximum(n, 1).astype(F32)
    large = max_exact + (jnp.log(nf / max_exact) / math.log(128 / max_exact)
                         * (N_BUCKETS - max_exact)).astype(jnp.int32)
    large = jnp.minimum(large, N_BUCKETS - 1)
    return jnp.where(n < max_exact, n, large)


def _relative_bias_rows(rel_table):
    dist = jnp.arange(2 * CHUNK)
    by_dist = LOG2E * (rel_table[_t5_bucket(dist)] - rel_table[N_BUCKETS - 1])
    return by_dist.T[:, None, :].astype(F32)


def _pad_lanes(vec, width=LANES):
    return jnp.pad(vec, (0, width - vec.shape[0]))[None, :].astype(F32)


def kernel(x, meta_tokens, ln_gain, ln_bias, ffn1_w_gate, ffn1_w_up, ffn1_w_down, ffn2_w_gate, ffn2_w_up, ffn2_w_down, even_w_in, even_conv_w, even_conv_b, ssd_dt_bias, ssd_a_log, ssd_d_skip, ssd_norm_w, fox_f_bias, even_w_out, diff_w_qkv, diff_lambda_q1, diff_lambda_k1, diff_lambda_q2, diff_lambda_k2, diff_subln_w, diff_w_o, rel_bias_table):
    nb = x.shape[0]
    lead = jnp.concatenate([jnp.zeros((PADL, D_MODEL), x.dtype), meta_tokens.astype(x.dtype)], axis=0)

    def ln_params(l, i):
        return ln_gain[l, i][None, :], ln_bias[l, i][None, :]

    ffn1 = tuple(w.astype(BF16) for w in (ffn1_w_gate, ffn1_w_up, ffn1_w_down))
    ffn2 = tuple(w.astype(BF16) for w in (ffn2_w_gate, ffn2_w_up, ffn2_w_down))

    hf = _ffn_ln_first(x, lead, *ffn1, 0, *ln_params(0, 0))

    w_in = even_w_in[0]
    o_z, o_xbc = 0, SSD_D_INNER
    o_dt = o_xbc + SSD_CONV_CH
    o_q = o_dt + SSD_HEADS
    o_k, o_v = o_q + FOX_WIDTH, o_q + 2 * FOX_WIDTH
    o_f = o_q + 3 * FOX_WIDTH
    w_small = jnp.concatenate(
        [w_in[:, o_dt:o_q], w_in[:, o_f:],
         jnp.zeros((D_MODEL, SMALL_W - SSD_HEADS - FOX_HEADS), w_in.dtype)], axis=1)
    w_even = jnp.concatenate([w_in[:, o_z:o_dt], w_in[:, o_k:o_v], w_small], axis=1).astype(BF16)
    wt_even = jnp.concatenate([w_in[:, o_q:o_k] * Q_SCALE, w_in[:, o_v:o_f]], axis=1).T.astype(BF16)
    z, xbc, k, small, qt, vt = _proj(
        hf, w_even, wt_even, (SSD_D_INNER, SSD_CONV_CH, FOX_WIDTH, SMALL_W),
        (F32, F32, BF16, F32), (FOX_WIDTH, FOX_WIDTH), nb, "even_in_proj")

    sbias = _pad_lanes(jnp.concatenate([ssd_dt_bias[0], fox_f_bias[0]]))
    alog = _pad_lanes(ssd_a_log[0])
    dskip = jnp.repeat(ssd_d_skip[0], SSD_HEAD_DIM)[None, :].astype(F32)
    expand = np.zeros((SMALL_W, SSD_D_INNER), np.float32)
    expand[np.arange(SSD_D_INNER) // SSD_HEAD_DIM, np.arange(SSD_D_INNER)] = 1.0
    y, cfull = _ssd(xbc.reshape(nb, LP, SSD_CONV_CH), z.reshape(nb, LP, SSD_D_INNER),
                    small.reshape(nb, LP, SMALL_W), even_conv_w[0], even_conv_b[0][None, :],
                    sbias, alog, dskip, ssd_norm_w[0][None, :],
                    jnp.asarray(np.concatenate([expand, expand], axis=0), BF16))
    ck = jnp.where(jnp.arange(LP)[None, :, None] < PADL, -NEG, LOG2E * cfull)
    o = _fox_attention(qt, k.reshape(nb, LP, FOX_WIDTH), vt, ck)
    w_out = even_w_out[0].astype(BF16)
    hf = _outproj_ln(hf, [y.reshape(nb * LP, SSD_D_INNER), o.reshape(nb * LP, FOX_WIDTH)],
                     [w_out[:SSD_D_INNER], w_out[SSD_D_INNER:]], *ln_params(0, 1),
                     name="even_out_proj_ln")
    hf = _ffn_ln(hf, *ffn2, 0, *ln_params(0, 2))

    hf = _ffn_ln(hf, *ffn1, 1, *ln_params(1, 0))
    qw = DIFF_HEADS * 2 * HEAD_DIM
    w_qkv = diff_w_qkv[0]
    wt_diff = jnp.concatenate([w_qkv[:, :qw] * Q_SCALE, w_qkv[:, 2 * qw:]], axis=1).T.astype(BF16)
    k, qt, vt = _proj(hf, w_qkv[:, qw:2 * qw].astype(BF16), wt_diff, (qw,), (BF16,),
                      (qw, DIFF_HEADS * LANES), nb, "diff_qkv_proj")
    lambda_init = 0.8 - 0.6 * math.exp(-0.3 * 1)
    lam_rows = jnp.concatenate(
        [_pad_lanes(diff_lambda_q1[0]), _pad_lanes(diff_lambda_k1[0]),
         _pad_lanes(diff_lambda_q2[0]), _pad_lanes(diff_lambda_k2[0]),
         jnp.zeros((4, LANES), F32)], axis=0)
    o = _diff_attention(qt, k.reshape(nb, LP, qw), vt, _relative_bias_rows(rel_bias_table),
                        lam_rows, diff_subln_w[0][None, :], lambda_init)
    hf = _outproj_ln(hf, [o.reshape(nb * LP, DIFF_HEADS * LANES)], [diff_w_o[0].astype(BF16)],
                     *ln_params(1, 1), name="diff_out_proj_ln")
    return _ffn_ln_final(hf.reshape(nb, LP, D_MODEL), *ffn2, 1, *ln_params(1, 2))
```

```python
import functools
import math

import numpy as np
import jax
import jax.numpy as jnp
from jax import lax
from jax.experimental import pallas as pl
from jax.experimental.pallas import tpu as pltpu

F32 = jnp.float32
BF16 = jnp.bfloat16

D_MODEL = 1024
SEQ = 8192
DEPTH = 2
N_META = 16
CHUNK = 128
SSD_D_INNER = 2048
SSD_HEAD_DIM = 64
SSD_HEADS = 32
SSD_GROUPS = 4
SSD_GROUP_W = SSD_D_INNER // SSD_GROUPS
SSD_STATE = 128
SSD_CONV = 4
SSD_CONV_CH = SSD_D_INNER + 2 * SSD_GROUPS * SSD_STATE
FOX_HEADS = 16
FOX_WIDTH = 1024
HEAD_DIM = 64
DIFF_HEADS = 8
N_BUCKETS = 32
D_FF = 2816
ALPHA = (2 * DEPTH) ** 0.25
LN_EPS = 1e-5
RMS_EPS = 1e-5
NEG = -1e30

LANES = 128
LP = 8448
PADL = LP - SEQ - N_META
ROW_TILE = 512
OUT_TILE = 256
ATT_TQ = 768
ATT_TK = 768
CONV_HALO = 8
SMALL_W = LANES
VMEM_LIMIT = 56 * 1024 * 1024

assert PADL % CHUNK == CHUNK - N_META
assert LP % ATT_TQ == 0 and ATT_TQ % ATT_TK == 0 and ATT_TK % LANES == 0
assert LP % CHUNK == 0 and (2 * LP) % ROW_TILE == 0
assert PADL <= OUT_TILE and (LP - OUT_TILE) == SEQ


def _const_spec(shape):
    nd = len(shape)
    return pl.BlockSpec(shape, lambda *_: (0,) * nd, pipeline_mode=pl.Buffered(1))


def _layer_spec(shape, layer):
    nd = len(shape)
    return pl.BlockSpec((None,) + tuple(shape), lambda *_: (layer,) + (0,) * nd,
                        pipeline_mode=pl.Buffered(1))


def _ffn_weight_specs(layer):
    return [_layer_spec((D_MODEL, D_FF), layer), _layer_spec((D_MODEL, D_FF), layer),
            _layer_spec((D_FF, D_MODEL), layer)]


def _layer_norm(r, g, b):
    mu = jnp.mean(r, axis=-1, keepdims=True)
    d = r - mu
    var = jnp.mean(d * d, axis=-1, keepdims=True)
    return d * lax.rsqrt(var + LN_EPS) * g + b


def _silu(x):
    return x / (1.0 + jnp.exp(-x))


def _softplus(x):
    return jnp.maximum(x, 0.0) + jnp.log(1.0 + jnp.exp(-jnp.abs(x)))


def _ffn_ln_body(h_ref, wg_ref, wu_ref, wd_ref, g_ref, b_ref, o_ref):
    h = h_ref[...]
    hb = h.astype(BF16)
    g = jnp.dot(hb, wg_ref[...], preferred_element_type=F32)
    u = jnp.dot(hb, wu_ref[...], preferred_element_type=F32)
    a = (_silu(g) * u).astype(BF16)
    y = jnp.dot(a, wd_ref[...], preferred_element_type=F32)
    o_ref[...] = _layer_norm(ALPHA * h + 0.5 * y, g_ref[...], b_ref[...])


def _ffn_ln(hf, wg, wu, wd, layer, g, b):
    rows = hf.shape[0]
    return pl.pallas_call(
        _ffn_ln_body,
        grid=(rows // ROW_TILE,),
        in_specs=[
            pl.BlockSpec((ROW_TILE, D_MODEL), lambda i: (i, 0)),
            *_ffn_weight_specs(layer),
            _const_spec((1, D_MODEL)),
            _const_spec((1, D_MODEL)),
        ],
        out_specs=pl.BlockSpec((ROW_TILE, D_MODEL), lambda i: (i, 0)),
        out_shape=jax.ShapeDtypeStruct((rows, D_MODEL), F32),
        compiler_params=pltpu.CompilerParams(
            dimension_semantics=("parallel",), vmem_limit_bytes=VMEM_LIMIT),
        name="ffn_ln",
    )(hf, wg, wu, wd, g, b)


def _ffn_ln_first_body(lead_ref, x_ref, wg_ref, wu_ref, wd_ref, g_ref, b_ref, o_ref, h_ref):
    h_ref[...] = jnp.where(pl.program_id(1) == 0, lead_ref[...], x_ref[...])
    _ffn_ln_body(h_ref, wg_ref, wu_ref, wd_ref, g_ref, b_ref, o_ref)


def _ffn_ln_first(x, lead, wg, wu, wd, layer, g, b):
    nb = x.shape[0]
    tiles = LP // OUT_TILE
    return pl.pallas_call(
        _ffn_ln_first_body,
        grid=(nb, tiles),
        in_specs=[
            _const_spec((OUT_TILE, D_MODEL)),
            pl.BlockSpec((None, OUT_TILE, D_MODEL), lambda bi, i: (bi, jnp.maximum(i - 1, 0), 0)),
            *_ffn_weight_specs(layer),
            _const_spec((1, D_MODEL)),
            _const_spec((1, D_MODEL)),
        ],
        out_specs=pl.BlockSpec((OUT_TILE, D_MODEL), lambda bi, i: (bi * tiles + i, 0)),
        out_shape=jax.ShapeDtypeStruct((nb * LP, D_MODEL), F32),
        scratch_shapes=[pltpu.VMEM((OUT_TILE, D_MODEL), F32)],
        compiler_params=pltpu.CompilerParams(
            dimension_semantics=("parallel", "arbitrary"), vmem_limit_bytes=VMEM_LIMIT),
        name="ffn_ln_first",
    )(lead, x, wg, wu, wd, g, b)


def _ffn_ln_final_body(h_ref, wg_ref, wu_ref, wd_ref, g_ref, b_ref, o_ref):
    @pl.when(pl.program_id(1) > 0)
    def _():
        _ffn_ln_body(h_ref, wg_ref, wu_ref, wd_ref, g_ref, b_ref, o_ref)


def _ffn_ln_final(h3, wg, wu, wd, layer, g, b):
    nb = h3.shape[0]
    return pl.pallas_call(
        _ffn_ln_final_body,
        grid=(nb, LP // OUT_TILE),
        in_specs=[
            pl.BlockSpec((None, OUT_TILE, D_MODEL), lambda bi, i: (bi, i, 0)),
            *_ffn_weight_specs(layer),
            _const_spec((1, D_MODEL)),
            _const_spec((1, D_MODEL)),
        ],
        out_specs=pl.BlockSpec((None, OUT_TILE, D_MODEL),
                               lambda bi, i: (bi, jnp.maximum(i - 1, 0), 0)),
        out_shape=jax.ShapeDtypeStruct((nb, SEQ, D_MODEL), F32),
        compiler_params=pltpu.CompilerParams(
            dimension_semantics=("arbitrary", "arbitrary"), vmem_limit_bytes=VMEM_LIMIT),
        name="ffn_ln_final",
    )(h3, wg, wu, wd, g, b)


PROJ_TILE = 256
assert LP % PROJ_TILE == 0


def _proj_body(n_t, h_ref, w_ref, wt_ref, *o_refs):
    hb = h_ref[...].astype(BF16)
    off = 0
    for o_ref in o_refs[:-n_t]:
        n = o_ref.shape[-1]
        o_ref[...] = jnp.dot(hb, w_ref[:, off:off + n],
                             preferred_element_type=F32).astype(o_ref.dtype)
        off += n
    off = 0
    for o_ref in o_refs[-n_t:]:
        n = o_ref.shape[0]
        o_ref[...] = lax.dot_general(wt_ref[off:off + n, :], hb, (((1,), (1,)), ((), ())),
                                     preferred_element_type=F32).astype(o_ref.dtype)
        off += n


def _proj(hf, w, wt, widths, dtypes, t_widths, nb, name):
    rows = hf.shape[0]
    tiles_per_batch = LP // PROJ_TILE
    out_specs = [pl.BlockSpec((PROJ_TILE, n), lambda i: (i, 0)) for n in widths]
    out_specs += [pl.BlockSpec((None, n, PROJ_TILE),
                               lambda i: (i // tiles_per_batch, 0, i % tiles_per_batch))
                  for n in t_widths]
    out_shape = [jax.ShapeDtypeStruct((rows, n), dt) for n, dt in zip(widths, dtypes)]
    out_shape += [jax.ShapeDtypeStruct((nb, n, LP), BF16) for n in t_widths]
    return pl.pallas_call(
        functools.partial(_proj_body, len(t_widths)),
        grid=(rows // PROJ_TILE,),
        in_specs=[pl.BlockSpec((PROJ_TILE, D_MODEL), lambda i: (i, 0)),
                  _const_spec(w.shape), _const_spec(wt.shape)],
        out_specs=out_specs,
        out_shape=out_shape,
        compiler_params=pltpu.CompilerParams(
            dimension_semantics=("parallel",), vmem_limit_bytes=VMEM_LIMIT),
        name=name,
    )(hf, w, wt)


def _even_proj_body(n_tiles, h_ref, w_ref, wt_ref, convw_ref, convb_ref,
                    xc_ref, z_ref, k_ref, small_ref, qt_ref, vt_ref, ext_ref):
    i = pl.program_id(0)
    tile = jnp.minimum(i, n_tiles - 1)

    @pl.when(i == 0)
    def _():
        ext_ref[...] = jnp.zeros_like(ext_ref)

    ext = ext_ref[...]
    conv = convb_ref[...] + convw_ref[SSD_CONV - 1:SSD_CONV, :] * ext[CONV_HALO:, :]
    for tap in range(SSD_CONV - 1):
        shifted = pltpu.roll(ext, SSD_CONV - 1 - tap, 0)[CONV_HALO:, :]
        conv = conv + convw_ref[tap:tap + 1, :] * shifted
    xc_ref[...] = _silu(conv)
    ext_ref[0:CONV_HALO, :] = ext_ref[PROJ_TILE:, :]

    hb = h_ref[...].astype(BF16)
    o_xbc, o_k = SSD_D_INNER, SSD_D_INNER + SSD_CONV_CH
    row = lax.broadcasted_iota(jnp.int32, (PROJ_TILE, 1), 0)
    real = (row >= PADL) | (tile % (LP // PROJ_TILE) > 0)
    xbc = jnp.dot(hb, w_ref[:, o_xbc:o_k], preferred_element_type=F32)
    ext_ref[CONV_HALO:, :] = jnp.where(real, xbc, 0.0)
    z_ref[...] = jnp.dot(hb, w_ref[:, :o_xbc], preferred_element_type=F32)
    k_ref[...] = jnp.dot(hb, w_ref[:, o_k:o_k + FOX_WIDTH],
                         preferred_element_type=F32).astype(k_ref.dtype)
    small_ref[...] = jnp.dot(hb, w_ref[:, o_k + FOX_WIDTH:], preferred_element_type=F32)
    for o_ref, off in ((qt_ref, 0), (vt_ref, FOX_WIDTH)):
        o_ref[...] = lax.dot_general(wt_ref[off:off + FOX_WIDTH, :], hb, (((1,), (1,)), ((), ())),
                                     preferred_element_type=F32).astype(o_ref.dtype)


def _even_proj(hf, w, wt, convw, convb, nb):
    rows = hf.shape[0]
    n_tiles = rows // PROJ_TILE
    tiles_per_batch = LP // PROJ_TILE
    this = lambda i: jnp.minimum(i, n_tiles - 1)
    row_spec = lambda n: pl.BlockSpec((PROJ_TILE, n), lambda i: (this(i), 0))
    t_spec = pl.BlockSpec((None, FOX_WIDTH, PROJ_TILE),
                          lambda i: (this(i) // tiles_per_batch, 0, this(i) % tiles_per_batch))
    return pl.pallas_call(
        functools.partial(_even_proj_body, n_tiles),
        grid=(n_tiles + 1,),
        in_specs=[row_spec(D_MODEL), _const_spec(w.shape), _const_spec(wt.shape),
                  _const_spec(convw.shape), _const_spec(convb.shape)],
        out_specs=[pl.BlockSpec((PROJ_TILE, SSD_CONV_CH), lambda i: (jnp.maximum(i - 1, 0), 0)),
                   row_spec(SSD_D_INNER), row_spec(FOX_WIDTH), row_spec(SMALL_W), t_spec, t_spec],
        out_shape=[jax.ShapeDtypeStruct((rows, SSD_CONV_CH), F32),
                   jax.ShapeDtypeStruct((rows, SSD_D_INNER), F32),
                   jax.ShapeDtypeStruct((rows, FOX_WIDTH), BF16),
                   jax.ShapeDtypeStruct((rows, SMALL_W), F32),
                   jax.ShapeDtypeStruct((nb, FOX_WIDTH, LP), BF16),
                   jax.ShapeDtypeStruct((nb, FOX_WIDTH, LP), BF16)],
        scratch_shapes=[pltpu.VMEM((PROJ_TILE + CONV_HALO, SSD_CONV_CH), F32)],
        compiler_params=pltpu.CompilerParams(
            dimension_semantics=("arbitrary",), vmem_limit_bytes=VMEM_LIMIT),
        name="even_in_proj",
    )(hf, w, wt, convw, convb)


def _outproj_ln_body(n_in, h_ref, *refs):
    a_refs = refs[:n_in]
    w_refs = refs[n_in:2 * n_in]
    g_ref, b_ref, o_ref = refs[2 * n_in:]
    m = jnp.dot(a_refs[0][...], w_refs[0][...], preferred_element_type=F32)
    for a_ref, w_ref in zip(a_refs[1:], w_refs[1:]):
        m = m + jnp.dot(a_ref[...], w_ref[...], preferred_element_type=F32)
    o_ref[...] = _layer_norm(ALPHA * h_ref[...] + m, g_ref[...], b_ref[...])


def _outproj_ln(hf, acts, ws, g, b, name):
    rows = hf.shape[0]
    n_in = len(acts)
    in_specs = [pl.BlockSpec((ROW_TILE, D_MODEL), lambda i: (i, 0))]
    in_specs += [pl.BlockSpec((ROW_TILE, a.shape[1]), lambda i: (i, 0)) for a in acts]
    in_specs += [_const_spec(w.shape) for w in ws]
    in_specs += [_const_spec((1, D_MODEL)), _const_spec((1, D_MODEL))]
    return pl.pallas_call(
        functools.partial(_outproj_ln_body, n_in),
        grid=(rows // ROW_TILE,),
        in_specs=in_specs,
        out_specs=pl.BlockSpec((ROW_TILE, D_MODEL), lambda i: (i, 0)),
        out_shape=jax.ShapeDtypeStruct((rows, D_MODEL), F32),
        compiler_params=pltpu.CompilerParams(
            dimension_semantics=("parallel",), vmem_limit_bytes=VMEM_LIMIT),
        name=name,
    )(hf, *acts, *ws, g, b)


def _split_dot(x, e2_ref):
    hi = x.astype(BF16)
    lo = (x - hi.astype(F32)).astype(BF16)
    return jnp.dot(jnp.concatenate([hi, lo], axis=1), e2_ref[...], preferred_element_type=F32)


def _ssd_body(xc_ref, z_ref, small_ref, sbias_ref, alog_ref,
              dskip_ref, normw_ref, expand_ref, y_ref, c_ref, state_ref, carry_ref):
    c = pl.program_id(1)

    @pl.when(c == 0)
    def _():
        state_ref[...] = jnp.zeros_like(state_ref)
        carry_ref[...] = jnp.zeros_like(carry_ref)

    row = lax.broadcasted_iota(jnp.int32, (CHUNK, 1), 0)
    valid = (c * CHUNK + row) >= PADL

    xs = xc_ref[:, :SSD_D_INNER]
    bm = xc_ref[:, SSD_D_INNER:SSD_D_INNER + SSD_GROUPS * SSD_STATE].astype(BF16)
    cm = xc_ref[:, SSD_D_INNER + SSD_GROUPS * SSD_STATE:].astype(BF16)

    lane = lax.broadcasted_iota(jnp.int32, (CHUNK, SMALL_W), 1)
    is_dt = lane < SSD_HEADS
    is_f = (lane >= SSD_HEADS) & (lane < SSD_HEADS + FOX_HEADS)
    v = small_ref[...] + sbias_ref[...]
    dt = jnp.where(valid & is_dt, _softplus(v), 0.0)
    log_f = jnp.where(valid & is_f, -_softplus(-v), 0.0)
    neg_a = -jnp.exp(alog_ref[...])
    steps = jnp.where(is_dt, dt * neg_a, log_f)
    r_i = lax.broadcasted_iota(jnp.int32, (CHUNK, CHUNK), 0)
    c_i = lax.broadcasted_iota(jnp.int32, (CHUNK, CHUNK), 1)
    causal = r_i >= c_i
    tril = jnp.where(causal, 1.0, 0.0).astype(F32)
    cum = jnp.dot(tril, steps, preferred_element_type=F32,
                  precision=lax.Precision.HIGHEST)
    c_total = jnp.where(is_f, cum + carry_ref[...], 0.0)
    c_ref[...] = c_total
    carry_ref[...] = c_total[CHUNK - 1:CHUNK, :]

    a_last = cum[CHUNK - 1:CHUNK, :]
    dt_x = _split_dot(dt, expand_ref)
    ea_x = _split_dot(jnp.exp(cum), expand_ref)
    de_x = _split_dot(jnp.exp(a_last - cum), expand_ref)
    x_dt = xs * dt_x
    xb = x_dt.astype(BF16)
    xe = (x_dt * de_x).astype(BF16)
    cum_t = cum.T
    lane_p = lax.broadcasted_iota(jnp.int32, (CHUNK, 2 * SSD_HEAD_DIM), 1)
    first_half = lane_p < SSD_HEAD_DIM

    y_groups = []
    for g in range(SSD_GROUPS):
        gs = slice(g * SSD_GROUP_W, (g + 1) * SSD_GROUP_W)
        bg = bm[:, g * SSD_STATE:(g + 1) * SSD_STATE]
        cg = cm[:, g * SSD_STATE:(g + 1) * SSD_STATE]
        cb = lax.dot_general(cg, bg, (((1,), (1,)), ((), ())),
                             preferred_element_type=F32)
        pair_out = []
        for pr in range(SSD_GROUP_W // (2 * SSD_HEAD_DIM)):
            col0 = g * SSD_GROUP_W + pr * 2 * SSD_HEAD_DIM
            x_pair = xb[:, col0:col0 + 2 * SSD_HEAD_DIM]
            ys = []
            for j in range(2):
                hd = col0 // SSD_HEAD_DIM + j
                diff = cum[:, hd:hd + 1] - cum_t[hd:hd + 1, :]
                decay = jnp.exp(jnp.where(causal, diff, -jnp.inf))
                mat = (cb * decay).astype(BF16)
                ys.append(jnp.dot(mat, x_pair, preferred_element_type=F32))
            pair_out.append(jnp.where(first_half, ys[0], ys[1]))
        y_diag = jnp.concatenate(pair_out, axis=1)
        st = state_ref[g]
        y_off = jnp.dot(cg, st.astype(BF16), preferred_element_type=F32) * ea_x[:, gs]
        new = lax.dot_general(bg, xe[:, gs], (((0,), (0,)), ((), ())),
                              preferred_element_type=F32)
        state_ref[g] = st * ea_x[CHUNK - 1:CHUNK, gs] + new
        yg = y_diag + y_off + dskip_ref[:, gs] * xs[:, gs]
        yg = yg * _silu(z_ref[:, gs])
        ms = jnp.mean(yg * yg, axis=-1, keepdims=True)
        y_groups.append(yg * lax.rsqrt(ms + RMS_EPS) * normw_ref[:, gs])
    y_ref[...] = jnp.concatenate(y_groups, axis=1).astype(y_ref.dtype)


def _ssd(xc, z, small, sbias, alog, dskip, normw, expand):
    nb = xc.shape[0]
    row_spec = lambda w: pl.BlockSpec((None, CHUNK, w), lambda bi, ci: (bi, ci, 0))
    return pl.pallas_call(
        _ssd_body,
        grid=(nb, LP // CHUNK),
        in_specs=[row_spec(SSD_CONV_CH), row_spec(SSD_D_INNER), row_spec(SMALL_W),
                  _const_spec(sbias.shape), _const_spec(alog.shape),
                  _const_spec(dskip.shape), _const_spec(normw.shape),
                  _const_spec(expand.shape)],
        out_specs=[row_spec(SSD_D_INNER), row_spec(SMALL_W)],
        out_shape=[jax.ShapeDtypeStruct((nb, LP, SSD_D_INNER), BF16),
                   jax.ShapeDtypeStruct((nb, LP, SMALL_W), F32)],
        scratch_shapes=[pltpu.VMEM((SSD_GROUPS, SSD_STATE, SSD_GROUP_W), F32),
                        pltpu.VMEM((1, SMALL_W), F32)],
        compiler_params=pltpu.CompilerParams(
            dimension_semantics=("arbitrary", "arbitrary"), vmem_limit_bytes=VMEM_LIMIT),
        name="ssd",
    )(xc, z, small, sbias, alog, dskip, normw, expand)


N_TK_CHUNKS = ATT_TK // LANES
N_TQ_CHUNKS = ATT_TQ // LANES
ONES_ROWS = 16
assert ATT_TQ == ATT_TK and PADL <= ATT_TK
LOG2E = math.log2(math.e)
Q_SCALE = LOG2E * HEAD_DIM ** -0.5


def _init_softmax_state(m_ref, acc_ref):
    m_ref[...] = jnp.full(m_ref.shape, NEG, F32)
    acc_ref[...] = jnp.zeros_like(acc_ref)


def _run_key_tiles(qb, scores, step, buf_a, buf_b):
    odd = qb % 2

    @pl.when(odd == 0)
    def _():
        scores(0, buf_a)

    @pl.when(odd == 1)
    def _():
        scores(0, buf_b)
        step(0, buf_b, buf_a)

    def two_below_diagonal(i, carry):
        kb = odd + 2 * i
        step(kb, buf_a, buf_b)
        step(kb + 1, buf_b, buf_a)
        return carry

    lax.fori_loop(0, qb // 2, two_below_diagonal, 0)
    step(qb, buf_a, None)


def _query_operands(qt_ref):
    zero_rows = jnp.zeros((HEAD_DIM, ATT_TQ), BF16)
    return [jnp.concatenate([qt_ref[:HEAD_DIM, :], zero_rows], axis=0),
            jnp.concatenate([zero_rows, qt_ref[HEAD_DIM:, :]], axis=0)]


def _value_rows(vt_ref, row0, n_rows, start):
    vt = vt_ref[row0:row0 + n_rows, pl.ds(start, ATT_TK)]
    return jnp.concatenate([vt, jnp.ones((ONES_ROWS, ATT_TK), BF16)], axis=0)


def _softmax_update(s_t, col_max, vt_aug, m_ref, acc_ref, idx):
    m_old = m_ref[idx]
    m_new = jnp.maximum(m_old, col_max)
    alpha = jnp.exp2(m_old - m_new)
    p_t = jnp.exp2(s_t - m_new).astype(BF16)
    pv = jnp.dot(vt_aug, p_t, preferred_element_type=F32)
    acc_ref[idx] = alpha * acc_ref[idx] + pv
    m_ref[idx] = m_new


def _causal_mask():
    kpos = lax.broadcasted_iota(jnp.int32, (ATT_TK, ATT_TQ), 0)
    qpos = lax.broadcasted_iota(jnp.int32, (ATT_TK, ATT_TQ), 1)
    return kpos <= qpos


def _fox_body(qt_ref, k_ref, vt_ref, c_ref, o_ref, m_ref, acc_ref,
              sa_ref, pa_ref, sb_ref, pb_ref):
    qb = pl.program_id(2)
    _init_softmax_state(m_ref, acc_ref)
    buf_a, buf_b = (sa_ref, pa_ref), (sb_ref, pb_ref)
    q_heads = _query_operands(qt_ref)
    lane = lax.broadcasted_iota(jnp.int32, (ATT_TK, LANES), 1)
    first_lane = SSD_HEADS + 2 * pl.program_id(1)

    def scores(kb, buf):
        s_dst, max_dst = buf
        start = pl.multiple_of(kb * ATT_TK, ATT_TK)
        k_rows = k_ref[pl.ds(start, ATT_TK), :]
        c_rows = c_ref[pl.ds(start, ATT_TK), :]
        for j in range(2):
            ck = jnp.sum(jnp.where(lane == first_lane + j, c_rows, 0.0), axis=1, keepdims=True)
            s_t = jnp.dot(k_rows, q_heads[j], preferred_element_type=F32) - ck
            s_dst[j] = s_t
            max_dst[j] = jnp.max(s_t, axis=0, keepdims=True)

    def step(kb, src, nxt):
        s_src, max_src = src
        start = pl.multiple_of(kb * ATT_TK, ATT_TK)
        if nxt is not None:
            scores(kb + 1, nxt)
        for j in range(2):
            s_t = s_src[j]
            col_max = max_src[j]
            if nxt is None:
                s_t = jnp.where(_causal_mask(), s_t, NEG)
                col_max = jnp.max(s_t, axis=0, keepdims=True)
            vt_aug = _value_rows(vt_ref, j * HEAD_DIM, HEAD_DIM, start)
            _softmax_update(s_t, col_max, vt_aug, m_ref, acc_ref, j)

    _run_key_tiles(qb, scores, step, buf_a, buf_b)

    o_t = jnp.concatenate(
        [acc_ref[j, :HEAD_DIM, :] / acc_ref[j, HEAD_DIM:HEAD_DIM + 1, :] for j in range(2)], axis=0)
    o_ref[...] = o_t.T.astype(o_ref.dtype)


def _fox_attention(qt, k, vt, c):
    nb = qt.shape[0]
    pairs = FOX_HEADS // 2
    acc_rows = HEAD_DIM + ONES_ROWS
    return pl.pallas_call(
        _fox_body,
        grid=(nb, pairs, LP // ATT_TQ),
        in_specs=[
            pl.BlockSpec((None, LANES, ATT_TQ), lambda bi, hp, qi: (bi, hp, qi)),
            pl.BlockSpec((None, LP, LANES), lambda bi, hp, qi: (bi, 0, hp)),
            pl.BlockSpec((None, LANES, LP), lambda bi, hp, qi: (bi, hp, 0)),
            pl.BlockSpec((None, LP, SMALL_W), lambda bi, hp, qi: (bi, 0, 0)),
        ],
        out_specs=pl.BlockSpec((None, ATT_TQ, LANES), lambda bi, hp, qi: (bi, qi, hp)),
        out_shape=jax.ShapeDtypeStruct((nb, LP, FOX_WIDTH), BF16),
        scratch_shapes=[pltpu.VMEM((2, 1, ATT_TQ), F32),
                        pltpu.VMEM((2, acc_rows, ATT_TQ), F32)]
                       + [pltpu.VMEM((2, ATT_TK, ATT_TQ), F32),
                          pltpu.VMEM((2, 1, ATT_TQ), F32)] * 2,
        compiler_params=pltpu.CompilerParams(
            dimension_semantics=("parallel", "parallel", "arbitrary"),
            vmem_limit_bytes=VMEM_LIMIT),
        name="fox_attention",
    )(qt, k, vt, c)


def _bias_tiles(bias_ref):
    rows = jnp.broadcast_to(bias_ref[...], (CHUNK, 2 * CHUNK))
    rolled = pltpu.roll(rows, 0, 1, stride=1, stride_axis=0)
    return rolled[:, :CHUNK], rolled[:, CHUNK:]


def _diagonal_bias(tiles):
    zeros = jnp.zeros((CHUNK, CHUNK), F32)
    rows = []
    for a in range(N_TK_CHUNKS):
        rows.append(jnp.concatenate(
            [tiles[b - a] if b - a in (0, 1) else zeros for b in range(N_TQ_CHUNKS)], axis=1))
    return jnp.concatenate(rows, axis=0)


def _diff_body(lambda_init, qt_ref, k_ref, vt_ref, bias_ref, lam_ref, subln_ref, o_ref,
               m_ref, acc_ref, tiles_ref, sa_ref, pa_ref, sb_ref, pb_ref):
    qb = pl.program_id(2)
    _init_softmax_state(m_ref, acc_ref)
    buf_a, buf_b = (sa_ref, pa_ref), (sb_ref, pb_ref)
    tile0, tile1 = _bias_tiles(bias_ref)
    tiles_ref[0] = tile0
    tiles_ref[1] = tile1
    q_parts = _query_operands(qt_ref)
    pad_rows = -(-PADL // CHUNK) * CHUNK
    corner0 = ATT_TK - CHUNK

    def scores(kb, buf):
        s_dst, max_dst = buf
        start = pl.multiple_of(kb * ATT_TK, ATT_TK)
        k_rows = k_ref[pl.ds(start, ATT_TK), :]
        kpos = lax.broadcasted_iota(jnp.int32, (pad_rows, ATT_TQ), 0)
        not_padding = (kpos >= PADL) | (kb > 0)
        for j in range(2):
            s_t = jnp.dot(k_rows, q_parts[j], preferred_element_type=F32)
            s_t = jnp.concatenate(
                [jnp.where(not_padding, s_t[:pad_rows], NEG), s_t[pad_rows:]], axis=0)
            s_dst[j] = s_t
            max_dst[j] = jnp.max(s_t, axis=0, keepdims=True)

    def step(kb, src, nxt):
        s_src, max_src = src
        start = pl.multiple_of(kb * ATT_TK, ATT_TK)
        diagonal = nxt is None
        if diagonal:
            bias = _diagonal_bias((tiles_ref[0], tiles_ref[1]))
            visible = _causal_mask()
        else:
            scores(kb + 1, nxt)
            corner = jnp.where(kb == qb - 1, 1.0, 0.0) * tiles_ref[1]
        vt_aug = _value_rows(vt_ref, 0, LANES, start)
        for j in range(2):
            s_t = s_src[j]
            if diagonal:
                s_t = jnp.where(visible, s_t + bias, NEG)
                col_max = jnp.max(s_t, axis=0, keepdims=True)
            else:
                near = s_t[corner0:, :CHUNK] + corner
                s_t = jnp.concatenate(
                    [s_t[:corner0],
                     jnp.concatenate([near, s_t[corner0:, CHUNK:]], axis=1)], axis=0)
                first = jnp.maximum(jnp.max(s_t[:corner0, :CHUNK], axis=0, keepdims=True),
                                    jnp.max(near, axis=0, keepdims=True))
                col_max = jnp.concatenate([first, max_src[j][:, CHUNK:]], axis=1)
            _softmax_update(s_t, col_max, vt_aug, m_ref, acc_ref, j)

    _run_key_tiles(qb, scores, step, buf_a, buf_b)

    lam1 = jnp.exp(jnp.sum(lam_ref[0:1, :] * lam_ref[1:2, :], axis=-1, keepdims=True))
    lam2 = jnp.exp(jnp.sum(lam_ref[2:3, :] * lam_ref[3:4, :], axis=-1, keepdims=True))
    lam = lam1 - lam2 + lambda_init
    o_t = (acc_ref[0, :LANES, :] / acc_ref[0, LANES:LANES + 1, :]
           - lam * (acc_ref[1, :LANES, :] / acc_ref[1, LANES:LANES + 1, :]))
    o = o_t.T
    ms = jnp.mean(o * o, axis=-1, keepdims=True)
    o = o * lax.rsqrt(ms + RMS_EPS) * subln_ref[...] * (1.0 - lambda_init)
    o_ref[...] = o.astype(o_ref.dtype)


def _diff_attention(qt, k, vt, bias_rows, lam_rows, subln, lambda_init):
    nb = qt.shape[0]
    acc_rows = LANES + ONES_ROWS
    return pl.pallas_call(
        functools.partial(_diff_body, lambda_init),
        grid=(nb, DIFF_HEADS, LP // ATT_TQ),
        in_specs=[
            pl.BlockSpec((None, LANES, ATT_TQ), lambda bi, hd, qi: (bi, hd, qi)),
            pl.BlockSpec((None, LP, LANES), lambda bi, hd, qi: (bi, 0, hd)),
            pl.BlockSpec((None, LANES, LP), lambda bi, hd, qi: (bi, hd, 0)),
            pl.BlockSpec((None, 1, 2 * CHUNK), lambda bi, hd, qi: (hd, 0, 0)),
            _const_spec(lam_rows.shape),
            _const_spec(subln.shape),
        ],
        out_specs=pl.BlockSpec((None, ATT_TQ, LANES), lambda bi, hd, qi: (bi, qi, hd)),
        out_shape=jax.ShapeDtypeStruct((nb, LP, DIFF_HEADS * LANES), BF16),
        scratch_shapes=[pltpu.VMEM((2, 1, ATT_TQ), F32),
                        pltpu.VMEM((2, acc_rows, ATT_TQ), F32),
                        pltpu.VMEM((2, CHUNK, CHUNK), F32)]
                       + [pltpu.VMEM((2, ATT_TK, ATT_TQ), F32),
                          pltpu.VMEM((2, 1, ATT_TQ), F32)] * 2,
        compiler_params=pltpu.CompilerParams(
            dimension_semantics=("parallel", "parallel", "arbitrary"),
            vmem_limit_bytes=VMEM_LIMIT),
        name="diff_attention",
    )(qt, k, vt, bias_rows, lam_rows, subln)


def _t5_bucket(n):
    max_exact = N_BUCKETS // 2
    nf = jnp.maximum(n, 1).astype(F32)
    large = max_exact + (jnp.log(nf / max_exact) / math.log(128 / max_exact)
                         * (N_BUCKETS - max_exact)).astype(jnp.int32)
    large = jnp.minimum(large, N_BUCKETS - 1)
    return jnp.where(n < max_exact, n, large)


def _relative_bias_rows(rel_table):
    dist = jnp.arange(2 * CHUNK)
    by_dist = LOG2E * (rel_table[_t5_bucket(dist)] - rel_table[N_BUCKETS - 1])
    return by_dist.T[:, None, :].astype(F32)


def _pad_lanes(vec, width=LANES):
    return jnp.pad(vec, (0, width - vec.shape[0]))[None, :].astype(F32)


def kernel(x, meta_tokens, ln_gain, ln_bias, ffn1_w_gate, ffn1_w_up, ffn1_w_down, ffn2_w_gate, ffn2_w_up, ffn2_w_down, even_w_in, even_conv_w, even_conv_b, ssd_dt_bias, ssd_a_log, ssd_d_skip, ssd_norm_w, fox_f_bias, even_w_out, diff_w_qkv, diff_lambda_q1, diff_lambda_k1, diff_lambda_q2, diff_lambda_k2, diff_subln_w, diff_w_o, rel_bias_table):
    nb = x.shape[0]
    lead = jnp.concatenate([jnp.zeros((PADL, D_MODEL), x.dtype), meta_tokens.astype(x.dtype)], axis=0)

    def ln_params(l, i):
        return ln_gain[l, i][None, :], ln_bias[l, i][None, :]

    ffn1 = tuple(w.astype(BF16) for w in (ffn1_w_gate, ffn1_w_up, ffn1_w_down))
    ffn2 = tuple(w.astype(BF16) for w in (ffn2_w_gate, ffn2_w_up, ffn2_w_down))

    hf = _ffn_ln_first(x, lead, *ffn1, 0, *ln_params(0, 0))

    w_in = even_w_in[0]
    o_z, o_xbc = 0, SSD_D_INNER
    o_dt = o_xbc + SSD_CONV_CH
    o_q = o_dt + SSD_HEADS
    o_k, o_v = o_q + FOX_WIDTH, o_q + 2 * FOX_WIDTH
    o_f = o_q + 3 * FOX_WIDTH
    w_small = jnp.concatenate(
        [w_in[:, o_dt:o_q], w_in[:, o_f:],
         jnp.zeros((D_MODEL, SMALL_W - SSD_HEADS - FOX_HEADS), w_in.dtype)], axis=1)
    w_even = jnp.concatenate([w_in[:, o_z:o_dt], w_in[:, o_k:o_v], w_small], axis=1).astype(BF16)
    wt_even = jnp.concatenate([w_in[:, o_q:o_k] * Q_SCALE, w_in[:, o_v:o_f]], axis=1).T.astype(BF16)
    xc, z, k, small, qt, vt = _even_proj(hf, w_even, wt_even, even_conv_w[0],
                                         even_conv_b[0][None, :], nb)

    sbias = _pad_lanes(jnp.concatenate([ssd_dt_bias[0], fox_f_bias[0]]))
    alog = _pad_lanes(ssd_a_log[0])
    dskip = jnp.repeat(ssd_d_skip[0], SSD_HEAD_DIM)[None, :].astype(F32)
    expand = np.zeros((SMALL_W, SSD_D_INNER), np.float32)
    expand[np.arange(SSD_D_INNER) // SSD_HEAD_DIM, np.arange(SSD_D_INNER)] = 1.0
    y, cfull = _ssd(xc.reshape(nb, LP, SSD_CONV_CH), z.reshape(nb, LP, SSD_D_INNER),
                    small.reshape(nb, LP, SMALL_W), sbias, alog, dskip, ssd_norm_w[0][None, :],
                    jnp.asarray(np.concatenate([expand, expand], axis=0), BF16))
    ck = jnp.where(jnp.arange(LP)[None, :, None] < PADL, -NEG, LOG2E * cfull)
    o = _fox_attention(qt, k.reshape(nb, LP, FOX_WIDTH), vt, ck)
    w_out = even_w_out[0].astype(BF16)
    hf = _outproj_ln(hf, [y.reshape(nb * LP, SSD_D_INNER), o.reshape(nb * LP, FOX_WIDTH)],
                     [w_out[:SSD_D_INNER], w_out[SSD_D_INNER:]], *ln_params(0, 1),
                     name="even_out_proj_ln")
    hf = _ffn_ln(hf, *ffn2, 0, *ln_params(0, 2))

    hf = _ffn_ln(hf, *ffn1, 1, *ln_params(1, 0))
    qw = DIFF_HEADS * 2 * HEAD_DIM
    w_qkv = diff_w_qkv[0]
    wt_diff = jnp.concatenate([w_qkv[:, :qw] * Q_SCALE, w_qkv[:, 2 * qw:]], axis=1).T.astype(BF16)
    k, qt, vt = _proj(hf, w_qkv[:, qw:2 * qw].astype(BF16), wt_diff, (qw,), (BF16,),
                      (qw, DIFF_HEADS * LANES), nb, "diff_qkv_proj")
    lambda_init = 0.8 - 0.6 * math.exp(-0.3 * 1)
    lam_rows = jnp.concatenate(
        [_pad_lanes(diff_lambda_q1[0]), _pad_lanes(diff_lambda_k1[0]),
         _pad_lanes(diff_lambda_q2[0]), _pad_lanes(diff_lambda_k2[0]),
         jnp.zeros((4, LANES), F32)], axis=0)
    o = _diff_attention(qt, k.reshape(nb, LP, qw), vt, _relative_bias_rows(rel_bias_table),
                        lam_rows, diff_subln_w[0][None, :], lambda_init)
    hf = _outproj_ln(hf, [o.reshape(nb * LP, DIFF_HEADS * LANES)], [diff_w_o[0].astype(BF16)],
                     *ln_params(1, 1), name="diff_out_proj_ln")
    return _ffn_ln_final(hf.reshape(nb, LP, D_MODEL), *ffn2, 1, *ln_params(1, 2))
```

```python
import functools
import math

import numpy as np
import jax
import jax.numpy as jnp
from jax import lax
from jax.experimental import pallas as pl
from jax.experimental.pallas import tpu as pltpu

F32 = jnp.float32
BF16 = jnp.bfloat16

D_MODEL = 1024
SEQ = 8192
DEPTH = 2
N_META = 16
CHUNK = 128
SSD_D_INNER = 2048
SSD_HEAD_DIM = 64
SSD_HEADS = 32
SSD_GROUPS = 4
SSD_GROUP_W = SSD_D_INNER // SSD_GROUPS
SSD_STATE = 128
SSD_CONV = 4
SSD_CONV_CH = SSD_D_INNER + 2 * SSD_GROUPS * SSD_STATE
FOX_HEADS = 16
FOX_WIDTH = 1024
HEAD_DIM = 64
DIFF_HEADS = 8
N_BUCKETS = 32
D_FF = 2816
ALPHA = (2 * DEPTH) ** 0.25
LN_EPS = 1e-5
RMS_EPS = 1e-5
NEG = -1e30

LANES = 128
LP = 8448
PADL = LP - SEQ - N_META
ROW_TILE = 512
OUT_TILE = 256
ATT_TQ = 768
ATT_TK = 768
CONV_HALO = 8
SMALL_W = LANES
VMEM_LIMIT = 56 * 1024 * 1024

assert PADL % CHUNK == CHUNK - N_META
assert LP % ATT_TQ == 0 and ATT_TQ % ATT_TK == 0 and ATT_TK % LANES == 0
assert LP % CHUNK == 0 and (2 * LP) % ROW_TILE == 0
assert PADL <= OUT_TILE and (LP - OUT_TILE) == SEQ


def _const_spec(shape):
    nd = len(shape)
    return pl.BlockSpec(shape, lambda *_: (0,) * nd, pipeline_mode=pl.Buffered(1))


def _layer_spec(shape, layer):
    nd = len(shape)
    return pl.BlockSpec((None,) + tuple(shape), lambda *_: (layer,) + (0,) * nd,
                        pipeline_mode=pl.Buffered(1))


def _ffn_weight_specs(layer):
    return [_layer_spec((D_MODEL, D_FF), layer), _layer_spec((D_MODEL, D_FF), layer),
            _layer_spec((D_FF, D_MODEL), layer)]


def _layer_norm(r, g, b):
    mu = jnp.mean(r, axis=-1, keepdims=True)
    d = r - mu
    var = jnp.mean(d * d, axis=-1, keepdims=True)
    return d * lax.rsqrt(var + LN_EPS) * g + b


def _silu(x):
    return x / (1.0 + jnp.exp(-x))


def _softplus(x):
    return jnp.maximum(x, 0.0) + jnp.log(1.0 + jnp.exp(-jnp.abs(x)))


def _ffn_ln_body(h_ref, wg_ref, wu_ref, wd_ref, g_ref, b_ref, o_ref):
    h = h_ref[...]
    hb = h.astype(BF16)
    g = jnp.dot(hb, wg_ref[...], preferred_element_type=F32)
    u = jnp.dot(hb, wu_ref[...], preferred_element_type=F32)
    a = (_silu(g) * u).astype(BF16)
    y = jnp.dot(a, wd_ref[...], preferred_element_type=F32)
    o_ref[...] = _layer_norm(ALPHA * h + 0.5 * y, g_ref[...], b_ref[...])


def _ffn_ln(hf, wg, wu, wd, layer, g, b):
    rows = hf.shape[0]
    return pl.pallas_call(
        _ffn_ln_body,
        grid=(rows // ROW_TILE,),
        in_specs=[
            pl.BlockSpec((ROW_TILE, D_MODEL), lambda i: (i, 0)),
            *_ffn_weight_specs(layer),
            _const_spec((1, D_MODEL)),
            _const_spec((1, D_MODEL)),
        ],
        out_specs=pl.BlockSpec((ROW_TILE, D_MODEL), lambda i: (i, 0)),
        out_shape=jax.ShapeDtypeStruct((rows, D_MODEL), F32),
        compiler_params=pltpu.CompilerParams(
            dimension_semantics=("parallel",), vmem_limit_bytes=VMEM_LIMIT),
        name="ffn_ln",
    )(hf, wg, wu, wd, g, b)


def _ffn_ln_first_body(lead_ref, x_ref, wg_ref, wu_ref, wd_ref, g_ref, b_ref, o_ref, h_ref):
    h_ref[...] = jnp.where(pl.program_id(1) == 0, lead_ref[...], x_ref[...])
    _ffn_ln_body(h_ref, wg_ref, wu_ref, wd_ref, g_ref, b_ref, o_ref)


def _ffn_ln_first(x, lead, wg, wu, wd, layer, g, b):
    nb = x.shape[0]
    tiles = LP // OUT_TILE
    return pl.pallas_call(
        _ffn_ln_first_body,
        grid=(nb, tiles),
        in_specs=[
            _const_spec((OUT_TILE, D_MODEL)),
            pl.BlockSpec((None, OUT_TILE, D_MODEL), lambda bi, i: (bi, jnp.maximum(i - 1, 0), 0)),
            *_ffn_weight_specs(layer),
            _const_spec((1, D_MODEL)),
            _const_spec((1, D_MODEL)),
        ],
        out_specs=pl.BlockSpec((OUT_TILE, D_MODEL), lambda bi, i: (bi * tiles + i, 0)),
        out_shape=jax.ShapeDtypeStruct((nb * LP, D_MODEL), F32),
        scratch_shapes=[pltpu.VMEM((OUT_TILE, D_MODEL), F32)],
        compiler_params=pltpu.CompilerParams(
            dimension_semantics=("parallel", "arbitrary"), vmem_limit_bytes=VMEM_LIMIT),
        name="ffn_ln_first",
    )(lead, x, wg, wu, wd, g, b)


def _ffn_ln_final_body(h_ref, wg_ref, wu_ref, wd_ref, g_ref, b_ref, o_ref):
    @pl.when(pl.program_id(1) > 0)
    def _():
        _ffn_ln_body(h_ref, wg_ref, wu_ref, wd_ref, g_ref, b_ref, o_ref)


def _ffn_ln_final(h3, wg, wu, wd, layer, g, b):
    nb = h3.shape[0]
    return pl.pallas_call(
        _ffn_ln_final_body,
        grid=(nb, LP // OUT_TILE),
        in_specs=[
            pl.BlockSpec((None, OUT_TILE, D_MODEL), lambda bi, i: (bi, i, 0)),
            *_ffn_weight_specs(layer),
            _const_spec((1, D_MODEL)),
            _const_spec((1, D_MODEL)),
        ],
        out_specs=pl.BlockSpec((None, OUT_TILE, D_MODEL),
                               lambda bi, i: (bi, jnp.maximum(i - 1, 0), 0)),
        out_shape=jax.ShapeDtypeStruct((nb, SEQ, D_MODEL), F32),
        compiler_params=pltpu.CompilerParams(
            dimension_semantics=("arbitrary", "arbitrary"), vmem_limit_bytes=VMEM_LIMIT),
        name="ffn_ln_final",
    )(h3, wg, wu, wd, g, b)


PROJ_TILE = 256
assert LP % PROJ_TILE == 0


def _proj_body(n_t, h_ref, w_ref, wt_ref, *o_refs):
    hb = h_ref[...].astype(BF16)
    off = 0
    for o_ref in o_refs[:-n_t]:
        n = o_ref.shape[-1]
        o_ref[...] = jnp.dot(hb, w_ref[:, off:off + n],
                             preferred_element_type=F32).astype(o_ref.dtype)
        off += n
    off = 0
    for o_ref in o_refs[-n_t:]:
        n = o_ref.shape[0]
        o_ref[...] = lax.dot_general(wt_ref[off:off + n, :], hb, (((1,), (1,)), ((), ())),
                                     preferred_element_type=F32).astype(o_ref.dtype)
        off += n


def _proj(hf, w, wt, widths, dtypes, t_widths, nb, name):
    rows = hf.shape[0]
    tiles_per_batch = LP // PROJ_TILE
    out_specs = [pl.BlockSpec((PROJ_TILE, n), lambda i: (i, 0)) for n in widths]
    out_specs += [pl.BlockSpec((None, n, PROJ_TILE),
                               lambda i: (i // tiles_per_batch, 0, i % tiles_per_batch))
                  for n in t_widths]
    out_shape = [jax.ShapeDtypeStruct((rows, n), dt) for n, dt in zip(widths, dtypes)]
    out_shape += [jax.ShapeDtypeStruct((nb, n, LP), BF16) for n in t_widths]
    return pl.pallas_call(
        functools.partial(_proj_body, len(t_widths)),
        grid=(rows // PROJ_TILE,),
        in_specs=[pl.BlockSpec((PROJ_TILE, D_MODEL), lambda i: (i, 0)),
                  _const_spec(w.shape), _const_spec(wt.shape)],
        out_specs=out_specs,
        out_shape=out_shape,
        compiler_params=pltpu.CompilerParams(
            dimension_semantics=("parallel",), vmem_limit_bytes=VMEM_LIMIT),
        name=name,
    )(hf, w, wt)


def _outproj_ln_body(n_in, h_ref, *refs):
    a_refs = refs[:n_in]
    w_refs = refs[n_in:2 * n_in]
    g_ref, b_ref, o_ref = refs[2 * n_in:]
    m = jnp.dot(a_refs[0][...], w_refs[0][...], preferred_element_type=F32)
    for a_ref, w_ref in zip(a_refs[1:], w_refs[1:]):
        m = m + jnp.dot(a_ref[...], w_ref[...], preferred_element_type=F32)
    o_ref[...] = _layer_norm(ALPHA * h_ref[...] + m, g_ref[...], b_ref[...])


def _outproj_ln(hf, acts, ws, g, b, name):
    rows = hf.shape[0]
    n_in = len(acts)
    in_specs = [pl.BlockSpec((ROW_TILE, D_MODEL), lambda i: (i, 0))]
    in_specs += [pl.BlockSpec((ROW_TILE, a.shape[1]), lambda i: (i, 0)) for a in acts]
    in_specs += [_const_spec(w.shape) for w in ws]
    in_specs += [_const_spec((1, D_MODEL)), _const_spec((1, D_MODEL))]
    return pl.pallas_call(
        functools.partial(_outproj_ln_body, n_in),
        grid=(rows // ROW_TILE,),
        in_specs=in_specs,
        out_specs=pl.BlockSpec((ROW_TILE, D_MODEL), lambda i: (i, 0)),
        out_shape=jax.ShapeDtypeStruct((rows, D_MODEL), F32),
        compiler_params=pltpu.CompilerParams(
            dimension_semantics=("parallel",), vmem_limit_bytes=VMEM_LIMIT),
        name=name,
    )(hf, *acts, *ws, g, b)


def _split_dot(x, e2_ref):
    hi = x.astype(BF16)
    lo = (x - hi.astype(F32)).astype(BF16)
    return jnp.dot(jnp.concatenate([hi, lo], axis=1), e2_ref[...], preferred_element_type=F32)


def _ssd_body(xbc_ref, z_ref, small_ref, convw_ref, convb_ref, sbias_ref, alog_ref,
              dskip_ref, normw_ref, expand_ref, y_ref, c_ref,
              ext_ref, state_ref, carry_ref):
    c = pl.program_id(1)

    @pl.when(c == 0)
    def _():
        ext_ref[0:CONV_HALO, :] = jnp.zeros((CONV_HALO, SSD_CONV_CH), F32)
        state_ref[...] = jnp.zeros_like(state_ref)
        carry_ref[...] = jnp.zeros_like(carry_ref)

    row = lax.broadcasted_iota(jnp.int32, (CHUNK, 1), 0)
    valid = (c * CHUNK + row) >= PADL

    ext_ref[CONV_HALO:, :] = xbc_ref[...]

    @pl.when(c * CHUNK < PADL)
    def _():
        ext_ref[CONV_HALO:, :] = jnp.where(valid, ext_ref[CONV_HALO:, :], 0.0)

    ext = ext_ref[...]
    conv = convb_ref[...] + convw_ref[SSD_CONV - 1:SSD_CONV, :] * ext[CONV_HALO:, :]
    for k in range(SSD_CONV - 1):
        shifted = pltpu.roll(ext, SSD_CONV - 1 - k, 0)[CONV_HALO:, :]
        conv = conv + convw_ref[k:k + 1, :] * shifted
    ext_ref[0:CONV_HALO, :] = ext_ref[CHUNK:CHUNK + CONV_HALO, :]
    xc = _silu(conv)
    xs = xc[:, :SSD_D_INNER]
    bm = xc[:, SSD_D_INNER:SSD_D_INNER + SSD_GROUPS * SSD_STATE].astype(BF16)
    cm = xc[:, SSD_D_INNER + SSD_GROUPS * SSD_STATE:].astype(BF16)

    lane = lax.broadcasted_iota(jnp.int32, (CHUNK, SMALL_W), 1)
    is_dt = lane < SSD_HEADS
    is_f = (lane >= SSD_HEADS) & (lane < SSD_HEADS + FOX_HEADS)
    v = small_ref[...] + sbias_ref[...]
    dt = jnp.where(valid & is_dt, _softplus(v), 0.0)
    log_f = jnp.where(valid & is_f, -_softplus(-v), 0.0)
    neg_a = -jnp.exp(alog_ref[...])
    steps = jnp.where(is_dt, dt * neg_a, log_f)
    r_i = lax.broadcasted_iota(jnp.int32, (CHUNK, CHUNK), 0)
    c_i = lax.broadcasted_iota(jnp.int32, (CHUNK, CHUNK), 1)
    causal = r_i >= c_i
    tril = jnp.where(causal, 1.0, 0.0).astype(F32)
    cum = jnp.dot(tril, steps, preferred_element_type=F32,
                  precision=lax.Precision.HIGHEST)
    c_total = jnp.where(is_f, cum + carry_ref[...], 0.0)
    c_ref[...] = c_total
    carry_ref[...] = c_total[CHUNK - 1:CHUNK, :]

    a_last = cum[CHUNK - 1:CHUNK, :]
    dt_x = _split_dot(dt, expand_ref)
    ea_x = _split_dot(jnp.exp(cum), expand_ref)
    de_x = _split_dot(jnp.exp(a_last - cum), expand_ref)
    x_dt = xs * dt_x
    xb = x_dt.astype(BF16)
    xe = (x_dt * de_x).astype(BF16)
    cum_t = cum.T
    lane_p = lax.broadcasted_iota(jnp.int32, (CHUNK, 2 * SSD_HEAD_DIM), 1)
    first_half = lane_p < SSD_HEAD_DIM

    y_groups = []
    for g in range(SSD_GROUPS):
        gs = slice(g * SSD_GROUP_W, (g + 1) * SSD_GROUP_W)
        bg = bm[:, g * SSD_STATE:(g + 1) * SSD_STATE]
        cg = cm[:, g * SSD_STATE:(g + 1) * SSD_STATE]
        cb = lax.dot_general(cg, bg, (((1,), (1,)), ((), ())),
                             preferred_element_type=F32)
        pair_out = []
        for pr in range(SSD_GROUP_W // (2 * SSD_HEAD_DIM)):
            col0 = g * SSD_GROUP_W + pr * 2 * SSD_HEAD_DIM
            x_pair = xb[:, col0:col0 + 2 * SSD_HEAD_DIM]
            ys = []
            for j in range(2):
                hd = col0 // SSD_HEAD_DIM + j
                diff = cum[:, hd:hd + 1] - cum_t[hd:hd + 1, :]
                decay = jnp.exp(jnp.where(causal, diff, -jnp.inf))
                mat = (cb * decay).astype(BF16)
                ys.append(jnp.dot(mat, x_pair, preferred_element_type=F32))
            pair_out.append(jnp.where(first_half, ys[0], ys[1]))
        y_diag = jnp.concatenate(pair_out, axis=1)
        st = state_ref[g]
        y_off = jnp.dot(cg, st.astype(BF16), preferred_element_type=F32) * ea_x[:, gs]
        new = lax.dot_general(bg, xe[:, gs], (((0,), (0,)), ((), ())),
                              preferred_element_type=F32)
        state_ref[g] = st * ea_x[CHUNK - 1:CHUNK, gs] + new
        yg = y_diag + y_off + dskip_ref[:, gs] * xs[:, gs]
        yg = yg * _silu(z_ref[:, gs])
        ms = jnp.mean(yg * yg, axis=-1, keepdims=True)
        y_groups.append(yg * lax.rsqrt(ms + RMS_EPS) * normw_ref[:, gs])
    y_ref[...] = jnp.concatenate(y_groups, axis=1).astype(y_ref.dtype)


def _ssd(xbc, z, small, convw, convb, sbias, alog, dskip, normw, expand):
    nb = xbc.shape[0]
    row_spec = lambda w: pl.BlockSpec((None, CHUNK, w), lambda bi, ci: (bi, ci, 0))
    return pl.pallas_call(
        _ssd_body,
        grid=(nb, LP // CHUNK),
        in_specs=[row_spec(SSD_CONV_CH), row_spec(SSD_D_INNER), row_spec(SMALL_W),
                  _const_spec(convw.shape), _const_spec(convb.shape),
                  _const_spec(sbias.shape), _const_spec(alog.shape),
                  _const_spec(dskip.shape), _const_spec(normw.shape),
                  _const_spec(expand.shape)],
        out_specs=[row_spec(SSD_D_INNER), row_spec(SMALL_W)],
        out_shape=[jax.ShapeDtypeStruct((nb, LP, SSD_D_INNER), BF16),
                   jax.ShapeDtypeStruct((nb, LP, SMALL_W), F32)],
        scratch_shapes=[pltpu.VMEM((CHUNK + CONV_HALO, SSD_CONV_CH), F32),
                        pltpu.VMEM((SSD_GROUPS, SSD_STATE, SSD_GROUP_W), F32),
                        pltpu.VMEM((1, SMALL_W), F32)],
        compiler_params=pltpu.CompilerParams(
            dimension_semantics=("arbitrary", "arbitrary"), vmem_limit_bytes=VMEM_LIMIT),
        name="ssd",
    )(xbc, z, small, convw, convb, sbias, alog, dskip, normw, expand)


N_TK_CHUNKS = ATT_TK // LANES
N_TQ_CHUNKS = ATT_TQ // LANES
ONES_ROWS = 16
assert ATT_TQ == ATT_TK and PADL <= ATT_TK
LOG2E = math.log2(math.e)
Q_SCALE = LOG2E * HEAD_DIM ** -0.5


def _init_softmax_state(m_ref, acc_ref):
    m_ref[...] = jnp.full(m_ref.shape, NEG, F32)
    acc_ref[...] = jnp.zeros_like(acc_ref)


def _run_key_tiles(qb, first, scores, step, buf_a, buf_b):
    n_below = qb - first
    odd = n_below % 2

    @pl.when(odd == 0)
    def _():
        scores(first, buf_a)

    @pl.when(odd == 1)
    def _():
        scores(first, buf_b)
        step(first, buf_b, buf_a)

    def two_below_diagonal(i, carry):
        kb = first + odd + 2 * i
        step(kb, buf_a, buf_b)
        step(kb + 1, buf_b, buf_a)
        return carry

    lax.fori_loop(0, n_below // 2, two_below_diagonal, 0)
    step(qb, buf_a, None)


def _query_operands(qt_ref):
    zero_rows = jnp.zeros((HEAD_DIM, ATT_TQ), BF16)
    return [jnp.concatenate([qt_ref[:HEAD_DIM, :], zero_rows], axis=0),
            jnp.concatenate([zero_rows, qt_ref[HEAD_DIM:, :]], axis=0)]


def _value_rows(vt_ref, row0, n_rows, start):
    vt = vt_ref[row0:row0 + n_rows, pl.ds(start, ATT_TK)]
    return jnp.concatenate([vt, jnp.ones((ONES_ROWS, ATT_TK), BF16)], axis=0)


def _softmax_update(s_t, col_max, vt_aug, m_ref, acc_ref, idx):
    m_old = m_ref[idx]
    m_new = jnp.maximum(m_old, col_max)
    alpha = jnp.exp2(m_old - m_new)
    p_t = jnp.exp2(s_t - m_new).astype(BF16)
    pv = jnp.dot(vt_aug, p_t, preferred_element_type=F32)
    acc_ref[idx] = alpha * acc_ref[idx] + pv
    m_ref[idx] = m_new


def _causal_mask():
    kpos = lax.broadcasted_iota(jnp.int32, (ATT_TK, ATT_TQ), 0)
    qpos = lax.broadcasted_iota(jnp.int32, (ATT_TK, ATT_TQ), 1)
    return kpos <= qpos


def _fox_body(first_ref, qt_ref, k_ref, vt_ref, c_ref, o_ref, m_ref, acc_ref,
              sa_ref, pa_ref, sb_ref, pb_ref):
    qb = pl.program_id(2)
    n_q = pl.num_programs(2)
    first = first_ref[(pl.program_id(0) * pl.num_programs(1) + pl.program_id(1)) * n_q + qb]
    first = jnp.minimum(first, qb)
    _init_softmax_state(m_ref, acc_ref)
    buf_a, buf_b = (sa_ref, pa_ref), (sb_ref, pb_ref)
    q_heads = _query_operands(qt_ref)
    lane = lax.broadcasted_iota(jnp.int32, (ATT_TK, LANES), 1)
    first_lane = SSD_HEADS + 2 * pl.program_id(1)

    def scores(kb, buf):
        s_dst, max_dst = buf
        start = pl.multiple_of(kb * ATT_TK, ATT_TK)
        k_rows = k_ref[pl.ds(start, ATT_TK), :]
        c_rows = c_ref[pl.ds(start, ATT_TK), :]
        for j in range(2):
            ck = jnp.sum(jnp.where(lane == first_lane + j, c_rows, 0.0), axis=1, keepdims=True)
            s_t = jnp.dot(k_rows, q_heads[j], preferred_element_type=F32) - ck
            s_dst[j] = s_t
            max_dst[j] = jnp.max(s_t, axis=0, keepdims=True)

    def step(kb, src, nxt):
        s_src, max_src = src
        start = pl.multiple_of(kb * ATT_TK, ATT_TK)
        if nxt is not None:
            scores(kb + 1, nxt)
        for j in range(2):
            s_t = s_src[j]
            col_max = max_src[j]
            if nxt is None:
                s_t = jnp.where(_causal_mask(), s_t, NEG)
                col_max = jnp.max(s_t, axis=0, keepdims=True)
            vt_aug = _value_rows(vt_ref, j * HEAD_DIM, HEAD_DIM, start)
            _softmax_update(s_t, col_max, vt_aug, m_ref, acc_ref, j)

    _run_key_tiles(qb, first, scores, step, buf_a, buf_b)

    o_t = jnp.concatenate(
        [acc_ref[j, :HEAD_DIM, :] / acc_ref[j, HEAD_DIM:HEAD_DIM + 1, :] for j in range(2)], axis=0)
    o_ref[...] = o_t.T.astype(o_ref.dtype)


def _fox_attention(first, qt, k, vt, c):
    nb = qt.shape[0]
    pairs = FOX_HEADS // 2
    acc_rows = HEAD_DIM + ONES_ROWS
    return pl.pallas_call(
        _fox_body,
        grid_spec=pltpu.PrefetchScalarGridSpec(
            num_scalar_prefetch=1,
            grid=(nb, pairs, LP // ATT_TQ),
            in_specs=[
                pl.BlockSpec((None, LANES, ATT_TQ), lambda bi, hp, qi, _: (bi, hp, qi)),
                pl.BlockSpec((None, LP, LANES), lambda bi, hp, qi, _: (bi, 0, hp)),
                pl.BlockSpec((None, LANES, LP), lambda bi, hp, qi, _: (bi, hp, 0)),
                pl.BlockSpec((None, LP, SMALL_W), lambda bi, hp, qi, _: (bi, 0, 0)),
            ],
            out_specs=pl.BlockSpec((None, ATT_TQ, LANES), lambda bi, hp, qi, _: (bi, qi, hp)),
            scratch_shapes=[pltpu.VMEM((2, 1, ATT_TQ), F32),
                            pltpu.VMEM((2, acc_rows, ATT_TQ), F32)]
                           + [pltpu.VMEM((2, ATT_TK, ATT_TQ), F32),
                              pltpu.VMEM((2, 1, ATT_TQ), F32)] * 2),
        out_shape=jax.ShapeDtypeStruct((nb, LP, FOX_WIDTH), BF16),
        compiler_params=pltpu.CompilerParams(
            dimension_semantics=("parallel", "parallel", "arbitrary"),
            vmem_limit_bytes=VMEM_LIMIT),
        name="fox_attention",
    )(first, qt, k, vt, c)


FORGOTTEN_LOG2 = 160.0


def _first_key_tiles(qt, k, c2):
    nb = qt.shape[0]
    n_q = LP // ATT_TQ
    q_norm = jnp.sqrt(jnp.sum(jnp.square(qt.astype(F32)).reshape(nb, FOX_HEADS, HEAD_DIM, LP), axis=2))
    q_max = jnp.max(q_norm.reshape(nb, FOX_HEADS, n_q, ATT_TQ), axis=-1)
    k_norm = jnp.sqrt(jnp.sum(jnp.square(k.astype(F32)).reshape(nb, LP, FOX_HEADS, HEAD_DIM), axis=-1))
    k_max = jnp.max(k_norm, axis=1)
    c_heads = c2[:, :, SSD_HEADS:SSD_HEADS + FOX_HEADS]
    c_query = c_heads[:, ::ATT_TQ, :].transpose(0, 2, 1)
    c_key = c_heads[:, ATT_TK - 1::ATT_TK, :].transpose(0, 2, 1)
    bound = (2.0 * q_max * k_max[:, :, None])[:, :, :, None] - (c_key[:, :, None, :] - c_query[:, :, :, None])
    needed = bound >= -FORGOTTEN_LOG2
    needed = needed.reshape(nb, FOX_HEADS // 2, 2, n_q, n_q).any(axis=2)
    needed = needed | (jnp.arange(n_q)[None, :] >= jnp.arange(n_q)[:, None])
    return jnp.argmax(needed, axis=-1).astype(jnp.int32).reshape(-1)


def _bias_tiles(bias_ref):
    rows = jnp.broadcast_to(bias_ref[...], (CHUNK, 2 * CHUNK))
    rolled = pltpu.roll(rows, 0, 1, stride=1, stride_axis=0)
    return rolled[:, :CHUNK], rolled[:, CHUNK:]


def _diagonal_bias(tiles):
    zeros = jnp.zeros((CHUNK, CHUNK), F32)
    rows = []
    for a in range(N_TK_CHUNKS):
        rows.append(jnp.concatenate(
            [tiles[b - a] if b - a in (0, 1) else zeros for b in range(N_TQ_CHUNKS)], axis=1))
    return jnp.concatenate(rows, axis=0)


def _diff_body(lambda_init, qt_ref, k_ref, vt_ref, bias_ref, lam_ref, subln_ref, o_ref,
               m_ref, acc_ref, tiles_ref, sa_ref, pa_ref, sb_ref, pb_ref):
    qb = pl.program_id(2)
    _init_softmax_state(m_ref, acc_ref)
    buf_a, buf_b = (sa_ref, pa_ref), (sb_ref, pb_ref)
    tile0, tile1 = _bias_tiles(bias_ref)
    tiles_ref[0] = tile0
    tiles_ref[1] = tile1
    q_parts = _query_operands(qt_ref)
    pad_rows = -(-PADL // CHUNK) * CHUNK
    corner0 = ATT_TK - CHUNK

    def scores(kb, buf):
        s_dst, max_dst = buf
        start = pl.multiple_of(kb * ATT_TK, ATT_TK)
        k_rows = k_ref[pl.ds(start, ATT_TK), :]
        kpos = lax.broadcasted_iota(jnp.int32, (pad_rows, ATT_TQ), 0)
        not_padding = (kpos >= PADL) | (kb > 0)
        for j in range(2):
            s_t = jnp.dot(k_rows, q_parts[j], preferred_element_type=F32)
            s_t = jnp.concatenate(
                [jnp.where(not_padding, s_t[:pad_rows], NEG), s_t[pad_rows:]], axis=0)
            s_dst[j] = s_t
            max_dst[j] = jnp.max(s_t, axis=0, keepdims=True)

    def step(kb, src, nxt):
        s_src, max_src = src
        start = pl.multiple_of(kb * ATT_TK, ATT_TK)
        diagonal = nxt is None
        if diagonal:
            bias = _diagonal_bias((tiles_ref[0], tiles_ref[1]))
            visible = _causal_mask()
        else:
            scores(kb + 1, nxt)
            corner = jnp.where(kb == qb - 1, 1.0, 0.0) * tiles_ref[1]
        vt_aug = _value_rows(vt_ref, 0, LANES, start)
        for j in range(2):
            s_t = s_src[j]
            if diagonal:
                s_t = jnp.where(visible, s_t + bias, NEG)
                col_max = jnp.max(s_t, axis=0, keepdims=True)
            else:
                near = s_t[corner0:, :CHUNK] + corner
                s_t = jnp.concatenate(
                    [s_t[:corner0],
                     jnp.concatenate([near, s_t[corner0:, CHUNK:]], axis=1)], axis=0)
                first = jnp.maximum(jnp.max(s_t[:corner0, :CHUNK], axis=0, keepdims=True),
                                    jnp.max(near, axis=0, keepdims=True))
                col_max = jnp.concatenate([first, max_src[j][:, CHUNK:]], axis=1)
            _softmax_update(s_t, col_max, vt_aug, m_ref, acc_ref, j)

    _run_key_tiles(qb, 0, scores, step, buf_a, buf_b)

    lam1 = jnp.exp(jnp.sum(lam_ref[0:1, :] * lam_ref[1:2, :], axis=-1, keepdims=True))
    lam2 = jnp.exp(jnp.sum(lam_ref[2:3, :] * lam_ref[3:4, :], axis=-1, keepdims=True))
    lam = lam1 - lam2 + lambda_init
    o_t = (acc_ref[0, :LANES, :] / acc_ref[0, LANES:LANES + 1, :]
           - lam * (acc_ref[1, :LANES, :] / acc_ref[1, LANES:LANES + 1, :]))
    o = o_t.T
    ms = jnp.mean(o * o, axis=-1, keepdims=True)
    o = o * lax.rsqrt(ms + RMS_EPS) * subln_ref[...] * (1.0 - lambda_init)
    o_ref[...] = o.astype(o_ref.dtype)


def _diff_attention(qt, k, vt, bias_rows, lam_rows, subln, lambda_init):
    nb = qt.shape[0]
    acc_rows = LANES + ONES_ROWS
    return pl.pallas_call(
        functools.partial(_diff_body, lambda_init),
        grid=(nb, DIFF_HEADS, LP // ATT_TQ),
        in_specs=[
            pl.BlockSpec((None, LANES, ATT_TQ), lambda bi, hd, qi: (bi, hd, qi)),
            pl.BlockSpec((None, LP, LANES), lambda bi, hd, qi: (bi, 0, hd)),
            pl.BlockSpec((None, LANES, LP), lambda bi, hd, qi: (bi, hd, 0)),
            pl.BlockSpec((None, 1, 2 * CHUNK), lambda bi, hd, qi: (hd, 0, 0)),
            _const_spec(lam_rows.shape),
            _const_spec(subln.shape),
        ],
        out_specs=pl.BlockSpec((None, ATT_TQ, LANES), lambda bi, hd, qi: (bi, qi, hd)),
        out_shape=jax.ShapeDtypeStruct((nb, LP, DIFF_HEADS * LANES), BF16),
        scratch_shapes=[pltpu.VMEM((2, 1, ATT_TQ), F32),
                        pltpu.VMEM((2, acc_rows, ATT_TQ), F32),
                        pltpu.VMEM((2, CHUNK, CHUNK), F32)]
                       + [pltpu.VMEM((2, ATT_TK, ATT_TQ), F32),
                          pltpu.VMEM((2, 1, ATT_TQ), F32)] * 2,
        compiler_params=pltpu.CompilerParams(
            dimension_semantics=("parallel", "parallel", "arbitrary"),
            vmem_limit_bytes=VMEM_LIMIT),
        name="diff_attention",
    )(qt, k, vt, bias_rows, lam_rows, subln)


def _t5_bucket(n):
    max_exact = N_BUCKETS // 2
    nf = jnp.maximum(n, 1).astype(F32)
    large = max_exact + (jnp.log(nf / max_exact) / math.log(128 / max_exact)
                         * (N_BUCKETS - max_exact)).astype(jnp.int32)
    large = jnp.minimum(large, N_BUCKETS - 1)
    return jnp.where(n < max_exact, n, large)


def _relative_bias_rows(rel_table):
    dist = jnp.arange(2 * CHUNK)
    by_dist = LOG2E * (rel_table[_t5_bucket(dist)] - rel_table[N_BUCKETS - 1])
    return by_dist.T[:, None, :].astype(F32)


def _pad_lanes(vec, width=LANES):
    return jnp.pad(vec, (0, width - vec.shape[0]))[None, :].astype(F32)


def kernel(x, meta_tokens, ln_gain, ln_bias, ffn1_w_gate, ffn1_w_up, ffn1_w_down, ffn2_w_gate, ffn2_w_up, ffn2_w_down, even_w_in, even_conv_w, even_conv_b, ssd_dt_bias, ssd_a_log, ssd_d_skip, ssd_norm_w, fox_f_bias, even_w_out, diff_w_qkv, diff_lambda_q1, diff_lambda_k1, diff_lambda_q2, diff_lambda_k2, diff_subln_w, diff_w_o, rel_bias_table):
    nb = x.shape[0]
    lead = jnp.concatenate([jnp.zeros((PADL, D_MODEL), x.dtype), meta_tokens.astype(x.dtype)], axis=0)

    def ln_params(l, i):
        return ln_gain[l, i][None, :], ln_bias[l, i][None, :]

    ffn1 = tuple(w.astype(BF16) for w in (ffn1_w_gate, ffn1_w_up, ffn1_w_down))
    ffn2 = tuple(w.astype(BF16) for w in (ffn2_w_gate, ffn2_w_up, ffn2_w_down))

    hf = _ffn_ln_first(x, lead, *ffn1, 0, *ln_params(0, 0))

    w_in = even_w_in[0]
    o_z, o_xbc = 0, SSD_D_INNER
    o_dt = o_xbc + SSD_CONV_CH
    o_q = o_dt + SSD_HEADS
    o_k, o_v = o_q + FOX_WIDTH, o_q + 2 * FOX_WIDTH
    o_f = o_q + 3 * FOX_WIDTH
    w_small = jnp.concatenate(
        [w_in[:, o_dt:o_q], w_in[:, o_f:],
         jnp.zeros((D_MODEL, SMALL_W - SSD_HEADS - FOX_HEADS), w_in.dtype)], axis=1)
    w_even = jnp.concatenate([w_in[:, o_z:o_dt], w_in[:, o_k:o_v], w_small], axis=1).astype(BF16)
    wt_even = jnp.concatenate([w_in[:, o_q:o_k] * Q_SCALE, w_in[:, o_v:o_f]], axis=1).T.astype(BF16)
    z, xbc, k, small, qt, vt = _proj(
        hf, w_even, wt_even, (SSD_D_INNER, SSD_CONV_CH, FOX_WIDTH, SMALL_W),
        (F32, F32, BF16, F32), (FOX_WIDTH, FOX_WIDTH), nb, "even_in_proj")

    sbias = _pad_lanes(jnp.concatenate([ssd_dt_bias[0], fox_f_bias[0]]))
    alog = _pad_lanes(ssd_a_log[0])
    dskip = jnp.repeat(ssd_d_skip[0], SSD_HEAD_DIM)[None, :].astype(F32)
    expand = np.zeros((SMALL_W, SSD_D_INNER), np.float32)
    expand[np.arange(SSD_D_INNER) // SSD_HEAD_DIM, np.arange(SSD_D_INNER)] = 1.0
    y, cfull = _ssd(xbc.reshape(nb, LP, SSD_CONV_CH), z.reshape(nb, LP, SSD_D_INNER),
                    small.reshape(nb, LP, SMALL_W), even_conv_w[0], even_conv_b[0][None, :],
                    sbias, alog, dskip, ssd_norm_w[0][None, :],
                    jnp.asarray(np.concatenate([expand, expand], axis=0), BF16))
    c2 = LOG2E * cfull
    k = k.reshape(nb, LP, FOX_WIDTH)
    ck = jnp.where(jnp.arange(LP)[None, :, None] < PADL, -NEG, c2)
    o = _fox_attention(_first_key_tiles(qt, k, c2), qt, k, vt, ck)
    w_out = even_w_out[0].astype(BF16)
    hf = _outproj_ln(hf, [y.reshape(nb * LP, SSD_D_INNER), o.reshape(nb * LP, FOX_WIDTH)],
                     [w_out[:SSD_D_INNER], w_out[SSD_D_INNER:]], *ln_params(0, 1),
                     name="even_out_proj_ln")
    hf = _ffn_ln(hf, *ffn2, 0, *ln_params(0, 2))

    hf = _ffn_ln(hf, *ffn1, 1, *ln_params(1, 0))
    qw = DIFF_HEADS * 2 * HEAD_DIM
    w_qkv = diff_w_qkv[0]
    wt_diff = jnp.concatenate([w_qkv[:, :qw] * Q_SCALE, w_qkv[:, 2 * qw:]], axis=1).T.astype(BF16)
    k, qt, vt = _proj(hf, w_qkv[:, qw:2 * qw].astype(BF16), wt_diff, (qw,), (BF16,),
                      (qw, DIFF_HEADS * LANES), nb, "diff_qkv_proj")
    lambda_init = 0.8 - 0.6 * math.exp(-0.3 * 1)
    lam_rows = jnp.concatenate(
        [_pad_lanes(diff_lambda_q1[0]), _pad_lanes(diff_lambda_k1[0]),
         _pad_lanes(diff_lambda_q2[0]), _pad_lanes(diff_lambda_k2[0]),
         jnp.zeros((4, LANES), F32)], axis=0)
    o = _diff_attention(qt, k.reshape(nb, LP, qw), vt, _relative_bias_rows(rel_bias_table),
                        lam_rows, diff_subln_w[0][None, :], lambda_init)
    hf = _outproj_ln(hf, [o.reshape(nb * LP, DIFF_HEADS * LANES)], [diff_w_o[0].astype(BF16)],
                     *ln_params(1, 1), name="diff_out_proj_ln")
    return _ffn_ln_final(hf.reshape(nb, LP, D_MODEL), *ffn2, 1, *ln_params(1, 2))
```

```python
import functools
import math

import numpy as np
import jax
import jax.numpy as jnp
from jax import lax
from jax.experimental import pallas as pl
from jax.experimental.pallas import tpu as pltpu

F32 = jnp.float32
BF16 = jnp.bfloat16

D_MODEL = 1024
SEQ = 8192
DEPTH = 2
N_META = 16
CHUNK = 128
SSD_D_INNER = 2048
SSD_HEAD_DIM = 64
SSD_HEADS = 32
SSD_GROUPS = 4
SSD_GROUP_W = SSD_D_INNER // SSD_GROUPS
SSD_STATE = 128
SSD_CONV = 4
SSD_CONV_CH = SSD_D_INNER + 2 * SSD_GROUPS * SSD_STATE
FOX_HEADS = 16
FOX_WIDTH = 1024
HEAD_DIM = 64
DIFF_HEADS = 8
N_BUCKETS = 32
D_FF = 2816
ALPHA = (2 * DEPTH) ** 0.25
LN_EPS = 1e-5
RMS_EPS = 1e-5
NEG = -1e30

LANES = 128
LP = 8448
PADL = LP - SEQ - N_META
ROW_TILE = 512
OUT_TILE = 256
ATT_TQ = 768
ATT_TK = 768
CONV_HALO = 8
SMALL_W = LANES
VMEM_LIMIT = 56 * 1024 * 1024

assert PADL % CHUNK == CHUNK - N_META
assert LP % ATT_TQ == 0 and ATT_TQ % ATT_TK == 0 and ATT_TK % LANES == 0
assert LP % CHUNK == 0 and (2 * LP) % ROW_TILE == 0
assert PADL <= OUT_TILE and (LP - OUT_TILE) == SEQ


def _const_spec(shape):
    nd = len(shape)
    return pl.BlockSpec(shape, lambda *_: (0,) * nd, pipeline_mode=pl.Buffered(1))


def _layer_spec(shape, layer):
    nd = len(shape)
    return pl.BlockSpec((None,) + tuple(shape), lambda *_: (layer,) + (0,) * nd,
                        pipeline_mode=pl.Buffered(1))


def _ffn_weight_specs(layer):
    return [_layer_spec((D_MODEL, D_FF), layer), _layer_spec((D_MODEL, D_FF), layer),
            _layer_spec((D_FF, D_MODEL), layer)]


def _layer_norm(r, g, b):
    mu = jnp.mean(r, axis=-1, keepdims=True)
    d = r - mu
    var = jnp.mean(d * d, axis=-1, keepdims=True)
    return d * lax.rsqrt(var + LN_EPS) * g + b


def _silu(x):
    return x / (1.0 + jnp.exp(-x))


def _softplus(x):
    return jnp.maximum(x, 0.0) + jnp.log(1.0 + jnp.exp(-jnp.abs(x)))


def _ffn_ln_body(h_ref, wg_ref, wu_ref, wd_ref, g_ref, b_ref, o_ref):
    h = h_ref[...]
    hb = h.astype(BF16)
    g = jnp.dot(hb, wg_ref[...], preferred_element_type=F32)
    u = jnp.dot(hb, wu_ref[...], preferred_element_type=F32)
    a = (_silu(g) * u).astype(BF16)
    y = jnp.dot(a, wd_ref[...], preferred_element_type=F32)
    o_ref[...] = _layer_norm(ALPHA * h + 0.5 * y, g_ref[...], b_ref[...])


def _ffn_ln(hf, wg, wu, wd, layer, g, b):
    rows = hf.shape[0]
    return pl.pallas_call(
        _ffn_ln_body,
        grid=(rows // ROW_TILE,),
        in_specs=[
            pl.BlockSpec((ROW_TILE, D_MODEL), lambda i: (i, 0)),
            *_ffn_weight_specs(layer),
            _const_spec((1, D_MODEL)),
            _const_spec((1, D_MODEL)),
        ],
        out_specs=pl.BlockSpec((ROW_TILE, D_MODEL), lambda i: (i, 0)),
        out_shape=jax.ShapeDtypeStruct((rows, D_MODEL), F32),
        compiler_params=pltpu.CompilerParams(
            dimension_semantics=("parallel",), vmem_limit_bytes=VMEM_LIMIT),
        name="ffn_ln",
    )(hf, wg, wu, wd, g, b)


def _ffn_ln_first_body(lead_ref, x_ref, wg_ref, wu_ref, wd_ref, g_ref, b_ref, o_ref, h_ref):
    h_ref[...] = jnp.where(pl.program_id(1) == 0, lead_ref[...], x_ref[...])
    _ffn_ln_body(h_ref, wg_ref, wu_ref, wd_ref, g_ref, b_ref, o_ref)


def _ffn_ln_first(x, lead, wg, wu, wd, layer, g, b):
    nb = x.shape[0]
    tiles = LP // OUT_TILE
    return pl.pallas_call(
        _ffn_ln_first_body,
        grid=(nb, tiles),
        in_specs=[
            _const_spec((OUT_TILE, D_MODEL)),
            pl.BlockSpec((None, OUT_TILE, D_MODEL), lambda bi, i: (bi, jnp.maximum(i - 1, 0), 0)),
            *_ffn_weight_specs(layer),
            _const_spec((1, D_MODEL)),
            _const_spec((1, D_MODEL)),
        ],
        out_specs=pl.BlockSpec((OUT_TILE, D_MODEL), lambda bi, i: (bi * tiles + i, 0)),
        out_shape=jax.ShapeDtypeStruct((nb * LP, D_MODEL), F32),
        scratch_shapes=[pltpu.VMEM((OUT_TILE, D_MODEL), F32)],
        compiler_params=pltpu.CompilerParams(
            dimension_semantics=("parallel", "arbitrary"), vmem_limit_bytes=VMEM_LIMIT),
        name="ffn_ln_first",
    )(lead, x, wg, wu, wd, g, b)


def _ffn_ln_final_body(h_ref, wg_ref, wu_ref, wd_ref, g_ref, b_ref, o_ref):
    @pl.when(pl.program_id(1) > 0)
    def _():
        _ffn_ln_body(h_ref, wg_ref, wu_ref, wd_ref, g_ref, b_ref, o_ref)


def _ffn_ln_final(h3, wg, wu, wd, layer, g, b):
    nb = h3.shape[0]
    return pl.pallas_call(
        _ffn_ln_final_body,
        grid=(nb, LP // OUT_TILE),
        in_specs=[
            pl.BlockSpec((None, OUT_TILE, D_MODEL), lambda bi, i: (bi, i, 0)),
            *_ffn_weight_specs(layer),
            _const_spec((1, D_MODEL)),
            _const_spec((1, D_MODEL)),
        ],
        out_specs=pl.BlockSpec((None, OUT_TILE, D_MODEL),
                               lambda bi, i: (bi, jnp.maximum(i - 1, 0), 0)),
        out_shape=jax.ShapeDtypeStruct((nb, SEQ, D_MODEL), F32),
        compiler_params=pltpu.CompilerParams(
            dimension_semantics=("arbitrary", "arbitrary"), vmem_limit_bytes=VMEM_LIMIT),
        name="ffn_ln_final",
    )(h3, wg, wu, wd, g, b)


PROJ_TILE = 256
assert LP % PROJ_TILE == 0


def _proj_body(n_t, h_ref, w_ref, wt_ref, *o_refs):
    hb = h_ref[...].astype(BF16)
    off = 0
    for o_ref in o_refs[:-n_t]:
        n = o_ref.shape[-1]
        o_ref[...] = jnp.dot(hb, w_ref[:, off:off + n],
                             preferred_element_type=F32).astype(o_ref.dtype)
        off += n
    off = 0
    for o_ref in o_refs[-n_t:]:
        n = o_ref.shape[0]
        o_ref[...] = lax.dot_general(wt_ref[off:off + n, :], hb, (((1,), (1,)), ((), ())),
                                     preferred_element_type=F32).astype(o_ref.dtype)
        off += n


def _proj(hf, w, wt, widths, dtypes, t_widths, nb, name):
    rows = hf.shape[0]
    tiles_per_batch = LP // PROJ_TILE
    out_specs = [pl.BlockSpec((PROJ_TILE, n), lambda i: (i, 0)) for n in widths]
    out_specs += [pl.BlockSpec((None, n, PROJ_TILE),
                               lambda i: (i // tiles_per_batch, 0, i % tiles_per_batch))
                  for n in t_widths]
    out_shape = [jax.ShapeDtypeStruct((rows, n), dt) for n, dt in zip(widths, dtypes)]
    out_shape += [jax.ShapeDtypeStruct((nb, n, LP), BF16) for n in t_widths]
    return pl.pallas_call(
        functools.partial(_proj_body, len(t_widths)),
        grid=(rows // PROJ_TILE,),
        in_specs=[pl.BlockSpec((PROJ_TILE, D_MODEL), lambda i: (i, 0)),
                  _const_spec(w.shape), _const_spec(wt.shape)],
        out_specs=out_specs,
        out_shape=out_shape,
        compiler_params=pltpu.CompilerParams(
            dimension_semantics=("parallel",), vmem_limit_bytes=VMEM_LIMIT),
        name=name,
    )(hf, w, wt)


def _outproj_ln_body(n_in, h_ref, *refs):
    a_refs = refs[:n_in]
    w_refs = refs[n_in:2 * n_in]
    g_ref, b_ref, o_ref = refs[2 * n_in:]
    m = jnp.dot(a_refs[0][...], w_refs[0][...], preferred_element_type=F32)
    for a_ref, w_ref in zip(a_refs[1:], w_refs[1:]):
        m = m + jnp.dot(a_ref[...], w_ref[...], preferred_element_type=F32)
    o_ref[...] = _layer_norm(ALPHA * h_ref[...] + m, g_ref[...], b_ref[...])


def _outproj_ln(hf, acts, ws, g, b, name):
    rows = hf.shape[0]
    n_in = len(acts)
    in_specs = [pl.BlockSpec((ROW_TILE, D_MODEL), lambda i: (i, 0))]
    in_specs += [pl.BlockSpec((ROW_TILE, a.shape[1]), lambda i: (i, 0)) for a in acts]
    in_specs += [_const_spec(w.shape) for w in ws]
    in_specs += [_const_spec((1, D_MODEL)), _const_spec((1, D_MODEL))]
    return pl.pallas_call(
        functools.partial(_outproj_ln_body, n_in),
        grid=(rows // ROW_TILE,),
        in_specs=in_specs,
        out_specs=pl.BlockSpec((ROW_TILE, D_MODEL), lambda i: (i, 0)),
        out_shape=jax.ShapeDtypeStruct((rows, D_MODEL), F32),
        compiler_params=pltpu.CompilerParams(
            dimension_semantics=("parallel",), vmem_limit_bytes=VMEM_LIMIT),
        name=name,
    )(hf, *acts, *ws, g, b)


def _split_dot(x, e2_ref):
    hi = x.astype(BF16)
    lo = (x - hi.astype(F32)).astype(BF16)
    return jnp.dot(jnp.concatenate([hi, lo], axis=1), e2_ref[...], preferred_element_type=F32)


def _ssd_body(xbc_ref, z_ref, small_ref, convw_ref, convb_ref, sbias_ref, alog_ref,
              dskip_ref, normw_ref, expand_ref, y_ref, c_ref,
              ext_ref, state_ref, carry_ref):
    c = pl.program_id(1)

    @pl.when(c == 0)
    def _():
        ext_ref[0:CONV_HALO, :] = jnp.zeros((CONV_HALO, SSD_CONV_CH), F32)
        state_ref[...] = jnp.zeros_like(state_ref)
        carry_ref[...] = jnp.zeros_like(carry_ref)

    row = lax.broadcasted_iota(jnp.int32, (CHUNK, 1), 0)
    valid = (c * CHUNK + row) >= PADL

    ext_ref[CONV_HALO:, :] = xbc_ref[...]

    @pl.when(c * CHUNK < PADL)
    def _():
        ext_ref[CONV_HALO:, :] = jnp.where(valid, ext_ref[CONV_HALO:, :], 0.0)

    ext = ext_ref[...]
    conv = convb_ref[...] + convw_ref[SSD_CONV - 1:SSD_CONV, :] * ext[CONV_HALO:, :]
    for k in range(SSD_CONV - 1):
        shifted = pltpu.roll(ext, SSD_CONV - 1 - k, 0)[CONV_HALO:, :]
        conv = conv + convw_ref[k:k + 1, :] * shifted
    ext_ref[0:CONV_HALO, :] = ext_ref[CHUNK:CHUNK + CONV_HALO, :]
    xc = _silu(conv)
    xs = xc[:, :SSD_D_INNER]
    bm = xc[:, SSD_D_INNER:SSD_D_INNER + SSD_GROUPS * SSD_STATE].astype(BF16)
    cm = xc[:, SSD_D_INNER + SSD_GROUPS * SSD_STATE:].astype(BF16)

    lane = lax.broadcasted_iota(jnp.int32, (CHUNK, SMALL_W), 1)
    is_dt = lane < SSD_HEADS
    is_f = (lane >= SSD_HEADS) & (lane < SSD_HEADS + FOX_HEADS)
    v = small_ref[...] + sbias_ref[...]
    dt = jnp.where(valid & is_dt, _softplus(v), 0.0)
    log_f = jnp.where(valid & is_f, -_softplus(-v), 0.0)
    neg_a = -jnp.exp(alog_ref[...])
    steps = jnp.where(is_dt, dt * neg_a, log_f)
    r_i = lax.broadcasted_iota(jnp.int32, (CHUNK, CHUNK), 0)
    c_i = lax.broadcasted_iota(jnp.int32, (CHUNK, CHUNK), 1)
    causal = r_i >= c_i
    tril = jnp.where(causal, 1.0, 0.0).astype(F32)
    cum = jnp.dot(tril, steps, preferred_element_type=F32,
                  precision=lax.Precision.HIGHEST)
    c_total = jnp.where(is_f, cum + carry_ref[...], 0.0)
    c_ref[...] = c_total
    carry_ref[...] = c_total[CHUNK - 1:CHUNK, :]

    a_last = cum[CHUNK - 1:CHUNK, :]
    dt_x = _split_dot(dt, expand_ref)
    ea_x = _split_dot(jnp.exp(cum), expand_ref)
    de_x = _split_dot(jnp.exp(a_last - cum), expand_ref)
    x_dt = xs * dt_x
    xb = x_dt.astype(BF16)
    xe = (x_dt * de_x).astype(BF16)
    cum_t = cum.T
    lane_p = lax.broadcasted_iota(jnp.int32, (CHUNK, 2 * SSD_HEAD_DIM), 1)
    first_half = lane_p < SSD_HEAD_DIM

    y_groups = []
    for g in range(SSD_GROUPS):
        gs = slice(g * SSD_GROUP_W, (g + 1) * SSD_GROUP_W)
        bg = bm[:, g * SSD_STATE:(g + 1) * SSD_STATE]
        cg = cm[:, g * SSD_STATE:(g + 1) * SSD_STATE]
        cb = lax.dot_general(cg, bg, (((1,), (1,)), ((), ())),
                             preferred_element_type=F32)
        pair_out = []
        for pr in range(SSD_GROUP_W // (2 * SSD_HEAD_DIM)):
            col0 = g * SSD_GROUP_W + pr * 2 * SSD_HEAD_DIM
            x_pair = xb[:, col0:col0 + 2 * SSD_HEAD_DIM]
            ys = []
            for j in range(2):
                hd = col0 // SSD_HEAD_DIM + j
                diff = cum[:, hd:hd + 1] - cum_t[hd:hd + 1, :]
                decay = jnp.exp(jnp.where(causal, diff, -jnp.inf))
                mat = (cb * decay).astype(BF16)
                ys.append(jnp.dot(mat, x_pair, preferred_element_type=F32))
            pair_out.append(jnp.where(first_half, ys[0], ys[1]))
        y_diag = jnp.concatenate(pair_out, axis=1)
        st = state_ref[g]
        y_off = jnp.dot(cg, st.astype(BF16), preferred_element_type=F32) * ea_x[:, gs]
        new = lax.dot_general(bg, xe[:, gs], (((0,), (0,)), ((), ())),
                              preferred_element_type=F32)
        state_ref[g] = st * ea_x[CHUNK - 1:CHUNK, gs] + new
        yg = y_diag + y_off + dskip_ref[:, gs] * xs[:, gs]
        yg = yg * _silu(z_ref[:, gs])
        ms = jnp.mean(yg * yg, axis=-1, keepdims=True)
        y_groups.append(yg * lax.rsqrt(ms + RMS_EPS) * normw_ref[:, gs])
    y_ref[...] = jnp.concatenate(y_groups, axis=1).astype(y_ref.dtype)


def _ssd(xbc, z, small, convw, convb, sbias, alog, dskip, normw, expand):
    nb = xbc.shape[0]
    row_spec = lambda w: pl.BlockSpec((None, CHUNK, w), lambda bi, ci: (bi, ci, 0))
    return pl.pallas_call(
        _ssd_body,
        grid=(nb, LP // CHUNK),
        in_specs=[row_spec(SSD_CONV_CH), row_spec(SSD_D_INNER), row_spec(SMALL_W),
                  _const_spec(convw.shape), _const_spec(convb.shape),
                  _const_spec(sbias.shape), _const_spec(alog.shape),
                  _const_spec(dskip.shape), _const_spec(normw.shape),
                  _const_spec(expand.shape)],
        out_specs=[row_spec(SSD_D_INNER), row_spec(SMALL_W)],
        out_shape=[jax.ShapeDtypeStruct((nb, LP, SSD_D_INNER), BF16),
                   jax.ShapeDtypeStruct((nb, LP, SMALL_W), F32)],
        scratch_shapes=[pltpu.VMEM((CHUNK + CONV_HALO, SSD_CONV_CH), F32),
                        pltpu.VMEM((SSD_GROUPS, SSD_STATE, SSD_GROUP_W), F32),
                        pltpu.VMEM((1, SMALL_W), F32)],
        compiler_params=pltpu.CompilerParams(
            dimension_semantics=("arbitrary", "arbitrary"), vmem_limit_bytes=VMEM_LIMIT),
        name="ssd",
    )(xbc, z, small, convw, convb, sbias, alog, dskip, normw, expand)


N_TK_CHUNKS = ATT_TK // LANES
N_TQ_CHUNKS = ATT_TQ // LANES
ONES_ROWS = 16
assert ATT_TQ == ATT_TK and PADL <= ATT_TK
LOG2E = math.log2(math.e)
Q_SCALE = LOG2E * HEAD_DIM ** -0.5


def _init_softmax_state(m_ref, acc_ref):
    m_ref[...] = jnp.full(m_ref.shape, NEG, F32)
    acc_ref[...] = jnp.zeros_like(acc_ref)


def _run_key_tiles(qb, first, scores, step, buf_a, buf_b):
    n_below = qb - first
    odd = n_below % 2

    @pl.when(odd == 0)
    def _():
        scores(first, buf_a)

    @pl.when(odd == 1)
    def _():
        scores(first, buf_b)
        step(first, buf_b, buf_a)

    def two_below_diagonal(i, carry):
        kb = first + odd + 2 * i
        step(kb, buf_a, buf_b)
        step(kb + 1, buf_b, buf_a)
        return carry

    lax.fori_loop(0, n_below // 2, two_below_diagonal, 0)
    step(qb, buf_a, None)


def _query_operands(qt_ref):
    zero_rows = jnp.zeros((HEAD_DIM, ATT_TQ), BF16)
    return [jnp.concatenate([qt_ref[:HEAD_DIM, :], zero_rows], axis=0),
            jnp.concatenate([zero_rows, qt_ref[HEAD_DIM:, :]], axis=0)]


def _value_rows(vt_ref, row0, n_rows, start):
    vt = vt_ref[row0:row0 + n_rows, pl.ds(start, ATT_TK)]
    return jnp.concatenate([vt, jnp.ones((ONES_ROWS, ATT_TK), BF16)], axis=0)


def _softmax_update(s_t, col_max, vt_aug, m_ref, acc_ref, idx):
    m_old = m_ref[idx]
    m_new = jnp.maximum(m_old, col_max)
    alpha = jnp.exp2(m_old - m_new)
    p_t = jnp.exp2(s_t - m_new).astype(BF16)
    pv = jnp.dot(vt_aug, p_t, preferred_element_type=F32)
    acc_ref[idx] = alpha * acc_ref[idx] + pv
    m_ref[idx] = m_new


def _causal_mask():
    kpos = lax.broadcasted_iota(jnp.int32, (ATT_TK, ATT_TQ), 0)
    qpos = lax.broadcasted_iota(jnp.int32, (ATT_TK, ATT_TQ), 1)
    return kpos <= qpos


def _fox_body(first_ref, qt_ref, k_ref, vt_ref, c_ref, o_ref, m_ref, acc_ref,
              sa_ref, pa_ref, sb_ref, pb_ref):
    qb = pl.program_id(2)
    n_q = pl.num_programs(2)
    first = first_ref[(pl.program_id(0) * pl.num_programs(1) + pl.program_id(1)) * n_q + qb]
    first = jnp.minimum(first, qb)
    _init_softmax_state(m_ref, acc_ref)
    buf_a, buf_b = (sa_ref, pa_ref), (sb_ref, pb_ref)
    q_heads = _query_operands(qt_ref)
    lane = lax.broadcasted_iota(jnp.int32, (ATT_TK, LANES), 1)
    first_lane = SSD_HEADS + 2 * pl.program_id(1)

    def scores(kb, buf):
        s_dst, max_dst = buf
        start = pl.multiple_of(kb * ATT_TK, ATT_TK)
        k_rows = k_ref[pl.ds(start, ATT_TK), :]
        c_rows = c_ref[pl.ds(start, ATT_TK), :]
        for j in range(2):
            ck = jnp.sum(jnp.where(lane == first_lane + j, c_rows, 0.0), axis=1, keepdims=True)
            s_t = jnp.dot(k_rows, q_heads[j], preferred_element_type=F32) - ck
            s_dst[j] = s_t
            max_dst[j] = jnp.max(s_t, axis=0, keepdims=True)

    def step(kb, src, nxt):
        s_src, max_src = src
        start = pl.multiple_of(kb * ATT_TK, ATT_TK)
        if nxt is not None:
            scores(kb + 1, nxt)
        for j in range(2):
            s_t = s_src[j]
            col_max = max_src[j]
            if nxt is None:
                s_t = jnp.where(_causal_mask(), s_t, NEG)
                col_max = jnp.max(s_t, axis=0, keepdims=True)
            vt_aug = _value_rows(vt_ref, j * HEAD_DIM, HEAD_DIM, start)
            _softmax_update(s_t, col_max, vt_aug, m_ref, acc_ref, j)

    _run_key_tiles(qb, first, scores, step, buf_a, buf_b)

    o_t = jnp.concatenate(
        [acc_ref[j, :HEAD_DIM, :] / acc_ref[j, HEAD_DIM:HEAD_DIM + 1, :] for j in range(2)], axis=0)
    o_ref[...] = o_t.T.astype(o_ref.dtype)


def _fox_attention(first, qt, k, vt, c):
    nb = qt.shape[0]
    pairs = FOX_HEADS // 2
    acc_rows = HEAD_DIM + ONES_ROWS
    return pl.pallas_call(
        _fox_body,
        grid_spec=pltpu.PrefetchScalarGridSpec(
            num_scalar_prefetch=1,
            grid=(nb, pairs, LP // ATT_TQ),
            in_specs=[
                pl.BlockSpec((None, LANES, ATT_TQ), lambda bi, hp, qi, _: (bi, hp, qi)),
                pl.BlockSpec((None, LP, LANES), lambda bi, hp, qi, _: (bi, 0, hp)),
                pl.BlockSpec((None, LANES, LP), lambda bi, hp, qi, _: (bi, hp, 0)),
                pl.BlockSpec((None, LP, SMALL_W), lambda bi, hp, qi, _: (bi, 0, 0)),
            ],
            out_specs=pl.BlockSpec((None, ATT_TQ, LANES), lambda bi, hp, qi, _: (bi, qi, hp)),
            scratch_shapes=[pltpu.VMEM((2, 1, ATT_TQ), F32),
                            pltpu.VMEM((2, acc_rows, ATT_TQ), F32)]
                           + [pltpu.VMEM((2, ATT_TK, ATT_TQ), F32),
                              pltpu.VMEM((2, 1, ATT_TQ), F32)] * 2),
        out_shape=jax.ShapeDtypeStruct((nb, LP, FOX_WIDTH), BF16),
        compiler_params=pltpu.CompilerParams(
            dimension_semantics=("parallel", "parallel", "arbitrary"),
            vmem_limit_bytes=VMEM_LIMIT),
        name="fox_attention",
    )(first, qt, k, vt, c)


FORGOTTEN_LOG2 = 160.0
NORM_SLACK = 1.01


def _first_key_tiles(qt, k, c2):
    nb = qt.shape[0]
    n_q = LP // ATT_TQ
    heads = jnp.asarray(np.arange(FOX_WIDTH)[:, None] // HEAD_DIM == np.arange(FOX_HEADS)[None, :], F32)
    q_sq = jnp.einsum("bdl,dh->bhl", (qt * qt).astype(F32), heads)
    k_sq = jnp.einsum("bld,dh->blh", (k * k).astype(F32), heads)
    q_max = NORM_SLACK * jnp.sqrt(jnp.max(q_sq.reshape(nb, FOX_HEADS, n_q, ATT_TQ), axis=-1))
    k_max = NORM_SLACK * jnp.sqrt(jnp.max(k_sq, axis=1))
    c_heads = c2[:, :, SSD_HEADS:SSD_HEADS + FOX_HEADS]
    c_query = c_heads[:, ::ATT_TQ, :].transpose(0, 2, 1)
    c_key = c_heads[:, ATT_TK - 1::ATT_TK, :].transpose(0, 2, 1)
    bound = (2.0 * q_max * k_max[:, :, None])[:, :, :, None] - (c_key[:, :, None, :] - c_query[:, :, :, None])
    needed = bound >= -FORGOTTEN_LOG2
    needed = needed.reshape(nb, FOX_HEADS // 2, 2, n_q, n_q).any(axis=2)
    needed = needed | (jnp.arange(n_q)[None, :] >= jnp.arange(n_q)[:, None])
    return jnp.argmax(needed, axis=-1).astype(jnp.int32).reshape(-1)


def _bias_tiles(bias_ref):
    rows = jnp.broadcast_to(bias_ref[...], (CHUNK, 2 * CHUNK))
    rolled = pltpu.roll(rows, 0, 1, stride=1, stride_axis=0)
    return rolled[:, :CHUNK], rolled[:, CHUNK:]


def _diagonal_bias(tiles):
    zeros = jnp.zeros((CHUNK, CHUNK), F32)
    rows = []
    for a in range(N_TK_CHUNKS):
        rows.append(jnp.concatenate(
            [tiles[b - a] if b - a in (0, 1) else zeros for b in range(N_TQ_CHUNKS)], axis=1))
    return jnp.concatenate(rows, axis=0)


def _diff_body(lambda_init, qt_ref, k_ref, vt_ref, bias_ref, lam_ref, subln_ref, o_ref,
               m_ref, acc_ref, tiles_ref, sa_ref, pa_ref, sb_ref, pb_ref):
    qb = pl.program_id(2)
    _init_softmax_state(m_ref, acc_ref)
    buf_a, buf_b = (sa_ref, pa_ref), (sb_ref, pb_ref)
    tile0, tile1 = _bias_tiles(bias_ref)
    tiles_ref[0] = tile0
    tiles_ref[1] = tile1
    q_parts = _query_operands(qt_ref)
    pad_rows = -(-PADL // CHUNK) * CHUNK
    corner0 = ATT_TK - CHUNK

    def scores(kb, buf):
        s_dst, max_dst = buf
        start = pl.multiple_of(kb * ATT_TK, ATT_TK)
        k_rows = k_ref[pl.ds(start, ATT_TK), :]
        kpos = lax.broadcasted_iota(jnp.int32, (pad_rows, ATT_TQ), 0)
        not_padding = (kpos >= PADL) | (kb > 0)
        for j in range(2):
            s_t = jnp.dot(k_rows, q_parts[j], preferred_element_type=F32)
            s_t = jnp.concatenate(
                [jnp.where(not_padding, s_t[:pad_rows], NEG), s_t[pad_rows:]], axis=0)
            s_dst[j] = s_t
            max_dst[j] = jnp.max(s_t, axis=0, keepdims=True)

    def step(kb, src, nxt):
        s_src, max_src = src
        start = pl.multiple_of(kb * ATT_TK, ATT_TK)
        diagonal = nxt is None
        if diagonal:
            bias = _diagonal_bias((tiles_ref[0], tiles_ref[1]))
            visible = _causal_mask()
        else:
            scores(kb + 1, nxt)
            corner = jnp.where(kb == qb - 1, 1.0, 0.0) * tiles_ref[1]
        vt_aug = _value_rows(vt_ref, 0, LANES, start)
        for j in range(2):
            s_t = s_src[j]
            if diagonal:
                s_t = jnp.where(visible, s_t + bias, NEG)
                col_max = jnp.max(s_t, axis=0, keepdims=True)
            else:
                near = s_t[corner0:, :CHUNK] + corner
                s_t = jnp.concatenate(
                    [s_t[:corner0],
                     jnp.concatenate([near, s_t[corner0:, CHUNK:]], axis=1)], axis=0)
                first = jnp.maximum(jnp.max(s_t[:corner0, :CHUNK], axis=0, keepdims=True),
                                    jnp.max(near, axis=0, keepdims=True))
                col_max = jnp.concatenate([first, max_src[j][:, CHUNK:]], axis=1)
            _softmax_update(s_t, col_max, vt_aug, m_ref, acc_ref, j)

    _run_key_tiles(qb, 0, scores, step, buf_a, buf_b)

    lam1 = jnp.exp(jnp.sum(lam_ref[0:1, :] * lam_ref[1:2, :], axis=-1, keepdims=True))
    lam2 = jnp.exp(jnp.sum(lam_ref[2:3, :] * lam_ref[3:4, :], axis=-1, keepdims=True))
    lam = lam1 - lam2 + lambda_init
    o_t = (acc_ref[0, :LANES, :] / acc_ref[0, LANES:LANES + 1, :]
           - lam * (acc_ref[1, :LANES, :] / acc_ref[1, LANES:LANES + 1, :]))
    o = o_t.T
    ms = jnp.mean(o * o, axis=-1, keepdims=True)
    o = o * lax.rsqrt(ms + RMS_EPS) * subln_ref[...] * (1.0 - lambda_init)
    o_ref[...] = o.astype(o_ref.dtype)


def _diff_attention(qt, k, vt, bias_rows, lam_rows, subln, lambda_init):
    nb = qt.shape[0]
    acc_rows = LANES + ONES_ROWS
    return pl.pallas_call(
        functools.partial(_diff_body, lambda_init),
        grid=(nb, DIFF_HEADS, LP // ATT_TQ),
        in_specs=[
            pl.BlockSpec((None, LANES, ATT_TQ), lambda bi, hd, qi: (bi, hd, qi)),
            pl.BlockSpec((None, LP, LANES), lambda bi, hd, qi: (bi, 0, hd)),
            pl.BlockSpec((None, LANES, LP), lambda bi, hd, qi: (bi, hd, 0)),
            pl.BlockSpec((None, 1, 2 * CHUNK), lambda bi, hd, qi: (hd, 0, 0)),
            _const_spec(lam_rows.shape),
            _const_spec(subln.shape),
        ],
        out_specs=pl.BlockSpec((None, ATT_TQ, LANES), lambda bi, hd, qi: (bi, qi, hd)),
        out_shape=jax.ShapeDtypeStruct((nb, LP, DIFF_HEADS * LANES), BF16),
        scratch_shapes=[pltpu.VMEM((2, 1, ATT_TQ), F32),
                        pltpu.VMEM((2, acc_rows, ATT_TQ), F32),
                        pltpu.VMEM((2, CHUNK, CHUNK), F32)]
                       + [pltpu.VMEM((2, ATT_TK, ATT_TQ), F32),
                          pltpu.VMEM((2, 1, ATT_TQ), F32)] * 2,
        compiler_params=pltpu.CompilerParams(
            dimension_semantics=("parallel", "parallel", "arbitrary"),
            vmem_limit_bytes=VMEM_LIMIT),
        name="diff_attention",
    )(qt, k, vt, bias_rows, lam_rows, subln)


def _t5_bucket(n):
    max_exact = N_BUCKETS // 2
    nf = jnp.maximum(n, 1).astype(F32)
    large = max_exact + (jnp.log(nf / max_exact) / math.log(128 / max_exact)
                         * (N_BUCKETS - max_exact)).astype(jnp.int32)
    large = jnp.minimum(large, N_BUCKETS - 1)
    return jnp.where(n < max_exact, n, large)


def _relative_bias_rows(rel_table):
    dist = jnp.arange(2 * CHUNK)
    by_dist = LOG2E * (rel_table[_t5_bucket(dist)] - rel_table[N_BUCKETS - 1])
    return by_dist.T[:, None, :].astype(F32)


def _pad_lanes(vec, width=LANES):
    return jnp.pad(vec, (0, width - vec.shape[0]))[None, :].astype(F32)


def kernel(x, meta_tokens, ln_gain, ln_bias, ffn1_w_gate, ffn1_w_up, ffn1_w_down, ffn2_w_gate, ffn2_w_up, ffn2_w_down, even_w_in, even_conv_w, even_conv_b, ssd_dt_bias, ssd_a_log, ssd_d_skip, ssd_norm_w, fox_f_bias, even_w_out, diff_w_qkv, diff_lambda_q1, diff_lambda_k1, diff_lambda_q2, diff_lambda_k2, diff_subln_w, diff_w_o, rel_bias_table):
    nb = x.shape[0]
    lead = jnp.concatenate([jnp.zeros((PADL, D_MODEL), x.dtype), meta_tokens.astype(x.dtype)], axis=0)

    def ln_params(l, i):
        return ln_gain[l, i][None, :], ln_bias[l, i][None, :]

    ffn1 = tuple(w.astype(BF16) for w in (ffn1_w_gate, ffn1_w_up, ffn1_w_down))
    ffn2 = tuple(w.astype(BF16) for w in (ffn2_w_gate, ffn2_w_up, ffn2_w_down))

    hf = _ffn_ln_first(x, lead, *ffn1, 0, *ln_params(0, 0))

    w_in = even_w_in[0]
    o_z, o_xbc = 0, SSD_D_INNER
    o_dt = o_xbc + SSD_CONV_CH
    o_q = o_dt + SSD_HEADS
    o_k, o_v = o_q + FOX_WIDTH, o_q + 2 * FOX_WIDTH
    o_f = o_q + 3 * FOX_WIDTH
    w_small = jnp.concatenate(
        [w_in[:, o_dt:o_q], w_in[:, o_f:],
         jnp.zeros((D_MODEL, SMALL_W - SSD_HEADS - FOX_HEADS), w_in.dtype)], axis=1)
    w_even = jnp.concatenate([w_in[:, o_z:o_dt], w_in[:, o_k:o_v], w_small], axis=1).astype(BF16)
    wt_even = jnp.concatenate([w_in[:, o_q:o_k] * Q_SCALE, w_in[:, o_v:o_f]], axis=1).T.astype(BF16)
    z, xbc, k, small, qt, vt = _proj(
        hf, w_even, wt_even, (SSD_D_INNER, SSD_CONV_CH, FOX_WIDTH, SMALL_W),
        (F32, F32, BF16, F32), (FOX_WIDTH, FOX_WIDTH), nb, "even_in_proj")

    sbias = _pad_lanes(jnp.concatenate([ssd_dt_bias[0], fox_f_bias[0]]))
    alog = _pad_lanes(ssd_a_log[0])
    dskip = jnp.repeat(ssd_d_skip[0], SSD_HEAD_DIM)[None, :].astype(F32)
    expand = np.zeros((SMALL_W, SSD_D_INNER), np.float32)
    expand[np.arange(SSD_D_INNER) // SSD_HEAD_DIM, np.arange(SSD_D_INNER)] = 1.0
    y, cfull = _ssd(xbc.reshape(nb, LP, SSD_CONV_CH), z.reshape(nb, LP, SSD_D_INNER),
                    small.reshape(nb, LP, SMALL_W), even_conv_w[0], even_conv_b[0][None, :],
                    sbias, alog, dskip, ssd_norm_w[0][None, :],
                    jnp.asarray(np.concatenate([expand, expand], axis=0), BF16))
    c2 = LOG2E * cfull
    k = k.reshape(nb, LP, FOX_WIDTH)
    ck = jnp.where(jnp.arange(LP)[None, :, None] < PADL, -NEG, c2)
    o = _fox_attention(_first_key_tiles(qt, k, c2), qt, k, vt, ck)
    w_out = even_w_out[0].astype(BF16)
    hf = _outproj_ln(hf, [y.reshape(nb * LP, SSD_D_INNER), o.reshape(nb * LP, FOX_WIDTH)],
                     [w_out[:SSD_D_INNER], w_out[SSD_D_INNER:]], *ln_params(0, 1),
                     name="even_out_proj_ln")
    hf = _ffn_ln(hf, *ffn2, 0, *ln_params(0, 2))

    hf = _ffn_ln(hf, *ffn1, 1, *ln_params(1, 0))
    qw = DIFF_HEADS * 2 * HEAD_DIM
    w_qkv = diff_w_qkv[0]
    wt_diff = jnp.concatenate([w_qkv[:, :qw] * Q_SCALE, w_qkv[:, 2 * qw:]], axis=1).T.astype(BF16)
    k, qt, vt = _proj(hf, w_qkv[:, qw:2 * qw].astype(BF16), wt_diff, (qw,), (BF16,),
                      (qw, DIFF_HEADS * LANES), nb, "diff_qkv_proj")
    lambda_init = 0.8 - 0.6 * math.exp(-0.3 * 1)
    lam_rows = jnp.concatenate(
        [_pad_lanes(diff_lambda_q1[0]), _pad_lanes(diff_lambda_k1[0]),
         _pad_lanes(diff_lambda_q2[0]), _pad_lanes(diff_lambda_k2[0]),
         jnp.zeros((4, LANES), F32)], axis=0)
    o = _diff_attention(qt, k.reshape(nb, LP, qw), vt, _relative_bias_rows(rel_bias_table),
                        lam_rows, diff_subln_w[0][None, :], lambda_init)
    hf = _outproj_ln(hf, [o.reshape(nb * LP, DIFF_HEADS * LANES)], [diff_w_o[0].astype(BF16)],
                     *ln_params(1, 1), name="diff_out_proj_ln")
    return _ffn_ln_final(hf.reshape(nb, LP, D_MODEL), *ffn2, 1, *ln_params(1, 2))
```

```python
import functools
import math

import numpy as np
import jax
import jax.numpy as jnp
from jax import lax
from jax.experimental import pallas as pl
from jax.experimental.pallas import tpu as pltpu

F32 = jnp.float32
BF16 = jnp.bfloat16

D_MODEL = 1024
SEQ = 8192
DEPTH = 2
N_META = 16
CHUNK = 128
SSD_D_INNER = 2048
SSD_HEAD_DIM = 64
SSD_HEADS = 32
SSD_GROUPS = 4
SSD_GROUP_W = SSD_D_INNER // SSD_GROUPS
SSD_STATE = 128
SSD_CONV = 4
SSD_CONV_CH = SSD_D_INNER + 2 * SSD_GROUPS * SSD_STATE
FOX_HEADS = 16
FOX_WIDTH = 1024
HEAD_DIM = 64
DIFF_HEADS = 8
N_BUCKETS = 32
D_FF = 2816
ALPHA = (2 * DEPTH) ** 0.25
LN_EPS = 1e-5
RMS_EPS = 1e-5
NEG = -1e30

LANES = 128
LP = 8448
PADL = LP - SEQ - N_META
ROW_TILE = 512
OUT_TILE = 256
ATT_TQ = 768
ATT_TK = 768
CONV_HALO = 8
SMALL_W = LANES
VMEM_LIMIT = 56 * 1024 * 1024

assert PADL % CHUNK == CHUNK - N_META
assert LP % ATT_TQ == 0 and ATT_TQ % ATT_TK == 0 and ATT_TK % LANES == 0
assert LP % CHUNK == 0 and (2 * LP) % ROW_TILE == 0
assert PADL <= OUT_TILE and (LP - OUT_TILE) == SEQ


def _const_spec(shape):
    nd = len(shape)
    return pl.BlockSpec(shape, lambda *_: (0,) * nd, pipeline_mode=pl.Buffered(1))


def _layer_spec(shape, layer):
    nd = len(shape)
    return pl.BlockSpec((None,) + tuple(shape), lambda *_: (layer,) + (0,) * nd,
                        pipeline_mode=pl.Buffered(1))


def _ffn_weight_specs(layer):
    return [_layer_spec((D_MODEL, D_FF), layer), _layer_spec((D_MODEL, D_FF), layer),
            _layer_spec((D_FF, D_MODEL), layer)]


def _layer_norm(r, g, b):
    mu = jnp.mean(r, axis=-1, keepdims=True)
    d = r - mu
    var = jnp.mean(d * d, axis=-1, keepdims=True)
    return d * lax.rsqrt(var + LN_EPS) * g + b


def _silu(x):
    return x / (1.0 + jnp.exp(-x))


def _softplus(x):
    return jnp.maximum(x, 0.0) + jnp.log(1.0 + jnp.exp(-jnp.abs(x)))


def _ffn_ln_body(h_ref, wg_ref, wu_ref, wd_ref, g_ref, b_ref, o_ref):
    h = h_ref[...]
    hb = h.astype(BF16)
    g = jnp.dot(hb, wg_ref[...], preferred_element_type=F32)
    u = jnp.dot(hb, wu_ref[...], preferred_element_type=F32)
    a = (_silu(g) * u).astype(BF16)
    y = jnp.dot(a, wd_ref[...], preferred_element_type=F32)
    o_ref[...] = _layer_norm(ALPHA * h + 0.5 * y, g_ref[...], b_ref[...])


def _ffn_ln(hf, wg, wu, wd, layer, g, b):
    rows = hf.shape[0]
    return pl.pallas_call(
        _ffn_ln_body,
        grid=(rows // ROW_TILE,),
        in_specs=[
            pl.BlockSpec((ROW_TILE, D_MODEL), lambda i: (i, 0)),
            *_ffn_weight_specs(layer),
            _const_spec((1, D_MODEL)),
            _const_spec((1, D_MODEL)),
        ],
        out_specs=pl.BlockSpec((ROW_TILE, D_MODEL), lambda i: (i, 0)),
        out_shape=jax.ShapeDtypeStruct((rows, D_MODEL), F32),
        compiler_params=pltpu.CompilerParams(
            dimension_semantics=("parallel",), vmem_limit_bytes=VMEM_LIMIT),
        name="ffn_ln",
    )(hf, wg, wu, wd, g, b)


def _ffn_ln_first_body(lead_ref, x_ref, wg_ref, wu_ref, wd_ref, g_ref, b_ref, o_ref, h_ref):
    h_ref[...] = jnp.where(pl.program_id(1) == 0, lead_ref[...], x_ref[...])
    _ffn_ln_body(h_ref, wg_ref, wu_ref, wd_ref, g_ref, b_ref, o_ref)


def _ffn_ln_first(x, lead, wg, wu, wd, layer, g, b):
    nb = x.shape[0]
    tiles = LP // OUT_TILE
    return pl.pallas_call(
        _ffn_ln_first_body,
        grid=(nb, tiles),
        in_specs=[
            _const_spec((OUT_TILE, D_MODEL)),
            pl.BlockSpec((None, OUT_TILE, D_MODEL), lambda bi, i: (bi, jnp.maximum(i - 1, 0), 0)),
            *_ffn_weight_specs(layer),
            _const_spec((1, D_MODEL)),
            _const_spec((1, D_MODEL)),
        ],
        out_specs=pl.BlockSpec((OUT_TILE, D_MODEL), lambda bi, i: (bi * tiles + i, 0)),
        out_shape=jax.ShapeDtypeStruct((nb * LP, D_MODEL), F32),
        scratch_shapes=[pltpu.VMEM((OUT_TILE, D_MODEL), F32)],
        compiler_params=pltpu.CompilerParams(
            dimension_semantics=("parallel", "arbitrary"), vmem_limit_bytes=VMEM_LIMIT),
        name="ffn_ln_first",
    )(lead, x, wg, wu, wd, g, b)


def _ffn_ln_final_body(h_ref, wg_ref, wu_ref, wd_ref, g_ref, b_ref, o_ref):
    @pl.when(pl.program_id(1) > 0)
    def _():
        _ffn_ln_body(h_ref, wg_ref, wu_ref, wd_ref, g_ref, b_ref, o_ref)


def _ffn_ln_final(h3, wg, wu, wd, layer, g, b):
    nb = h3.shape[0]
    return pl.pallas_call(
        _ffn_ln_final_body,
        grid=(nb, LP // OUT_TILE),
        in_specs=[
            pl.BlockSpec((None, OUT_TILE, D_MODEL), lambda bi, i: (bi, i, 0)),
            *_ffn_weight_specs(layer),
            _const_spec((1, D_MODEL)),
            _const_spec((1, D_MODEL)),
        ],
        out_specs=pl.BlockSpec((None, OUT_TILE, D_MODEL),
                               lambda bi, i: (bi, jnp.maximum(i - 1, 0), 0)),
        out_shape=jax.ShapeDtypeStruct((nb, SEQ, D_MODEL), F32),
        compiler_params=pltpu.CompilerParams(
            dimension_semantics=("arbitrary", "arbitrary"), vmem_limit_bytes=VMEM_LIMIT),
        name="ffn_ln_final",
    )(h3, wg, wu, wd, g, b)


PROJ_TILE = 256
assert LP % PROJ_TILE == 0


def _proj_body(n_t, h_ref, w_ref, wt_ref, *o_refs):
    hb = h_ref[...].astype(BF16)
    off = 0
    for o_ref in o_refs[:-n_t]:
        n = o_ref.shape[-1]
        o_ref[...] = jnp.dot(hb, w_ref[:, off:off + n],
                             preferred_element_type=F32).astype(o_ref.dtype)
        off += n
    off = 0
    for o_ref in o_refs[-n_t:]:
        n = o_ref.shape[0]
        o_ref[...] = lax.dot_general(wt_ref[off:off + n, :], hb, (((1,), (1,)), ((), ())),
                                     preferred_element_type=F32).astype(o_ref.dtype)
        off += n


def _proj(hf, w, wt, widths, dtypes, t_widths, nb, name):
    rows = hf.shape[0]
    tiles_per_batch = LP // PROJ_TILE
    out_specs = [pl.BlockSpec((PROJ_TILE, n), lambda i: (i, 0)) for n in widths]
    out_specs += [pl.BlockSpec((None, n, PROJ_TILE),
                               lambda i: (i // tiles_per_batch, 0, i % tiles_per_batch))
                  for n in t_widths]
    out_shape = [jax.ShapeDtypeStruct((rows, n), dt) for n, dt in zip(widths, dtypes)]
    out_shape += [jax.ShapeDtypeStruct((nb, n, LP), BF16) for n in t_widths]
    return pl.pallas_call(
        functools.partial(_proj_body, len(t_widths)),
        grid=(rows // PROJ_TILE,),
        in_specs=[pl.BlockSpec((PROJ_TILE, D_MODEL), lambda i: (i, 0)),
                  _const_spec(w.shape), _const_spec(wt.shape)],
        out_specs=out_specs,
        out_shape=out_shape,
        compiler_params=pltpu.CompilerParams(
            dimension_semantics=("parallel",), vmem_limit_bytes=VMEM_LIMIT),
        name=name,
    )(hf, w, wt)


def _outproj_ln_body(n_in, h_ref, *refs):
    a_refs = refs[:n_in]
    w_refs = refs[n_in:2 * n_in]
    g_ref, b_ref, o_ref = refs[2 * n_in:]
    m = jnp.dot(a_refs[0][...], w_refs[0][...], preferred_element_type=F32)
    for a_ref, w_ref in zip(a_refs[1:], w_refs[1:]):
        m = m + jnp.dot(a_ref[...], w_ref[...], preferred_element_type=F32)
    o_ref[...] = _layer_norm(ALPHA * h_ref[...] + m, g_ref[...], b_ref[...])


def _outproj_ln(hf, acts, ws, g, b, name):
    rows = hf.shape[0]
    n_in = len(acts)
    in_specs = [pl.BlockSpec((ROW_TILE, D_MODEL), lambda i: (i, 0))]
    in_specs += [pl.BlockSpec((ROW_TILE, a.shape[1]), lambda i: (i, 0)) for a in acts]
    in_specs += [_const_spec(w.shape) for w in ws]
    in_specs += [_const_spec((1, D_MODEL)), _const_spec((1, D_MODEL))]
    return pl.pallas_call(
        functools.partial(_outproj_ln_body, n_in),
        grid=(rows // ROW_TILE,),
        in_specs=in_specs,
        out_specs=pl.BlockSpec((ROW_TILE, D_MODEL), lambda i: (i, 0)),
        out_shape=jax.ShapeDtypeStruct((rows, D_MODEL), F32),
        compiler_params=pltpu.CompilerParams(
            dimension_semantics=("parallel",), vmem_limit_bytes=VMEM_LIMIT),
        name=name,
    )(hf, *acts, *ws, g, b)


def _split_dot(x, e2_ref):
    hi = x.astype(BF16)
    lo = (x - hi.astype(F32)).astype(BF16)
    return jnp.dot(jnp.concatenate([hi, lo], axis=1), e2_ref[...], preferred_element_type=F32)


def _ssd_body(xbc_ref, z_ref, small_ref, convw_ref, convb_ref, sbias_ref, alog_ref,
              dskip_ref, normw_ref, expand_ref, y_ref, c_ref,
              ext_ref, state_ref, carry_ref):
    c = pl.program_id(1)

    @pl.when(c == 0)
    def _():
        ext_ref[0:CONV_HALO, :] = jnp.zeros((CONV_HALO, SSD_CONV_CH), F32)
        state_ref[...] = jnp.zeros_like(state_ref)
        carry_ref[...] = jnp.zeros_like(carry_ref)

    row = lax.broadcasted_iota(jnp.int32, (CHUNK, 1), 0)
    valid = (c * CHUNK + row) >= PADL

    ext_ref[CONV_HALO:, :] = xbc_ref[...]

    @pl.when(c * CHUNK < PADL)
    def _():
        ext_ref[CONV_HALO:, :] = jnp.where(valid, ext_ref[CONV_HALO:, :], 0.0)

    ext = ext_ref[...]
    conv = convb_ref[...] + convw_ref[SSD_CONV - 1:SSD_CONV, :] * ext[CONV_HALO:, :]
    for k in range(SSD_CONV - 1):
        shifted = pltpu.roll(ext, SSD_CONV - 1 - k, 0)[CONV_HALO:, :]
        conv = conv + convw_ref[k:k + 1, :] * shifted
    ext_ref[0:CONV_HALO, :] = ext_ref[CHUNK:CHUNK + CONV_HALO, :]
    xc = _silu(conv)
    xs = xc[:, :SSD_D_INNER]
    bm = xc[:, SSD_D_INNER:SSD_D_INNER + SSD_GROUPS * SSD_STATE].astype(BF16)
    cm = xc[:, SSD_D_INNER + SSD_GROUPS * SSD_STATE:].astype(BF16)

    lane = lax.broadcasted_iota(jnp.int32, (CHUNK, SMALL_W), 1)
    is_dt = lane < SSD_HEADS
    is_f = (lane >= SSD_HEADS) & (lane < SSD_HEADS + FOX_HEADS)
    v = small_ref[...] + sbias_ref[...]
    dt = jnp.where(valid & is_dt, _softplus(v), 0.0)
    log_f = jnp.where(valid & is_f, -_softplus(-v), 0.0)
    neg_a = -jnp.exp(alog_ref[...])
    steps = jnp.where(is_dt, dt * neg_a, log_f)
    r_i = lax.broadcasted_iota(jnp.int32, (CHUNK, CHUNK), 0)
    c_i = lax.broadcasted_iota(jnp.int32, (CHUNK, CHUNK), 1)
    causal = r_i >= c_i
    tril = jnp.where(causal, 1.0, 0.0).astype(F32)
    cum = jnp.dot(tril, steps, preferred_element_type=F32,
                  precision=lax.Precision.HIGHEST)
    c_total = jnp.where(is_f, cum + carry_ref[...], 0.0)
    c_ref[...] = c_total
    carry_ref[...] = c_total[CHUNK - 1:CHUNK, :]

    a_last = cum[CHUNK - 1:CHUNK, :]
    dt_x = _split_dot(dt, expand_ref)
    ea_x = _split_dot(jnp.exp(cum), expand_ref)
    de_x = _split_dot(jnp.exp(a_last - cum), expand_ref)
    x_dt = xs * dt_x
    xb = x_dt.astype(BF16)
    xe = (x_dt * de_x).astype(BF16)
    cum_t = cum.T
    lane_p = lax.broadcasted_iota(jnp.int32, (CHUNK, 2 * SSD_HEAD_DIM), 1)
    first_half = lane_p < SSD_HEAD_DIM

    y_groups = []
    for g in range(SSD_GROUPS):
        gs = slice(g * SSD_GROUP_W, (g + 1) * SSD_GROUP_W)
        bg = bm[:, g * SSD_STATE:(g + 1) * SSD_STATE]
        cg = cm[:, g * SSD_STATE:(g + 1) * SSD_STATE]
        cb = lax.dot_general(cg, bg, (((1,), (1,)), ((), ())),
                             preferred_element_type=F32)
        pair_out = []
        for pr in range(SSD_GROUP_W // (2 * SSD_HEAD_DIM)):
            col0 = g * SSD_GROUP_W + pr * 2 * SSD_HEAD_DIM
            x_pair = xb[:, col0:col0 + 2 * SSD_HEAD_DIM]
            ys = []
            for j in range(2):
                hd = col0 // SSD_HEAD_DIM + j
                diff = cum[:, hd:hd + 1] - cum_t[hd:hd + 1, :]
                decay = jnp.exp(jnp.where(causal, diff, -jnp.inf))
                mat = (cb * decay).astype(BF16)
                ys.append(jnp.dot(mat, x_pair, preferred_element_type=F32))
            pair_out.append(jnp.where(first_half, ys[0], ys[1]))
        y_diag = jnp.concatenate(pair_out, axis=1)
        st = state_ref[g]
        y_off = jnp.dot(cg, st.astype(BF16), preferred_element_type=F32) * ea_x[:, gs]
        new = lax.dot_general(bg, xe[:, gs], (((0,), (0,)), ((), ())),
                              preferred_element_type=F32)
        state_ref[g] = st * ea_x[CHUNK - 1:CHUNK, gs] + new
        yg = y_diag + y_off + dskip_ref[:, gs] * xs[:, gs]
        yg = yg * _silu(z_ref[:, gs])
        ms = jnp.mean(yg * yg, axis=-1, keepdims=True)
        y_groups.append(yg * lax.rsqrt(ms + RMS_EPS) * normw_ref[:, gs])
    y_ref[...] = jnp.concatenate(y_groups, axis=1).astype(y_ref.dtype)


def _ssd(xbc, z, small, convw, convb, sbias, alog, dskip, normw, expand):
    nb = xbc.shape[0]
    row_spec = lambda w: pl.BlockSpec((None, CHUNK, w), lambda bi, ci: (bi, ci, 0))
    return pl.pallas_call(
        _ssd_body,
        grid=(nb, LP // CHUNK),
        in_specs=[row_spec(SSD_CONV_CH), row_spec(SSD_D_INNER), row_spec(SMALL_W),
                  _const_spec(convw.shape), _const_spec(convb.shape),
                  _const_spec(sbias.shape), _const_spec(alog.shape),
                  _const_spec(dskip.shape), _const_spec(normw.shape),
                  _const_spec(expand.shape)],
        out_specs=[row_spec(SSD_D_INNER), row_spec(SMALL_W)],
        out_shape=[jax.ShapeDtypeStruct((nb, LP, SSD_D_INNER), BF16),
                   jax.ShapeDtypeStruct((nb, LP, SMALL_W), F32)],
        scratch_shapes=[pltpu.VMEM((CHUNK + CONV_HALO, SSD_CONV_CH), F32),
                        pltpu.VMEM((SSD_GROUPS, SSD_STATE, SSD_GROUP_W), F32),
                        pltpu.VMEM((1, SMALL_W), F32)],
        compiler_params=pltpu.CompilerParams(
            dimension_semantics=("arbitrary", "arbitrary"), vmem_limit_bytes=VMEM_LIMIT),
        name="ssd",
    )(xbc, z, small, convw, convb, sbias, alog, dskip, normw, expand)


N_TK_CHUNKS = ATT_TK // LANES
N_TQ_CHUNKS = ATT_TQ // LANES
ONES_ROWS = 16
assert ATT_TQ == ATT_TK and PADL <= ATT_TK
LOG2E = math.log2(math.e)
Q_SCALE = LOG2E * HEAD_DIM ** -0.5


def _init_softmax_state(m_ref, acc_ref):
    m_ref[...] = jnp.full(m_ref.shape, NEG, F32)
    acc_ref[...] = jnp.zeros_like(acc_ref)


def _run_key_tiles(qb, first, scores, step, buf_a, buf_b):
    n_below = qb - first
    odd = n_below % 2

    @pl.when(odd == 0)
    def _():
        scores(first, buf_a)

    @pl.when(odd == 1)
    def _():
        scores(first, buf_b)
        step(first, buf_b, buf_a)

    def two_below_diagonal(i, carry):
        kb = first + odd + 2 * i
        step(kb, buf_a, buf_b)
        step(kb + 1, buf_b, buf_a)
        return carry

    lax.fori_loop(0, n_below // 2, two_below_diagonal, 0)
    step(qb, buf_a, None)


def _query_operands(qt_ref):
    zero_rows = jnp.zeros((HEAD_DIM, ATT_TQ), BF16)
    return [jnp.concatenate([qt_ref[:HEAD_DIM, :], zero_rows], axis=0),
            jnp.concatenate([zero_rows, qt_ref[HEAD_DIM:, :]], axis=0)]


def _value_rows(vt_ref, row0, n_rows, start):
    vt = vt_ref[row0:row0 + n_rows, pl.ds(start, ATT_TK)]
    return jnp.concatenate([vt, jnp.ones((ONES_ROWS, ATT_TK), BF16)], axis=0)


def _softmax_update(s_t, col_max, vt_aug, m_ref, acc_ref, idx):
    m_old = m_ref[idx]
    m_new = jnp.maximum(m_old, col_max)
    alpha = jnp.exp2(m_old - m_new)
    p_t = jnp.exp2(s_t - m_new).astype(BF16)
    pv = jnp.dot(vt_aug, p_t, preferred_element_type=F32)
    acc_ref[idx] = alpha * acc_ref[idx] + pv
    m_ref[idx] = m_new


def _causal_mask():
    kpos = lax.broadcasted_iota(jnp.int32, (ATT_TK, ATT_TQ), 0)
    qpos = lax.broadcasted_iota(jnp.int32, (ATT_TK, ATT_TQ), 1)
    return kpos <= qpos


def _fox_body(first_ref, qt_ref, k_ref, vt_ref, c_ref, o_ref, m_ref, acc_ref,
              sa_ref, pa_ref, sb_ref, pb_ref):
    qb = pl.program_id(2)
    n_q = pl.num_programs(2)
    first = first_ref[(pl.program_id(0) * pl.num_programs(1) + pl.program_id(1)) * n_q + qb]
    first = jnp.minimum(first, qb)
    _init_softmax_state(m_ref, acc_ref)
    buf_a, buf_b = (sa_ref, pa_ref), (sb_ref, pb_ref)
    q_heads = _query_operands(qt_ref)
    lane = lax.broadcasted_iota(jnp.int32, (ATT_TK, LANES), 1)
    first_lane = SSD_HEADS + 2 * pl.program_id(1)

    def scores(kb, buf):
        s_dst, max_dst = buf
        start = pl.multiple_of(kb * ATT_TK, ATT_TK)
        k_rows = k_ref[pl.ds(start, ATT_TK), :]
        c_rows = c_ref[pl.ds(start, ATT_TK), :]
        for j in range(2):
            ck = jnp.sum(jnp.where(lane == first_lane + j, c_rows, 0.0), axis=1, keepdims=True)
            s_t = jnp.dot(k_rows, q_heads[j], preferred_element_type=F32) - ck
            s_dst[j] = s_t
            max_dst[j] = jnp.max(s_t, axis=0, keepdims=True)

    def step(kb, src, nxt):
        s_src, max_src = src
        start = pl.multiple_of(kb * ATT_TK, ATT_TK)
        if nxt is not None:
            scores(kb + 1, nxt)
        for j in range(2):
            s_t = s_src[j]
            col_max = max_src[j]
            if nxt is None:
                s_t = jnp.where(_causal_mask(), s_t, NEG)
                col_max = jnp.max(s_t, axis=0, keepdims=True)
            vt_aug = _value_rows(vt_ref, j * HEAD_DIM, HEAD_DIM, start)
            _softmax_update(s_t, col_max, vt_aug, m_ref, acc_ref, j)

    _run_key_tiles(qb, first, scores, step, buf_a, buf_b)

    o_t = jnp.concatenate(
        [acc_ref[j, :HEAD_DIM, :] / acc_ref[j, HEAD_DIM:HEAD_DIM + 1, :] for j in range(2)], axis=0)
    o_ref[...] = o_t.T.astype(o_ref.dtype)


def _fox_attention(first, qt, k, vt, c):
    nb = qt.shape[0]
    pairs = FOX_HEADS // 2
    acc_rows = HEAD_DIM + ONES_ROWS
    return pl.pallas_call(
        _fox_body,
        grid_spec=pltpu.PrefetchScalarGridSpec(
            num_scalar_prefetch=1,
            grid=(nb, pairs, LP // ATT_TQ),
            in_specs=[
                pl.BlockSpec((None, LANES, ATT_TQ), lambda bi, hp, qi, _: (bi, hp, qi)),
                pl.BlockSpec((None, LP, LANES), lambda bi, hp, qi, _: (bi, 0, hp)),
                pl.BlockSpec((None, LANES, LP), lambda bi, hp, qi, _: (bi, hp, 0)),
                pl.BlockSpec((None, LP, SMALL_W), lambda bi, hp, qi, _: (bi, 0, 0)),
            ],
            out_specs=pl.BlockSpec((None, ATT_TQ, LANES), lambda bi, hp, qi, _: (bi, qi, hp)),
            scratch_shapes=[pltpu.VMEM((2, 1, ATT_TQ), F32),
                            pltpu.VMEM((2, acc_rows, ATT_TQ), F32)]
                           + [pltpu.VMEM((2, ATT_TK, ATT_TQ), F32),
                              pltpu.VMEM((2, 1, ATT_TQ), F32)] * 2),
        out_shape=jax.ShapeDtypeStruct((nb, LP, FOX_WIDTH), BF16),
        compiler_params=pltpu.CompilerParams(
            dimension_semantics=("parallel", "parallel", "arbitrary"),
            vmem_limit_bytes=VMEM_LIMIT),
        name="fox_attention",
    )(first, qt, k, vt, c)


FORGOTTEN_LOG2 = 160.0
NORM_SLACK = 1.01


def _first_key_tiles(qt, k, c2):
    nb = qt.shape[0]
    n_q = LP // ATT_TQ
    heads = jnp.asarray(np.arange(FOX_WIDTH)[:, None] // HEAD_DIM == np.arange(FOX_HEADS)[None, :], F32)
    q_sq = jnp.einsum("bdl,dh->bhl", (qt * qt).astype(F32), heads)
    k_sq = jnp.einsum("bld,dh->blh", (k * k).astype(F32), heads)
    q_max = NORM_SLACK * jnp.sqrt(jnp.max(q_sq.reshape(nb, FOX_HEADS, n_q, ATT_TQ), axis=-1))
    k_max = NORM_SLACK * jnp.sqrt(jnp.max(k_sq, axis=1))
    c_heads = c2[:, :, SSD_HEADS:SSD_HEADS + FOX_HEADS]
    c_query = c_heads[:, ::ATT_TQ, :].transpose(0, 2, 1)
    c_key = c_heads[:, ATT_TK - 1::ATT_TK, :].transpose(0, 2, 1)
    bound = (2.0 * q_max * k_max[:, :, None])[:, :, :, None] - (c_key[:, :, None, :] - c_query[:, :, :, None])
    needed = bound >= -FORGOTTEN_LOG2
    needed = needed.reshape(nb, FOX_HEADS // 2, 2, n_q, n_q).any(axis=2)
    needed = needed | (jnp.arange(n_q)[None, :] >= jnp.arange(n_q)[:, None])
    return jnp.argmax(needed, axis=-1).astype(jnp.int32).reshape(-1)


def _bias_tiles(bias_ref):
    rows = jnp.broadcast_to(bias_ref[...], (CHUNK, 2 * CHUNK))
    rolled = pltpu.roll(rows, 0, 1, stride=1, stride_axis=0)
    return rolled[:, :CHUNK], rolled[:, CHUNK:]


def _diagonal_bias(tiles):
    zeros = jnp.zeros((CHUNK, CHUNK), F32)
    rows = []
    for a in range(N_TK_CHUNKS):
        rows.append(jnp.concatenate(
            [tiles[b - a] if b - a in (0, 1) else zeros for b in range(N_TQ_CHUNKS)], axis=1))
    return jnp.concatenate(rows, axis=0)


def _diff_body(lambda_init, qt_ref, k_ref, vt_ref, bias_ref, lam_ref, subln_ref, o_ref,
               m_ref, acc_ref, tiles_ref, sa_ref, pa_ref, sb_ref, pb_ref):
    qb = pl.program_id(2)
    _init_softmax_state(m_ref, acc_ref)
    buf_a, buf_b = (sa_ref, pa_ref), (sb_ref, pb_ref)
    tile0, tile1 = _bias_tiles(bias_ref)
    tiles_ref[0] = tile0
    tiles_ref[1] = tile1
    q_parts = _query_operands(qt_ref)
    pad_rows = -(-PADL // CHUNK) * CHUNK
    corner0 = ATT_TK - CHUNK

    def scores(kb, buf):
        s_dst, max_dst = buf
        start = pl.multiple_of(kb * ATT_TK, ATT_TK)
        k_rows = k_ref[pl.ds(start, ATT_TK), :]
        kpos = lax.broadcasted_iota(jnp.int32, (pad_rows, ATT_TQ), 0)
        not_padding = (kpos >= PADL) | (kb > 0)
        for j in range(2):
            s_t = jnp.dot(k_rows, q_parts[j], preferred_element_type=F32)
            s_t = jnp.concatenate(
                [jnp.where(not_padding, s_t[:pad_rows], NEG), s_t[pad_rows:]], axis=0)
            s_dst[j] = s_t
            max_dst[j] = jnp.max(s_t, axis=0, keepdims=True)

    def step(kb, src, nxt):
        s_src, max_src = src
        start = pl.multiple_of(kb * ATT_TK, ATT_TK)
        diagonal = nxt is None
        if diagonal:
            bias = _diagonal_bias((tiles_ref[0], tiles_ref[1]))
            visible = _causal_mask()
        else:
            scores(kb + 1, nxt)
            corner = jnp.where(kb == qb - 1, 1.0, 0.0) * tiles_ref[1]
        vt_aug = _value_rows(vt_ref, 0, LANES, start)
        for j in range(2):
            s_t = s_src[j]
            if diagonal:
                s_t = jnp.where(visible, s_t + bias, NEG)
                col_max = jnp.max(s_t, axis=0, keepdims=True)
            else:
                near = s_t[corner0:, :CHUNK] + corner
                s_t = jnp.concatenate(
                    [s_t[:corner0],
                     jnp.concatenate([near, s_t[corner0:, CHUNK:]], axis=1)], axis=0)
                first = jnp.maximum(jnp.max(s_t[:corner0, :CHUNK], axis=0, keepdims=True),
                                    jnp.max(near, axis=0, keepdims=True))
                col_max = jnp.concatenate([first, max_src[j][:, CHUNK:]], axis=1)
            _softmax_update(s_t, col_max, vt_aug, m_ref, acc_ref, j)

    _run_key_tiles(qb, 0, scores, step, buf_a, buf_b)

    lam1 = jnp.exp(jnp.sum(lam_ref[0:1, :] * lam_ref[1:2, :], axis=-1, keepdims=True))
    lam2 = jnp.exp(jnp.sum(lam_ref[2:3, :] * lam_ref[3:4, :], axis=-1, keepdims=True))
    lam = lam1 - lam2 + lambda_init
    o_t = (acc_ref[0, :LANES, :] / acc_ref[0, LANES:LANES + 1, :]
           - lam * (acc_ref[1, :LANES, :] / acc_ref[1, LANES:LANES + 1, :]))
    o = o_t.T
    ms = jnp.mean(o * o, axis=-1, keepdims=True)
    o = o * lax.rsqrt(ms + RMS_EPS) * subln_ref[...] * (1.0 - lambda_init)
    o_ref[...] = o.astype(o_ref.dtype)


def _diff_attention(qt, k, vt, bias_rows, lam_rows, subln, lambda_init):
    nb = qt.shape[0]
    acc_rows = LANES + ONES_ROWS
    return pl.pallas_call(
        functools.partial(_diff_body, lambda_init),
        grid=(nb, DIFF_HEADS, LP // ATT_TQ),
        in_specs=[
            pl.BlockSpec((None, LANES, ATT_TQ), lambda bi, hd, qi: (bi, hd, qi)),
            pl.BlockSpec((None, LP, LANES), lambda bi, hd, qi: (bi, 0, hd)),
            pl.BlockSpec((None, LANES, LP), lambda bi, hd, qi: (bi, hd, 0)),
            pl.BlockSpec((None, 1, 2 * CHUNK), lambda bi, hd, qi: (hd, 0, 0)),
            _const_spec(lam_rows.shape),
            _const_spec(subln.shape),
        ],
        out_specs=pl.BlockSpec((None, ATT_TQ, LANES), lambda bi, hd, qi: (bi, qi, hd)),
        out_shape=jax.ShapeDtypeStruct((nb, LP, DIFF_HEADS * LANES), BF16),
        scratch_shapes=[pltpu.VMEM((2, 1, ATT_TQ), F32),
                        pltpu.VMEM((2, acc_rows, ATT_TQ), F32),
                        pltpu.VMEM((2, CHUNK, CHUNK), F32)]
                       + [pltpu.VMEM((2, ATT_TK, ATT_TQ), F32),
                          pltpu.VMEM((2, 1, ATT_TQ), F32)] * 2,
        compiler_params=pltpu.CompilerParams(
            dimension_semantics=("parallel", "parallel", "arbitrary"),
            vmem_limit_bytes=VMEM_LIMIT),
        name="diff_attention",
    )(qt, k, vt, bias_rows, lam_rows, subln)


def _t5_bucket(n):
    max_exact = N_BUCKETS // 2
    nf = jnp.maximum(n, 1).astype(F32)
    large = max_exact + (jnp.log(nf / max_exact) / math.log(128 / max_exact)
                         * (N_BUCKETS - max_exact)).astype(jnp.int32)
    large = jnp.minimum(large, N_BUCKETS - 1)
    return jnp.where(n < max_exact, n, large)


def _relative_bias_rows(rel_table):
    dist = jnp.arange(2 * CHUNK)
    by_dist = LOG2E * (rel_table[_t5_bucket(dist)] - rel_table[N_BUCKETS - 1])
    return by_dist.T[:, None, :].astype(F32)


def _pad_lanes(vec, width=LANES):
    return jnp.pad(vec, (0, width - vec.shape[0]))[None, :].astype(F32)


def kernel(x, meta_tokens, ln_gain, ln_bias, ffn1_w_gate, ffn1_w_up, ffn1_w_down, ffn2_w_gate, ffn2_w_up, ffn2_w_down, even_w_in, even_conv_w, even_conv_b, ssd_dt_bias, ssd_a_log, ssd_d_skip, ssd_norm_w, fox_f_bias, even_w_out, diff_w_qkv, diff_lambda_q1, diff_lambda_k1, diff_lambda_q2, diff_lambda_k2, diff_subln_w, diff_w_o, rel_bias_table):
    nb = x.shape[0]
    lead = jnp.concatenate([jnp.zeros((PADL, D_MODEL), x.dtype), meta_tokens.astype(x.dtype)], axis=0)

    def ln_params(l, i):
        return ln_gain[l, i][None, :], ln_bias[l, i][None, :]

    ffn1 = tuple(w.astype(BF16) for w in (ffn1_w_gate, ffn1_w_up, ffn1_w_down))
    ffn2 = tuple(w.astype(BF16) for w in (ffn2_w_gate, ffn2_w_up, ffn2_w_down))

    hf = _ffn_ln_first(x, lead, *ffn1, 0, *ln_params(0, 0))

    w_in = even_w_in[0]
    o_z, o_xbc = 0, SSD_D_INNER
    o_dt = o_xbc + SSD_CONV_CH
    o_q = o_dt + SSD_HEADS
    o_k, o_v = o_q + FOX_WIDTH, o_q + 2 * FOX_WIDTH
    o_f = o_q + 3 * FOX_WIDTH
    order = jnp.argsort(fox_f_bias[0])

    def by_head(w, axis):
        shape = w.shape
        w = w.reshape(shape[:axis] + (FOX_HEADS, HEAD_DIM) + shape[axis + 1:])
        return jnp.take(w, order, axis=axis).reshape(shape)

    w_q, w_k, w_v = (by_head(w_in[:, o:o + FOX_WIDTH], 1) for o in (o_q, o_k, o_v))
    w_small = jnp.concatenate(
        [w_in[:, o_dt:o_q], w_in[:, o_f:][:, order],
         jnp.zeros((D_MODEL, SMALL_W - SSD_HEADS - FOX_HEADS), w_in.dtype)], axis=1)
    w_even = jnp.concatenate([w_in[:, o_z:o_dt], w_k, w_small], axis=1).astype(BF16)
    wt_even = jnp.concatenate([w_q * Q_SCALE, w_v], axis=1).T.astype(BF16)
    z, xbc, k, small, qt, vt = _proj(
        hf, w_even, wt_even, (SSD_D_INNER, SSD_CONV_CH, FOX_WIDTH, SMALL_W),
        (F32, F32, BF16, F32), (FOX_WIDTH, FOX_WIDTH), nb, "even_in_proj")

    sbias = _pad_lanes(jnp.concatenate([ssd_dt_bias[0], fox_f_bias[0][order]]))
    alog = _pad_lanes(ssd_a_log[0])
    dskip = jnp.repeat(ssd_d_skip[0], SSD_HEAD_DIM)[None, :].astype(F32)
    expand = np.zeros((SMALL_W, SSD_D_INNER), np.float32)
    expand[np.arange(SSD_D_INNER) // SSD_HEAD_DIM, np.arange(SSD_D_INNER)] = 1.0
    y, cfull = _ssd(xbc.reshape(nb, LP, SSD_CONV_CH), z.reshape(nb, LP, SSD_D_INNER),
                    small.reshape(nb, LP, SMALL_W), even_conv_w[0], even_conv_b[0][None, :],
                    sbias, alog, dskip, ssd_norm_w[0][None, :],
                    jnp.asarray(np.concatenate([expand, expand], axis=0), BF16))
    c2 = LOG2E * cfull
    k = k.reshape(nb, LP, FOX_WIDTH)
    ck = jnp.where(jnp.arange(LP)[None, :, None] < PADL, -NEG, c2)
    o = _fox_attention(_first_key_tiles(qt, k, c2), qt, k, vt, ck)
    w_out = even_w_out[0].astype(BF16)
    hf = _outproj_ln(hf, [y.reshape(nb * LP, SSD_D_INNER), o.reshape(nb * LP, FOX_WIDTH)],
                     [w_out[:SSD_D_INNER], by_head(w_out[SSD_D_INNER:], 0)], *ln_params(0, 1),
                     name="even_out_proj_ln")
    hf = _ffn_ln(hf, *ffn2, 0, *ln_params(0, 2))

    hf = _ffn_ln(hf, *ffn1, 1, *ln_params(1, 0))
    qw = DIFF_HEADS * 2 * HEAD_DIM
    w_qkv = diff_w_qkv[0]
    wt_diff = jnp.concatenate([w_qkv[:, :qw] * Q_SCALE, w_qkv[:, 2 * qw:]], axis=1).T.astype(BF16)
    k, qt, vt = _proj(hf, w_qkv[:, qw:2 * qw].astype(BF16), wt_diff, (qw,), (BF16,),
                      (qw, DIFF_HEADS * LANES), nb, "diff_qkv_proj")
    lambda_init = 0.8 - 0.6 * math.exp(-0.3 * 1)
    lam_rows = jnp.concatenate(
        [_pad_lanes(diff_lambda_q1[0]), _pad_lanes(diff_lambda_k1[0]),
         _pad_lanes(diff_lambda_q2[0]), _pad_lanes(diff_lambda_k2[0]),
         jnp.zeros((4, LANES), F32)], axis=0)
    o = _diff_attention(qt, k.reshape(nb, LP, qw), vt, _relative_bias_rows(rel_bias_table),
                        lam_rows, diff_subln_w[0][None, :], lambda_init)
    hf = _outproj_ln(hf, [o.reshape(nb * LP, DIFF_HEADS * LANES)], [diff_w_o[0].astype(BF16)],
                     *ln_params(1, 1), name="diff_out_proj_ln")
    return _ffn_ln_final(hf.reshape(nb, LP, D_MODEL), *ffn2, 1, *ln_params(1, 2))
```

```python
import functools
import math

import numpy as np
import jax
import jax.numpy as jnp
from jax import lax
from jax.experimental import pallas as pl
from jax.experimental.pallas import tpu as pltpu

F32 = jnp.float32
BF16 = jnp.bfloat16

D_MODEL = 1024
SEQ = 8192
DEPTH = 2
N_META = 16
CHUNK = 128
SSD_D_INNER = 2048
SSD_HEAD_DIM = 64
SSD_HEADS = 32
SSD_GROUPS = 4
SSD_GROUP_W = SSD_D_INNER // SSD_GROUPS
SSD_STATE = 128
SSD_CONV = 4
SSD_CONV_CH = SSD_D_INNER + 2 * SSD_GROUPS * SSD_STATE
FOX_HEADS = 16
FOX_WIDTH = 1024
HEAD_DIM = 64
DIFF_HEADS = 8
N_BUCKETS = 32
D_FF = 2816
ALPHA = (2 * DEPTH) ** 0.25
LN_EPS = 1e-5
RMS_EPS = 1e-5
NEG = -1e30

LANES = 128
LP = 8448
PADL = LP - SEQ - N_META
ROW_TILE = 512
OUT_TILE = 256
ATT_TQ = 768
ATT_TK = 768
CONV_HALO = 8
SMALL_W = LANES
VMEM_LIMIT = 56 * 1024 * 1024

assert PADL % CHUNK == CHUNK - N_META
assert LP % ATT_TQ == 0 and ATT_TQ % ATT_TK == 0 and ATT_TK % LANES == 0
assert LP % CHUNK == 0 and (2 * LP) % ROW_TILE == 0
assert PADL <= OUT_TILE and (LP - OUT_TILE) == SEQ


def _const_spec(shape):
    nd = len(shape)
    return pl.BlockSpec(shape, lambda *_: (0,) * nd, pipeline_mode=pl.Buffered(1))


def _layer_spec(shape, layer):
    nd = len(shape)
    return pl.BlockSpec((None,) + tuple(shape), lambda *_: (layer,) + (0,) * nd,
                        pipeline_mode=pl.Buffered(1))


def _ffn_weight_specs(layer):
    return [_layer_spec((D_MODEL, D_FF), layer), _layer_spec((D_MODEL, D_FF), layer),
            _layer_spec((D_FF, D_MODEL), layer)]


def _layer_norm(r, g, b):
    mu = jnp.mean(r, axis=-1, keepdims=True)
    d = r - mu
    var = jnp.mean(d * d, axis=-1, keepdims=True)
    return d * lax.rsqrt(var + LN_EPS) * g + b


def _silu(x):
    return x / (1.0 + jnp.exp(-x))


def _softplus(x):
    return jnp.maximum(x, 0.0) + jnp.log(1.0 + jnp.exp(-jnp.abs(x)))


def _ffn_ln_body(h_ref, wg_ref, wu_ref, wd_ref, g_ref, b_ref, o_ref):
    h = h_ref[...]
    hb = h.astype(BF16)
    g = jnp.dot(hb, wg_ref[...], preferred_element_type=F32)
    u = jnp.dot(hb, wu_ref[...], preferred_element_type=F32)
    a = (_silu(g) * u).astype(BF16)
    y = jnp.dot(a, wd_ref[...], preferred_element_type=F32)
    o_ref[...] = _layer_norm(ALPHA * h + 0.5 * y, g_ref[...], b_ref[...])


def _ffn_ln(hf, wg, wu, wd, layer, g, b):
    rows = hf.shape[0]
    return pl.pallas_call(
        _ffn_ln_body,
        grid=(rows // ROW_TILE,),
        in_specs=[
            pl.BlockSpec((ROW_TILE, D_MODEL), lambda i: (i, 0)),
            *_ffn_weight_specs(layer),
            _const_spec((1, D_MODEL)),
            _const_spec((1, D_MODEL)),
        ],
        out_specs=pl.BlockSpec((ROW_TILE, D_MODEL), lambda i: (i, 0)),
        out_shape=jax.ShapeDtypeStruct((rows, D_MODEL), F32),
        compiler_params=pltpu.CompilerParams(
            dimension_semantics=("parallel",), vmem_limit_bytes=VMEM_LIMIT),
        name="ffn_ln",
    )(hf, wg, wu, wd, g, b)


def _ffn_ln_first_body(lead_ref, x_ref, wg_ref, wu_ref, wd_ref, g_ref, b_ref, o_ref, h_ref):
    h_ref[...] = jnp.where(pl.program_id(1) == 0, lead_ref[...], x_ref[...])
    _ffn_ln_body(h_ref, wg_ref, wu_ref, wd_ref, g_ref, b_ref, o_ref)


def _ffn_ln_first(x, lead, wg, wu, wd, layer, g, b):
    nb = x.shape[0]
    tiles = LP // OUT_TILE
    return pl.pallas_call(
        _ffn_ln_first_body,
        grid=(nb, tiles),
        in_specs=[
            _const_spec((OUT_TILE, D_MODEL)),
            pl.BlockSpec((None, OUT_TILE, D_MODEL), lambda bi, i: (bi, jnp.maximum(i - 1, 0), 0)),
            *_ffn_weight_specs(layer),
            _const_spec((1, D_MODEL)),
            _const_spec((1, D_MODEL)),
        ],
        out_specs=pl.BlockSpec((OUT_TILE, D_MODEL), lambda bi, i: (bi * tiles + i, 0)),
        out_shape=jax.ShapeDtypeStruct((nb * LP, D_MODEL), F32),
        scratch_shapes=[pltpu.VMEM((OUT_TILE, D_MODEL), F32)],
        compiler_params=pltpu.CompilerParams(
            dimension_semantics=("parallel", "arbitrary"), vmem_limit_bytes=VMEM_LIMIT),
        name="ffn_ln_first",
    )(lead, x, wg, wu, wd, g, b)


def _ffn_ln_final_body(h_ref, wg_ref, wu_ref, wd_ref, g_ref, b_ref, o_ref):
    @pl.when(pl.program_id(1) > 0)
    def _():
        _ffn_ln_body(h_ref, wg_ref, wu_ref, wd_ref, g_ref, b_ref, o_ref)


def _ffn_ln_final(h3, wg, wu, wd, layer, g, b):
    nb = h3.shape[0]
    return pl.pallas_call(
        _ffn_ln_final_body,
        grid=(nb, LP // OUT_TILE),
        in_specs=[
            pl.BlockSpec((None, OUT_TILE, D_MODEL), lambda bi, i: (bi, i, 0)),
            *_ffn_weight_specs(layer),
            _const_spec((1, D_MODEL)),
            _const_spec((1, D_MODEL)),
        ],
        out_specs=pl.BlockSpec((None, OUT_TILE, D_MODEL),
                               lambda bi, i: (bi, jnp.maximum(i - 1, 0), 0)),
        out_shape=jax.ShapeDtypeStruct((nb, SEQ, D_MODEL), F32),
        compiler_params=pltpu.CompilerParams(
            dimension_semantics=("arbitrary", "arbitrary"), vmem_limit_bytes=VMEM_LIMIT),
        name="ffn_ln_final",
    )(h3, wg, wu, wd, g, b)


PROJ_TILE = 256
assert LP % PROJ_TILE == 0


def _proj_body(n_t, h_ref, w_ref, wt_ref, *o_refs):
    hb = h_ref[...].astype(BF16)
    off = 0
    for o_ref in o_refs[:-n_t]:
        n = o_ref.shape[-1]
        o_ref[...] = jnp.dot(hb, w_ref[:, off:off + n],
                             preferred_element_type=F32).astype(o_ref.dtype)
        off += n
    off = 0
    for o_ref in o_refs[-n_t:]:
        n = o_ref.shape[0]
        o_ref[...] = lax.dot_general(wt_ref[off:off + n, :], hb, (((1,), (1,)), ((), ())),
                                     preferred_element_type=F32).astype(o_ref.dtype)
        off += n


def _proj(hf, w, wt, widths, dtypes, t_widths, nb, name):
    rows = hf.shape[0]
    tiles_per_batch = LP // PROJ_TILE
    out_specs = [pl.BlockSpec((PROJ_TILE, n), lambda i: (i, 0)) for n in widths]
    out_specs += [pl.BlockSpec((None, n, PROJ_TILE),
                               lambda i: (i // tiles_per_batch, 0, i % tiles_per_batch))
                  for n in t_widths]
    out_shape = [jax.ShapeDtypeStruct((rows, n), dt) for n, dt in zip(widths, dtypes)]
    out_shape += [jax.ShapeDtypeStruct((nb, n, LP), BF16) for n in t_widths]
    return pl.pallas_call(
        functools.partial(_proj_body, len(t_widths)),
        grid=(rows // PROJ_TILE,),
        in_specs=[pl.BlockSpec((PROJ_TILE, D_MODEL), lambda i: (i, 0)),
                  _const_spec(w.shape), _const_spec(wt.shape)],
        out_specs=out_specs,
        out_shape=out_shape,
        compiler_params=pltpu.CompilerParams(
            dimension_semantics=("parallel",), vmem_limit_bytes=VMEM_LIMIT),
        name=name,
    )(hf, w, wt)


def _outproj_ln_body(n_in, h_ref, *refs):
    a_refs = refs[:n_in]
    w_refs = refs[n_in:2 * n_in]
    g_ref, b_ref, o_ref = refs[2 * n_in:]
    m = jnp.dot(a_refs[0][...], w_refs[0][...], preferred_element_type=F32)
    for a_ref, w_ref in zip(a_refs[1:], w_refs[1:]):
        m = m + jnp.dot(a_ref[...], w_ref[...], preferred_element_type=F32)
    o_ref[...] = _layer_norm(ALPHA * h_ref[...] + m, g_ref[...], b_ref[...])


def _outproj_ln(hf, acts, ws, g, b, name):
    rows = hf.shape[0]
    n_in = len(acts)
    in_specs = [pl.BlockSpec((ROW_TILE, D_MODEL), lambda i: (i, 0))]
    in_specs += [pl.BlockSpec((ROW_TILE, a.shape[1]), lambda i: (i, 0)) for a in acts]
    in_specs += [_const_spec(w.shape) for w in ws]
    in_specs += [_const_spec((1, D_MODEL)), _const_spec((1, D_MODEL))]
    return pl.pallas_call(
        functools.partial(_outproj_ln_body, n_in),
        grid=(rows // ROW_TILE,),
        in_specs=in_specs,
        out_specs=pl.BlockSpec((ROW_TILE, D_MODEL), lambda i: (i, 0)),
        out_shape=jax.ShapeDtypeStruct((rows, D_MODEL), F32),
        compiler_params=pltpu.CompilerParams(
            dimension_semantics=("parallel",), vmem_limit_bytes=VMEM_LIMIT),
        name=name,
    )(hf, *acts, *ws, g, b)


def _split_dot(x, e2_ref):
    hi = x.astype(BF16)
    lo = (x - hi.astype(F32)).astype(BF16)
    return jnp.dot(jnp.concatenate([hi, lo], axis=1), e2_ref[...], preferred_element_type=F32)


def _ssd_body(xbc_ref, z_ref, small_ref, convw_ref, convb_ref, sbias_ref, alog_ref,
              dskip_ref, normw_ref, expand_ref, y_ref, c_ref,
              ext_ref, state_ref, carry_ref):
    c = pl.program_id(1)

    @pl.when(c == 0)
    def _():
        ext_ref[0:CONV_HALO, :] = jnp.zeros((CONV_HALO, SSD_CONV_CH), F32)
        state_ref[...] = jnp.zeros_like(state_ref)
        carry_ref[...] = jnp.zeros_like(carry_ref)

    row = lax.broadcasted_iota(jnp.int32, (CHUNK, 1), 0)
    valid = (c * CHUNK + row) >= PADL

    ext_ref[CONV_HALO:, :] = xbc_ref[...]

    @pl.when(c * CHUNK < PADL)
    def _():
        ext_ref[CONV_HALO:, :] = jnp.where(valid, ext_ref[CONV_HALO:, :], 0.0)

    ext = ext_ref[...]
    conv = convb_ref[...] + convw_ref[SSD_CONV - 1:SSD_CONV, :] * ext[CONV_HALO:, :]
    for k in range(SSD_CONV - 1):
        shifted = pltpu.roll(ext, SSD_CONV - 1 - k, 0)[CONV_HALO:, :]
        conv = conv + convw_ref[k:k + 1, :] * shifted
    ext_ref[0:CONV_HALO, :] = ext_ref[CHUNK:CHUNK + CONV_HALO, :]
    xc = _silu(conv)
    xs = xc[:, :SSD_D_INNER]
    bm = xc[:, SSD_D_INNER:SSD_D_INNER + SSD_GROUPS * SSD_STATE].astype(BF16)
    cm = xc[:, SSD_D_INNER + SSD_GROUPS * SSD_STATE:].astype(BF16)

    lane = lax.broadcasted_iota(jnp.int32, (CHUNK, SMALL_W), 1)
    is_dt = lane < SSD_HEADS
    is_f = (lane >= SSD_HEADS) & (lane < SSD_HEADS + FOX_HEADS)
    v = small_ref[...] + sbias_ref[...]
    dt = jnp.where(valid & is_dt, _softplus(v), 0.0)
    log_f = jnp.where(valid & is_f, -_softplus(-v), 0.0)
    neg_a = -jnp.exp(alog_ref[...])
    steps = jnp.where(is_dt, dt * neg_a, log_f)
    r_i = lax.broadcasted_iota(jnp.int32, (CHUNK, CHUNK), 0)
    c_i = lax.broadcasted_iota(jnp.int32, (CHUNK, CHUNK), 1)
    causal = r_i >= c_i
    tril = jnp.where(causal, 1.0, 0.0).astype(F32)
    cum = jnp.dot(tril, steps, preferred_element_type=F32,
                  precision=lax.Precision.HIGHEST)
    c_total = jnp.where(is_f, cum + carry_ref[...], 0.0)
    c_ref[...] = c_total
    carry_ref[...] = c_total[CHUNK - 1:CHUNK, :]

    a_last = cum[CHUNK - 1:CHUNK, :]
    dt_x = _split_dot(dt, expand_ref)
    ea_x = _split_dot(jnp.exp(cum), expand_ref)
    de_x = _split_dot(jnp.exp(a_last - cum), expand_ref)
    x_dt = xs * dt_x
    xb = x_dt.astype(BF16)
    xe = (x_dt * de_x).astype(BF16)
    cum_t = cum.T
    lane_p = lax.broadcasted_iota(jnp.int32, (CHUNK, 2 * SSD_HEAD_DIM), 1)
    first_half = lane_p < SSD_HEAD_DIM

    y_groups = []
    for g in range(SSD_GROUPS):
        gs = slice(g * SSD_GROUP_W, (g + 1) * SSD_GROUP_W)
        bg = bm[:, g * SSD_STATE:(g + 1) * SSD_STATE]
        cg = cm[:, g * SSD_STATE:(g + 1) * SSD_STATE]
        cb = lax.dot_general(cg, bg, (((1,), (1,)), ((), ())),
                             preferred_element_type=F32)
        pair_out = []
        for pr in range(SSD_GROUP_W // (2 * SSD_HEAD_DIM)):
            col0 = g * SSD_GROUP_W + pr * 2 * SSD_HEAD_DIM
            x_pair = xb[:, col0:col0 + 2 * SSD_HEAD_DIM]
            ys = []
            for j in range(2):
                hd = col0 // SSD_HEAD_DIM + j
                diff = cum[:, hd:hd + 1] - cum_t[hd:hd + 1, :]
                decay = jnp.exp(jnp.where(causal, diff, -jnp.inf))
                mat = (cb * decay).astype(BF16)
                ys.append(jnp.dot(mat, x_pair, preferred_element_type=F32))
            pair_out.append(jnp.where(first_half, ys[0], ys[1]))
        y_diag = jnp.concatenate(pair_out, axis=1)
        st = state_ref[g]
        y_off = jnp.dot(cg, st.astype(BF16), preferred_element_type=F32) * ea_x[:, gs]
        new = lax.dot_general(bg, xe[:, gs], (((0,), (0,)), ((), ())),
                              preferred_element_type=F32)
        state_ref[g] = st * ea_x[CHUNK - 1:CHUNK, gs] + new
        yg = y_diag + y_off + dskip_ref[:, gs] * xs[:, gs]
        yg = yg * _silu(z_ref[:, gs])
        ms = jnp.mean(yg * yg, axis=-1, keepdims=True)
        y_groups.append(yg * lax.rsqrt(ms + RMS_EPS) * normw_ref[:, gs])
    y_ref[...] = jnp.concatenate(y_groups, axis=1).astype(y_ref.dtype)


def _ssd(xbc, z, small, convw, convb, sbias, alog, dskip, normw, expand):
    nb = xbc.shape[0]
    row_spec = lambda w: pl.BlockSpec((None, CHUNK, w), lambda bi, ci: (bi, ci, 0))
    return pl.pallas_call(
        _ssd_body,
        grid=(nb, LP // CHUNK),
        in_specs=[row_spec(SSD_CONV_CH), row_spec(SSD_D_INNER), row_spec(SMALL_W),
                  _const_spec(convw.shape), _const_spec(convb.shape),
                  _const_spec(sbias.shape), _const_spec(alog.shape),
                  _const_spec(dskip.shape), _const_spec(normw.shape),
                  _const_spec(expand.shape)],
        out_specs=[row_spec(SSD_D_INNER), row_spec(SMALL_W)],
        out_shape=[jax.ShapeDtypeStruct((nb, LP, SSD_D_INNER), BF16),
                   jax.ShapeDtypeStruct((nb, LP, SMALL_W), F32)],
        scratch_shapes=[pltpu.VMEM((CHUNK + CONV_HALO, SSD_CONV_CH), F32),
                        pltpu.VMEM((SSD_GROUPS, SSD_STATE, SSD_GROUP_W), F32),
                        pltpu.VMEM((1, SMALL_W), F32)],
        compiler_params=pltpu.CompilerParams(
            dimension_semantics=("arbitrary", "arbitrary"), vmem_limit_bytes=VMEM_LIMIT),
        name="ssd",
    )(xbc, z, small, convw, convb, sbias, alog, dskip, normw, expand)


N_TK_CHUNKS = ATT_TK // LANES
N_TQ_CHUNKS = ATT_TQ // LANES
ONES_ROWS = 16
assert ATT_TQ == ATT_TK and PADL <= ATT_TK
LOG2E = math.log2(math.e)
Q_SCALE = LOG2E * HEAD_DIM ** -0.5


def _init_softmax_state(m_ref, acc_ref):
    m_ref[...] = jnp.full(m_ref.shape, NEG, F32)
    acc_ref[...] = jnp.zeros_like(acc_ref)


def _run_key_tiles(qb, first, scores, step, buf_a, buf_b):
    n_below = qb - first
    odd = n_below % 2

    @pl.when(odd == 0)
    def _():
        scores(first, buf_a)

    @pl.when(odd == 1)
    def _():
        scores(first, buf_b)
        step(first, buf_b, buf_a)

    def two_below_diagonal(i, carry):
        kb = first + odd + 2 * i
        step(kb, buf_a, buf_b)
        step(kb + 1, buf_b, buf_a)
        return carry

    lax.fori_loop(0, n_below // 2, two_below_diagonal, 0)
    step(qb, buf_a, None)


def _query_operands(qt_ref):
    zero_rows = jnp.zeros((HEAD_DIM, ATT_TQ), BF16)
    return [jnp.concatenate([qt_ref[:HEAD_DIM, :], zero_rows], axis=0),
            jnp.concatenate([zero_rows, qt_ref[HEAD_DIM:, :]], axis=0)]


def _value_rows(vt_ref, row0, n_rows, start):
    vt = vt_ref[row0:row0 + n_rows, pl.ds(start, ATT_TK)]
    return jnp.concatenate([vt, jnp.ones((ONES_ROWS, ATT_TK), BF16)], axis=0)


def _softmax_update(s_t, col_max, vt_aug, m_ref, acc_ref, idx):
    m_old = m_ref[idx]
    m_new = jnp.maximum(m_old, col_max)
    alpha = jnp.exp2(m_old - m_new)
    p_t = jnp.exp2(s_t - m_new).astype(BF16)
    pv = jnp.dot(vt_aug, p_t, preferred_element_type=F32)
    acc_ref[idx] = alpha * acc_ref[idx] + pv
    m_ref[idx] = m_new


def _causal_mask():
    kpos = lax.broadcasted_iota(jnp.int32, (ATT_TK, ATT_TQ), 0)
    qpos = lax.broadcasted_iota(jnp.int32, (ATT_TK, ATT_TQ), 1)
    return kpos <= qpos


def _fox_body(first_ref, qt_ref, k_ref, vt_ref, c_ref, o_ref, m_ref, acc_ref,
              sa_ref, pa_ref, sb_ref, pb_ref):
    qb = pl.program_id(2)
    n_q = pl.num_programs(2)
    first = first_ref[(pl.program_id(0) * pl.num_programs(1) + pl.program_id(1)) * n_q + qb]
    first = jnp.minimum(first, qb)
    _init_softmax_state(m_ref, acc_ref)
    buf_a, buf_b = (sa_ref, pa_ref), (sb_ref, pb_ref)
    q_heads = _query_operands(qt_ref)
    lane = lax.broadcasted_iota(jnp.int32, (ATT_TK, LANES), 1)
    first_lane = SSD_HEADS + 2 * pl.program_id(1)

    def scores(kb, buf):
        s_dst, max_dst = buf
        start = pl.multiple_of(kb * ATT_TK, ATT_TK)
        k_rows = k_ref[pl.ds(start, ATT_TK), :]
        c_rows = c_ref[pl.ds(start, ATT_TK), :]
        for j in range(2):
            ck = jnp.sum(jnp.where(lane == first_lane + j, c_rows, 0.0), axis=1, keepdims=True)
            s_t = jnp.dot(k_rows, q_heads[j], preferred_element_type=F32) - ck
            s_dst[j] = s_t
            max_dst[j] = jnp.max(s_t, axis=0, keepdims=True)

    def step(kb, src, nxt):
        s_src, max_src = src
        start = pl.multiple_of(kb * ATT_TK, ATT_TK)
        if nxt is not None:
            scores(kb + 1, nxt)
        for j in range(2):
            s_t = s_src[j]
            col_max = max_src[j]
            if nxt is None:
                s_t = jnp.where(_causal_mask(), s_t, NEG)
                col_max = jnp.max(s_t, axis=0, keepdims=True)
            vt_aug = _value_rows(vt_ref, j * HEAD_DIM, HEAD_DIM, start)
            _softmax_update(s_t, col_max, vt_aug, m_ref, acc_ref, j)

    _run_key_tiles(qb, first, scores, step, buf_a, buf_b)

    o_t = jnp.concatenate(
        [acc_ref[j, :HEAD_DIM, :] / acc_ref[j, HEAD_DIM:HEAD_DIM + 1, :] for j in range(2)], axis=0)
    o_ref[...] = o_t.T.astype(o_ref.dtype)


def _fox_attention(first, qt, k, vt, c):
    nb = qt.shape[0]
    pairs = FOX_HEADS // 2
    acc_rows = HEAD_DIM + ONES_ROWS
    return pl.pallas_call(
        _fox_body,
        grid_spec=pltpu.PrefetchScalarGridSpec(
            num_scalar_prefetch=1,
            grid=(nb, pairs, LP // ATT_TQ),
            in_specs=[
                pl.BlockSpec((None, LANES, ATT_TQ), lambda bi, hp, qi, _: (bi, hp, qi)),
                pl.BlockSpec((None, LP, LANES), lambda bi, hp, qi, _: (bi, 0, hp)),
                pl.BlockSpec((None, LANES, LP), lambda bi, hp, qi, _: (bi, hp, 0)),
                pl.BlockSpec((None, LP, SMALL_W), lambda bi, hp, qi, _: (bi, 0, 0)),
            ],
            out_specs=pl.BlockSpec((None, ATT_TQ, LANES), lambda bi, hp, qi, _: (bi, qi, hp)),
            scratch_shapes=[pltpu.VMEM((2, 1, ATT_TQ), F32),
                            pltpu.VMEM((2, acc_rows, ATT_TQ), F32)]
                           + [pltpu.VMEM((2, ATT_TK, ATT_TQ), F32),
                              pltpu.VMEM((2, 1, ATT_TQ), F32)] * 2),
        out_shape=jax.ShapeDtypeStruct((nb, LP, FOX_WIDTH), BF16),
        compiler_params=pltpu.CompilerParams(
            dimension_semantics=("parallel", "parallel", "arbitrary"),
            vmem_limit_bytes=VMEM_LIMIT),
        name="fox_attention",
    )(first, qt, k, vt, c)


FORGOTTEN_LOG2 = 160.0
NORM_SLACK = 1.01


def _first_key_tiles(qt, k, c2):
    nb = qt.shape[0]
    n_q = LP // ATT_TQ
    heads = jnp.asarray(np.arange(FOX_WIDTH)[:, None] // HEAD_DIM == np.arange(FOX_HEADS)[None, :], F32)
    q_sq = jnp.einsum("bdl,dh->bhl", (qt * qt).astype(F32), heads)
    k_sq = jnp.einsum("bld,dh->blh", (k * k).astype(F32), heads)
    q_max = NORM_SLACK * jnp.sqrt(jnp.max(q_sq.reshape(nb, FOX_HEADS, n_q, ATT_TQ), axis=-1))
    k_max = NORM_SLACK * jnp.sqrt(jnp.max(k_sq, axis=1))
    c_heads = c2[:, :, SSD_HEADS:SSD_HEADS + FOX_HEADS]
    c_query = c_heads[:, ::ATT_TQ, :].transpose(0, 2, 1)
    c_key = c_heads[:, ATT_TK - 1::ATT_TK, :].transpose(0, 2, 1)
    bound = (2.0 * q_max * k_max[:, :, None])[:, :, :, None] - (c_key[:, :, None, :] - c_query[:, :, :, None])
    needed = bound >= -FORGOTTEN_LOG2
    needed = needed.reshape(nb, FOX_HEADS // 2, 2, n_q, n_q).any(axis=2)
    needed = needed | (jnp.arange(n_q)[None, :] >= jnp.arange(n_q)[:, None])
    return jnp.argmax(needed, axis=-1).astype(jnp.int32).reshape(-1)


def _bias_tiles(bias_ref):
    rows = jnp.broadcast_to(bias_ref[...], (CHUNK, 2 * CHUNK))
    rolled = pltpu.roll(rows, 0, 1, stride=1, stride_axis=0)
    return rolled[:, :CHUNK], rolled[:, CHUNK:]


def _diagonal_bias(tiles):
    zeros = jnp.zeros((CHUNK, CHUNK), F32)
    rows = []
    for a in range(N_TK_CHUNKS):
        rows.append(jnp.concatenate(
            [tiles[b - a] if b - a in (0, 1) else zeros for b in range(N_TQ_CHUNKS)], axis=1))
    return jnp.concatenate(rows, axis=0)


def _diff_body(lambda_init, qt_ref, k_ref, vt_ref, bias_ref, lam_ref, subln_ref, o_ref,
               m_ref, acc_ref, tiles_ref, sa_ref, pa_ref, sb_ref, pb_ref):
    qb = pl.program_id(2)
    _init_softmax_state(m_ref, acc_ref)
    buf_a, buf_b = (sa_ref, pa_ref), (sb_ref, pb_ref)
    tile0, tile1 = _bias_tiles(bias_ref)
    tiles_ref[0] = tile0
    tiles_ref[1] = tile1
    q_parts = _query_operands(qt_ref)
    pad_rows = -(-PADL // CHUNK) * CHUNK
    corner0 = ATT_TK - CHUNK

    def scores(kb, buf):
        s_dst, max_dst = buf
        start = pl.multiple_of(kb * ATT_TK, ATT_TK)
        k_rows = k_ref[pl.ds(start, ATT_TK), :]
        kpos = lax.broadcasted_iota(jnp.int32, (pad_rows, ATT_TQ), 0)
        not_padding = (kpos >= PADL) | (kb > 0)
        for j in range(2):
            s_t = jnp.dot(k_rows, q_parts[j], preferred_element_type=F32)
            s_t = jnp.concatenate(
                [jnp.where(not_padding, s_t[:pad_rows], NEG), s_t[pad_rows:]], axis=0)
            s_dst[j] = s_t
            max_dst[j] = jnp.max(s_t, axis=0, keepdims=True)

    def step(kb, src, nxt):
        s_src, max_src = src
        start = pl.multiple_of(kb * ATT_TK, ATT_TK)
        diagonal = nxt is None
        if diagonal:
            bias = _diagonal_bias((tiles_ref[0], tiles_ref[1]))
            visible = _causal_mask()
        else:
            scores(kb + 1, nxt)
            corner = jnp.where(kb == qb - 1, 1.0, 0.0) * tiles_ref[1]
        vt_aug = _value_rows(vt_ref, 0, LANES, start)
        for j in range(2):
            s_t = s_src[j]
            if diagonal:
                s_t = jnp.where(visible, s_t + bias, NEG)
                col_max = jnp.max(s_t, axis=0, keepdims=True)
            else:
                near = s_t[corner0:, :CHUNK] + corner
                s_t = jnp.concatenate(
                    [s_t[:corner0],
                     jnp.concatenate([near, s_t[corner0:, CHUNK:]], axis=1)], axis=0)
                first = jnp.maximum(jnp.max(s_t[:corner0, :CHUNK], axis=0, keepdims=True),
                                    jnp.max(near, axis=0, keepdims=True))
                col_max = jnp.concatenate([first, max_src[j][:, CHUNK:]], axis=1)
            _softmax_update(s_t, col_max, vt_aug, m_ref, acc_ref, j)

    _run_key_tiles(qb, 0, scores, step, buf_a, buf_b)

    lam1 = jnp.exp(jnp.sum(lam_ref[0:1, :] * lam_ref[1:2, :], axis=-1, keepdims=True))
    lam2 = jnp.exp(jnp.sum(lam_ref[2:3, :] * lam_ref[3:4, :], axis=-1, keepdims=True))
    lam = lam1 - lam2 + lambda_init
    o_t = (acc_ref[0, :LANES, :] / acc_ref[0, LANES:LANES + 1, :]
           - lam * (acc_ref[1, :LANES, :] / acc_ref[1, LANES:LANES + 1, :]))
    o = o_t.T
    ms = jnp.mean(o * o, axis=-1, keepdims=True)
    o = o * lax.rsqrt(ms + RMS_EPS) * subln_ref[...] * (1.0 - lambda_init)
    o_ref[...] = o.astype(o_ref.dtype)


def _diff_attention(qt, k, vt, bias_rows, lam_rows, subln, lambda_init):
    nb = qt.shape[0]
    acc_rows = LANES + ONES_ROWS
    return pl.pallas_call(
        functools.partial(_diff_body, lambda_init),
        grid=(nb, DIFF_HEADS, LP // ATT_TQ),
        in_specs=[
            pl.BlockSpec((None, LANES, ATT_TQ), lambda bi, hd, qi: (bi, hd, qi)),
            pl.BlockSpec((None, LP, LANES), lambda bi, hd, qi: (bi, 0, hd)),
            pl.BlockSpec((None, LANES, LP), lambda bi, hd, qi: (bi, hd, 0)),
            pl.BlockSpec((None, 1, 2 * CHUNK), lambda bi, hd, qi: (hd, 0, 0)),
            _const_spec(lam_rows.shape),
            _const_spec(subln.shape),
        ],
        out_specs=pl.BlockSpec((None, ATT_TQ, LANES), lambda bi, hd, qi: (bi, qi, hd)),
        out_shape=jax.ShapeDtypeStruct((nb, LP, DIFF_HEADS * LANES), BF16),
        scratch_shapes=[pltpu.VMEM((2, 1, ATT_TQ), F32),
                        pltpu.VMEM((2, acc_rows, ATT_TQ), F32),
                        pltpu.VMEM((2, CHUNK, CHUNK), F32)]
                       + [pltpu.VMEM((2, ATT_TK, ATT_TQ), F32),
                          pltpu.VMEM((2, 1, ATT_TQ), F32)] * 2,
        compiler_params=pltpu.CompilerParams(
            dimension_semantics=("parallel", "parallel", "arbitrary"),
            vmem_limit_bytes=VMEM_LIMIT),
        name="diff_attention",
    )(qt, k, vt, bias_rows, lam_rows, subln)


def _t5_bucket(n):
    max_exact = N_BUCKETS // 2
    nf = jnp.maximum(n, 1).astype(F32)
    large = max_exact + (jnp.log(nf / max_exact) / math.log(128 / max_exact)
                         * (N_BUCKETS - max_exact)).astype(jnp.int32)
    large = jnp.minimum(large, N_BUCKETS - 1)
    return jnp.where(n < max_exact, n, large)


def _relative_bias_rows(rel_table):
    dist = jnp.arange(2 * CHUNK)
    by_dist = LOG2E * (rel_table[_t5_bucket(dist)] - rel_table[N_BUCKETS - 1])
    return by_dist.T[:, None, :].astype(F32)


def _pad_lanes(vec, width=LANES):
    return jnp.pad(vec, (0, width - vec.shape[0]))[None, :].astype(F32)


def kernel(x, meta_tokens, ln_gain, ln_bias, ffn1_w_gate, ffn1_w_up, ffn1_w_down, ffn2_w_gate, ffn2_w_up, ffn2_w_down, even_w_in, even_conv_w, even_conv_b, ssd_dt_bias, ssd_a_log, ssd_d_skip, ssd_norm_w, fox_f_bias, even_w_out, diff_w_qkv, diff_lambda_q1, diff_lambda_k1, diff_lambda_q2, diff_lambda_k2, diff_subln_w, diff_w_o, rel_bias_table):
    nb = x.shape[0]
    lead = jnp.concatenate([jnp.zeros((PADL, D_MODEL), x.dtype), meta_tokens.astype(x.dtype)], axis=0)

    def ln_params(l, i):
        return ln_gain[l, i][None, :], ln_bias[l, i][None, :]

    ffn1 = tuple(w.astype(BF16) for w in (ffn1_w_gate, ffn1_w_up, ffn1_w_down))
    ffn2 = tuple(w.astype(BF16) for w in (ffn2_w_gate, ffn2_w_up, ffn2_w_down))

    hf = _ffn_ln_first(x, lead, *ffn1, 0, *ln_params(0, 0))

    w_in = even_w_in[0]
    o_z, o_xbc = 0, SSD_D_INNER
    o_dt = o_xbc + SSD_CONV_CH
    o_q = o_dt + SSD_HEADS
    o_k, o_v = o_q + FOX_WIDTH, o_q + 2 * FOX_WIDTH
    o_f = o_q + 3 * FOX_WIDTH
    order = jnp.argsort(fox_f_bias[0])

    def by_head(w):
        return jnp.take(w.reshape((FOX_HEADS, HEAD_DIM) + w.shape[1:]), order, axis=0).reshape(w.shape)

    wt_q, wt_k, wt_v = (by_head(w.T.astype(BF16)) for w in
                        (w_in[:, o_q:o_k] * Q_SCALE, w_in[:, o_k:o_v], w_in[:, o_v:o_f]))
    w_small = jnp.concatenate(
        [w_in[:, o_dt:o_q], w_in[:, o_f:][:, order],
         jnp.zeros((D_MODEL, SMALL_W - SSD_HEADS - FOX_HEADS), w_in.dtype)], axis=1)
    w_even = jnp.concatenate(
        [w_in[:, o_z:o_dt].astype(BF16), wt_k.T, w_small.astype(BF16)], axis=1)
    wt_even = jnp.concatenate([wt_q, wt_v], axis=0)
    z, xbc, k, small, qt, vt = _proj(
        hf, w_even, wt_even, (SSD_D_INNER, SSD_CONV_CH, FOX_WIDTH, SMALL_W),
        (F32, F32, BF16, F32), (FOX_WIDTH, FOX_WIDTH), nb, "even_in_proj")

    sbias = _pad_lanes(jnp.concatenate([ssd_dt_bias[0], fox_f_bias[0][order]]))
    alog = _pad_lanes(ssd_a_log[0])
    dskip = jnp.repeat(ssd_d_skip[0], SSD_HEAD_DIM)[None, :].astype(F32)
    expand = np.zeros((SMALL_W, SSD_D_INNER), np.float32)
    expand[np.arange(SSD_D_INNER) // SSD_HEAD_DIM, np.arange(SSD_D_INNER)] = 1.0
    y, cfull = _ssd(xbc.reshape(nb, LP, SSD_CONV_CH), z.reshape(nb, LP, SSD_D_INNER),
                    small.reshape(nb, LP, SMALL_W), even_conv_w[0], even_conv_b[0][None, :],
                    sbias, alog, dskip, ssd_norm_w[0][None, :],
                    jnp.asarray(np.concatenate([expand, expand], axis=0), BF16))
    c2 = LOG2E * cfull
    k = k.reshape(nb, LP, FOX_WIDTH)
    ck = jnp.where(jnp.arange(LP)[None, :, None] < PADL, -NEG, c2)
    o = _fox_attention(_first_key_tiles(qt, k, c2), qt, k, vt, ck)
    w_out = even_w_out[0].astype(BF16)
    hf = _outproj_ln(hf, [y.reshape(nb * LP, SSD_D_INNER), o.reshape(nb * LP, FOX_WIDTH)],
                     [w_out[:SSD_D_INNER], by_head(w_out[SSD_D_INNER:])], *ln_params(0, 1),
                     name="even_out_proj_ln")
    hf = _ffn_ln(hf, *ffn2, 0, *ln_params(0, 2))

    hf = _ffn_ln(hf, *ffn1, 1, *ln_params(1, 0))
    qw = DIFF_HEADS * 2 * HEAD_DIM
    w_qkv = diff_w_qkv[0]
    wt_diff = jnp.concatenate([w_qkv[:, :qw] * Q_SCALE, w_qkv[:, 2 * qw:]], axis=1).T.astype(BF16)
    k, qt, vt = _proj(hf, w_qkv[:, qw:2 * qw].astype(BF16), wt_diff, (qw,), (BF16,),
                      (qw, DIFF_HEADS * LANES), nb, "diff_qkv_proj")
    lambda_init = 0.8 - 0.6 * math.exp(-0.3 * 1)
    lam_rows = jnp.concatenate(
        [_pad_lanes(diff_lambda_q1[0]), _pad_lanes(diff_lambda_k1[0]),
         _pad_lanes(diff_lambda_q2[0]), _pad_lanes(diff_lambda_k2[0]),
         jnp.zeros((4, LANES), F32)], axis=0)
    o = _diff_attention(qt, k.reshape(nb, LP, qw), vt, _relative_bias_rows(rel_bias_table),
                        lam_rows, diff_subln_w[0][None, :], lambda_init)
    hf = _outproj_ln(hf, [o.reshape(nb * LP, DIFF_HEADS * LANES)], [diff_w_o[0].astype(BF16)],
                     *ln_params(1, 1), name="diff_out_proj_ln")
    return _ffn_ln_final(hf.reshape(nb, LP, D_MODEL), *ffn2, 1, *ln_params(1, 2))
```

```python
import functools
import math

import numpy as np
import jax
import jax.numpy as jnp
from jax import lax
from jax.experimental import pallas as pl
from jax.experimental.pallas import tpu as pltpu

F32 = jnp.float32
BF16 = jnp.bfloat16

D_MODEL = 1024
SEQ = 8192
DEPTH = 2
N_META = 16
CHUNK = 128
SSD_D_INNER = 2048
SSD_HEAD_DIM = 64
SSD_HEADS = 32
SSD_GROUPS = 4
SSD_GROUP_W = SSD_D_INNER // SSD_GROUPS
SSD_STATE = 128
SSD_CONV = 4
SSD_CONV_CH = SSD_D_INNER + 2 * SSD_GROUPS * SSD_STATE
FOX_HEADS = 16
FOX_WIDTH = 1024
HEAD_DIM = 64
DIFF_HEADS = 8
N_BUCKETS = 32
D_FF = 2816
ALPHA = (2 * DEPTH) ** 0.25
LN_EPS = 1e-5
RMS_EPS = 1e-5
NEG = -1e30

LANES = 128
LP = 8448
PADL = LP - SEQ - N_META
ROW_TILE = 512
OUT_TILE = 256
ATT_TQ = 768
ATT_TK = 768
CONV_HALO = 8
SMALL_W = LANES
VMEM_LIMIT = 56 * 1024 * 1024

assert PADL % CHUNK == CHUNK - N_META
assert LP % ATT_TQ == 0 and ATT_TQ % ATT_TK == 0 and ATT_TK % LANES == 0
assert LP % CHUNK == 0 and (2 * LP) % ROW_TILE == 0
assert PADL <= OUT_TILE and (LP - OUT_TILE) == SEQ


def _const_spec(shape):
    nd = len(shape)
    return pl.BlockSpec(shape, lambda *_: (0,) * nd, pipeline_mode=pl.Buffered(1))


def _layer_spec(shape, layer):
    nd = len(shape)
    return pl.BlockSpec((None,) + tuple(shape), lambda *_: (layer,) + (0,) * nd,
                        pipeline_mode=pl.Buffered(1))


def _ffn_weight_specs(layer):
    return [_layer_spec((D_MODEL, D_FF), layer), _layer_spec((D_MODEL, D_FF), layer),
            _layer_spec((D_FF, D_MODEL), layer)]


def _layer_norm(r, g, b):
    mu = jnp.mean(r, axis=-1, keepdims=True)
    d = r - mu
    var = jnp.mean(d * d, axis=-1, keepdims=True)
    return d * lax.rsqrt(var + LN_EPS) * g + b


def _silu(x):
    return x / (1.0 + jnp.exp(-x))


def _softplus(x):
    return jnp.maximum(x, 0.0) + jnp.log(1.0 + jnp.exp(-jnp.abs(x)))


def _ffn_ln_body(h_ref, wg_ref, wu_ref, wd_ref, g_ref, b_ref, o_ref):
    h = h_ref[...]
    hb = h.astype(BF16)
    g = jnp.dot(hb, wg_ref[...], preferred_element_type=F32)
    u = jnp.dot(hb, wu_ref[...], preferred_element_type=F32)
    a = (_silu(g) * u).astype(BF16)
    y = jnp.dot(a, wd_ref[...], preferred_element_type=F32)
    o_ref[...] = _layer_norm(ALPHA * h + 0.5 * y, g_ref[...], b_ref[...])


def _ffn_ln(hf, wg, wu, wd, layer, g, b):
    rows = hf.shape[0]
    return pl.pallas_call(
        _ffn_ln_body,
        grid=(rows // ROW_TILE,),
        in_specs=[
            pl.BlockSpec((ROW_TILE, D_MODEL), lambda i: (i, 0)),
            *_ffn_weight_specs(layer),
            _const_spec((1, D_MODEL)),
            _const_spec((1, D_MODEL)),
        ],
        out_specs=pl.BlockSpec((ROW_TILE, D_MODEL), lambda i: (i, 0)),
        out_shape=jax.ShapeDtypeStruct((rows, D_MODEL), F32),
        compiler_params=pltpu.CompilerParams(
            dimension_semantics=("parallel",), vmem_limit_bytes=VMEM_LIMIT),
        name="ffn_ln",
    )(hf, wg, wu, wd, g, b)


def _ffn_ln_first_body(lead_ref, x_ref, wg_ref, wu_ref, wd_ref, g_ref, b_ref, o_ref, h_ref):
    h_ref[...] = jnp.where(pl.program_id(1) == 0, lead_ref[...], x_ref[...])
    _ffn_ln_body(h_ref, wg_ref, wu_ref, wd_ref, g_ref, b_ref, o_ref)


def _ffn_ln_first(x, lead, wg, wu, wd, layer, g, b):
    nb = x.shape[0]
    tiles = LP // OUT_TILE
    return pl.pallas_call(
        _ffn_ln_first_body,
        grid=(nb, tiles),
        in_specs=[
            _const_spec((OUT_TILE, D_MODEL)),
            pl.BlockSpec((None, OUT_TILE, D_MODEL), lambda bi, i: (bi, jnp.maximum(i - 1, 0), 0)),
            *_ffn_weight_specs(layer),
            _const_spec((1, D_MODEL)),
            _const_spec((1, D_MODEL)),
        ],
        out_specs=pl.BlockSpec((OUT_TILE, D_MODEL), lambda bi, i: (bi * tiles + i, 0)),
        out_shape=jax.ShapeDtypeStruct((nb * LP, D_MODEL), F32),
        scratch_shapes=[pltpu.VMEM((OUT_TILE, D_MODEL), F32)],
        compiler_params=pltpu.CompilerParams(
            dimension_semantics=("parallel", "arbitrary"), vmem_limit_bytes=VMEM_LIMIT),
        name="ffn_ln_first",
    )(lead, x, wg, wu, wd, g, b)


def _ffn_ln_final_body(h_ref, wg_ref, wu_ref, wd_ref, g_ref, b_ref, o_ref):
    @pl.when(pl.program_id(1) > 0)
    def _():
        _ffn_ln_body(h_ref, wg_ref, wu_ref, wd_ref, g_ref, b_ref, o_ref)


def _ffn_ln_final(h3, wg, wu, wd, layer, g, b):
    nb = h3.shape[0]
    return pl.pallas_call(
        _ffn_ln_final_body,
        grid=(nb, LP // OUT_TILE),
        in_specs=[
            pl.BlockSpec((None, OUT_TILE, D_MODEL), lambda bi, i: (bi, i, 0)),
            *_ffn_weight_specs(layer),
            _const_spec((1, D_MODEL)),
            _const_spec((1, D_MODEL)),
        ],
        out_specs=pl.BlockSpec((None, OUT_TILE, D_MODEL),
                               lambda bi, i: (bi, jnp.maximum(i - 1, 0), 0)),
        out_shape=jax.ShapeDtypeStruct((nb, SEQ, D_MODEL), F32),
        compiler_params=pltpu.CompilerParams(
            dimension_semantics=("arbitrary", "arbitrary"), vmem_limit_bytes=VMEM_LIMIT),
        name="ffn_ln_final",
    )(h3, wg, wu, wd, g, b)


PROJ_TILE = 256
assert LP % PROJ_TILE == 0


def _proj_body(n_t, h_ref, w_ref, wt_ref, *o_refs):
    hb = h_ref[...].astype(BF16)
    off = 0
    for o_ref in o_refs[:-n_t]:
        n = o_ref.shape[-1]
        o_ref[...] = jnp.dot(hb, w_ref[:, off:off + n],
                             preferred_element_type=F32).astype(o_ref.dtype)
        off += n
    off = 0
    for o_ref in o_refs[-n_t:]:
        n = o_ref.shape[0]
        o_ref[...] = lax.dot_general(wt_ref[off:off + n, :], hb, (((1,), (1,)), ((), ())),
                                     preferred_element_type=F32).astype(o_ref.dtype)
        off += n


def _proj(hf, w, wt, widths, dtypes, t_widths, nb, name):
    rows = hf.shape[0]
    tiles_per_batch = LP // PROJ_TILE
    out_specs = [pl.BlockSpec((PROJ_TILE, n), lambda i: (i, 0)) for n in widths]
    out_specs += [pl.BlockSpec((None, n, PROJ_TILE),
                               lambda i: (i // tiles_per_batch, 0, i % tiles_per_batch))
                  for n in t_widths]
    out_shape = [jax.ShapeDtypeStruct((rows, n), dt) for n, dt in zip(widths, dtypes)]
    out_shape += [jax.ShapeDtypeStruct((nb, n, LP), BF16) for n in t_widths]
    return pl.pallas_call(
        functools.partial(_proj_body, len(t_widths)),
        grid=(rows // PROJ_TILE,),
        in_specs=[pl.BlockSpec((PROJ_TILE, D_MODEL), lambda i: (i, 0)),
                  _const_spec(w.shape), _const_spec(wt.shape)],
        out_specs=out_specs,
        out_shape=out_shape,
        compiler_params=pltpu.CompilerParams(
            dimension_semantics=("parallel",), vmem_limit_bytes=VMEM_LIMIT),
        name=name,
    )(hf, w, wt)


def _outproj_ln_body(n_in, h_ref, *refs):
    a_refs = refs[:n_in]
    w_refs = refs[n_in:2 * n_in]
    g_ref, b_ref, o_ref = refs[2 * n_in:]
    m = jnp.dot(a_refs[0][...], w_refs[0][...], preferred_element_type=F32)
    for a_ref, w_ref in zip(a_refs[1:], w_refs[1:]):
        m = m + jnp.dot(a_ref[...], w_ref[...], preferred_element_type=F32)
    o_ref[...] = _layer_norm(ALPHA * h_ref[...] + m, g_ref[...], b_ref[...])


def _outproj_ln(hf, acts, ws, g, b, name):
    rows = hf.shape[0]
    n_in = len(acts)
    in_specs = [pl.BlockSpec((ROW_TILE, D_MODEL), lambda i: (i, 0))]
    in_specs += [pl.BlockSpec((ROW_TILE, a.shape[1]), lambda i: (i, 0)) for a in acts]
    in_specs += [_const_spec(w.shape) for w in ws]
    in_specs += [_const_spec((1, D_MODEL)), _const_spec((1, D_MODEL))]
    return pl.pallas_call(
        functools.partial(_outproj_ln_body, n_in),
        grid=(rows // ROW_TILE,),
        in_specs=in_specs,
        out_specs=pl.BlockSpec((ROW_TILE, D_MODEL), lambda i: (i, 0)),
        out_shape=jax.ShapeDtypeStruct((rows, D_MODEL), F32),
        compiler_params=pltpu.CompilerParams(
            dimension_semantics=("parallel",), vmem_limit_bytes=VMEM_LIMIT),
        name=name,
    )(hf, *acts, *ws, g, b)


def _split_dot(x, e2_ref):
    hi = x.astype(BF16)
    lo = (x - hi.astype(F32)).astype(BF16)
    return jnp.dot(jnp.concatenate([hi, lo], axis=1), e2_ref[...], preferred_element_type=F32)


def _ssd_body(xbc_ref, z_ref, small_ref, convw_ref, convb_ref, sbias_ref, alog_ref,
              dskip_ref, normw_ref, expand_ref, y_ref, c_ref,
              ext_ref, state_ref, carry_ref):
    c = pl.program_id(1)

    @pl.when(c == 0)
    def _():
        ext_ref[0:CONV_HALO, :] = jnp.zeros((CONV_HALO, SSD_CONV_CH), F32)
        state_ref[...] = jnp.zeros_like(state_ref)
        carry_ref[...] = jnp.zeros_like(carry_ref)

    row = lax.broadcasted_iota(jnp.int32, (CHUNK, 1), 0)
    valid = (c * CHUNK + row) >= PADL

    ext_ref[CONV_HALO:, :] = xbc_ref[...]

    @pl.when(c * CHUNK < PADL)
    def _():
        ext_ref[CONV_HALO:, :] = jnp.where(valid, ext_ref[CONV_HALO:, :], 0.0)

    ext = ext_ref[...]
    conv = convb_ref[...] + convw_ref[SSD_CONV - 1:SSD_CONV, :] * ext[CONV_HALO:, :]
    for k in range(SSD_CONV - 1):
        shifted = pltpu.roll(ext, SSD_CONV - 1 - k, 0)[CONV_HALO:, :]
        conv = conv + convw_ref[k:k + 1, :] * shifted
    ext_ref[0:CONV_HALO, :] = ext_ref[CHUNK:CHUNK + CONV_HALO, :]
    xc = _silu(conv)
    xs = xc[:, :SSD_D_INNER]
    bm = xc[:, SSD_D_INNER:SSD_D_INNER + SSD_GROUPS * SSD_STATE].astype(BF16)
    cm = xc[:, SSD_D_INNER + SSD_GROUPS * SSD_STATE:].astype(BF16)

    lane = lax.broadcasted_iota(jnp.int32, (CHUNK, SMALL_W), 1)
    is_dt = lane < SSD_HEADS
    is_f = (lane >= SSD_HEADS) & (lane < SSD_HEADS + FOX_HEADS)
    v = small_ref[...] + sbias_ref[...]
    dt = jnp.where(valid & is_dt, _softplus(v), 0.0)
    log_f = jnp.where(valid & is_f, -_softplus(-v), 0.0)
    neg_a = -jnp.exp(alog_ref[...])
    steps = jnp.where(is_dt, dt * neg_a, log_f)
    r_i = lax.broadcasted_iota(jnp.int32, (CHUNK, CHUNK), 0)
    c_i = lax.broadcasted_iota(jnp.int32, (CHUNK, CHUNK), 1)
    causal = r_i >= c_i
    tril = jnp.where(causal, 1.0, 0.0).astype(F32)
    cum = jnp.dot(tril, steps, preferred_element_type=F32,
                  precision=lax.Precision.HIGHEST)
    c_total = jnp.where(is_f, cum + carry_ref[...], 0.0)
    c_ref[...] = c_total
    carry_ref[...] = c_total[CHUNK - 1:CHUNK, :]

    a_last = cum[CHUNK - 1:CHUNK, :]
    dt_x = _split_dot(dt, expand_ref)
    ea_x = _split_dot(jnp.exp(cum), expand_ref)
    de_x = _split_dot(jnp.exp(a_last - cum), expand_ref)
    x_dt = xs * dt_x
    xb = x_dt.astype(BF16)
    xe = (x_dt * de_x).astype(BF16)
    cum_t = cum.T
    lane_p = lax.broadcasted_iota(jnp.int32, (CHUNK, 2 * SSD_HEAD_DIM), 1)
    first_half = lane_p < SSD_HEAD_DIM

    y_groups = []
    for g in range(SSD_GROUPS):
        gs = slice(g * SSD_GROUP_W, (g + 1) * SSD_GROUP_W)
        bg = bm[:, g * SSD_STATE:(g + 1) * SSD_STATE]
        cg = cm[:, g * SSD_STATE:(g + 1) * SSD_STATE]
        cb = lax.dot_general(cg, bg, (((1,), (1,)), ((), ())),
                             preferred_element_type=F32)
        pair_out = []
        for pr in range(SSD_GROUP_W // (2 * SSD_HEAD_DIM)):
            col0 = g * SSD_GROUP_W + pr * 2 * SSD_HEAD_DIM
            x_pair = xb[:, col0:col0 + 2 * SSD_HEAD_DIM]
            ys = []
            for j in range(2):
                hd = col0 // SSD_HEAD_DIM + j
                diff = cum[:, hd:hd + 1] - cum_t[hd:hd + 1, :]
                decay = jnp.exp(jnp.where(causal, diff, -jnp.inf))
                mat = (cb * decay).astype(BF16)
                ys.append(jnp.dot(mat, x_pair, preferred_element_type=F32))
            pair_out.append(jnp.where(first_half, ys[0], ys[1]))
        y_diag = jnp.concatenate(pair_out, axis=1)
        st = state_ref[g]
        y_off = jnp.dot(cg, st.astype(BF16), preferred_element_type=F32) * ea_x[:, gs]
        new = lax.dot_general(bg, xe[:, gs], (((0,), (0,)), ((), ())),
                              preferred_element_type=F32)
        state_ref[g] = st * ea_x[CHUNK - 1:CHUNK, gs] + new
        yg = y_diag + y_off + dskip_ref[:, gs] * xs[:, gs]
        yg = yg * _silu(z_ref[:, gs])
        ms = jnp.mean(yg * yg, axis=-1, keepdims=True)
        y_groups.append(yg * lax.rsqrt(ms + RMS_EPS) * normw_ref[:, gs])
    y_ref[...] = jnp.concatenate(y_groups, axis=1).astype(y_ref.dtype)


def _ssd(xbc, z, small, convw, convb, sbias, alog, dskip, normw, expand):
    nb = xbc.shape[0]
    row_spec = lambda w: pl.BlockSpec((None, CHUNK, w), lambda bi, ci: (bi, ci, 0))
    return pl.pallas_call(
        _ssd_body,
        grid=(nb, LP // CHUNK),
        in_specs=[row_spec(SSD_CONV_CH), row_spec(SSD_D_INNER), row_spec(SMALL_W),
                  _const_spec(convw.shape), _const_spec(convb.shape),
                  _const_spec(sbias.shape), _const_spec(alog.shape),
                  _const_spec(dskip.shape), _const_spec(normw.shape),
                  _const_spec(expand.shape)],
        out_specs=[row_spec(SSD_D_INNER), row_spec(SMALL_W)],
        out_shape=[jax.ShapeDtypeStruct((nb, LP, SSD_D_INNER), BF16),
                   jax.ShapeDtypeStruct((nb, LP, SMALL_W), F32)],
        scratch_shapes=[pltpu.VMEM((CHUNK + CONV_HALO, SSD_CONV_CH), F32),
                        pltpu.VMEM((SSD_GROUPS, SSD_STATE, SSD_GROUP_W), F32),
                        pltpu.VMEM((1, SMALL_W), F32)],
        compiler_params=pltpu.CompilerParams(
            dimension_semantics=("arbitrary", "arbitrary"), vmem_limit_bytes=VMEM_LIMIT),
        name="ssd",
    )(xbc, z, small, convw, convb, sbias, alog, dskip, normw, expand)


N_TK_CHUNKS = ATT_TK // LANES
N_TQ_CHUNKS = ATT_TQ // LANES
ONES_ROWS = 16
assert ATT_TQ == ATT_TK and PADL <= ATT_TK
LOG2E = math.log2(math.e)
Q_SCALE = LOG2E * HEAD_DIM ** -0.5


def _init_softmax_state(m_ref, acc_ref):
    m_ref[...] = jnp.full(m_ref.shape, NEG, F32)
    acc_ref[...] = jnp.zeros_like(acc_ref)


def _run_key_tiles(qb, first, scores, step, buf_a, buf_b):
    n_below = qb - first
    odd = n_below % 2

    @pl.when(odd == 0)
    def _():
        scores(first, buf_a)

    @pl.when(odd == 1)
    def _():
        scores(first, buf_b)
        step(first, buf_b, buf_a)

    def two_below_diagonal(i, carry):
        kb = first + odd + 2 * i
        step(kb, buf_a, buf_b)
        step(kb + 1, buf_b, buf_a)
        return carry

    lax.fori_loop(0, n_below // 2, two_below_diagonal, 0)
    step(qb, buf_a, None)


def _query_operands(qt_ref):
    zero_rows = jnp.zeros((HEAD_DIM, ATT_TQ), BF16)
    return [jnp.concatenate([qt_ref[:HEAD_DIM, :], zero_rows], axis=0),
            jnp.concatenate([zero_rows, qt_ref[HEAD_DIM:, :]], axis=0)]


def _value_rows(vt_ref, row0, n_rows, start):
    vt = vt_ref[row0:row0 + n_rows, pl.ds(start, ATT_TK)]
    return jnp.concatenate([vt, jnp.ones((ONES_ROWS, ATT_TK), BF16)], axis=0)


def _softmax_update(s_t, col_max, vt_aug, m_ref, acc_ref, idx, cols=slice(None)):
    m_old = m_ref[idx, :, cols]
    m_new = jnp.maximum(m_old, col_max)
    alpha = jnp.exp2(m_old - m_new)
    p_t = jnp.exp2(s_t - m_new).astype(BF16)
    pv = jnp.dot(vt_aug, p_t, preferred_element_type=F32)
    acc_ref[idx, :, cols] = alpha * acc_ref[idx, :, cols] + pv
    m_ref[idx, :, cols] = m_new


DIAG_BLOCK = 256


def _diagonal_update(s_ref, bias, vt_ref, row0, n_rows, start, m_ref, acc_ref, idx):
    kpos = lax.broadcasted_iota(jnp.int32, (DIAG_BLOCK, DIAG_BLOCK), 0)
    qpos = lax.broadcasted_iota(jnp.int32, (DIAG_BLOCK, DIAG_BLOCK), 1)
    causal = kpos <= qpos
    for b in range(ATT_TQ // DIAG_BLOCK):
        keys = (b + 1) * DIAG_BLOCK
        cols = slice(b * DIAG_BLOCK, keys)
        s_t = s_ref[:keys, cols]
        if bias is not None:
            s_t = s_t + bias[:keys, cols]
        last = jnp.where(causal, s_t[keys - DIAG_BLOCK:], NEG)
        s_t = last if b == 0 else jnp.concatenate([s_t[:keys - DIAG_BLOCK], last], axis=0)
        col_max = jnp.max(s_t, axis=0, keepdims=True)
        vt = vt_ref[row0:row0 + n_rows, pl.ds(start, keys)]
        vt_aug = jnp.concatenate([vt, jnp.ones((ONES_ROWS, keys), BF16)], axis=0)
        _softmax_update(s_t, col_max, vt_aug, m_ref, acc_ref, idx, cols)


def _fox_body(first_ref, qt_ref, k_ref, vt_ref, c_ref, o_ref, m_ref, acc_ref,
              sa_ref, pa_ref, sb_ref, pb_ref):
    qb = pl.program_id(2)
    n_q = pl.num_programs(2)
    first = first_ref[(pl.program_id(0) * pl.num_programs(1) + pl.program_id(1)) * n_q + qb]
    first = jnp.minimum(first, qb)
    _init_softmax_state(m_ref, acc_ref)
    buf_a, buf_b = (sa_ref, pa_ref), (sb_ref, pb_ref)
    q_heads = _query_operands(qt_ref)
    lane = lax.broadcasted_iota(jnp.int32, (ATT_TK, LANES), 1)
    first_lane = SSD_HEADS + 2 * pl.program_id(1)

    def scores(kb, buf):
        s_dst, max_dst = buf
        start = pl.multiple_of(kb * ATT_TK, ATT_TK)
        k_rows = k_ref[pl.ds(start, ATT_TK), :]
        c_rows = c_ref[pl.ds(start, ATT_TK), :]
        for j in range(2):
            ck = jnp.sum(jnp.where(lane == first_lane + j, c_rows, 0.0), axis=1, keepdims=True)
            s_t = jnp.dot(k_rows, q_heads[j], preferred_element_type=F32) - ck
            s_dst[j] = s_t
            max_dst[j] = jnp.max(s_t, axis=0, keepdims=True)

    def step(kb, src, nxt):
        s_src, max_src = src
        start = pl.multiple_of(kb * ATT_TK, ATT_TK)
        if nxt is not None:
            scores(kb + 1, nxt)
        for j in range(2):
            if nxt is None:
                _diagonal_update(s_src.at[j], None, vt_ref, j * HEAD_DIM, HEAD_DIM, start,
                                 m_ref, acc_ref, j)
            else:
                vt_aug = _value_rows(vt_ref, j * HEAD_DIM, HEAD_DIM, start)
                _softmax_update(s_src[j], max_src[j], vt_aug, m_ref, acc_ref, j)

    _run_key_tiles(qb, first, scores, step, buf_a, buf_b)

    o_t = jnp.concatenate(
        [acc_ref[j, :HEAD_DIM, :] / acc_ref[j, HEAD_DIM:HEAD_DIM + 1, :] for j in range(2)], axis=0)
    o_ref[...] = o_t.T.astype(o_ref.dtype)


def _fox_attention(first, qt, k, vt, c):
    nb = qt.shape[0]
    pairs = FOX_HEADS // 2
    acc_rows = HEAD_DIM + ONES_ROWS
    return pl.pallas_call(
        _fox_body,
        grid_spec=pltpu.PrefetchScalarGridSpec(
            num_scalar_prefetch=1,
            grid=(nb, pairs, LP // ATT_TQ),
            in_specs=[
                pl.BlockSpec((None, LANES, ATT_TQ), lambda bi, hp, qi, _: (bi, hp, qi)),
                pl.BlockSpec((None, LP, LANES), lambda bi, hp, qi, _: (bi, 0, hp)),
                pl.BlockSpec((None, LANES, LP), lambda bi, hp, qi, _: (bi, hp, 0)),
                pl.BlockSpec((None, LP, SMALL_W), lambda bi, hp, qi, _: (bi, 0, 0)),
            ],
            out_specs=pl.BlockSpec((None, ATT_TQ, LANES), lambda bi, hp, qi, _: (bi, qi, hp)),
            scratch_shapes=[pltpu.VMEM((2, 1, ATT_TQ), F32),
                            pltpu.VMEM((2, acc_rows, ATT_TQ), F32)]
                           + [pltpu.VMEM((2, ATT_TK, ATT_TQ), F32),
                              pltpu.VMEM((2, 1, ATT_TQ), F32)] * 2),
        out_shape=jax.ShapeDtypeStruct((nb, LP, FOX_WIDTH), BF16),
        compiler_params=pltpu.CompilerParams(
            dimension_semantics=("parallel", "parallel", "arbitrary"),
            vmem_limit_bytes=VMEM_LIMIT),
        name="fox_attention",
    )(first, qt, k, vt, c)


FORGOTTEN_LOG2 = 160.0
NORM_SLACK = 1.01


def _first_key_tiles(qt, k, c2):
    nb = qt.shape[0]
    n_q = LP // ATT_TQ
    heads = jnp.asarray(np.arange(FOX_WIDTH)[:, None] // HEAD_DIM == np.arange(FOX_HEADS)[None, :], F32)
    q_sq = jnp.einsum("bdl,dh->bhl", (qt * qt).astype(F32), heads)
    k_sq = jnp.einsum("bld,dh->blh", (k * k).astype(F32), heads)
    q_max = NORM_SLACK * jnp.sqrt(jnp.max(q_sq.reshape(nb, FOX_HEADS, n_q, ATT_TQ), axis=-1))
    k_max = NORM_SLACK * jnp.sqrt(jnp.max(k_sq, axis=1))
    c_heads = c2[:, :, SSD_HEADS:SSD_HEADS + FOX_HEADS]
    c_query = c_heads[:, ::ATT_TQ, :].transpose(0, 2, 1)
    c_key = c_heads[:, ATT_TK - 1::ATT_TK, :].transpose(0, 2, 1)
    bound = (2.0 * q_max * k_max[:, :, None])[:, :, :, None] - (c_key[:, :, None, :] - c_query[:, :, :, None])
    needed = bound >= -FORGOTTEN_LOG2
    needed = needed.reshape(nb, FOX_HEADS // 2, 2, n_q, n_q).any(axis=2)
    needed = needed | (jnp.arange(n_q)[None, :] >= jnp.arange(n_q)[:, None])
    return jnp.argmax(needed, axis=-1).astype(jnp.int32).reshape(-1)


def _bias_tiles(bias_ref):
    rows = jnp.broadcast_to(bias_ref[...], (CHUNK, 2 * CHUNK))
    rolled = pltpu.roll(rows, 0, 1, stride=1, stride_axis=0)
    return rolled[:, :CHUNK], rolled[:, CHUNK:]


def _diagonal_bias(tiles):
    zeros = jnp.zeros((CHUNK, CHUNK), F32)
    rows = []
    for a in range(N_TK_CHUNKS):
        rows.append(jnp.concatenate(
            [tiles[b - a] if b - a in (0, 1) else zeros for b in range(N_TQ_CHUNKS)], axis=1))
    return jnp.concatenate(rows, axis=0)


def _diff_body(lambda_init, qt_ref, k_ref, vt_ref, bias_ref, lam_ref, subln_ref, o_ref,
               m_ref, acc_ref, tiles_ref, sa_ref, pa_ref, sb_ref, pb_ref):
    qb = pl.program_id(2)
    _init_softmax_state(m_ref, acc_ref)
    buf_a, buf_b = (sa_ref, pa_ref), (sb_ref, pb_ref)
    tile0, tile1 = _bias_tiles(bias_ref)
    tiles_ref[0] = tile0
    tiles_ref[1] = tile1
    q_parts = _query_operands(qt_ref)
    pad_rows = -(-PADL // CHUNK) * CHUNK
    corner0 = ATT_TK - CHUNK

    def scores(kb, buf):
        s_dst, max_dst = buf
        start = pl.multiple_of(kb * ATT_TK, ATT_TK)
        k_rows = k_ref[pl.ds(start, ATT_TK), :]
        kpos = lax.broadcasted_iota(jnp.int32, (pad_rows, ATT_TQ), 0)
        not_padding = (kpos >= PADL) | (kb > 0)
        for j in range(2):
            s_t = jnp.dot(k_rows, q_parts[j], preferred_element_type=F32)
            s_t = jnp.concatenate(
                [jnp.where(not_padding, s_t[:pad_rows], NEG), s_t[pad_rows:]], axis=0)
            s_dst[j] = s_t
            max_dst[j] = jnp.max(s_t, axis=0, keepdims=True)

    def step(kb, src, nxt):
        s_src, max_src = src
        start = pl.multiple_of(kb * ATT_TK, ATT_TK)
        diagonal = nxt is None
        if diagonal:
            bias = _diagonal_bias((tiles_ref[0], tiles_ref[1]))
            for j in range(2):
                _diagonal_update(s_src.at[j], bias, vt_ref, 0, LANES, start, m_ref, acc_ref, j)
            return
        scores(kb + 1, nxt)
        corner = jnp.where(kb == qb - 1, 1.0, 0.0) * tiles_ref[1]
        vt_aug = _value_rows(vt_ref, 0, LANES, start)
        for j in range(2):
            s_t = s_src[j]
            near = s_t[corner0:, :CHUNK] + corner
            s_t = jnp.concatenate(
                [s_t[:corner0], jnp.concatenate([near, s_t[corner0:, CHUNK:]], axis=1)], axis=0)
            first = jnp.maximum(jnp.max(s_t[:corner0, :CHUNK], axis=0, keepdims=True),
                                jnp.max(near, axis=0, keepdims=True))
            col_max = jnp.concatenate([first, max_src[j][:, CHUNK:]], axis=1)
            _softmax_update(s_t, col_max, vt_aug, m_ref, acc_ref, j)

    _run_key_tiles(qb, 0, scores, step, buf_a, buf_b)

    lam1 = jnp.exp(jnp.sum(lam_ref[0:1, :] * lam_ref[1:2, :], axis=-1, keepdims=True))
    lam2 = jnp.exp(jnp.sum(lam_ref[2:3, :] * lam_ref[3:4, :], axis=-1, keepdims=True))
    lam = lam1 - lam2 + lambda_init
    o_t = (acc_ref[0, :LANES, :] / acc_ref[0, LANES:LANES + 1, :]
           - lam * (acc_ref[1, :LANES, :] / acc_ref[1, LANES:LANES + 1, :]))
    o = o_t.T
    ms = jnp.mean(o * o, axis=-1, keepdims=True)
    o = o * lax.rsqrt(ms + RMS_EPS) * subln_ref[...] * (1.0 - lambda_init)
    o_ref[...] = o.astype(o_ref.dtype)


def _diff_attention(qt, k, vt, bias_rows, lam_rows, subln, lambda_init):
    nb = qt.shape[0]
    acc_rows = LANES + ONES_ROWS
    return pl.pallas_call(
        functools.partial(_diff_body, lambda_init),
        grid=(nb, DIFF_HEADS, LP // ATT_TQ),
        in_specs=[
            pl.BlockSpec((None, LANES, ATT_TQ), lambda bi, hd, qi: (bi, hd, qi)),
            pl.BlockSpec((None, LP, LANES), lambda bi, hd, qi: (bi, 0, hd)),
            pl.BlockSpec((None, LANES, LP), lambda bi, hd, qi: (bi, hd, 0)),
            pl.BlockSpec((None, 1, 2 * CHUNK), lambda bi, hd, qi: (hd, 0, 0)),
            _const_spec(lam_rows.shape),
            _const_spec(subln.shape),
        ],
        out_specs=pl.BlockSpec((None, ATT_TQ, LANES), lambda bi, hd, qi: (bi, qi, hd)),
        out_shape=jax.ShapeDtypeStruct((nb, LP, DIFF_HEADS * LANES), BF16),
        scratch_shapes=[pltpu.VMEM((2, 1, ATT_TQ), F32),
                        pltpu.VMEM((2, acc_rows, ATT_TQ), F32),
                        pltpu.VMEM((2, CHUNK, CHUNK), F32)]
                       + [pltpu.VMEM((2, ATT_TK, ATT_TQ), F32),
                          pltpu.VMEM((2, 1, ATT_TQ), F32)] * 2,
        compiler_params=pltpu.CompilerParams(
            dimension_semantics=("parallel", "parallel", "arbitrary"),
            vmem_limit_bytes=VMEM_LIMIT),
        name="diff_attention",
    )(qt, k, vt, bias_rows, lam_rows, subln)


def _t5_bucket(n):
    max_exact = N_BUCKETS // 2
    nf = jnp.maximum(n, 1).astype(F32)
    large = max_exact + (jnp.log(nf / max_exact) / math.log(128 / max_exact)
                         * (N_BUCKETS - max_exact)).astype(jnp.int32)
    large = jnp.minimum(large, N_BUCKETS - 1)
    return jnp.where(n < max_exact, n, large)


def _relative_bias_rows(rel_table):
    dist = jnp.arange(2 * CHUNK)
    by_dist = LOG2E * (rel_table[_t5_bucket(dist)] - rel_table[N_BUCKETS - 1])
    return by_dist.T[:, None, :].astype(F32)


def _pad_lanes(vec, width=LANES):
    return jnp.pad(vec, (0, width - vec.shape[0]))[None, :].astype(F32)


def kernel(x, meta_tokens, ln_gain, ln_bias, ffn1_w_gate, ffn1_w_up, ffn1_w_down, ffn2_w_gate, ffn2_w_up, ffn2_w_down, even_w_in, even_conv_w, even_conv_b, ssd_dt_bias, ssd_a_log, ssd_d_skip, ssd_norm_w, fox_f_bias, even_w_out, diff_w_qkv, diff_lambda_q1, diff_lambda_k1, diff_lambda_q2, diff_lambda_k2, diff_subln_w, diff_w_o, rel_bias_table):
    nb = x.shape[0]
    lead = jnp.concatenate([jnp.zeros((PADL, D_MODEL), x.dtype), meta_tokens.astype(x.dtype)], axis=0)

    def ln_params(l, i):
        return ln_gain[l, i][None, :], ln_bias[l, i][None, :]

    ffn1 = tuple(w.astype(BF16) for w in (ffn1_w_gate, ffn1_w_up, ffn1_w_down))
    ffn2 = tuple(w.astype(BF16) for w in (ffn2_w_gate, ffn2_w_up, ffn2_w_down))

    hf = _ffn_ln_first(x, lead, *ffn1, 0, *ln_params(0, 0))

    w_in = even_w_in[0]
    o_z, o_xbc = 0, SSD_D_INNER
    o_dt = o_xbc + SSD_CONV_CH
    o_q = o_dt + SSD_HEADS
    o_k, o_v = o_q + FOX_WIDTH, o_q + 2 * FOX_WIDTH
    o_f = o_q + 3 * FOX_WIDTH
    bias = fox_f_bias[0]
    head = jnp.arange(FOX_HEADS)
    before = (bias[None, :] < bias[:, None]) | ((bias[None, :] == bias[:, None])
                                                & (head[None, :] < head[:, None]))
    rank = jnp.sum(before, axis=1)
    order = jnp.argmax(rank[None, :] == head[:, None], axis=1)

    def by_head(w):
        return jnp.take(w.reshape((FOX_HEADS, HEAD_DIM) + w.shape[1:]), order, axis=0).reshape(w.shape)

    wt_q, wt_k, wt_v = (by_head(w.T.astype(BF16)) for w in
                        (w_in[:, o_q:o_k] * Q_SCALE, w_in[:, o_k:o_v], w_in[:, o_v:o_f]))
    w_small = jnp.concatenate(
        [w_in[:, o_dt:o_q], w_in[:, o_f:][:, order],
         jnp.zeros((D_MODEL, SMALL_W - SSD_HEADS - FOX_HEADS), w_in.dtype)], axis=1)
    w_even = jnp.concatenate(
        [w_in[:, o_z:o_dt].astype(BF16), wt_k.T, w_small.astype(BF16)], axis=1)
    wt_even = jnp.concatenate([wt_q, wt_v], axis=0)
    z, xbc, k, small, qt, vt = _proj(
        hf, w_even, wt_even, (SSD_D_INNER, SSD_CONV_CH, FOX_WIDTH, SMALL_W),
        (F32, F32, BF16, F32), (FOX_WIDTH, FOX_WIDTH), nb, "even_in_proj")

    sbias = _pad_lanes(jnp.concatenate([ssd_dt_bias[0], fox_f_bias[0][order]]))
    alog = _pad_lanes(ssd_a_log[0])
    dskip = jnp.repeat(ssd_d_skip[0], SSD_HEAD_DIM)[None, :].astype(F32)
    expand = np.zeros((SMALL_W, SSD_D_INNER), np.float32)
    expand[np.arange(SSD_D_INNER) // SSD_HEAD_DIM, np.arange(SSD_D_INNER)] = 1.0
    y, cfull = _ssd(xbc.reshape(nb, LP, SSD_CONV_CH), z.reshape(nb, LP, SSD_D_INNER),
                    small.reshape(nb, LP, SMALL_W), even_conv_w[0], even_conv_b[0][None, :],
                    sbias, alog, dskip, ssd_norm_w[0][None, :],
                    jnp.asarray(np.concatenate([expand, expand], axis=0), BF16))
    c2 = LOG2E * cfull
    k = k.reshape(nb, LP, FOX_WIDTH)
    ck = jnp.where(jnp.arange(LP)[None, :, None] < PADL, -NEG, c2)
    o = _fox_attention(_first_key_tiles(qt, k, c2), qt, k, vt, ck)
    w_out = even_w_out[0].astype(BF16)
    hf = _outproj_ln(hf, [y.reshape(nb * LP, SSD_D_INNER), o.reshape(nb * LP, FOX_WIDTH)],
                     [w_out[:SSD_D_INNER], by_head(w_out[SSD_D_INNER:])], *ln_params(0, 1),
                     name="even_out_proj_ln")
    hf = _ffn_ln(hf, *ffn2, 0, *ln_params(0, 2))

    hf = _ffn_ln(hf, *ffn1, 1, *ln_params(1, 0))
    qw = DIFF_HEADS * 2 * HEAD_DIM
    w_qkv = diff_w_qkv[0]
    wt_diff = jnp.concatenate([w_qkv[:, :qw] * Q_SCALE, w_qkv[:, 2 * qw:]], axis=1).T.astype(BF16)
    k, qt, vt = _proj(hf, w_qkv[:, qw:2 * qw].astype(BF16), wt_diff, (qw,), (BF16,),
                      (qw, DIFF_HEADS * LANES), nb, "diff_qkv_proj")
    lambda_init = 0.8 - 0.6 * math.exp(-0.3 * 1)
    lam_rows = jnp.concatenate(
        [_pad_lanes(diff_lambda_q1[0]), _pad_lanes(diff_lambda_k1[0]),
         _pad_lanes(diff_lambda_q2[0]), _pad_lanes(diff_lambda_k2[0]),
         jnp.zeros((4, LANES), F32)], axis=0)
    o = _diff_attention(qt, k.reshape(nb, LP, qw), vt, _relative_bias_rows(rel_bias_table),
                        lam_rows, diff_subln_w[0][None, :], lambda_init)
    hf = _outproj_ln(hf, [o.reshape(nb * LP, DIFF_HEADS * LANES)], [diff_w_o[0].astype(BF16)],
                     *ln_params(1, 1), name="diff_out_proj_ln")
    return _ffn_ln_final(hf.reshape(nb, LP, D_MODEL), *ffn2, 1, *ln_params(1, 2))
```

```python
import functools
import math

import numpy as np
import jax
import jax.numpy as jnp
from jax import lax
from jax.experimental import pallas as pl
from jax.experimental.pallas import tpu as pltpu

F32 = jnp.float32
BF16 = jnp.bfloat16

D_MODEL = 1024
SEQ = 8192
DEPTH = 2
N_META = 16
CHUNK = 128
SSD_D_INNER = 2048
SSD_HEAD_DIM = 64
SSD_HEADS = 32
SSD_GROUPS = 4
SSD_GROUP_W = SSD_D_INNER // SSD_GROUPS
SSD_STATE = 128
SSD_CONV = 4
SSD_CONV_CH = SSD_D_INNER + 2 * SSD_GROUPS * SSD_STATE
FOX_HEADS = 16
FOX_WIDTH = 1024
HEAD_DIM = 64
DIFF_HEADS = 8
N_BUCKETS = 32
D_FF = 2816
ALPHA = (2 * DEPTH) ** 0.25
LN_EPS = 1e-5
RMS_EPS = 1e-5
NEG = -1e30

LANES = 128
LP = 8448
PADL = LP - SEQ - N_META
ROW_TILE = 512
OUT_TILE = 256
ATT_TQ = 768
ATT_TK = 768
CONV_HALO = 8
SMALL_W = LANES
VMEM_LIMIT = 56 * 1024 * 1024

assert PADL % CHUNK == CHUNK - N_META
assert LP % ATT_TQ == 0 and ATT_TQ % ATT_TK == 0 and ATT_TK % LANES == 0
assert LP % CHUNK == 0 and (2 * LP) % ROW_TILE == 0
assert PADL <= OUT_TILE and (LP - OUT_TILE) == SEQ


def _const_spec(shape):
    nd = len(shape)
    return pl.BlockSpec(shape, lambda *_: (0,) * nd, pipeline_mode=pl.Buffered(1))


def _layer_spec(shape, layer):
    nd = len(shape)
    return pl.BlockSpec((None,) + tuple(shape), lambda *_: (layer,) + (0,) * nd,
                        pipeline_mode=pl.Buffered(1))


def _ffn_weight_specs(layer):
    return [_layer_spec((D_MODEL, D_FF), layer), _layer_spec((D_MODEL, D_FF), layer),
            _layer_spec((D_FF, D_MODEL), layer)]


def _layer_norm(r, g, b):
    mu = jnp.mean(r, axis=-1, keepdims=True)
    d = r - mu
    var = jnp.mean(d * d, axis=-1, keepdims=True)
    return d * lax.rsqrt(var + LN_EPS) * g + b


def _silu(x):
    return x / (1.0 + jnp.exp(-x))


def _softplus(x):
    return jnp.maximum(x, 0.0) + jnp.log(1.0 + jnp.exp(-jnp.abs(x)))


def _ffn_ln_body(h_ref, wg_ref, wu_ref, wd_ref, g_ref, b_ref, o_ref):
    h = h_ref[...]
    hb = h.astype(BF16)
    g = jnp.dot(hb, wg_ref[...], preferred_element_type=F32)
    u = jnp.dot(hb, wu_ref[...], preferred_element_type=F32)
    a = (_silu(g) * u).astype(BF16)
    y = jnp.dot(a, wd_ref[...], preferred_element_type=F32)
    o_ref[...] = _layer_norm(ALPHA * h + 0.5 * y, g_ref[...], b_ref[...])


def _ffn_ln(hf, wg, wu, wd, layer, g, b):
    rows = hf.shape[0]
    return pl.pallas_call(
        _ffn_ln_body,
        grid=(rows // ROW_TILE,),
        in_specs=[
            pl.BlockSpec((ROW_TILE, D_MODEL), lambda i: (i, 0)),
            *_ffn_weight_specs(layer),
            _const_spec((1, D_MODEL)),
            _const_spec((1, D_MODEL)),
        ],
        out_specs=pl.BlockSpec((ROW_TILE, D_MODEL), lambda i: (i, 0)),
        out_shape=jax.ShapeDtypeStruct((rows, D_MODEL), F32),
        compiler_params=pltpu.CompilerParams(
            dimension_semantics=("parallel",), vmem_limit_bytes=VMEM_LIMIT),
        name="ffn_ln",
    )(hf, wg, wu, wd, g, b)


def _ffn_ln_first_body(lead_ref, x_ref, wg_ref, wu_ref, wd_ref, g_ref, b_ref, o_ref, h_ref):
    h_ref[...] = jnp.where(pl.program_id(1) == 0, lead_ref[...], x_ref[...])
    _ffn_ln_body(h_ref, wg_ref, wu_ref, wd_ref, g_ref, b_ref, o_ref)


def _ffn_ln_first(x, lead, wg, wu, wd, layer, g, b):
    nb = x.shape[0]
    tiles = LP // OUT_TILE
    return pl.pallas_call(
        _ffn_ln_first_body,
        grid=(nb, tiles),
        in_specs=[
            _const_spec((OUT_TILE, D_MODEL)),
            pl.BlockSpec((None, OUT_TILE, D_MODEL), lambda bi, i: (bi, jnp.maximum(i - 1, 0), 0)),
            *_ffn_weight_specs(layer),
            _const_spec((1, D_MODEL)),
            _const_spec((1, D_MODEL)),
        ],
        out_specs=pl.BlockSpec((OUT_TILE, D_MODEL), lambda bi, i: (bi * tiles + i, 0)),
        out_shape=jax.ShapeDtypeStruct((nb * LP, D_MODEL), F32),
        scratch_shapes=[pltpu.VMEM((OUT_TILE, D_MODEL), F32)],
        compiler_params=pltpu.CompilerParams(
            dimension_semantics=("parallel", "arbitrary"), vmem_limit_bytes=VMEM_LIMIT),
        name="ffn_ln_first",
    )(lead, x, wg, wu, wd, g, b)


def _ffn_ln_final_body(h_ref, wg_ref, wu_ref, wd_ref, g_ref, b_ref, o_ref):
    @pl.when(pl.program_id(1) > 0)
    def _():
        _ffn_ln_body(h_ref, wg_ref, wu_ref, wd_ref, g_ref, b_ref, o_ref)


def _ffn_ln_final(h3, wg, wu, wd, layer, g, b):
    nb = h3.shape[0]
    return pl.pallas_call(
        _ffn_ln_final_body,
        grid=(nb, LP // OUT_TILE),
        in_specs=[
            pl.BlockSpec((None, OUT_TILE, D_MODEL), lambda bi, i: (bi, i, 0)),
            *_ffn_weight_specs(layer),
            _const_spec((1, D_MODEL)),
            _const_spec((1, D_MODEL)),
        ],
        out_specs=pl.BlockSpec((None, OUT_TILE, D_MODEL),
                               lambda bi, i: (bi, jnp.maximum(i - 1, 0), 0)),
        out_shape=jax.ShapeDtypeStruct((nb, SEQ, D_MODEL), F32),
        compiler_params=pltpu.CompilerParams(
            dimension_semantics=("arbitrary", "arbitrary"), vmem_limit_bytes=VMEM_LIMIT),
        name="ffn_ln_final",
    )(h3, wg, wu, wd, g, b)


PROJ_TILE = 256
assert LP % PROJ_TILE == 0


def _proj_body(n_plain, n_t, norms, h_ref, w_ref, wt_ref, *refs):
    o_refs = refs[-(n_plain + n_t + (2 if norms else 0)):]
    hb = h_ref[...].astype(BF16)
    off = 0
    for idx, o_ref in enumerate(o_refs[:n_plain]):
        n = o_ref.shape[-1]
        out = jnp.dot(hb, w_ref[:, off:off + n], preferred_element_type=F32).astype(o_ref.dtype)
        o_ref[...] = out
        if norms and idx == norms[0]:
            o_refs[-2][...] = jnp.dot(out * out, refs[0][...], preferred_element_type=F32)
        off += n
    off = 0
    for idx, o_ref in enumerate(o_refs[n_plain:n_plain + n_t]):
        n = o_ref.shape[0]
        out = lax.dot_general(wt_ref[off:off + n, :], hb, (((1,), (1,)), ((), ())),
                              preferred_element_type=F32).astype(o_ref.dtype)
        o_ref[...] = out
        if norms and idx == norms[1]:
            o_refs[-1][...] = jnp.dot(refs[1][...], out * out, preferred_element_type=F32)
        off += n


def _proj(hf, w, wt, widths, dtypes, t_widths, nb, name, norms=None):
    rows = hf.shape[0]
    tiles_per_batch = LP // PROJ_TILE
    t_spec = lambda n: pl.BlockSpec((None, n, PROJ_TILE),
                                    lambda i: (i // tiles_per_batch, 0, i % tiles_per_batch))
    out_specs = [pl.BlockSpec((PROJ_TILE, n), lambda i: (i, 0)) for n in widths]
    out_specs += [t_spec(n) for n in t_widths]
    out_shape = [jax.ShapeDtypeStruct((rows, n), dt) for n, dt in zip(widths, dtypes)]
    out_shape += [jax.ShapeDtypeStruct((nb, n, LP), BF16) for n in t_widths]
    extra_in, extra_specs = [], []
    if norms:
        width = widths[norms[0]]
        indicator = np.arange(width)[:, None] // HEAD_DIM == np.arange(LANES)[None, :]
        extra_in = [jnp.asarray(indicator, BF16), jnp.asarray(indicator.T, BF16)]
        extra_specs = [_const_spec(a.shape) for a in extra_in]
        out_specs += [pl.BlockSpec((PROJ_TILE, LANES), lambda i: (i, 0)), t_spec(LANES)]
        out_shape += [jax.ShapeDtypeStruct((rows, LANES), F32),
                      jax.ShapeDtypeStruct((nb, LANES, LP), F32)]
    return pl.pallas_call(
        functools.partial(_proj_body, len(widths), len(t_widths), norms),
        grid=(rows // PROJ_TILE,),
        in_specs=[pl.BlockSpec((PROJ_TILE, D_MODEL), lambda i: (i, 0)),
                  _const_spec(w.shape), _const_spec(wt.shape)] + extra_specs,
        out_specs=out_specs,
        out_shape=out_shape,
        compiler_params=pltpu.CompilerParams(
            dimension_semantics=("parallel",), vmem_limit_bytes=VMEM_LIMIT),
        name=name,
    )(hf, w, wt, *extra_in)


def _outproj_ln_body(n_in, h_ref, *refs):
    a_refs = refs[:n_in]
    w_refs = refs[n_in:2 * n_in]
    g_ref, b_ref, o_ref = refs[2 * n_in:]
    m = jnp.dot(a_refs[0][...], w_refs[0][...], preferred_element_type=F32)
    for a_ref, w_ref in zip(a_refs[1:], w_refs[1:]):
        m = m + jnp.dot(a_ref[...], w_ref[...], preferred_element_type=F32)
    o_ref[...] = _layer_norm(ALPHA * h_ref[...] + m, g_ref[...], b_ref[...])


def _outproj_ln(hf, acts, ws, g, b, name):
    rows = hf.shape[0]
    n_in = len(acts)
    in_specs = [pl.BlockSpec((ROW_TILE, D_MODEL), lambda i: (i, 0))]
    in_specs += [pl.BlockSpec((ROW_TILE, a.shape[1]), lambda i: (i, 0)) for a in acts]
    in_specs += [_const_spec(w.shape) for w in ws]
    in_specs += [_const_spec((1, D_MODEL)), _const_spec((1, D_MODEL))]
    return pl.pallas_call(
        functools.partial(_outproj_ln_body, n_in),
        grid=(rows // ROW_TILE,),
        in_specs=in_specs,
        out_specs=pl.BlockSpec((ROW_TILE, D_MODEL), lambda i: (i, 0)),
        out_shape=jax.ShapeDtypeStruct((rows, D_MODEL), F32),
        compiler_params=pltpu.CompilerParams(
            dimension_semantics=("parallel",), vmem_limit_bytes=VMEM_LIMIT),
        name=name,
    )(hf, *acts, *ws, g, b)


def _split_dot(x, e2_ref):
    hi = x.astype(BF16)
    lo = (x - hi.astype(F32)).astype(BF16)
    return jnp.dot(jnp.concatenate([hi, lo], axis=1), e2_ref[...], preferred_element_type=F32)


def _ssd_body(xbc_ref, z_ref, small_ref, convw_ref, convb_ref, sbias_ref, alog_ref,
              dskip_ref, normw_ref, expand_ref, y_ref, c_ref,
              ext_ref, state_ref, carry_ref):
    c = pl.program_id(1)

    @pl.when(c == 0)
    def _():
        ext_ref[0:CONV_HALO, :] = jnp.zeros((CONV_HALO, SSD_CONV_CH), F32)
        state_ref[...] = jnp.zeros_like(state_ref)
        carry_ref[...] = jnp.zeros_like(carry_ref)

    row = lax.broadcasted_iota(jnp.int32, (CHUNK, 1), 0)
    valid = (c * CHUNK + row) >= PADL

    ext_ref[CONV_HALO:, :] = xbc_ref[...]

    @pl.when(c * CHUNK < PADL)
    def _():
        ext_ref[CONV_HALO:, :] = jnp.where(valid, ext_ref[CONV_HALO:, :], 0.0)

    ext = ext_ref[...]
    conv = convb_ref[...] + convw_ref[SSD_CONV - 1:SSD_CONV, :] * ext[CONV_HALO:, :]
    for k in range(SSD_CONV - 1):
        shifted = pltpu.roll(ext, SSD_CONV - 1 - k, 0)[CONV_HALO:, :]
        conv = conv + convw_ref[k:k + 1, :] * shifted
    ext_ref[0:CONV_HALO, :] = ext_ref[CHUNK:CHUNK + CONV_HALO, :]
    xc = _silu(conv)
    xs = xc[:, :SSD_D_INNER]
    bm = xc[:, SSD_D_INNER:SSD_D_INNER + SSD_GROUPS * SSD_STATE].astype(BF16)
    cm = xc[:, SSD_D_INNER + SSD_GROUPS * SSD_STATE:].astype(BF16)

    lane = lax.broadcasted_iota(jnp.int32, (CHUNK, SMALL_W), 1)
    is_dt = lane < SSD_HEADS
    is_f = (lane >= SSD_HEADS) & (lane < SSD_HEADS + FOX_HEADS)
    v = small_ref[...] + sbias_ref[...]
    dt = jnp.where(valid & is_dt, _softplus(v), 0.0)
    log_f = jnp.where(valid & is_f, -_softplus(-v), 0.0)
    neg_a = -jnp.exp(alog_ref[...])
    steps = jnp.where(is_dt, dt * neg_a, log_f)
    r_i = lax.broadcasted_iota(jnp.int32, (CHUNK, CHUNK), 0)
    c_i = lax.broadcasted_iota(jnp.int32, (CHUNK, CHUNK), 1)
    causal = r_i >= c_i
    tril = jnp.where(causal, 1.0, 0.0).astype(F32)
    cum = jnp.dot(tril, steps, preferred_element_type=F32,
                  precision=lax.Precision.HIGHEST)
    c_total = jnp.where(is_f, cum + carry_ref[...], 0.0)
    c_ref[...] = c_total
    carry_ref[...] = c_total[CHUNK - 1:CHUNK, :]

    a_last = cum[CHUNK - 1:CHUNK, :]
    dt_x = _split_dot(dt, expand_ref)
    ea_x = _split_dot(jnp.exp(cum), expand_ref)
    de_x = _split_dot(jnp.exp(a_last - cum), expand_ref)
    x_dt = xs * dt_x
    xb = x_dt.astype(BF16)
    xe = (x_dt * de_x).astype(BF16)
    cum_t = cum.T
    lane_p = lax.broadcasted_iota(jnp.int32, (CHUNK, 2 * SSD_HEAD_DIM), 1)
    first_half = lane_p < SSD_HEAD_DIM

    y_groups = []
    for g in range(SSD_GROUPS):
        gs = slice(g * SSD_GROUP_W, (g + 1) * SSD_GROUP_W)
        bg = bm[:, g * SSD_STATE:(g + 1) * SSD_STATE]
        cg = cm[:, g * SSD_STATE:(g + 1) * SSD_STATE]
        cb = lax.dot_general(cg, bg, (((1,), (1,)), ((), ())),
                             preferred_element_type=F32)
        pair_out = []
        for pr in range(SSD_GROUP_W // (2 * SSD_HEAD_DIM)):
            col0 = g * SSD_GROUP_W + pr * 2 * SSD_HEAD_DIM
            x_pair = xb[:, col0:col0 + 2 * SSD_HEAD_DIM]
            ys = []
            for j in range(2):
                hd = col0 // SSD_HEAD_DIM + j
                diff = cum[:, hd:hd + 1] - cum_t[hd:hd + 1, :]
                decay = jnp.exp(jnp.where(causal, diff, -jnp.inf))
                mat = (cb * decay).astype(BF16)
                ys.append(jnp.dot(mat, x_pair, preferred_element_type=F32))
            pair_out.append(jnp.where(first_half, ys[0], ys[1]))
        y_diag = jnp.concatenate(pair_out, axis=1)
        st = state_ref[g]
        y_off = jnp.dot(cg, st.astype(BF16), preferred_element_type=F32) * ea_x[:, gs]
        new = lax.dot_general(bg, xe[:, gs], (((0,), (0,)), ((), ())),
                              preferred_element_type=F32)
        state_ref[g] = st * ea_x[CHUNK - 1:CHUNK, gs] + new
        yg = y_diag + y_off + dskip_ref[:, gs] * xs[:, gs]
        yg = yg * _silu(z_ref[:, gs])
        ms = jnp.mean(yg * yg, axis=-1, keepdims=True)
        y_groups.append(yg * lax.rsqrt(ms + RMS_EPS) * normw_ref[:, gs])
    y_ref[...] = jnp.concatenate(y_groups, axis=1).astype(y_ref.dtype)


def _ssd(xbc, z, small, convw, convb, sbias, alog, dskip, normw, expand):
    nb = xbc.shape[0]
    row_spec = lambda w: pl.BlockSpec((None, CHUNK, w), lambda bi, ci: (bi, ci, 0))
    return pl.pallas_call(
        _ssd_body,
        grid=(nb, LP // CHUNK),
        in_specs=[row_spec(SSD_CONV_CH), row_spec(SSD_D_INNER), row_spec(SMALL_W),
                  _const_spec(convw.shape), _const_spec(convb.shape),
                  _const_spec(sbias.shape), _const_spec(alog.shape),
                  _const_spec(dskip.shape), _const_spec(normw.shape),
                  _const_spec(expand.shape)],
        out_specs=[row_spec(SSD_D_INNER), row_spec(SMALL_W)],
        out_shape=[jax.ShapeDtypeStruct((nb, LP, SSD_D_INNER), BF16),
                   jax.ShapeDtypeStruct((nb, LP, SMALL_W), F32)],
        scratch_shapes=[pltpu.VMEM((CHUNK + CONV_HALO, SSD_CONV_CH), F32),
                        pltpu.VMEM((SSD_GROUPS, SSD_STATE, SSD_GROUP_W), F32),
                        pltpu.VMEM((1, SMALL_W), F32)],
        compiler_params=pltpu.CompilerParams(
            dimension_semantics=("arbitrary", "arbitrary"), vmem_limit_bytes=VMEM_LIMIT),
        name="ssd",
    )(xbc, z, small, convw, convb, sbias, alog, dskip, normw, expand)


N_TK_CHUNKS = ATT_TK // LANES
N_TQ_CHUNKS = ATT_TQ // LANES
ONES_ROWS = 16
assert ATT_TQ == ATT_TK and PADL <= ATT_TK
LOG2E = math.log2(math.e)
Q_SCALE = LOG2E * HEAD_DIM ** -0.5


def _init_softmax_state(m_ref, acc_ref):
    m_ref[...] = jnp.full(m_ref.shape, NEG, F32)
    acc_ref[...] = jnp.zeros_like(acc_ref)


def _run_key_tiles(qb, first, scores, step, buf_a, buf_b):
    n_below = qb - first
    odd = n_below % 2

    @pl.when(odd == 0)
    def _():
        scores(first, buf_a)

    @pl.when(odd == 1)
    def _():
        scores(first, buf_b)
        step(first, buf_b, buf_a)

    def two_below_diagonal(i, carry):
        kb = first + odd + 2 * i
        step(kb, buf_a, buf_b)
        step(kb + 1, buf_b, buf_a)
        return carry

    lax.fori_loop(0, n_below // 2, two_below_diagonal, 0)
    step(qb, buf_a, None)


def _query_operands(qt_ref):
    zero_rows = jnp.zeros((HEAD_DIM, ATT_TQ), BF16)
    return [jnp.concatenate([qt_ref[:HEAD_DIM, :], zero_rows], axis=0),
            jnp.concatenate([zero_rows, qt_ref[HEAD_DIM:, :]], axis=0)]


def _value_rows(vt_ref, row0, n_rows, start):
    vt = vt_ref[row0:row0 + n_rows, pl.ds(start, ATT_TK)]
    return jnp.concatenate([vt, jnp.ones((ONES_ROWS, ATT_TK), BF16)], axis=0)


def _softmax_update(s_t, col_max, vt_aug, m_ref, acc_ref, idx, cols=slice(None)):
    m_old = m_ref[idx, :, cols]
    m_new = jnp.maximum(m_old, col_max)
    alpha = jnp.exp2(m_old - m_new)
    p_t = jnp.exp2(s_t - m_new).astype(BF16)
    pv = jnp.dot(vt_aug, p_t, preferred_element_type=F32)
    acc_ref[idx, :, cols] = alpha * acc_ref[idx, :, cols] + pv
    m_ref[idx, :, cols] = m_new


DIAG_BLOCK = 256


def _diagonal_update(s_ref, bias, vt_ref, row0, n_rows, start, m_ref, acc_ref, idx):
    kpos = lax.broadcasted_iota(jnp.int32, (DIAG_BLOCK, DIAG_BLOCK), 0)
    qpos = lax.broadcasted_iota(jnp.int32, (DIAG_BLOCK, DIAG_BLOCK), 1)
    causal = kpos <= qpos
    for b in range(ATT_TQ // DIAG_BLOCK):
        keys = (b + 1) * DIAG_BLOCK
        cols = slice(b * DIAG_BLOCK, keys)
        s_t = s_ref[:keys, cols]
        if bias is not None:
            s_t = s_t + bias[:keys, cols]
        last = jnp.where(causal, s_t[keys - DIAG_BLOCK:], NEG)
        s_t = last if b == 0 else jnp.concatenate([s_t[:keys - DIAG_BLOCK], last], axis=0)
        col_max = jnp.max(s_t, axis=0, keepdims=True)
        vt = vt_ref[row0:row0 + n_rows, pl.ds(start, keys)]
        vt_aug = jnp.concatenate([vt, jnp.ones((ONES_ROWS, keys), BF16)], axis=0)
        _softmax_update(s_t, col_max, vt_aug, m_ref, acc_ref, idx, cols)


def _fox_body(first_ref, qt_ref, k_ref, vt_ref, c_ref, o_ref, m_ref, acc_ref,
              sa_ref, pa_ref, sb_ref, pb_ref):
    qb = pl.program_id(2)
    n_q = pl.num_programs(2)
    first = first_ref[(pl.program_id(0) * pl.num_programs(1) + pl.program_id(1)) * n_q + qb]
    first = jnp.minimum(first, qb)
    _init_softmax_state(m_ref, acc_ref)
    buf_a, buf_b = (sa_ref, pa_ref), (sb_ref, pb_ref)
    q_heads = _query_operands(qt_ref)
    lane = lax.broadcasted_iota(jnp.int32, (ATT_TK, LANES), 1)
    first_lane = SSD_HEADS + 2 * pl.program_id(1)

    def scores(kb, buf):
        s_dst, max_dst = buf
        start = pl.multiple_of(kb * ATT_TK, ATT_TK)
        k_rows = k_ref[pl.ds(start, ATT_TK), :]
        c_rows = c_ref[pl.ds(start, ATT_TK), :]
        for j in range(2):
            ck = jnp.sum(jnp.where(lane == first_lane + j, c_rows, 0.0), axis=1, keepdims=True)
            s_t = jnp.dot(k_rows, q_heads[j], preferred_element_type=F32) - ck
            s_dst[j] = s_t
            max_dst[j] = jnp.max(s_t, axis=0, keepdims=True)

    def step(kb, src, nxt):
        s_src, max_src = src
        start = pl.multiple_of(kb * ATT_TK, ATT_TK)
        if nxt is not None:
            scores(kb + 1, nxt)
        for j in range(2):
            if nxt is None:
                _diagonal_update(s_src.at[j], None, vt_ref, j * HEAD_DIM, HEAD_DIM, start,
                                 m_ref, acc_ref, j)
            else:
                vt_aug = _value_rows(vt_ref, j * HEAD_DIM, HEAD_DIM, start)
                _softmax_update(s_src[j], max_src[j], vt_aug, m_ref, acc_ref, j)

    _run_key_tiles(qb, first, scores, step, buf_a, buf_b)

    o_t = jnp.concatenate(
        [acc_ref[j, :HEAD_DIM, :] / acc_ref[j, HEAD_DIM:HEAD_DIM + 1, :] for j in range(2)], axis=0)
    o_ref[...] = o_t.T.astype(o_ref.dtype)


def _fox_attention(first, qt, k, vt, c):
    nb = qt.shape[0]
    pairs = FOX_HEADS // 2
    acc_rows = HEAD_DIM + ONES_ROWS
    return pl.pallas_call(
        _fox_body,
        grid_spec=pltpu.PrefetchScalarGridSpec(
            num_scalar_prefetch=1,
            grid=(nb, pairs, LP // ATT_TQ),
            in_specs=[
                pl.BlockSpec((None, LANES, ATT_TQ), lambda bi, hp, qi, _: (bi, hp, qi)),
                pl.BlockSpec((None, LP, LANES), lambda bi, hp, qi, _: (bi, 0, hp)),
                pl.BlockSpec((None, LANES, LP), lambda bi, hp, qi, _: (bi, hp, 0)),
                pl.BlockSpec((None, LP, SMALL_W), lambda bi, hp, qi, _: (bi, 0, 0)),
            ],
            out_specs=pl.BlockSpec((None, ATT_TQ, LANES), lambda bi, hp, qi, _: (bi, qi, hp)),
            scratch_shapes=[pltpu.VMEM((2, 1, ATT_TQ), F32),
                            pltpu.VMEM((2, acc_rows, ATT_TQ), F32)]
                           + [pltpu.VMEM((2, ATT_TK, ATT_TQ), F32),
                              pltpu.VMEM((2, 1, ATT_TQ), F32)] * 2),
        out_shape=jax.ShapeDtypeStruct((nb, LP, FOX_WIDTH), BF16),
        compiler_params=pltpu.CompilerParams(
            dimension_semantics=("parallel", "parallel", "arbitrary"),
            vmem_limit_bytes=VMEM_LIMIT),
        name="fox_attention",
    )(first, qt, k, vt, c)


FORGOTTEN_LOG2 = 160.0
NORM_SLACK = 1.01


def _first_key_tiles(q_sq, k_sq, c2):
    nb = q_sq.shape[0]
    n_q = LP // ATT_TQ
    q_sq = q_sq[:, :FOX_HEADS, :].reshape(nb, FOX_HEADS, n_q, ATT_TQ)
    k_sq = k_sq.reshape(nb, LP, LANES)[:, :, :FOX_HEADS]
    q_max = NORM_SLACK * jnp.sqrt(jnp.max(q_sq, axis=-1))
    k_max = NORM_SLACK * jnp.sqrt(jnp.max(k_sq, axis=1))
    c_heads = c2[:, :, SSD_HEADS:SSD_HEADS + FOX_HEADS]
    c_query = c_heads[:, ::ATT_TQ, :].transpose(0, 2, 1)
    c_key = c_heads[:, ATT_TK - 1::ATT_TK, :].transpose(0, 2, 1)
    bound = (2.0 * q_max * k_max[:, :, None])[:, :, :, None] - (c_key[:, :, None, :] - c_query[:, :, :, None])
    needed = bound >= -FORGOTTEN_LOG2
    needed = needed.reshape(nb, FOX_HEADS // 2, 2, n_q, n_q).any(axis=2)
    needed = needed | (jnp.arange(n_q)[None, :] >= jnp.arange(n_q)[:, None])
    return jnp.argmax(needed, axis=-1).astype(jnp.int32).reshape(-1)


def _bias_tiles(bias_ref):
    rows = jnp.broadcast_to(bias_ref[...], (CHUNK, 2 * CHUNK))
    rolled = pltpu.roll(rows, 0, 1, stride=1, stride_axis=0)
    return rolled[:, :CHUNK], rolled[:, CHUNK:]


def _diagonal_bias(tiles):
    zeros = jnp.zeros((CHUNK, CHUNK), F32)
    rows = []
    for a in range(N_TK_CHUNKS):
        rows.append(jnp.concatenate(
            [tiles[b - a] if b - a in (0, 1) else zeros for b in range(N_TQ_CHUNKS)], axis=1))
    return jnp.concatenate(rows, axis=0)


def _diff_body(lambda_init, qt_ref, k_ref, vt_ref, bias_ref, lam_ref, subln_ref, o_ref,
               m_ref, acc_ref, tiles_ref, sa_ref, pa_ref, sb_ref, pb_ref):
    qb = pl.program_id(2)
    _init_softmax_state(m_ref, acc_ref)
    buf_a, buf_b = (sa_ref, pa_ref), (sb_ref, pb_ref)
    tile0, tile1 = _bias_tiles(bias_ref)
    tiles_ref[0] = tile0
    tiles_ref[1] = tile1
    q_parts = _query_operands(qt_ref)
    pad_rows = -(-PADL // CHUNK) * CHUNK
    corner0 = ATT_TK - CHUNK

    def scores(kb, buf):
        s_dst, max_dst = buf
        start = pl.multiple_of(kb * ATT_TK, ATT_TK)
        k_rows = k_ref[pl.ds(start, ATT_TK), :]
        kpos = lax.broadcasted_iota(jnp.int32, (pad_rows, ATT_TQ), 0)
        not_padding = (kpos >= PADL) | (kb > 0)
        for j in range(2):
            s_t = jnp.dot(k_rows, q_parts[j], preferred_element_type=F32)
            s_t = jnp.concatenate(
                [jnp.where(not_padding, s_t[:pad_rows], NEG), s_t[pad_rows:]], axis=0)
            s_dst[j] = s_t
            max_dst[j] = jnp.max(s_t, axis=0, keepdims=True)

    def step(kb, src, nxt):
        s_src, max_src = src
        start = pl.multiple_of(kb * ATT_TK, ATT_TK)
        diagonal = nxt is None
        if diagonal:
            bias = _diagonal_bias((tiles_ref[0], tiles_ref[1]))
            for j in range(2):
                _diagonal_update(s_src.at[j], bias, vt_ref, 0, LANES, start, m_ref, acc_ref, j)
            return
        scores(kb + 1, nxt)
        corner = jnp.where(kb == qb - 1, 1.0, 0.0) * tiles_ref[1]
        vt_aug = _value_rows(vt_ref, 0, LANES, start)
        for j in range(2):
            s_t = s_src[j]
            near = s_t[corner0:, :CHUNK] + corner
            s_t = jnp.concatenate(
                [s_t[:corner0], jnp.concatenate([near, s_t[corner0:, CHUNK:]], axis=1)], axis=0)
            first = jnp.maximum(jnp.max(s_t[:corner0, :CHUNK], axis=0, keepdims=True),
                                jnp.max(near, axis=0, keepdims=True))
            col_max = jnp.concatenate([first, max_src[j][:, CHUNK:]], axis=1)
            _softmax_update(s_t, col_max, vt_aug, m_ref, acc_ref, j)

    _run_key_tiles(qb, 0, scores, step, buf_a, buf_b)

    lam1 = jnp.exp(jnp.sum(lam_ref[0:1, :] * lam_ref[1:2, :], axis=-1, keepdims=True))
    lam2 = jnp.exp(jnp.sum(lam_ref[2:3, :] * lam_ref[3:4, :], axis=-1, keepdims=True))
    lam = lam1 - lam2 + lambda_init
    o_t = (acc_ref[0, :LANES, :] / acc_ref[0, LANES:LANES + 1, :]
           - lam * (acc_ref[1, :LANES, :] / acc_ref[1, LANES:LANES + 1, :]))
    o = o_t.T
    ms = jnp.mean(o * o, axis=-1, keepdims=True)
    o = o * lax.rsqrt(ms + RMS_EPS) * subln_ref[...] * (1.0 - lambda_init)
    o_ref[...] = o.astype(o_ref.dtype)


def _diff_attention(qt, k, vt, bias_rows, lam_rows, subln, lambda_init):
    nb = qt.shape[0]
    acc_rows = LANES + ONES_ROWS
    return pl.pallas_call(
        functools.partial(_diff_body, lambda_init),
        grid=(nb, DIFF_HEADS, LP // ATT_TQ),
        in_specs=[
            pl.BlockSpec((None, LANES, ATT_TQ), lambda bi, hd, qi: (bi, hd, qi)),
            pl.BlockSpec((None, LP, LANES), lambda bi, hd, qi: (bi, 0, hd)),
            pl.BlockSpec((None, LANES, LP), lambda bi, hd, qi: (bi, hd, 0)),
            pl.BlockSpec((None, 1, 2 * CHUNK), lambda bi, hd, qi: (hd, 0, 0)),
            _const_spec(lam_rows.shape),
            _const_spec(subln.shape),
        ],
        out_specs=pl.BlockSpec((None, ATT_TQ, LANES), lambda bi, hd, qi: (bi, qi, hd)),
        out_shape=jax.ShapeDtypeStruct((nb, LP, DIFF_HEADS * LANES), BF16),
        scratch_shapes=[pltpu.VMEM((2, 1, ATT_TQ), F32),
                        pltpu.VMEM((2, acc_rows, ATT_TQ), F32),
                        pltpu.VMEM((2, CHUNK, CHUNK), F32)]
                       + [pltpu.VMEM((2, ATT_TK, ATT_TQ), F32),
                          pltpu.VMEM((2, 1, ATT_TQ), F32)] * 2,
        compiler_params=pltpu.CompilerParams(
            dimension_semantics=("parallel", "parallel", "arbitrary"),
            vmem_limit_bytes=VMEM_LIMIT),
        name="diff_attention",
    )(qt, k, vt, bias_rows, lam_rows, subln)


def _t5_bucket(n):
    max_exact = N_BUCKETS // 2
    nf = jnp.maximum(n, 1).astype(F32)
    large = max_exact + (jnp.log(nf / max_exact) / math.log(128 / max_exact)
                         * (N_BUCKETS - max_exact)).astype(jnp.int32)
    large = jnp.minimum(large, N_BUCKETS - 1)
    return jnp.where(n < max_exact, n, large)


def _relative_bias_rows(rel_table):
    dist = jnp.arange(2 * CHUNK)
    by_dist = LOG2E * (rel_table[_t5_bucket(dist)] - rel_table[N_BUCKETS - 1])
    return by_dist.T[:, None, :].astype(F32)


def _pad_lanes(vec, width=LANES):
    return jnp.pad(vec, (0, width - vec.shape[0]))[None, :].astype(F32)


def kernel(x, meta_tokens, ln_gain, ln_bias, ffn1_w_gate, ffn1_w_up, ffn1_w_down, ffn2_w_gate, ffn2_w_up, ffn2_w_down, even_w_in, even_conv_w, even_conv_b, ssd_dt_bias, ssd_a_log, ssd_d_skip, ssd_norm_w, fox_f_bias, even_w_out, diff_w_qkv, diff_lambda_q1, diff_lambda_k1, diff_lambda_q2, diff_lambda_k2, diff_subln_w, diff_w_o, rel_bias_table):
    nb = x.shape[0]
    lead = jnp.concatenate([jnp.zeros((PADL, D_MODEL), x.dtype), meta_tokens.astype(x.dtype)], axis=0)

    def ln_params(l, i):
        return ln_gain[l, i][None, :], ln_bias[l, i][None, :]

    ffn1 = tuple(w.astype(BF16) for w in (ffn1_w_gate, ffn1_w_up, ffn1_w_down))
    ffn2 = tuple(w.astype(BF16) for w in (ffn2_w_gate, ffn2_w_up, ffn2_w_down))

    hf = _ffn_ln_first(x, lead, *ffn1, 0, *ln_params(0, 0))

    w_in = even_w_in[0]
    o_z, o_xbc = 0, SSD_D_INNER
    o_dt = o_xbc + SSD_CONV_CH
    o_q = o_dt + SSD_HEADS
    o_k, o_v = o_q + FOX_WIDTH, o_q + 2 * FOX_WIDTH
    o_f = o_q + 3 * FOX_WIDTH
    bias = fox_f_bias[0]
    head = jnp.arange(FOX_HEADS)
    before = (bias[None, :] < bias[:, None]) | ((bias[None, :] == bias[:, None])
                                                & (head[None, :] < head[:, None]))
    rank = jnp.sum(before, axis=1)
    order = jnp.argmax(rank[None, :] == head[:, None], axis=1)

    def by_head(w):
        return jnp.take(w.reshape((FOX_HEADS, HEAD_DIM) + w.shape[1:]), order, axis=0).reshape(w.shape)

    wt_q, wt_k, wt_v = (by_head(w.T.astype(BF16)) for w in
                        (w_in[:, o_q:o_k] * Q_SCALE, w_in[:, o_k:o_v], w_in[:, o_v:o_f]))
    w_small = jnp.concatenate(
        [w_in[:, o_dt:o_q], w_in[:, o_f:][:, order],
         jnp.zeros((D_MODEL, SMALL_W - SSD_HEADS - FOX_HEADS), w_in.dtype)], axis=1)
    w_even = jnp.concatenate(
        [w_in[:, o_z:o_dt].astype(BF16), wt_k.T, w_small.astype(BF16)], axis=1)
    wt_even = jnp.concatenate([wt_q, wt_v], axis=0)
    z, xbc, k, small, qt, vt, k_sq, q_sq = _proj(
        hf, w_even, wt_even, (SSD_D_INNER, SSD_CONV_CH, FOX_WIDTH, SMALL_W),
        (F32, F32, BF16, F32), (FOX_WIDTH, FOX_WIDTH), nb, "even_in_proj", norms=(2, 0))

    sbias = _pad_lanes(jnp.concatenate([ssd_dt_bias[0], fox_f_bias[0][order]]))
    alog = _pad_lanes(ssd_a_log[0])
    dskip = jnp.repeat(ssd_d_skip[0], SSD_HEAD_DIM)[None, :].astype(F32)
    expand = np.zeros((SMALL_W, SSD_D_INNER), np.float32)
    expand[np.arange(SSD_D_INNER) // SSD_HEAD_DIM, np.arange(SSD_D_INNER)] = 1.0
    y, cfull = _ssd(xbc.reshape(nb, LP, SSD_CONV_CH), z.reshape(nb, LP, SSD_D_INNER),
                    small.reshape(nb, LP, SMALL_W), even_conv_w[0], even_conv_b[0][None, :],
                    sbias, alog, dskip, ssd_norm_w[0][None, :],
                    jnp.asarray(np.concatenate([expand, expand], axis=0), BF16))
    c2 = LOG2E * cfull
    k = k.reshape(nb, LP, FOX_WIDTH)
    ck = jnp.where(jnp.arange(LP)[None, :, None] < PADL, -NEG, c2)
    o = _fox_attention(_first_key_tiles(q_sq, k_sq, c2), qt, k, vt, ck)
    w_out = even_w_out[0].astype(BF16)
    hf = _outproj_ln(hf, [y.reshape(nb * LP, SSD_D_INNER), o.reshape(nb * LP, FOX_WIDTH)],
                     [w_out[:SSD_D_INNER], by_head(w_out[SSD_D_INNER:])], *ln_params(0, 1),
                     name="even_out_proj_ln")
    hf = _ffn_ln(hf, *ffn2, 0, *ln_params(0, 2))

    hf = _ffn_ln(hf, *ffn1, 1, *ln_params(1, 0))
    qw = DIFF_HEADS * 2 * HEAD_DIM
    w_qkv = diff_w_qkv[0]
    wt_diff = jnp.concatenate([w_qkv[:, :qw] * Q_SCALE, w_qkv[:, 2 * qw:]], axis=1).T.astype(BF16)
    k, qt, vt = _proj(hf, w_qkv[:, qw:2 * qw].astype(BF16), wt_diff, (qw,), (BF16,),
                      (qw, DIFF_HEADS * LANES), nb, "diff_qkv_proj")
    lambda_init = 0.8 - 0.6 * math.exp(-0.3 * 1)
    lam_rows = jnp.concatenate(
        [_pad_lanes(diff_lambda_q1[0]), _pad_lanes(diff_lambda_k1[0]),
         _pad_lanes(diff_lambda_q2[0]), _pad_lanes(diff_lambda_k2[0]),
         jnp.zeros((4, LANES), F32)], axis=0)
    o = _diff_attention(qt, k.reshape(nb, LP, qw), vt, _relative_bias_rows(rel_bias_table),
                        lam_rows, diff_subln_w[0][None, :], lambda_init)
    hf = _outproj_ln(hf, [o.reshape(nb * LP, DIFF_HEADS * LANES)], [diff_w_o[0].astype(BF16)],
                     *ln_params(1, 1), name="diff_out_proj_ln")
    return _ffn_ln_final(hf.reshape(nb, LP, D_MODEL), *ffn2, 1, *ln_params(1, 2))
```

```python
import functools
import math

import numpy as np
import jax
import jax.numpy as jnp
from jax import lax
from jax.experimental import pallas as pl
from jax.experimental.pallas import tpu as pltpu

F32 = jnp.float32
BF16 = jnp.bfloat16

D_MODEL = 1024
SEQ = 8192
DEPTH = 2
N_META = 16
CHUNK = 128
SSD_D_INNER = 2048
SSD_HEAD_DIM = 64
SSD_HEADS = 32
SSD_GROUPS = 4
SSD_GROUP_W = SSD_D_INNER // SSD_GROUPS
SSD_STATE = 128
SSD_CONV = 4
SSD_CONV_CH = SSD_D_INNER + 2 * SSD_GROUPS * SSD_STATE
FOX_HEADS = 16
FOX_WIDTH = 1024
HEAD_DIM = 64
DIFF_HEADS = 8
N_BUCKETS = 32
D_FF = 2816
ALPHA = (2 * DEPTH) ** 0.25
LN_EPS = 1e-5
RMS_EPS = 1e-5
NEG = -1e30

LANES = 128
LP = 8448
PADL = LP - SEQ - N_META
ROW_TILE = 512
OUT_TILE = 256
ATT_TQ = 768
ATT_TK = 768
CONV_HALO = 8
SMALL_W = LANES
VMEM_LIMIT = 56 * 1024 * 1024

assert PADL % CHUNK == CHUNK - N_META
assert LP % ATT_TQ == 0 and ATT_TQ % ATT_TK == 0 and ATT_TK % LANES == 0
assert LP % CHUNK == 0 and (2 * LP) % ROW_TILE == 0
assert PADL <= OUT_TILE and (LP - OUT_TILE) == SEQ


def _const_spec(shape):
    nd = len(shape)
    return pl.BlockSpec(shape, lambda *_: (0,) * nd, pipeline_mode=pl.Buffered(1))


def _layer_spec(shape, layer):
    nd = len(shape)
    return pl.BlockSpec((None,) + tuple(shape), lambda *_: (layer,) + (0,) * nd,
                        pipeline_mode=pl.Buffered(1))


def _ffn_weight_specs(layer):
    return [_layer_spec((D_MODEL, D_FF), layer), _layer_spec((D_MODEL, D_FF), layer),
            _layer_spec((D_FF, D_MODEL), layer)]


def _layer_norm(r, g, b):
    mu = jnp.mean(r, axis=-1, keepdims=True)
    d = r - mu
    var = jnp.mean(d * d, axis=-1, keepdims=True)
    return d * lax.rsqrt(var + LN_EPS) * g + b


LOG2E = math.log2(math.e)


def _silu(x):
    return x / (1.0 + jnp.exp2(x * -LOG2E))


def _softplus(x):
    return jnp.maximum(x, 0.0) + jnp.log(1.0 + jnp.exp(-jnp.abs(x)))


def _ffn_ln_body(h_ref, wg_ref, wu_ref, wd_ref, g_ref, b_ref, o_ref):
    h = h_ref[...]
    hb = h.astype(BF16)
    g = jnp.dot(hb, wg_ref[...], preferred_element_type=F32)
    u = jnp.dot(hb, wu_ref[...], preferred_element_type=F32)
    a = (_silu(g) * u).astype(BF16)
    y = jnp.dot(a, wd_ref[...], preferred_element_type=F32)
    o_ref[...] = _layer_norm(ALPHA * h + 0.5 * y, g_ref[...], b_ref[...])


def _ffn_ln(hf, wg, wu, wd, layer, g, b):
    rows = hf.shape[0]
    return pl.pallas_call(
        _ffn_ln_body,
        grid=(rows // ROW_TILE,),
        in_specs=[
            pl.BlockSpec((ROW_TILE, D_MODEL), lambda i: (i, 0)),
            *_ffn_weight_specs(layer),
            _const_spec((1, D_MODEL)),
            _const_spec((1, D_MODEL)),
        ],
        out_specs=pl.BlockSpec((ROW_TILE, D_MODEL), lambda i: (i, 0)),
        out_shape=jax.ShapeDtypeStruct((rows, D_MODEL), F32),
        compiler_params=pltpu.CompilerParams(
            dimension_semantics=("parallel",), vmem_limit_bytes=VMEM_LIMIT),
        name="ffn_ln",
    )(hf, wg, wu, wd, g, b)


def _ffn_ln_first_body(lead_ref, x_ref, wg_ref, wu_ref, wd_ref, g_ref, b_ref, o_ref, h_ref):
    h_ref[...] = jnp.where(pl.program_id(1) == 0, lead_ref[...], x_ref[...])
    _ffn_ln_body(h_ref, wg_ref, wu_ref, wd_ref, g_ref, b_ref, o_ref)


def _ffn_ln_first(x, lead, wg, wu, wd, layer, g, b):
    nb = x.shape[0]
    tiles = LP // OUT_TILE
    return pl.pallas_call(
        _ffn_ln_first_body,
        grid=(nb, tiles),
        in_specs=[
            _const_spec((OUT_TILE, D_MODEL)),
            pl.BlockSpec((None, OUT_TILE, D_MODEL), lambda bi, i: (bi, jnp.maximum(i - 1, 0), 0)),
            *_ffn_weight_specs(layer),
            _const_spec((1, D_MODEL)),
            _const_spec((1, D_MODEL)),
        ],
        out_specs=pl.BlockSpec((OUT_TILE, D_MODEL), lambda bi, i: (bi * tiles + i, 0)),
        out_shape=jax.ShapeDtypeStruct((nb * LP, D_MODEL), F32),
        scratch_shapes=[pltpu.VMEM((OUT_TILE, D_MODEL), F32)],
        compiler_params=pltpu.CompilerParams(
            dimension_semantics=("parallel", "arbitrary"), vmem_limit_bytes=VMEM_LIMIT),
        name="ffn_ln_first",
    )(lead, x, wg, wu, wd, g, b)


def _ffn_ln_final_body(h_ref, wg_ref, wu_ref, wd_ref, g_ref, b_ref, o_ref):
    @pl.when(pl.program_id(1) > 0)
    def _():
        _ffn_ln_body(h_ref, wg_ref, wu_ref, wd_ref, g_ref, b_ref, o_ref)


def _ffn_ln_final(h3, wg, wu, wd, layer, g, b):
    nb = h3.shape[0]
    return pl.pallas_call(
        _ffn_ln_final_body,
        grid=(nb, LP // OUT_TILE),
        in_specs=[
            pl.BlockSpec((None, OUT_TILE, D_MODEL), lambda bi, i: (bi, i, 0)),
            *_ffn_weight_specs(layer),
            _const_spec((1, D_MODEL)),
            _const_spec((1, D_MODEL)),
        ],
        out_specs=pl.BlockSpec((None, OUT_TILE, D_MODEL),
                               lambda bi, i: (bi, jnp.maximum(i - 1, 0), 0)),
        out_shape=jax.ShapeDtypeStruct((nb, SEQ, D_MODEL), F32),
        compiler_params=pltpu.CompilerParams(
            dimension_semantics=("arbitrary", "arbitrary"), vmem_limit_bytes=VMEM_LIMIT),
        name="ffn_ln_final",
    )(h3, wg, wu, wd, g, b)


PROJ_TILE = 256
assert LP % PROJ_TILE == 0


def _proj_body(n_t, h_ref, w_ref, wt_ref, *o_refs):
    hb = h_ref[...].astype(BF16)
    off = 0
    for o_ref in o_refs[:-n_t]:
        n = o_ref.shape[-1]
        o_ref[...] = jnp.dot(hb, w_ref[:, off:off + n],
                             preferred_element_type=F32).astype(o_ref.dtype)
        off += n
    off = 0
    for o_ref in o_refs[-n_t:]:
        n = o_ref.shape[0]
        o_ref[...] = lax.dot_general(wt_ref[off:off + n, :], hb, (((1,), (1,)), ((), ())),
                                     preferred_element_type=F32).astype(o_ref.dtype)
        off += n


def _proj(hf, w, wt, widths, dtypes, t_widths, nb, name):
    rows = hf.shape[0]
    tiles_per_batch = LP // PROJ_TILE
    out_specs = [pl.BlockSpec((PROJ_TILE, n), lambda i: (i, 0)) for n in widths]
    out_specs += [pl.BlockSpec((None, n, PROJ_TILE),
                               lambda i: (i // tiles_per_batch, 0, i % tiles_per_batch))
                  for n in t_widths]
    out_shape = [jax.ShapeDtypeStruct((rows, n), dt) for n, dt in zip(widths, dtypes)]
    out_shape += [jax.ShapeDtypeStruct((nb, n, LP), BF16) for n in t_widths]
    return pl.pallas_call(
        functools.partial(_proj_body, len(t_widths)),
        grid=(rows // PROJ_TILE,),
        in_specs=[pl.BlockSpec((PROJ_TILE, D_MODEL), lambda i: (i, 0)),
                  _const_spec(w.shape), _const_spec(wt.shape)],
        out_specs=out_specs,
        out_shape=out_shape,
        compiler_params=pltpu.CompilerParams(
            dimension_semantics=("parallel",), vmem_limit_bytes=VMEM_LIMIT),
        name=name,
    )(hf, w, wt)


def _outproj_ln_body(n_in, h_ref, *refs):
    a_refs = refs[:n_in]
    w_refs = refs[n_in:2 * n_in]
    g_ref, b_ref, o_ref = refs[2 * n_in:]
    m = jnp.dot(a_refs[0][...], w_refs[0][...], preferred_element_type=F32)
    for a_ref, w_ref in zip(a_refs[1:], w_refs[1:]):
        m = m + jnp.dot(a_ref[...], w_ref[...], preferred_element_type=F32)
    o_ref[...] = _layer_norm(ALPHA * h_ref[...] + m, g_ref[...], b_ref[...])


def _outproj_ln(hf, acts, ws, g, b, name):
    rows = hf.shape[0]
    n_in = len(acts)
    in_specs = [pl.BlockSpec((ROW_TILE, D_MODEL), lambda i: (i, 0))]
    in_specs += [pl.BlockSpec((ROW_TILE, a.shape[1]), lambda i: (i, 0)) for a in acts]
    in_specs += [_const_spec(w.shape) for w in ws]
    in_specs += [_const_spec((1, D_MODEL)), _const_spec((1, D_MODEL))]
    return pl.pallas_call(
        functools.partial(_outproj_ln_body, n_in),
        grid=(rows // ROW_TILE,),
        in_specs=in_specs,
        out_specs=pl.BlockSpec((ROW_TILE, D_MODEL), lambda i: (i, 0)),
        out_shape=jax.ShapeDtypeStruct((rows, D_MODEL), F32),
        compiler_params=pltpu.CompilerParams(
            dimension_semantics=("parallel",), vmem_limit_bytes=VMEM_LIMIT),
        name=name,
    )(hf, *acts, *ws, g, b)


def _split_dot(x, e2_ref):
    hi = x.astype(BF16)
    lo = (x - hi.astype(F32)).astype(BF16)
    return jnp.dot(jnp.concatenate([hi, lo], axis=1), e2_ref[...], preferred_element_type=F32)


def _ssd_body(xbc_ref, z_ref, small_ref, convw_ref, convb_ref, sbias_ref, alog_ref,
              dskip_ref, normw_ref, expand_ref, y_ref, c_ref,
              ext_ref, state_ref, carry_ref):
    c = pl.program_id(1)

    @pl.when(c == 0)
    def _():
        ext_ref[0:CONV_HALO, :] = jnp.zeros((CONV_HALO, SSD_CONV_CH), F32)
        state_ref[...] = jnp.zeros_like(state_ref)
        carry_ref[...] = jnp.zeros_like(carry_ref)

    row = lax.broadcasted_iota(jnp.int32, (CHUNK, 1), 0)
    valid = (c * CHUNK + row) >= PADL

    ext_ref[CONV_HALO:, :] = xbc_ref[...]

    @pl.when(c * CHUNK < PADL)
    def _():
        ext_ref[CONV_HALO:, :] = jnp.where(valid, ext_ref[CONV_HALO:, :], 0.0)

    ext = ext_ref[...]
    conv = convb_ref[...] + convw_ref[SSD_CONV - 1:SSD_CONV, :] * ext[CONV_HALO:, :]
    for k in range(SSD_CONV - 1):
        shifted = pltpu.roll(ext, SSD_CONV - 1 - k, 0)[CONV_HALO:, :]
        conv = conv + convw_ref[k:k + 1, :] * shifted
    ext_ref[0:CONV_HALO, :] = ext_ref[CHUNK:CHUNK + CONV_HALO, :]
    xc = _silu(conv)
    xs = xc[:, :SSD_D_INNER]
    bm = xc[:, SSD_D_INNER:SSD_D_INNER + SSD_GROUPS * SSD_STATE].astype(BF16)
    cm = xc[:, SSD_D_INNER + SSD_GROUPS * SSD_STATE:].astype(BF16)

    lane = lax.broadcasted_iota(jnp.int32, (CHUNK, SMALL_W), 1)
    is_dt = lane < SSD_HEADS
    is_f = (lane >= SSD_HEADS) & (lane < SSD_HEADS + FOX_HEADS)
    v = small_ref[...] + sbias_ref[...]
    dt = jnp.where(valid & is_dt, _softplus(v), 0.0)
    log_f = jnp.where(valid & is_f, -_softplus(-v), 0.0)
    neg_a = -jnp.exp(alog_ref[...])
    steps = jnp.where(is_dt, dt * neg_a, log_f)
    r_i = lax.broadcasted_iota(jnp.int32, (CHUNK, CHUNK), 0)
    c_i = lax.broadcasted_iota(jnp.int32, (CHUNK, CHUNK), 1)
    causal = r_i >= c_i
    tril = jnp.where(causal, 1.0, 0.0).astype(F32)
    cum = jnp.dot(tril, steps, preferred_element_type=F32,
                  precision=lax.Precision.HIGHEST)
    c_total = jnp.where(is_f, cum + carry_ref[...], 0.0)
    c_ref[...] = c_total
    carry_ref[...] = c_total[CHUNK - 1:CHUNK, :]

    cum2 = cum * LOG2E
    a_last = cum2[CHUNK - 1:CHUNK, :]
    dt_x = _split_dot(dt, expand_ref)
    ea_x = _split_dot(jnp.exp2(cum2), expand_ref)
    de_x = _split_dot(jnp.exp2(a_last - cum2), expand_ref)
    x_dt = xs * dt_x
    xb = x_dt.astype(BF16)
    xe = (x_dt * de_x).astype(BF16)
    cum_t = cum2.T
    lane_p = lax.broadcasted_iota(jnp.int32, (CHUNK, 2 * SSD_HEAD_DIM), 1)
    first_half = lane_p < SSD_HEAD_DIM

    y_groups = []
    for g in range(SSD_GROUPS):
        gs = slice(g * SSD_GROUP_W, (g + 1) * SSD_GROUP_W)
        bg = bm[:, g * SSD_STATE:(g + 1) * SSD_STATE]
        cg = cm[:, g * SSD_STATE:(g + 1) * SSD_STATE]
        cb = lax.dot_general(cg, bg, (((1,), (1,)), ((), ())),
                             preferred_element_type=F32)
        pair_out = []
        for pr in range(SSD_GROUP_W // (2 * SSD_HEAD_DIM)):
            col0 = g * SSD_GROUP_W + pr * 2 * SSD_HEAD_DIM
            x_pair = xb[:, col0:col0 + 2 * SSD_HEAD_DIM]
            ys = []
            for j in range(2):
                hd = col0 // SSD_HEAD_DIM + j
                diff = cum2[:, hd:hd + 1] - cum_t[hd:hd + 1, :]
                decay = jnp.exp2(jnp.where(causal, diff, -jnp.inf))
                mat = (cb * decay).astype(BF16)
                ys.append(jnp.dot(mat, x_pair, preferred_element_type=F32))
            pair_out.append(jnp.where(first_half, ys[0], ys[1]))
        y_diag = jnp.concatenate(pair_out, axis=1)
        st = state_ref[g]
        y_off = jnp.dot(cg, st.astype(BF16), preferred_element_type=F32) * ea_x[:, gs]
        new = lax.dot_general(bg, xe[:, gs], (((0,), (0,)), ((), ())),
                              preferred_element_type=F32)
        state_ref[g] = st * ea_x[CHUNK - 1:CHUNK, gs] + new
        yg = y_diag + y_off + dskip_ref[:, gs] * xs[:, gs]
        yg = yg * _silu(z_ref[:, gs])
        ms = jnp.mean(yg * yg, axis=-1, keepdims=True)
        y_groups.append(yg * lax.rsqrt(ms + RMS_EPS) * normw_ref[:, gs])
    y_ref[...] = jnp.concatenate(y_groups, axis=1).astype(y_ref.dtype)


def _ssd(xbc, z, small, convw, convb, sbias, alog, dskip, normw, expand):
    nb = xbc.shape[0]
    row_spec = lambda w: pl.BlockSpec((None, CHUNK, w), lambda bi, ci: (bi, ci, 0))
    return pl.pallas_call(
        _ssd_body,
        grid=(nb, LP // CHUNK),
        in_specs=[row_spec(SSD_CONV_CH), row_spec(SSD_D_INNER), row_spec(SMALL_W),
                  _const_spec(convw.shape), _const_spec(convb.shape),
                  _const_spec(sbias.shape), _const_spec(alog.shape),
                  _const_spec(dskip.shape), _const_spec(normw.shape),
                  _const_spec(expand.shape)],
        out_specs=[row_spec(SSD_D_INNER), row_spec(SMALL_W)],
        out_shape=[jax.ShapeDtypeStruct((nb, LP, SSD_D_INNER), BF16),
                   jax.ShapeDtypeStruct((nb, LP, SMALL_W), F32)],
        scratch_shapes=[pltpu.VMEM((CHUNK + CONV_HALO, SSD_CONV_CH), F32),
                        pltpu.VMEM((SSD_GROUPS, SSD_STATE, SSD_GROUP_W), F32),
                        pltpu.VMEM((1, SMALL_W), F32)],
        compiler_params=pltpu.CompilerParams(
            dimension_semantics=("arbitrary", "arbitrary"), vmem_limit_bytes=VMEM_LIMIT),
        name="ssd",
    )(xbc, z, small, convw, convb, sbias, alog, dskip, normw, expand)


N_TK_CHUNKS = ATT_TK // LANES
N_TQ_CHUNKS = ATT_TQ // LANES
ONES_ROWS = 16
assert ATT_TQ == ATT_TK and PADL <= ATT_TK
Q_SCALE = LOG2E * HEAD_DIM ** -0.5


def _init_softmax_state(m_ref, acc_ref):
    m_ref[...] = jnp.full(m_ref.shape, NEG, F32)
    acc_ref[...] = jnp.zeros_like(acc_ref)


def _run_key_tiles(qb, first, scores, step, buf_a, buf_b):
    n_below = qb - first
    odd = n_below % 2

    @pl.when(odd == 0)
    def _():
        scores(first, buf_a)

    @pl.when(odd == 1)
    def _():
        scores(first, buf_b)
        step(first, buf_b, buf_a)

    def two_below_diagonal(i, carry):
        kb = first + odd + 2 * i
        step(kb, buf_a, buf_b)
        step(kb + 1, buf_b, buf_a)
        return carry

    lax.fori_loop(0, n_below // 2, two_below_diagonal, 0)
    step(qb, buf_a, None)


def _query_operands(qt_ref):
    zero_rows = jnp.zeros((HEAD_DIM, ATT_TQ), BF16)
    return [jnp.concatenate([qt_ref[:HEAD_DIM, :], zero_rows], axis=0),
            jnp.concatenate([zero_rows, qt_ref[HEAD_DIM:, :]], axis=0)]


def _value_rows(vt_ref, row0, n_rows, start):
    vt = vt_ref[row0:row0 + n_rows, pl.ds(start, ATT_TK)]
    return jnp.concatenate([vt, jnp.ones((ONES_ROWS, ATT_TK), BF16)], axis=0)


def _softmax_update(s_t, col_max, vt_aug, m_ref, acc_ref, idx, cols=slice(None)):
    m_old = m_ref[idx, :, cols]
    m_new = jnp.maximum(m_old, col_max)
    alpha = jnp.exp2(m_old - m_new)
    p_t = jnp.exp2(s_t - m_new).astype(BF16)
    pv = jnp.dot(vt_aug, p_t, preferred_element_type=F32)
    acc_ref[idx, :, cols] = alpha * acc_ref[idx, :, cols] + pv
    m_ref[idx, :, cols] = m_new


DIAG_BLOCK = 256


def _diagonal_update(s_ref, bias, vt_ref, row0, n_rows, start, m_ref, acc_ref, idx):
    kpos = lax.broadcasted_iota(jnp.int32, (DIAG_BLOCK, DIAG_BLOCK), 0)
    qpos = lax.broadcasted_iota(jnp.int32, (DIAG_BLOCK, DIAG_BLOCK), 1)
    causal = kpos <= qpos
    for b in range(ATT_TQ // DIAG_BLOCK):
        keys = (b + 1) * DIAG_BLOCK
        cols = slice(b * DIAG_BLOCK, keys)
        s_t = s_ref[:keys, cols]
        if bias is not None:
            s_t = s_t + bias[:keys, cols]
        last = jnp.where(causal, s_t[keys - DIAG_BLOCK:], NEG)
        s_t = last if b == 0 else jnp.concatenate([s_t[:keys - DIAG_BLOCK], last], axis=0)
        col_max = jnp.max(s_t, axis=0, keepdims=True)
        vt = vt_ref[row0:row0 + n_rows, pl.ds(start, keys)]
        vt_aug = jnp.concatenate([vt, jnp.ones((ONES_ROWS, keys), BF16)], axis=0)
        _softmax_update(s_t, col_max, vt_aug, m_ref, acc_ref, idx, cols)


def _fox_body(first_ref, qt_ref, k_ref, vt_ref, c_ref, o_ref, m_ref, acc_ref,
              sa_ref, pa_ref, sb_ref, pb_ref):
    qb = pl.program_id(2)
    n_q = pl.num_programs(2)
    first = first_ref[(pl.program_id(0) * pl.num_programs(1) + pl.program_id(1)) * n_q + qb]
    first = jnp.minimum(first, qb)
    _init_softmax_state(m_ref, acc_ref)
    buf_a, buf_b = (sa_ref, pa_ref), (sb_ref, pb_ref)
    q_heads = _query_operands(qt_ref)
    lane = lax.broadcasted_iota(jnp.int32, (ATT_TK, LANES), 1)
    first_lane = SSD_HEADS + 2 * pl.program_id(1)

    def scores(kb, buf):
        s_dst, max_dst = buf
        start = pl.multiple_of(kb * ATT_TK, ATT_TK)
        k_rows = k_ref[pl.ds(start, ATT_TK), :]
        c_rows = c_ref[pl.ds(start, ATT_TK), :]
        for j in range(2):
            ck = jnp.sum(jnp.where(lane == first_lane + j, c_rows, 0.0), axis=1, keepdims=True)
            s_t = jnp.dot(k_rows, q_heads[j], preferred_element_type=F32) - ck
            s_dst[j] = s_t
            max_dst[j] = jnp.max(s_t, axis=0, keepdims=True)

    def step(kb, src, nxt):
        s_src, max_src = src
        start = pl.multiple_of(kb * ATT_TK, ATT_TK)
        if nxt is not None:
            scores(kb + 1, nxt)
        for j in range(2):
            if nxt is None:
                _diagonal_update(s_src.at[j], None, vt_ref, j * HEAD_DIM, HEAD_DIM, start,
                                 m_ref, acc_ref, j)
            else:
                vt_aug = _value_rows(vt_ref, j * HEAD_DIM, HEAD_DIM, start)
                _softmax_update(s_src[j], max_src[j], vt_aug, m_ref, acc_ref, j)

    _run_key_tiles(qb, first, scores, step, buf_a, buf_b)

    o_t = jnp.concatenate(
        [acc_ref[j, :HEAD_DIM, :] / acc_ref[j, HEAD_DIM:HEAD_DIM + 1, :] for j in range(2)], axis=0)
    o_ref[...] = o_t.T.astype(o_ref.dtype)


def _fox_attention(first, qt, k, vt, c):
    nb = qt.shape[0]
    pairs = FOX_HEADS // 2
    acc_rows = HEAD_DIM + ONES_ROWS
    return pl.pallas_call(
        _fox_body,
        grid_spec=pltpu.PrefetchScalarGridSpec(
            num_scalar_prefetch=1,
            grid=(nb, pairs, LP // ATT_TQ),
            in_specs=[
                pl.BlockSpec((None, LANES, ATT_TQ), lambda bi, hp, qi, _: (bi, hp, qi)),
                pl.BlockSpec((None, LP, LANES), lambda bi, hp, qi, _: (bi, 0, hp)),
                pl.BlockSpec((None, LANES, LP), lambda bi, hp, qi, _: (bi, hp, 0)),
                pl.BlockSpec((None, LP, SMALL_W), lambda bi, hp, qi, _: (bi, 0, 0)),
            ],
            out_specs=pl.BlockSpec((None, ATT_TQ, LANES), lambda bi, hp, qi, _: (bi, qi, hp)),
            scratch_shapes=[pltpu.VMEM((2, 1, ATT_TQ), F32),
                            pltpu.VMEM((2, acc_rows, ATT_TQ), F32)]
                           + [pltpu.VMEM((2, ATT_TK, ATT_TQ), F32),
                              pltpu.VMEM((2, 1, ATT_TQ), F32)] * 2),
        out_shape=jax.ShapeDtypeStruct((nb, LP, FOX_WIDTH), BF16),
        compiler_params=pltpu.CompilerParams(
            dimension_semantics=("parallel", "parallel", "arbitrary"),
            vmem_limit_bytes=VMEM_LIMIT),
        name="fox_attention",
    )(first, qt, k, vt, c)


FORGOTTEN_LOG2 = 160.0
NORM_SLACK = 1.01


def _first_key_tiles(qt, k, c2):
    nb = qt.shape[0]
    n_q = LP // ATT_TQ
    heads = jnp.asarray(np.arange(FOX_WIDTH)[:, None] // HEAD_DIM == np.arange(FOX_HEADS)[None, :], F32)
    q_sq = jnp.einsum("bdl,dh->bhl", (qt * qt).astype(F32), heads)
    k_sq = jnp.einsum("bld,dh->blh", (k * k).astype(F32), heads)
    q_max = NORM_SLACK * jnp.sqrt(jnp.max(q_sq.reshape(nb, FOX_HEADS, n_q, ATT_TQ), axis=-1))
    k_max = NORM_SLACK * jnp.sqrt(jnp.max(k_sq, axis=1))
    c_heads = c2[:, :, SSD_HEADS:SSD_HEADS + FOX_HEADS]
    c_query = c_heads[:, ::ATT_TQ, :].transpose(0, 2, 1)
    c_key = c_heads[:, ATT_TK - 1::ATT_TK, :].transpose(0, 2, 1)
    bound = (2.0 * q_max * k_max[:, :, None])[:, :, :, None] - (c_key[:, :, None, :] - c_query[:, :, :, None])
    needed = bound >= -FORGOTTEN_LOG2
    needed = needed.reshape(nb, FOX_HEADS // 2, 2, n_q, n_q).any(axis=2)
    needed = needed | (jnp.arange(n_q)[None, :] >= jnp.arange(n_q)[:, None])
    return jnp.argmax(needed, axis=-1).astype(jnp.int32).reshape(-1)


def _bias_tiles(bias_ref):
    rows = jnp.broadcast_to(bias_ref[...], (CHUNK, 2 * CHUNK))
    rolled = pltpu.roll(rows, 0, 1, stride=1, stride_axis=0)
    return rolled[:, :CHUNK], rolled[:, CHUNK:]


def _diagonal_bias(tiles):
    zeros = jnp.zeros((CHUNK, CHUNK), F32)
    rows = []
    for a in range(N_TK_CHUNKS):
        rows.append(jnp.concatenate(
            [tiles[b - a] if b - a in (0, 1) else zeros for b in range(N_TQ_CHUNKS)], axis=1))
    return jnp.concatenate(rows, axis=0)


def _diff_body(lambda_init, qt_ref, k_ref, vt_ref, bias_ref, lam_ref, subln_ref, o_ref,
               m_ref, acc_ref, tiles_ref, sa_ref, pa_ref, sb_ref, pb_ref):
    qb = pl.program_id(2)
    _init_softmax_state(m_ref, acc_ref)
    buf_a, buf_b = (sa_ref, pa_ref), (sb_ref, pb_ref)
    tile0, tile1 = _bias_tiles(bias_ref)
    tiles_ref[0] = tile0
    tiles_ref[1] = tile1
    q_parts = _query_operands(qt_ref)
    pad_rows = -(-PADL // CHUNK) * CHUNK
    corner0 = ATT_TK - CHUNK

    def scores(kb, buf):
        s_dst, max_dst = buf
        start = pl.multiple_of(kb * ATT_TK, ATT_TK)
        k_rows = k_ref[pl.ds(start, ATT_TK), :]
        kpos = lax.broadcasted_iota(jnp.int32, (pad_rows, ATT_TQ), 0)
        not_padding = (kpos >= PADL) | (kb > 0)
        for j in range(2):
            s_t = jnp.dot(k_rows, q_parts[j], preferred_element_type=F32)
            s_t = jnp.concatenate(
                [jnp.where(not_padding, s_t[:pad_rows], NEG), s_t[pad_rows:]], axis=0)
            s_dst[j] = s_t
            max_dst[j] = jnp.max(s_t, axis=0, keepdims=True)

    def step(kb, src, nxt):
        s_src, max_src = src
        start = pl.multiple_of(kb * ATT_TK, ATT_TK)
        diagonal = nxt is None
        if diagonal:
            bias = _diagonal_bias((tiles_ref[0], tiles_ref[1]))
            for j in range(2):
                _diagonal_update(s_src.at[j], bias, vt_ref, 0, LANES, start, m_ref, acc_ref, j)
            return
        scores(kb + 1, nxt)
        corner = jnp.where(kb == qb - 1, 1.0, 0.0) * tiles_ref[1]
        vt_aug = _value_rows(vt_ref, 0, LANES, start)
        for j in range(2):
            s_t = s_src[j]
            near = s_t[corner0:, :CHUNK] + corner
            s_t = jnp.concatenate(
                [s_t[:corner0], jnp.concatenate([near, s_t[corner0:, CHUNK:]], axis=1)], axis=0)
            first = jnp.maximum(jnp.max(s_t[:corner0, :CHUNK], axis=0, keepdims=True),
                                jnp.max(near, axis=0, keepdims=True))
            col_max = jnp.concatenate([first, max_src[j][:, CHUNK:]], axis=1)
            _softmax_update(s_t, col_max, vt_aug, m_ref, acc_ref, j)

    _run_key_tiles(qb, 0, scores, step, buf_a, buf_b)

    lam1 = jnp.exp(jnp.sum(lam_ref[0:1, :] * lam_ref[1:2, :], axis=-1, keepdims=True))
    lam2 = jnp.exp(jnp.sum(lam_ref[2:3, :] * lam_ref[3:4, :], axis=-1, keepdims=True))
    lam = lam1 - lam2 + lambda_init
    o_t = (acc_ref[0, :LANES, :] / acc_ref[0, LANES:LANES + 1, :]
           - lam * (acc_ref[1, :LANES, :] / acc_ref[1, LANES:LANES + 1, :]))
    o = o_t.T
    ms = jnp.mean(o * o, axis=-1, keepdims=True)
    o = o * lax.rsqrt(ms + RMS_EPS) * subln_ref[...] * (1.0 - lambda_init)
    o_ref[...] = o.astype(o_ref.dtype)


def _diff_attention(qt, k, vt, bias_rows, lam_rows, subln, lambda_init):
    nb = qt.shape[0]
    acc_rows = LANES + ONES_ROWS
    return pl.pallas_call(
        functools.partial(_diff_body, lambda_init),
        grid=(nb, DIFF_HEADS, LP // ATT_TQ),
        in_specs=[
            pl.BlockSpec((None, LANES, ATT_TQ), lambda bi, hd, qi: (bi, hd, qi)),
            pl.BlockSpec((None, LP, LANES), lambda bi, hd, qi: (bi, 0, hd)),
            pl.BlockSpec((None, LANES, LP), lambda bi, hd, qi: (bi, hd, 0)),
            pl.BlockSpec((None, 1, 2 * CHUNK), lambda bi, hd, qi: (hd, 0, 0)),
            _const_spec(lam_rows.shape),
            _const_spec(subln.shape),
        ],
        out_specs=pl.BlockSpec((None, ATT_TQ, LANES), lambda bi, hd, qi: (bi, qi, hd)),
        out_shape=jax.ShapeDtypeStruct((nb, LP, DIFF_HEADS * LANES), BF16),
        scratch_shapes=[pltpu.VMEM((2, 1, ATT_TQ), F32),
                        pltpu.VMEM((2, acc_rows, ATT_TQ), F32),
                        pltpu.VMEM((2, CHUNK, CHUNK), F32)]
                       + [pltpu.VMEM((2, ATT_TK, ATT_TQ), F32),
                          pltpu.VMEM((2, 1, ATT_TQ), F32)] * 2,
        compiler_params=pltpu.CompilerParams(
            dimension_semantics=("parallel", "parallel", "arbitrary"),
            vmem_limit_bytes=VMEM_LIMIT),
        name="diff_attention",
    )(qt, k, vt, bias_rows, lam_rows, subln)


def _t5_bucket(n):
    max_exact = N_BUCKETS // 2
    nf = jnp.maximum(n, 1).astype(F32)
    large = max_exact + (jnp.log(nf / max_exact) / math.log(128 / max_exact)
                         * (N_BUCKETS - max_exact)).astype(jnp.int32)
    large = jnp.minimum(large, N_BUCKETS - 1)
    return jnp.where(n < max_exact, n, large)


def _relative_bias_rows(rel_table):
    dist = jnp.arange(2 * CHUNK)
    by_dist = LOG2E * (rel_table[_t5_bucket(dist)] - rel_table[N_BUCKETS - 1])
    return by_dist.T[:, None, :].astype(F32)


def _pad_lanes(vec, width=LANES):
    return jnp.pad(vec, (0, width - vec.shape[0]))[None, :].astype(F32)


def kernel(x, meta_tokens, ln_gain, ln_bias, ffn1_w_gate, ffn1_w_up, ffn1_w_down, ffn2_w_gate, ffn2_w_up, ffn2_w_down, even_w_in, even_conv_w, even_conv_b, ssd_dt_bias, ssd_a_log, ssd_d_skip, ssd_norm_w, fox_f_bias, even_w_out, diff_w_qkv, diff_lambda_q1, diff_lambda_k1, diff_lambda_q2, diff_lambda_k2, diff_subln_w, diff_w_o, rel_bias_table):
    nb = x.shape[0]
    lead = jnp.concatenate([jnp.zeros((PADL, D_MODEL), x.dtype), meta_tokens.astype(x.dtype)], axis=0)

    def ln_params(l, i):
        return ln_gain[l, i][None, :], ln_bias[l, i][None, :]

    ffn1 = tuple(w.astype(BF16) for w in (ffn1_w_gate, ffn1_w_up, ffn1_w_down))
    ffn2 = tuple(w.astype(BF16) for w in (ffn2_w_gate, ffn2_w_up, ffn2_w_down))

    hf = _ffn_ln_first(x, lead, *ffn1, 0, *ln_params(0, 0))

    w_in = even_w_in[0]
    o_z, o_xbc = 0, SSD_D_INNER
    o_dt = o_xbc + SSD_CONV_CH
    o_q = o_dt + SSD_HEADS
    o_k, o_v = o_q + FOX_WIDTH, o_q + 2 * FOX_WIDTH
    o_f = o_q + 3 * FOX_WIDTH
    bias = fox_f_bias[0]
    head = jnp.arange(FOX_HEADS)
    before = (bias[None, :] < bias[:, None]) | ((bias[None, :] == bias[:, None])
                                                & (head[None, :] < head[:, None]))
    rank = jnp.sum(before, axis=1)
    order = jnp.argmax(rank[None, :] == head[:, None], axis=1)

    def by_head(w):
        return jnp.take(w.reshape((FOX_HEADS, HEAD_DIM) + w.shape[1:]), order, axis=0).reshape(w.shape)

    wt_q, wt_k, wt_v = (by_head(w.T.astype(BF16)) for w in
                        (w_in[:, o_q:o_k] * Q_SCALE, w_in[:, o_k:o_v], w_in[:, o_v:o_f]))
    w_small = jnp.concatenate(
        [w_in[:, o_dt:o_q], w_in[:, o_f:][:, order],
         jnp.zeros((D_MODEL, SMALL_W - SSD_HEADS - FOX_HEADS), w_in.dtype)], axis=1)
    w_even = jnp.concatenate(
        [w_in[:, o_z:o_dt].astype(BF16), wt_k.T, w_small.astype(BF16)], axis=1)
    wt_even = jnp.concatenate([wt_q, wt_v], axis=0)
    z, xbc, k, small, qt, vt = _proj(
        hf, w_even, wt_even, (SSD_D_INNER, SSD_CONV_CH, FOX_WIDTH, SMALL_W),
        (F32, F32, BF16, F32), (FOX_WIDTH, FOX_WIDTH), nb, "even_in_proj")

    sbias = _pad_lanes(jnp.concatenate([ssd_dt_bias[0], fox_f_bias[0][order]]))
    alog = _pad_lanes(ssd_a_log[0])
    dskip = jnp.repeat(ssd_d_skip[0], SSD_HEAD_DIM)[None, :].astype(F32)
    expand = np.zeros((SMALL_W, SSD_D_INNER), np.float32)
    expand[np.arange(SSD_D_INNER) // SSD_HEAD_DIM, np.arange(SSD_D_INNER)] = 1.0
    y, cfull = _ssd(xbc.reshape(nb, LP, SSD_CONV_CH), z.reshape(nb, LP, SSD_D_INNER),
                    small.reshape(nb, LP, SMALL_W), even_conv_w[0], even_conv_b[0][None, :],
                    sbias, alog, dskip, ssd_norm_w[0][None, :],
                    jnp.asarray(np.concatenate([expand, expand], axis=0), BF16))
    c2 = LOG2E * cfull
    k = k.reshape(nb, LP, FOX_WIDTH)
    ck = jnp.where(jnp.arange(LP)[None, :, None] < PADL, -NEG, c2)
    o = _fox_attention(_first_key_tiles(qt, k, c2), qt, k, vt, ck)
    w_out = even_w_out[0].astype(BF16)
    hf = _outproj_ln(hf, [y.reshape(nb * LP, SSD_D_INNER), o.reshape(nb * LP, FOX_WIDTH)],
                     [w_out[:SSD_D_INNER], by_head(w_out[SSD_D_INNER:])], *ln_params(0, 1),
                     name="even_out_proj_ln")
    hf = _ffn_ln(hf, *ffn2, 0, *ln_params(0, 2))

    hf = _ffn_ln(hf, *ffn1, 1, *ln_params(1, 0))
    qw = DIFF_HEADS * 2 * HEAD_DIM
    w_qkv = diff_w_qkv[0]
    wt_diff = jnp.concatenate([w_qkv[:, :qw] * Q_SCALE, w_qkv[:, 2 * qw:]], axis=1).T.astype(BF16)
    k, qt, vt = _proj(hf, w_qkv[:, qw:2 * qw].astype(BF16), wt_diff, (qw,), (BF16,),
                      (qw, DIFF_HEADS * LANES), nb, "diff_qkv_proj")
    lambda_init = 0.8 - 0.6 * math.exp(-0.3 * 1)
    lam_rows = jnp.concatenate(
        [_pad_lanes(diff_lambda_q1[0]), _pad_lanes(diff_lambda_k1[0]),
         _pad_lanes(diff_lambda_q2[0]), _pad_lanes(diff_lambda_k2[0]),
         jnp.zeros((4, LANES), F32)], axis=0)
    o = _diff_attention(qt, k.reshape(nb, LP, qw), vt, _relative_bias_rows(rel_bias_table),
                        lam_rows, diff_subln_w[0][None, :], lambda_init)
    hf = _outproj_ln(hf, [o.reshape(nb * LP, DIFF_HEADS * LANES)], [diff_w_o[0].astype(BF16)],
                     *ln_params(1, 1), name="diff_out_proj_ln")
    return _ffn_ln_final(hf.reshape(nb, LP, D_MODEL), *ffn2, 1, *ln_params(1, 2))
```

```python
import functools
import math

import numpy as np
import jax
import jax.numpy as jnp
from jax import lax
from jax.experimental import pallas as pl
from jax.experimental.pallas import tpu as pltpu

F32 = jnp.float32
BF16 = jnp.bfloat16

D_MODEL = 1024
SEQ = 8192
DEPTH = 2
N_META = 16
CHUNK = 128
SSD_D_INNER = 2048
SSD_HEAD_DIM = 64
SSD_HEADS = 32
SSD_GROUPS = 4
SSD_GROUP_W = SSD_D_INNER // SSD_GROUPS
SSD_STATE = 128
SSD_CONV = 4
SSD_CONV_CH = SSD_D_INNER + 2 * SSD_GROUPS * SSD_STATE
FOX_HEADS = 16
FOX_WIDTH = 1024
HEAD_DIM = 64
DIFF_HEADS = 8
N_BUCKETS = 32
D_FF = 2816
ALPHA = (2 * DEPTH) ** 0.25
LN_EPS = 1e-5
RMS_EPS = 1e-5
NEG = -1e30

LANES = 128
LP = 8448
PADL = LP - SEQ - N_META
ROW_TILE = 512
OUT_TILE = 256
ATT_TQ = 768
ATT_TK = 768
CONV_HALO = 8
SMALL_W = LANES
VMEM_LIMIT = 56 * 1024 * 1024

assert PADL % CHUNK == CHUNK - N_META
assert LP % ATT_TQ == 0 and ATT_TQ % ATT_TK == 0 and ATT_TK % LANES == 0
assert LP % CHUNK == 0 and (2 * LP) % ROW_TILE == 0
assert PADL <= OUT_TILE and (LP - OUT_TILE) == SEQ


def _const_spec(shape):
    nd = len(shape)
    return pl.BlockSpec(shape, lambda *_: (0,) * nd, pipeline_mode=pl.Buffered(1))


def _layer_spec(shape, layer):
    nd = len(shape)
    return pl.BlockSpec((None,) + tuple(shape), lambda *_: (layer,) + (0,) * nd,
                        pipeline_mode=pl.Buffered(1))


def _ffn_weight_specs(layer):
    return [_layer_spec((D_MODEL, D_FF), layer), _layer_spec((D_MODEL, D_FF), layer),
            _layer_spec((D_FF, D_MODEL), layer)]


def _layer_norm(r, g, b):
    mu = jnp.mean(r, axis=-1, keepdims=True)
    d = r - mu
    var = jnp.mean(d * d, axis=-1, keepdims=True)
    return d * lax.rsqrt(var + LN_EPS) * g + b


LOG2E = math.log2(math.e)


def _silu(x):
    return x / (1.0 + jnp.exp2(x * -LOG2E))


def _softplus(x):
    return jnp.maximum(x, 0.0) + jnp.log(1.0 + jnp.exp(-jnp.abs(x)))


def _ffn_ln_body(h_ref, wg_ref, wu_ref, wd_ref, g_ref, b_ref, o_ref):
    h = h_ref[...]
    hb = h.astype(BF16)
    g = jnp.dot(hb, wg_ref[...], preferred_element_type=F32)
    u = jnp.dot(hb, wu_ref[...], preferred_element_type=F32)
    a = (_silu(g) * u).astype(BF16)
    y = jnp.dot(a, wd_ref[...], preferred_element_type=F32)
    o_ref[...] = _layer_norm(ALPHA * h + 0.5 * y, g_ref[...], b_ref[...])


def _ffn_ln(hf, wg, wu, wd, layer, g, b):
    rows = hf.shape[0]
    return pl.pallas_call(
        _ffn_ln_body,
        grid=(rows // ROW_TILE,),
        in_specs=[
            pl.BlockSpec((ROW_TILE, D_MODEL), lambda i: (i, 0)),
            *_ffn_weight_specs(layer),
            _const_spec((1, D_MODEL)),
            _const_spec((1, D_MODEL)),
        ],
        out_specs=pl.BlockSpec((ROW_TILE, D_MODEL), lambda i: (i, 0)),
        out_shape=jax.ShapeDtypeStruct((rows, D_MODEL), F32),
        compiler_params=pltpu.CompilerParams(
            dimension_semantics=("parallel",), vmem_limit_bytes=VMEM_LIMIT),
        name="ffn_ln",
    )(hf, wg, wu, wd, g, b)


def _ffn_ln_first_body(lead_ref, x_ref, wg_ref, wu_ref, wd_ref, g_ref, b_ref, o_ref, h_ref):
    h_ref[...] = jnp.where(pl.program_id(1) == 0, lead_ref[...], x_ref[...])
    _ffn_ln_body(h_ref, wg_ref, wu_ref, wd_ref, g_ref, b_ref, o_ref)


def _ffn_ln_first(x, lead, wg, wu, wd, layer, g, b):
    nb = x.shape[0]
    tiles = LP // OUT_TILE
    return pl.pallas_call(
        _ffn_ln_first_body,
        grid=(nb, tiles),
        in_specs=[
            _const_spec((OUT_TILE, D_MODEL)),
            pl.BlockSpec((None, OUT_TILE, D_MODEL), lambda bi, i: (bi, jnp.maximum(i - 1, 0), 0)),
            *_ffn_weight_specs(layer),
            _const_spec((1, D_MODEL)),
            _const_spec((1, D_MODEL)),
        ],
        out_specs=pl.BlockSpec((OUT_TILE, D_MODEL), lambda bi, i: (bi * tiles + i, 0)),
        out_shape=jax.ShapeDtypeStruct((nb * LP, D_MODEL), F32),
        scratch_shapes=[pltpu.VMEM((OUT_TILE, D_MODEL), F32)],
        compiler_params=pltpu.CompilerParams(
            dimension_semantics=("parallel", "arbitrary"), vmem_limit_bytes=VMEM_LIMIT),
        name="ffn_ln_first",
    )(lead, x, wg, wu, wd, g, b)


def _ffn_ln_final_body(h_ref, a_ref, wo_ref, g1_ref, b1_ref, wg_ref, wu_ref, wd_ref, g_ref, b_ref,
                       o_ref, mid_ref):
    @pl.when(pl.program_id(1) > 0)
    def _():
        mixed = jnp.dot(a_ref[...], wo_ref[...], preferred_element_type=F32)
        mid_ref[...] = _layer_norm(ALPHA * h_ref[...] + mixed, g1_ref[...], b1_ref[...])
        _ffn_ln_body(mid_ref, wg_ref, wu_ref, wd_ref, g_ref, b_ref, o_ref)


def _ffn_ln_final(h3, a3, wo, g1, b1, wg, wu, wd, layer, g, b):
    nb = h3.shape[0]
    row_spec = lambda n: pl.BlockSpec((None, OUT_TILE, n), lambda bi, i: (bi, i, 0))
    return pl.pallas_call(
        _ffn_ln_final_body,
        grid=(nb, LP // OUT_TILE),
        in_specs=[
            row_spec(D_MODEL), row_spec(a3.shape[-1]), _const_spec(wo.shape),
            _const_spec((1, D_MODEL)), _const_spec((1, D_MODEL)),
            *_ffn_weight_specs(layer),
            _const_spec((1, D_MODEL)),
            _const_spec((1, D_MODEL)),
        ],
        out_specs=pl.BlockSpec((None, OUT_TILE, D_MODEL),
                               lambda bi, i: (bi, jnp.maximum(i - 1, 0), 0)),
        out_shape=jax.ShapeDtypeStruct((nb, SEQ, D_MODEL), F32),
        scratch_shapes=[pltpu.VMEM((OUT_TILE, D_MODEL), F32)],
        compiler_params=pltpu.CompilerParams(
            dimension_semantics=("arbitrary", "arbitrary"), vmem_limit_bytes=VMEM_LIMIT),
        name="ffn_ln_final",
    )(h3, a3, wo, g1, b1, wg, wu, wd, g, b)


PROJ_TILE = 256
assert LP % PROJ_TILE == 0


def _proj_body(n_t, h_ref, w_ref, wt_ref, *o_refs):
    hb = h_ref[...].astype(BF16)
    off = 0
    for o_ref in o_refs[:-n_t]:
        n = o_ref.shape[-1]
        o_ref[...] = jnp.dot(hb, w_ref[:, off:off + n],
                             preferred_element_type=F32).astype(o_ref.dtype)
        off += n
    off = 0
    for o_ref in o_refs[-n_t:]:
        n = o_ref.shape[0]
        o_ref[...] = lax.dot_general(wt_ref[off:off + n, :], hb, (((1,), (1,)), ((), ())),
                                     preferred_element_type=F32).astype(o_ref.dtype)
        off += n


def _proj(hf, w, wt, widths, dtypes, t_widths, nb, name):
    rows = hf.shape[0]
    tiles_per_batch = LP // PROJ_TILE
    out_specs = [pl.BlockSpec((PROJ_TILE, n), lambda i: (i, 0)) for n in widths]
    out_specs += [pl.BlockSpec((None, n, PROJ_TILE),
                               lambda i: (i // tiles_per_batch, 0, i % tiles_per_batch))
                  for n in t_widths]
    out_shape = [jax.ShapeDtypeStruct((rows, n), dt) for n, dt in zip(widths, dtypes)]
    out_shape += [jax.ShapeDtypeStruct((nb, n, LP), BF16) for n in t_widths]
    return pl.pallas_call(
        functools.partial(_proj_body, len(t_widths)),
        grid=(rows // PROJ_TILE,),
        in_specs=[pl.BlockSpec((PROJ_TILE, D_MODEL), lambda i: (i, 0)),
                  _const_spec(w.shape), _const_spec(wt.shape)],
        out_specs=out_specs,
        out_shape=out_shape,
        compiler_params=pltpu.CompilerParams(
            dimension_semantics=("parallel",), vmem_limit_bytes=VMEM_LIMIT),
        name=name,
    )(hf, w, wt)


def _outproj_ln_body(n_in, h_ref, *refs):
    a_refs = refs[:n_in]
    w_refs = refs[n_in:2 * n_in]
    g_ref, b_ref, o_ref = refs[2 * n_in:]
    m = jnp.dot(a_refs[0][...], w_refs[0][...], preferred_element_type=F32)
    for a_ref, w_ref in zip(a_refs[1:], w_refs[1:]):
        m = m + jnp.dot(a_ref[...], w_ref[...], preferred_element_type=F32)
    o_ref[...] = _layer_norm(ALPHA * h_ref[...] + m, g_ref[...], b_ref[...])


def _outproj_ln(hf, acts, ws, g, b, name):
    rows = hf.shape[0]
    n_in = len(acts)
    in_specs = [pl.BlockSpec((ROW_TILE, D_MODEL), lambda i: (i, 0))]
    in_specs += [pl.BlockSpec((ROW_TILE, a.shape[1]), lambda i: (i, 0)) for a in acts]
    in_specs += [_const_spec(w.shape) for w in ws]
    in_specs += [_const_spec((1, D_MODEL)), _const_spec((1, D_MODEL))]
    return pl.pallas_call(
        functools.partial(_outproj_ln_body, n_in),
        grid=(rows // ROW_TILE,),
        in_specs=in_specs,
        out_specs=pl.BlockSpec((ROW_TILE, D_MODEL), lambda i: (i, 0)),
        out_shape=jax.ShapeDtypeStruct((rows, D_MODEL), F32),
        compiler_params=pltpu.CompilerParams(
            dimension_semantics=("parallel",), vmem_limit_bytes=VMEM_LIMIT),
        name=name,
    )(hf, *acts, *ws, g, b)


def _split_dot(x, e2_ref):
    hi = x.astype(BF16)
    lo = (x - hi.astype(F32)).astype(BF16)
    return jnp.dot(jnp.concatenate([hi, lo], axis=1), e2_ref[...], preferred_element_type=F32)


def _ssd_body(xbc_ref, z_ref, small_ref, convw_ref, convb_ref, sbias_ref, alog_ref,
              dskip_ref, normw_ref, expand_ref, y_ref, c_ref,
              ext_ref, state_ref, carry_ref):
    c = pl.program_id(1)

    @pl.when(c == 0)
    def _():
        ext_ref[0:CONV_HALO, :] = jnp.zeros((CONV_HALO, SSD_CONV_CH), F32)
        state_ref[...] = jnp.zeros_like(state_ref)
        carry_ref[...] = jnp.zeros_like(carry_ref)

    row = lax.broadcasted_iota(jnp.int32, (CHUNK, 1), 0)
    valid = (c * CHUNK + row) >= PADL

    ext_ref[CONV_HALO:, :] = xbc_ref[...]

    @pl.when(c * CHUNK < PADL)
    def _():
        ext_ref[CONV_HALO:, :] = jnp.where(valid, ext_ref[CONV_HALO:, :], 0.0)

    ext = ext_ref[...]
    conv = convb_ref[...] + convw_ref[SSD_CONV - 1:SSD_CONV, :] * ext[CONV_HALO:, :]
    for k in range(SSD_CONV - 1):
        shifted = pltpu.roll(ext, SSD_CONV - 1 - k, 0)[CONV_HALO:, :]
        conv = conv + convw_ref[k:k + 1, :] * shifted
    ext_ref[0:CONV_HALO, :] = ext_ref[CHUNK:CHUNK + CONV_HALO, :]
    xc = _silu(conv)
    xs = xc[:, :SSD_D_INNER]
    bm = xc[:, SSD_D_INNER:SSD_D_INNER + SSD_GROUPS * SSD_STATE].astype(BF16)
    cm = xc[:, SSD_D_INNER + SSD_GROUPS * SSD_STATE:].astype(BF16)

    lane = lax.broadcasted_iota(jnp.int32, (CHUNK, SMALL_W), 1)
    is_dt = lane < SSD_HEADS
    is_f = (lane >= SSD_HEADS) & (lane < SSD_HEADS + FOX_HEADS)
    v = small_ref[...] + sbias_ref[...]
    dt = jnp.where(valid & is_dt, _softplus(v), 0.0)
    log_f = jnp.where(valid & is_f, -_softplus(-v), 0.0)
    neg_a = -jnp.exp(alog_ref[...])
    steps = jnp.where(is_dt, dt * neg_a, log_f)
    r_i = lax.broadcasted_iota(jnp.int32, (CHUNK, CHUNK), 0)
    c_i = lax.broadcasted_iota(jnp.int32, (CHUNK, CHUNK), 1)
    causal = r_i >= c_i
    tril = jnp.where(causal, 1.0, 0.0).astype(F32)
    cum = jnp.dot(tril, steps, preferred_element_type=F32,
                  precision=lax.Precision.HIGHEST)
    c_total = jnp.where(is_f, cum + carry_ref[...], 0.0)
    c_ref[...] = c_total
    carry_ref[...] = c_total[CHUNK - 1:CHUNK, :]

    cum2 = cum * LOG2E
    a_last = cum2[CHUNK - 1:CHUNK, :]
    dt_x = _split_dot(dt, expand_ref)
    ea_x = _split_dot(jnp.exp2(cum2), expand_ref)
    de_x = _split_dot(jnp.exp2(a_last - cum2), expand_ref)
    x_dt = xs * dt_x
    xb = x_dt.astype(BF16)
    xe = (x_dt * de_x).astype(BF16)
    cum_t = cum2.T
    lane_p = lax.broadcasted_iota(jnp.int32, (CHUNK, 2 * SSD_HEAD_DIM), 1)
    first_half = lane_p < SSD_HEAD_DIM

    y_groups = []
    for g in range(SSD_GROUPS):
        gs = slice(g * SSD_GROUP_W, (g + 1) * SSD_GROUP_W)
        bg = bm[:, g * SSD_STATE:(g + 1) * SSD_STATE]
        cg = cm[:, g * SSD_STATE:(g + 1) * SSD_STATE]
        cb = lax.dot_general(cg, bg, (((1,), (1,)), ((), ())),
                             preferred_element_type=F32)
        pair_out = []
        for pr in range(SSD_GROUP_W // (2 * SSD_HEAD_DIM)):
            col0 = g * SSD_GROUP_W + pr * 2 * SSD_HEAD_DIM
            x_pair = xb[:, col0:col0 + 2 * SSD_HEAD_DIM]
            ys = []
            for j in range(2):
                hd = col0 // SSD_HEAD_DIM + j
                diff = cum2[:, hd:hd + 1] - cum_t[hd:hd + 1, :]
                decay = jnp.exp2(jnp.where(causal, diff, -jnp.inf))
                mat = (cb * decay).astype(BF16)
                ys.append(jnp.dot(mat, x_pair, preferred_element_type=F32))
            pair_out.append(jnp.where(first_half, ys[0], ys[1]))
        y_diag = jnp.concatenate(pair_out, axis=1)
        st = state_ref[g]
        y_off = jnp.dot(cg, st.astype(BF16), preferred_element_type=F32) * ea_x[:, gs]
        new = lax.dot_general(bg, xe[:, gs], (((0,), (0,)), ((), ())),
                              preferred_element_type=F32)
        state_ref[g] = st * ea_x[CHUNK - 1:CHUNK, gs] + new
        yg = y_diag + y_off + dskip_ref[:, gs] * xs[:, gs]
        yg = yg * _silu(z_ref[:, gs])
        ms = jnp.mean(yg * yg, axis=-1, keepdims=True)
        y_groups.append(yg * lax.rsqrt(ms + RMS_EPS) * normw_ref[:, gs])
    y_ref[...] = jnp.concatenate(y_groups, axis=1).astype(y_ref.dtype)


def _ssd(xbc, z, small, convw, convb, sbias, alog, dskip, normw, expand):
    nb = xbc.shape[0]
    row_spec = lambda w: pl.BlockSpec((None, CHUNK, w), lambda bi, ci: (bi, ci, 0))
    return pl.pallas_call(
        _ssd_body,
        grid=(nb, LP // CHUNK),
        in_specs=[row_spec(SSD_CONV_CH), row_spec(SSD_D_INNER), row_spec(SMALL_W),
                  _const_spec(convw.shape), _const_spec(convb.shape),
                  _const_spec(sbias.shape), _const_spec(alog.shape),
                  _const_spec(dskip.shape), _const_spec(normw.shape),
                  _const_spec(expand.shape)],
        out_specs=[row_spec(SSD_D_INNER), row_spec(SMALL_W)],
        out_shape=[jax.ShapeDtypeStruct((nb, LP, SSD_D_INNER), BF16),
                   jax.ShapeDtypeStruct((nb, LP, SMALL_W), F32)],
        scratch_shapes=[pltpu.VMEM((CHUNK + CONV_HALO, SSD_CONV_CH), F32),
                        pltpu.VMEM((SSD_GROUPS, SSD_STATE, SSD_GROUP_W), F32),
                        pltpu.VMEM((1, SMALL_W), F32)],
        compiler_params=pltpu.CompilerParams(
            dimension_semantics=("arbitrary", "arbitrary"), vmem_limit_bytes=VMEM_LIMIT),
        name="ssd",
    )(xbc, z, small, convw, convb, sbias, alog, dskip, normw, expand)


N_TK_CHUNKS = ATT_TK // LANES
N_TQ_CHUNKS = ATT_TQ // LANES
ONES_ROWS = 16
assert ATT_TQ == ATT_TK and PADL <= ATT_TK
Q_SCALE = LOG2E * HEAD_DIM ** -0.5


def _init_softmax_state(m_ref, acc_ref):
    m_ref[...] = jnp.full(m_ref.shape, NEG, F32)
    acc_ref[...] = jnp.zeros_like(acc_ref)


def _run_key_tiles(qb, first, scores, step, buf_a, buf_b):
    n_below = qb - first
    odd = n_below % 2

    @pl.when(odd == 0)
    def _():
        scores(first, buf_a)

    @pl.when(odd == 1)
    def _():
        scores(first, buf_b)
        step(first, buf_b, buf_a)

    def two_below_diagonal(i, carry):
        kb = first + odd + 2 * i
        step(kb, buf_a, buf_b)
        step(kb + 1, buf_b, buf_a)
        return carry

    lax.fori_loop(0, n_below // 2, two_below_diagonal, 0)
    step(qb, buf_a, None)


def _query_operands(qt_ref):
    zero_rows = jnp.zeros((HEAD_DIM, ATT_TQ), BF16)
    return [jnp.concatenate([qt_ref[:HEAD_DIM, :], zero_rows], axis=0),
            jnp.concatenate([zero_rows, qt_ref[HEAD_DIM:, :]], axis=0)]


def _value_rows(vt_ref, row0, n_rows, start):
    vt = vt_ref[row0:row0 + n_rows, pl.ds(start, ATT_TK)]
    return jnp.concatenate([vt, jnp.ones((ONES_ROWS, ATT_TK), BF16)], axis=0)


def _softmax_update(s_t, col_max, vt_aug, m_ref, acc_ref, idx, cols=slice(None)):
    m_old = m_ref[idx, :, cols]
    m_new = jnp.maximum(m_old, col_max)
    alpha = jnp.exp2(m_old - m_new)
    p_t = jnp.exp2(s_t - m_new).astype(BF16)
    pv = jnp.dot(vt_aug, p_t, preferred_element_type=F32)
    acc_ref[idx, :, cols] = alpha * acc_ref[idx, :, cols] + pv
    m_ref[idx, :, cols] = m_new


DIAG_BLOCK = 256


def _diagonal_update(s_ref, bias, vt_ref, row0, n_rows, start, m_ref, acc_ref, idx):
    kpos = lax.broadcasted_iota(jnp.int32, (DIAG_BLOCK, DIAG_BLOCK), 0)
    qpos = lax.broadcasted_iota(jnp.int32, (DIAG_BLOCK, DIAG_BLOCK), 1)
    causal = kpos <= qpos
    for b in range(ATT_TQ // DIAG_BLOCK):
        keys = (b + 1) * DIAG_BLOCK
        cols = slice(b * DIAG_BLOCK, keys)
        s_t = s_ref[:keys, cols]
        if bias is not None:
            s_t = s_t + bias[:keys, cols]
        last = jnp.where(causal, s_t[keys - DIAG_BLOCK:], NEG)
        s_t = last if b == 0 else jnp.concatenate([s_t[:keys - DIAG_BLOCK], last], axis=0)
        col_max = jnp.max(s_t, axis=0, keepdims=True)
        vt = vt_ref[row0:row0 + n_rows, pl.ds(start, keys)]
        vt_aug = jnp.concatenate([vt, jnp.ones((ONES_ROWS, keys), BF16)], axis=0)
        _softmax_update(s_t, col_max, vt_aug, m_ref, acc_ref, idx, cols)


def _fox_body(first_ref, qt_ref, k_ref, vt_ref, c_ref, o_ref, m_ref, acc_ref,
              sa_ref, pa_ref, sb_ref, pb_ref):
    qb = pl.program_id(2)
    n_q = pl.num_programs(2)
    first = first_ref[(pl.program_id(0) * pl.num_programs(1) + pl.program_id(1)) * n_q + qb]
    first = jnp.minimum(first, qb)
    _init_softmax_state(m_ref, acc_ref)
    buf_a, buf_b = (sa_ref, pa_ref), (sb_ref, pb_ref)
    q_heads = _query_operands(qt_ref)
    lane = lax.broadcasted_iota(jnp.int32, (ATT_TK, LANES), 1)
    first_lane = SSD_HEADS + 2 * pl.program_id(1)

    def scores(kb, buf):
        s_dst, max_dst = buf
        start = pl.multiple_of(kb * ATT_TK, ATT_TK)
        k_rows = k_ref[pl.ds(start, ATT_TK), :]
        c_rows = c_ref[pl.ds(start, ATT_TK), :]
        for j in range(2):
            ck = jnp.sum(jnp.where(lane == first_lane + j, c_rows, 0.0), axis=1, keepdims=True)
            s_t = jnp.dot(k_rows, q_heads[j], preferred_element_type=F32) - ck
            s_dst[j] = s_t
            max_dst[j] = jnp.max(s_t, axis=0, keepdims=True)

    def step(kb, src, nxt):
        s_src, max_src = src
        start = pl.multiple_of(kb * ATT_TK, ATT_TK)
        if nxt is not None:
            scores(kb + 1, nxt)
        for j in range(2):
            if nxt is None:
                _diagonal_update(s_src.at[j], None, vt_ref, j * HEAD_DIM, HEAD_DIM, start,
                                 m_ref, acc_ref, j)
            else:
                vt_aug = _value_rows(vt_ref, j * HEAD_DIM, HEAD_DIM, start)
                _softmax_update(s_src[j], max_src[j], vt_aug, m_ref, acc_ref, j)

    _run_key_tiles(qb, first, scores, step, buf_a, buf_b)

    o_t = jnp.concatenate(
        [acc_ref[j, :HEAD_DIM, :] / acc_ref[j, HEAD_DIM:HEAD_DIM + 1, :] for j in range(2)], axis=0)
    o_ref[...] = o_t.T.astype(o_ref.dtype)


def _fox_attention(first, qt, k, vt, c):
    nb = qt.shape[0]
    pairs = FOX_HEADS // 2
    acc_rows = HEAD_DIM + ONES_ROWS
    return pl.pallas_call(
        _fox_body,
        grid_spec=pltpu.PrefetchScalarGridSpec(
            num_scalar_prefetch=1,
            grid=(nb, pairs, LP // ATT_TQ),
            in_specs=[
                pl.BlockSpec((None, LANES, ATT_TQ), lambda bi, hp, qi, _: (bi, hp, qi)),
                pl.BlockSpec((None, LP, LANES), lambda bi, hp, qi, _: (bi, 0, hp)),
                pl.BlockSpec((None, LANES, LP), lambda bi, hp, qi, _: (bi, hp, 0)),
                pl.BlockSpec((None, LP, SMALL_W), lambda bi, hp, qi, _: (bi, 0, 0)),
            ],
            out_specs=pl.BlockSpec((None, ATT_TQ, LANES), lambda bi, hp, qi, _: (bi, qi, hp)),
            scratch_shapes=[pltpu.VMEM((2, 1, ATT_TQ), F32),
                            pltpu.VMEM((2, acc_rows, ATT_TQ), F32)]
                           + [pltpu.VMEM((2, ATT_TK, ATT_TQ), F32),
                              pltpu.VMEM((2, 1, ATT_TQ), F32)] * 2),
        out_shape=jax.ShapeDtypeStruct((nb, LP, FOX_WIDTH), BF16),
        compiler_params=pltpu.CompilerParams(
            dimension_semantics=("parallel", "parallel", "arbitrary"),
            vmem_limit_bytes=VMEM_LIMIT),
        name="fox_attention",
    )(first, qt, k, vt, c)


FORGOTTEN_LOG2 = 160.0
NORM_SLACK = 1.01


def _first_key_tiles(qt, k, c2):
    nb = qt.shape[0]
    n_q = LP // ATT_TQ
    heads = jnp.asarray(np.arange(FOX_WIDTH)[:, None] // HEAD_DIM == np.arange(FOX_HEADS)[None, :], F32)
    q_sq = jnp.einsum("bdl,dh->bhl", (qt * qt).astype(F32), heads)
    k_sq = jnp.einsum("bld,dh->blh", (k * k).astype(F32), heads)
    q_max = NORM_SLACK * jnp.sqrt(jnp.max(q_sq.reshape(nb, FOX_HEADS, n_q, ATT_TQ), axis=-1))
    k_max = NORM_SLACK * jnp.sqrt(jnp.max(k_sq, axis=1))
    c_heads = c2[:, :, SSD_HEADS:SSD_HEADS + FOX_HEADS]
    c_query = c_heads[:, ::ATT_TQ, :].transpose(0, 2, 1)
    c_key = c_heads[:, ATT_TK - 1::ATT_TK, :].transpose(0, 2, 1)
    bound = (2.0 * q_max * k_max[:, :, None])[:, :, :, None] - (c_key[:, :, None, :] - c_query[:, :, :, None])
    needed = bound >= -FORGOTTEN_LOG2
    needed = needed.reshape(nb, FOX_HEADS // 2, 2, n_q, n_q).any(axis=2)
    needed = needed | (jnp.arange(n_q)[None, :] >= jnp.arange(n_q)[:, None])
    return jnp.argmax(needed, axis=-1).astype(jnp.int32).reshape(-1)


def _bias_tiles(bias_ref):
    rows = jnp.broadcast_to(bias_ref[...], (CHUNK, 2 * CHUNK))
    rolled = pltpu.roll(rows, 0, 1, stride=1, stride_axis=0)
    return rolled[:, :CHUNK], rolled[:, CHUNK:]


def _diagonal_bias(tiles):
    zeros = jnp.zeros((CHUNK, CHUNK), F32)
    rows = []
    for a in range(N_TK_CHUNKS):
        rows.append(jnp.concatenate(
            [tiles[b - a] if b - a in (0, 1) else zeros for b in range(N_TQ_CHUNKS)], axis=1))
    return jnp.concatenate(rows, axis=0)


def _diff_body(lambda_init, qt_ref, k_ref, vt_ref, bias_ref, lam_ref, subln_ref, o_ref,
               m_ref, acc_ref, tiles_ref, sa_ref, pa_ref, sb_ref, pb_ref):
    qb = pl.program_id(2)
    _init_softmax_state(m_ref, acc_ref)
    buf_a, buf_b = (sa_ref, pa_ref), (sb_ref, pb_ref)
    tile0, tile1 = _bias_tiles(bias_ref)
    tiles_ref[0] = tile0
    tiles_ref[1] = tile1
    q_parts = _query_operands(qt_ref)
    pad_rows = -(-PADL // CHUNK) * CHUNK
    corner0 = ATT_TK - CHUNK

    def scores(kb, buf):
        s_dst, max_dst = buf
        start = pl.multiple_of(kb * ATT_TK, ATT_TK)
        k_rows = k_ref[pl.ds(start, ATT_TK), :]
        kpos = lax.broadcasted_iota(jnp.int32, (pad_rows, ATT_TQ), 0)
        not_padding = (kpos >= PADL) | (kb > 0)
        for j in range(2):
            s_t = jnp.dot(k_rows, q_parts[j], preferred_element_type=F32)
            s_t = jnp.concatenate(
                [jnp.where(not_padding, s_t[:pad_rows], NEG), s_t[pad_rows:]], axis=0)
            s_dst[j] = s_t
            max_dst[j] = jnp.max(s_t, axis=0, keepdims=True)

    def step(kb, src, nxt):
        s_src, max_src = src
        start = pl.multiple_of(kb * ATT_TK, ATT_TK)
        diagonal = nxt is None
        if diagonal:
            bias = _diagonal_bias((tiles_ref[0], tiles_ref[1]))
            for j in range(2):
                _diagonal_update(s_src.at[j], bias, vt_ref, 0, LANES, start, m_ref, acc_ref, j)
            return
        scores(kb + 1, nxt)
        corner = jnp.where(kb == qb - 1, 1.0, 0.0) * tiles_ref[1]
        vt_aug = _value_rows(vt_ref, 0, LANES, start)
        for j in range(2):
            s_t = s_src[j]
            near = s_t[corner0:, :CHUNK] + corner
            s_t = jnp.concatenate(
                [s_t[:corner0], jnp.concatenate([near, s_t[corner0:, CHUNK:]], axis=1)], axis=0)
            first = jnp.maximum(jnp.max(s_t[:corner0, :CHUNK], axis=0, keepdims=True),
                                jnp.max(near, axis=0, keepdims=True))
            col_max = jnp.concatenate([first, max_src[j][:, CHUNK:]], axis=1)
            _softmax_update(s_t, col_max, vt_aug, m_ref, acc_ref, j)

    _run_key_tiles(qb, 0, scores, step, buf_a, buf_b)

    lam1 = jnp.exp(jnp.sum(lam_ref[0:1, :] * lam_ref[1:2, :], axis=-1, keepdims=True))
    lam2 = jnp.exp(jnp.sum(lam_ref[2:3, :] * lam_ref[3:4, :], axis=-1, keepdims=True))
    lam = lam1 - lam2 + lambda_init
    o_t = (acc_ref[0, :LANES, :] / acc_ref[0, LANES:LANES + 1, :]
           - lam * (acc_ref[1, :LANES, :] / acc_ref[1, LANES:LANES + 1, :]))
    o = o_t.T
    ms = jnp.mean(o * o, axis=-1, keepdims=True)
    o = o * lax.rsqrt(ms + RMS_EPS) * subln_ref[...] * (1.0 - lambda_init)
    o_ref[...] = o.astype(o_ref.dtype)


def _diff_attention(qt, k, vt, bias_rows, lam_rows, subln, lambda_init):
    nb = qt.shape[0]
    acc_rows = LANES + ONES_ROWS
    return pl.pallas_call(
        functools.partial(_diff_body, lambda_init),
        grid=(nb, DIFF_HEADS, LP // ATT_TQ),
        in_specs=[
            pl.BlockSpec((None, LANES, ATT_TQ), lambda bi, hd, qi: (bi, hd, qi)),
            pl.BlockSpec((None, LP, LANES), lambda bi, hd, qi: (bi, 0, hd)),
            pl.BlockSpec((None, LANES, LP), lambda bi, hd, qi: (bi, hd, 0)),
            pl.BlockSpec((None, 1, 2 * CHUNK), lambda bi, hd, qi: (hd, 0, 0)),
            _const_spec(lam_rows.shape),
            _const_spec(subln.shape),
        ],
        out_specs=pl.BlockSpec((None, ATT_TQ, LANES), lambda bi, hd, qi: (bi, qi, hd)),
        out_shape=jax.ShapeDtypeStruct((nb, LP, DIFF_HEADS * LANES), BF16),
        scratch_shapes=[pltpu.VMEM((2, 1, ATT_TQ), F32),
                        pltpu.VMEM((2, acc_rows, ATT_TQ), F32),
                        pltpu.VMEM((2, CHUNK, CHUNK), F32)]
                       + [pltpu.VMEM((2, ATT_TK, ATT_TQ), F32),
                          pltpu.VMEM((2, 1, ATT_TQ), F32)] * 2,
        compiler_params=pltpu.CompilerParams(
            dimension_semantics=("parallel", "parallel", "arbitrary"),
            vmem_limit_bytes=VMEM_LIMIT),
        name="diff_attention",
    )(qt, k, vt, bias_rows, lam_rows, subln)


def _t5_bucket(n):
    max_exact = N_BUCKETS // 2
    nf = jnp.maximum(n, 1).astype(F32)
    large = max_exact + (jnp.log(nf / max_exact) / math.log(128 / max_exact)
                         * (N_BUCKETS - max_exact)).astype(jnp.int32)
    large = jnp.minimum(large, N_BUCKETS - 1)
    return jnp.where(n < max_exact, n, large)


def _relative_bias_rows(rel_table):
    dist = jnp.arange(2 * CHUNK)
    by_dist = LOG2E * (rel_table[_t5_bucket(dist)] - rel_table[N_BUCKETS - 1])
    return by_dist.T[:, None, :].astype(F32)


def _pad_lanes(vec, width=LANES):
    return jnp.pad(vec, (0, width - vec.shape[0]))[None, :].astype(F32)


def kernel(x, meta_tokens, ln_gain, ln_bias, ffn1_w_gate, ffn1_w_up, ffn1_w_down, ffn2_w_gate, ffn2_w_up, ffn2_w_down, even_w_in, even_conv_w, even_conv_b, ssd_dt_bias, ssd_a_log, ssd_d_skip, ssd_norm_w, fox_f_bias, even_w_out, diff_w_qkv, diff_lambda_q1, diff_lambda_k1, diff_lambda_q2, diff_lambda_k2, diff_subln_w, diff_w_o, rel_bias_table):
    nb = x.shape[0]
    lead = jnp.concatenate([jnp.zeros((PADL, D_MODEL), x.dtype), meta_tokens.astype(x.dtype)], axis=0)

    def ln_params(l, i):
        return ln_gain[l, i][None, :], ln_bias[l, i][None, :]

    ffn1 = tuple(w.astype(BF16) for w in (ffn1_w_gate, ffn1_w_up, ffn1_w_down))
    ffn2 = tuple(w.astype(BF16) for w in (ffn2_w_gate, ffn2_w_up, ffn2_w_down))

    hf = _ffn_ln_first(x, lead, *ffn1, 0, *ln_params(0, 0))

    w_in = even_w_in[0]
    o_z, o_xbc = 0, SSD_D_INNER
    o_dt = o_xbc + SSD_CONV_CH
    o_q = o_dt + SSD_HEADS
    o_k, o_v = o_q + FOX_WIDTH, o_q + 2 * FOX_WIDTH
    o_f = o_q + 3 * FOX_WIDTH
    bias = fox_f_bias[0]
    head = jnp.arange(FOX_HEADS)
    before = (bias[None, :] < bias[:, None]) | ((bias[None, :] == bias[:, None])
                                                & (head[None, :] < head[:, None]))
    rank = jnp.sum(before, axis=1)
    order = jnp.argmax(rank[None, :] == head[:, None], axis=1)

    def by_head(w):
        return jnp.take(w.reshape((FOX_HEADS, HEAD_DIM) + w.shape[1:]), order, axis=0).reshape(w.shape)

    wt_q, wt_k, wt_v = (by_head(w.T.astype(BF16)) for w in
                        (w_in[:, o_q:o_k] * Q_SCALE, w_in[:, o_k:o_v], w_in[:, o_v:o_f]))
    w_small = jnp.concatenate(
        [w_in[:, o_dt:o_q], w_in[:, o_f:][:, order],
         jnp.zeros((D_MODEL, SMALL_W - SSD_HEADS - FOX_HEADS), w_in.dtype)], axis=1)
    w_even = jnp.concatenate(
        [w_in[:, o_z:o_dt].astype(BF16), wt_k.T, w_small.astype(BF16)], axis=1)
    wt_even = jnp.concatenate([wt_q, wt_v], axis=0)
    z, xbc, k, small, qt, vt = _proj(
        hf, w_even, wt_even, (SSD_D_INNER, SSD_CONV_CH, FOX_WIDTH, SMALL_W),
        (F32, F32, BF16, F32), (FOX_WIDTH, FOX_WIDTH), nb, "even_in_proj")

    sbias = _pad_lanes(jnp.concatenate([ssd_dt_bias[0], fox_f_bias[0][order]]))
    alog = _pad_lanes(ssd_a_log[0])
    dskip = jnp.repeat(ssd_d_skip[0], SSD_HEAD_DIM)[None, :].astype(F32)
    expand = np.zeros((SMALL_W, SSD_D_INNER), np.float32)
    expand[np.arange(SSD_D_INNER) // SSD_HEAD_DIM, np.arange(SSD_D_INNER)] = 1.0
    y, cfull = _ssd(xbc.reshape(nb, LP, SSD_CONV_CH), z.reshape(nb, LP, SSD_D_INNER),
                    small.reshape(nb, LP, SMALL_W), even_conv_w[0], even_conv_b[0][None, :],
                    sbias, alog, dskip, ssd_norm_w[0][None, :],
                    jnp.asarray(np.concatenate([expand, expand], axis=0), BF16))
    c2 = LOG2E * cfull
    k = k.reshape(nb, LP, FOX_WIDTH)
    ck = jnp.where(jnp.arange(LP)[None, :, None] < PADL, -NEG, c2)
    o = _fox_attention(_first_key_tiles(qt, k, c2), qt, k, vt, ck)
    w_out = even_w_out[0].astype(BF16)
    hf = _outproj_ln(hf, [y.reshape(nb * LP, SSD_D_INNER), o.reshape(nb * LP, FOX_WIDTH)],
                     [w_out[:SSD_D_INNER], by_head(w_out[SSD_D_INNER:])], *ln_params(0, 1),
                     name="even_out_proj_ln")
    hf = _ffn_ln(hf, *ffn2, 0, *ln_params(0, 2))

    hf = _ffn_ln(hf, *ffn1, 1, *ln_params(1, 0))
    qw = DIFF_HEADS * 2 * HEAD_DIM
    w_qkv = diff_w_qkv[0]
    wt_diff = jnp.concatenate([w_qkv[:, :qw] * Q_SCALE, w_qkv[:, 2 * qw:]], axis=1).T.astype(BF16)
    k, qt, vt = _proj(hf, w_qkv[:, qw:2 * qw].astype(BF16), wt_diff, (qw,), (BF16,),
                      (qw, DIFF_HEADS * LANES), nb, "diff_qkv_proj")
    lambda_init = 0.8 - 0.6 * math.exp(-0.3 * 1)
    lam_rows = jnp.concatenate(
        [_pad_lanes(diff_lambda_q1[0]), _pad_lanes(diff_lambda_k1[0]),
         _pad_lanes(diff_lambda_q2[0]), _pad_lanes(diff_lambda_k2[0]),
         jnp.zeros((4, LANES), F32)], axis=0)
    o = _diff_attention(qt, k.reshape(nb, LP, qw), vt, _relative_bias_rows(rel_bias_table),
                        lam_rows, diff_subln_w[0][None, :], lambda_init)
    return _ffn_ln_final(hf.reshape(nb, LP, D_MODEL), o, diff_w_o[0].astype(BF16), *ln_params(1, 1),
                         *ffn2, 1, *ln_params(1, 2))
```

```python
import functools
import math

import numpy as np
import jax
import jax.numpy as jnp
from jax import lax
from jax.experimental import pallas as pl
from jax.experimental.pallas import tpu as pltpu

F32 = jnp.float32
BF16 = jnp.bfloat16

D_MODEL = 1024
SEQ = 8192
DEPTH = 2
N_META = 16
CHUNK = 128
SSD_D_INNER = 2048
SSD_HEAD_DIM = 64
SSD_HEADS = 32
SSD_GROUPS = 4
SSD_GROUP_W = SSD_D_INNER // SSD_GROUPS
SSD_STATE = 128
SSD_CONV = 4
SSD_CONV_CH = SSD_D_INNER + 2 * SSD_GROUPS * SSD_STATE
FOX_HEADS = 16
FOX_WIDTH = 1024
HEAD_DIM = 64
DIFF_HEADS = 8
N_BUCKETS = 32
D_FF = 2816
ALPHA = (2 * DEPTH) ** 0.25
LN_EPS = 1e-5
RMS_EPS = 1e-5
NEG = -1e30

LANES = 128
LP = 8448
PADL = LP - SEQ - N_META
ROW_TILE = 512
OUT_TILE = 256
ATT_TQ = 768
ATT_TK = 768
CONV_HALO = 8
SMALL_W = LANES
VMEM_LIMIT = 56 * 1024 * 1024

assert PADL % CHUNK == CHUNK - N_META
assert LP % ATT_TQ == 0 and ATT_TQ % ATT_TK == 0 and ATT_TK % LANES == 0
assert LP % CHUNK == 0 and (2 * LP) % ROW_TILE == 0
assert PADL <= OUT_TILE and (LP - OUT_TILE) == SEQ


def _const_spec(shape):
    nd = len(shape)
    return pl.BlockSpec(shape, lambda *_: (0,) * nd, pipeline_mode=pl.Buffered(1))


def _layer_spec(shape, layer):
    nd = len(shape)
    return pl.BlockSpec((None,) + tuple(shape), lambda *_: (layer,) + (0,) * nd,
                        pipeline_mode=pl.Buffered(1))


def _ffn_weight_specs(layer):
    return [_layer_spec((D_MODEL, D_FF), layer), _layer_spec((D_MODEL, D_FF), layer),
            _layer_spec((D_FF, D_MODEL), layer)]


def _layer_norm(r, g, b):
    mu = jnp.mean(r, axis=-1, keepdims=True)
    d = r - mu
    var = jnp.mean(d * d, axis=-1, keepdims=True)
    return d * lax.rsqrt(var + LN_EPS) * g + b


LOG2E = math.log2(math.e)


def _silu(x):
    return x / (1.0 + jnp.exp2(x * -LOG2E))


def _softplus(x):
    return jnp.maximum(x, 0.0) + jnp.log(1.0 + jnp.exp(-jnp.abs(x)))


def _ffn_ln_body(h_ref, wg_ref, wu_ref, wd_ref, g_ref, b_ref, o_ref):
    h = h_ref[...]
    hb = h.astype(BF16)
    g = jnp.dot(hb, wg_ref[...], preferred_element_type=F32)
    u = jnp.dot(hb, wu_ref[...], preferred_element_type=F32)
    a = (_silu(g) * u).astype(BF16)
    y = jnp.dot(a, wd_ref[...], preferred_element_type=F32)
    o_ref[...] = _layer_norm(ALPHA * h + 0.5 * y, g_ref[...], b_ref[...])


def _ffn_ln(hf, wg, wu, wd, layer, g, b):
    rows = hf.shape[0]
    return pl.pallas_call(
        _ffn_ln_body,
        grid=(rows // ROW_TILE,),
        in_specs=[
            pl.BlockSpec((ROW_TILE, D_MODEL), lambda i: (i, 0)),
            *_ffn_weight_specs(layer),
            _const_spec((1, D_MODEL)),
            _const_spec((1, D_MODEL)),
        ],
        out_specs=pl.BlockSpec((ROW_TILE, D_MODEL), lambda i: (i, 0)),
        out_shape=jax.ShapeDtypeStruct((rows, D_MODEL), F32),
        compiler_params=pltpu.CompilerParams(
            dimension_semantics=("parallel",), vmem_limit_bytes=VMEM_LIMIT),
        name="ffn_ln",
    )(hf, wg, wu, wd, g, b)


def _mix_ffn_ln_body(n_in, h_ref, *refs):
    a_refs, w_refs = refs[:n_in], refs[n_in:2 * n_in]
    g1_ref, b1_ref, wg_ref, wu_ref, wd_ref, g_ref, b_ref, o_ref, mid_ref = refs[2 * n_in:]
    mixed = jnp.dot(a_refs[0][...], w_refs[0][...], preferred_element_type=F32)
    for a_ref, w_ref in zip(a_refs[1:], w_refs[1:]):
        mixed = mixed + jnp.dot(a_ref[...], w_ref[...], preferred_element_type=F32)
    mid_ref[...] = _layer_norm(ALPHA * h_ref[...] + mixed, g1_ref[...], b1_ref[...])
    _ffn_ln_body(mid_ref, wg_ref, wu_ref, wd_ref, g_ref, b_ref, o_ref)


def _mix_ffn_ln(hf, acts, ws, g1, b1, wg, wu, wd, layer, g, b):
    rows = hf.shape[0]
    row_spec = lambda n: pl.BlockSpec((OUT_TILE, n), lambda i: (i, 0))
    vec = _const_spec((1, D_MODEL))
    return pl.pallas_call(
        functools.partial(_mix_ffn_ln_body, len(acts)),
        grid=(rows // OUT_TILE,),
        in_specs=[row_spec(D_MODEL)] + [row_spec(a.shape[1]) for a in acts]
                 + [_const_spec(w.shape) for w in ws] + [vec, vec]
                 + _ffn_weight_specs(layer) + [vec, vec],
        out_specs=row_spec(D_MODEL),
        out_shape=jax.ShapeDtypeStruct((rows, D_MODEL), F32),
        scratch_shapes=[pltpu.VMEM((OUT_TILE, D_MODEL), F32)],
        compiler_params=pltpu.CompilerParams(
            dimension_semantics=("parallel",), vmem_limit_bytes=VMEM_LIMIT),
        name="mix_ffn_ln",
    )(hf, *acts, *ws, g1, b1, wg, wu, wd, g, b)


def _ffn_ln_first_body(lead_ref, x_ref, wg_ref, wu_ref, wd_ref, g_ref, b_ref, o_ref, h_ref):
    h_ref[...] = jnp.where(pl.program_id(1) == 0, lead_ref[...], x_ref[...])
    _ffn_ln_body(h_ref, wg_ref, wu_ref, wd_ref, g_ref, b_ref, o_ref)


def _ffn_ln_first(x, lead, wg, wu, wd, layer, g, b):
    nb = x.shape[0]
    tiles = LP // OUT_TILE
    return pl.pallas_call(
        _ffn_ln_first_body,
        grid=(nb, tiles),
        in_specs=[
            _const_spec((OUT_TILE, D_MODEL)),
            pl.BlockSpec((None, OUT_TILE, D_MODEL), lambda bi, i: (bi, jnp.maximum(i - 1, 0), 0)),
            *_ffn_weight_specs(layer),
            _const_spec((1, D_MODEL)),
            _const_spec((1, D_MODEL)),
        ],
        out_specs=pl.BlockSpec((OUT_TILE, D_MODEL), lambda bi, i: (bi * tiles + i, 0)),
        out_shape=jax.ShapeDtypeStruct((nb * LP, D_MODEL), F32),
        scratch_shapes=[pltpu.VMEM((OUT_TILE, D_MODEL), F32)],
        compiler_params=pltpu.CompilerParams(
            dimension_semantics=("parallel", "arbitrary"), vmem_limit_bytes=VMEM_LIMIT),
        name="ffn_ln_first",
    )(lead, x, wg, wu, wd, g, b)


def _ffn_ln_final_body(h_ref, a_ref, wo_ref, g1_ref, b1_ref, wg_ref, wu_ref, wd_ref, g_ref, b_ref,
                       o_ref, mid_ref):
    @pl.when(pl.program_id(1) > 0)
    def _():
        mixed = jnp.dot(a_ref[...], wo_ref[...], preferred_element_type=F32)
        mid_ref[...] = _layer_norm(ALPHA * h_ref[...] + mixed, g1_ref[...], b1_ref[...])
        _ffn_ln_body(mid_ref, wg_ref, wu_ref, wd_ref, g_ref, b_ref, o_ref)


def _ffn_ln_final(h3, a3, wo, g1, b1, wg, wu, wd, layer, g, b):
    nb = h3.shape[0]
    row_spec = lambda n: pl.BlockSpec((None, OUT_TILE, n), lambda bi, i: (bi, i, 0))
    return pl.pallas_call(
        _ffn_ln_final_body,
        grid=(nb, LP // OUT_TILE),
        in_specs=[
            row_spec(D_MODEL), row_spec(a3.shape[-1]), _const_spec(wo.shape),
            _const_spec((1, D_MODEL)), _const_spec((1, D_MODEL)),
            *_ffn_weight_specs(layer),
            _const_spec((1, D_MODEL)),
            _const_spec((1, D_MODEL)),
        ],
        out_specs=pl.BlockSpec((None, OUT_TILE, D_MODEL),
                               lambda bi, i: (bi, jnp.maximum(i - 1, 0), 0)),
        out_shape=jax.ShapeDtypeStruct((nb, SEQ, D_MODEL), F32),
        scratch_shapes=[pltpu.VMEM((OUT_TILE, D_MODEL), F32)],
        compiler_params=pltpu.CompilerParams(
            dimension_semantics=("arbitrary", "arbitrary"), vmem_limit_bytes=VMEM_LIMIT),
        name="ffn_ln_final",
    )(h3, a3, wo, g1, b1, wg, wu, wd, g, b)


PROJ_TILE = 256
assert LP % PROJ_TILE == 0


def _proj_body(n_t, h_ref, w_ref, wt_ref, *o_refs):
    hb = h_ref[...].astype(BF16)
    off = 0
    for o_ref in o_refs[:-n_t]:
        n = o_ref.shape[-1]
        o_ref[...] = jnp.dot(hb, w_ref[:, off:off + n],
                             preferred_element_type=F32).astype(o_ref.dtype)
        off += n
    off = 0
    for o_ref in o_refs[-n_t:]:
        n = o_ref.shape[0]
        o_ref[...] = lax.dot_general(wt_ref[off:off + n, :], hb, (((1,), (1,)), ((), ())),
                                     preferred_element_type=F32).astype(o_ref.dtype)
        off += n


def _proj(hf, w, wt, widths, dtypes, t_widths, nb, name):
    rows = hf.shape[0]
    tiles_per_batch = LP // PROJ_TILE
    out_specs = [pl.BlockSpec((PROJ_TILE, n), lambda i: (i, 0)) for n in widths]
    out_specs += [pl.BlockSpec((None, n, PROJ_TILE),
                               lambda i: (i // tiles_per_batch, 0, i % tiles_per_batch))
                  for n in t_widths]
    out_shape = [jax.ShapeDtypeStruct((rows, n), dt) for n, dt in zip(widths, dtypes)]
    out_shape += [jax.ShapeDtypeStruct((nb, n, LP), BF16) for n in t_widths]
    return pl.pallas_call(
        functools.partial(_proj_body, len(t_widths)),
        grid=(rows // PROJ_TILE,),
        in_specs=[pl.BlockSpec((PROJ_TILE, D_MODEL), lambda i: (i, 0)),
                  _const_spec(w.shape), _const_spec(wt.shape)],
        out_specs=out_specs,
        out_shape=out_shape,
        compiler_params=pltpu.CompilerParams(
            dimension_semantics=("parallel",), vmem_limit_bytes=VMEM_LIMIT),
        name=name,
    )(hf, w, wt)


def _split_dot(x, e2_ref):
    hi = x.astype(BF16)
    lo = (x - hi.astype(F32)).astype(BF16)
    return jnp.dot(jnp.concatenate([hi, lo], axis=1), e2_ref[...], preferred_element_type=F32)


def _ssd_body(xbc_ref, z_ref, small_ref, convw_ref, convb_ref, sbias_ref, alog_ref,
              dskip_ref, normw_ref, expand_ref, y_ref, c_ref,
              ext_ref, state_ref, carry_ref):
    c = pl.program_id(1)

    @pl.when(c == 0)
    def _():
        ext_ref[0:CONV_HALO, :] = jnp.zeros((CONV_HALO, SSD_CONV_CH), F32)
        state_ref[...] = jnp.zeros_like(state_ref)
        carry_ref[...] = jnp.zeros_like(carry_ref)

    row = lax.broadcasted_iota(jnp.int32, (CHUNK, 1), 0)
    valid = (c * CHUNK + row) >= PADL

    ext_ref[CONV_HALO:, :] = xbc_ref[...]

    @pl.when(c * CHUNK < PADL)
    def _():
        ext_ref[CONV_HALO:, :] = jnp.where(valid, ext_ref[CONV_HALO:, :], 0.0)

    ext = ext_ref[...]
    conv = convb_ref[...] + convw_ref[SSD_CONV - 1:SSD_CONV, :] * ext[CONV_HALO:, :]
    for k in range(SSD_CONV - 1):
        shifted = pltpu.roll(ext, SSD_CONV - 1 - k, 0)[CONV_HALO:, :]
        conv = conv + convw_ref[k:k + 1, :] * shifted
    ext_ref[0:CONV_HALO, :] = ext_ref[CHUNK:CHUNK + CONV_HALO, :]
    xc = _silu(conv)
    xs = xc[:, :SSD_D_INNER]
    bm = xc[:, SSD_D_INNER:SSD_D_INNER + SSD_GROUPS * SSD_STATE].astype(BF16)
    cm = xc[:, SSD_D_INNER + SSD_GROUPS * SSD_STATE:].astype(BF16)

    lane = lax.broadcasted_iota(jnp.int32, (CHUNK, SMALL_W), 1)
    is_dt = lane < SSD_HEADS
    is_f = (lane >= SSD_HEADS) & (lane < SSD_HEADS + FOX_HEADS)
    v = small_ref[...] + sbias_ref[...]
    dt = jnp.where(valid & is_dt, _softplus(v), 0.0)
    log_f = jnp.where(valid & is_f, -_softplus(-v), 0.0)
    neg_a = -jnp.exp(alog_ref[...])
    steps = jnp.where(is_dt, dt * neg_a, log_f)
    r_i = lax.broadcasted_iota(jnp.int32, (CHUNK, CHUNK), 0)
    c_i = lax.broadcasted_iota(jnp.int32, (CHUNK, CHUNK), 1)
    causal = r_i >= c_i
    tril = jnp.where(causal, 1.0, 0.0).astype(F32)
    cum = jnp.dot(tril, steps, preferred_element_type=F32,
                  precision=lax.Precision.HIGHEST)
    c_total = jnp.where(is_f, cum + carry_ref[...], 0.0)
    c_ref[...] = c_total
    carry_ref[...] = c_total[CHUNK - 1:CHUNK, :]

    cum2 = cum * LOG2E
    a_last = cum2[CHUNK - 1:CHUNK, :]
    dt_x = _split_dot(dt, expand_ref)
    ea_x = _split_dot(jnp.exp2(cum2), expand_ref)
    de_x = _split_dot(jnp.exp2(a_last - cum2), expand_ref)
    x_dt = xs * dt_x
    xb = x_dt.astype(BF16)
    xe = (x_dt * de_x).astype(BF16)
    cum_t = cum2.T
    lane_p = lax.broadcasted_iota(jnp.int32, (CHUNK, 2 * SSD_HEAD_DIM), 1)
    first_half = lane_p < SSD_HEAD_DIM

    y_groups = []
    for g in range(SSD_GROUPS):
        gs = slice(g * SSD_GROUP_W, (g + 1) * SSD_GROUP_W)
        bg = bm[:, g * SSD_STATE:(g + 1) * SSD_STATE]
        cg = cm[:, g * SSD_STATE:(g + 1) * SSD_STATE]
        cb = lax.dot_general(cg, bg, (((1,), (1,)), ((), ())),
                             preferred_element_type=F32)
        pair_out = []
        for pr in range(SSD_GROUP_W // (2 * SSD_HEAD_DIM)):
            col0 = g * SSD_GROUP_W + pr * 2 * SSD_HEAD_DIM
            x_pair = xb[:, col0:col0 + 2 * SSD_HEAD_DIM]
            ys = []
            for j in range(2):
                hd = col0 // SSD_HEAD_DIM + j
                diff = cum2[:, hd:hd + 1] - cum_t[hd:hd + 1, :]
                decay = jnp.exp2(jnp.where(causal, diff, -jnp.inf))
                mat = (cb * decay).astype(BF16)
                ys.append(jnp.dot(mat, x_pair, preferred_element_type=F32))
            pair_out.append(jnp.where(first_half, ys[0], ys[1]))
        y_diag = jnp.concatenate(pair_out, axis=1)
        st = state_ref[g]
        y_off = jnp.dot(cg, st.astype(BF16), preferred_element_type=F32) * ea_x[:, gs]
        new = lax.dot_general(bg, xe[:, gs], (((0,), (0,)), ((), ())),
                              preferred_element_type=F32)
        state_ref[g] = st * ea_x[CHUNK - 1:CHUNK, gs] + new
        yg = y_diag + y_off + dskip_ref[:, gs] * xs[:, gs]
        yg = yg * _silu(z_ref[:, gs])
        ms = jnp.mean(yg * yg, axis=-1, keepdims=True)
        y_groups.append(yg * lax.rsqrt(ms + RMS_EPS) * normw_ref[:, gs])
    y_ref[...] = jnp.concatenate(y_groups, axis=1).astype(y_ref.dtype)


def _ssd(xbc, z, small, convw, convb, sbias, alog, dskip, normw, expand):
    nb = xbc.shape[0]
    row_spec = lambda w: pl.BlockSpec((None, CHUNK, w), lambda bi, ci: (bi, ci, 0))
    return pl.pallas_call(
        _ssd_body,
        grid=(nb, LP // CHUNK),
        in_specs=[row_spec(SSD_CONV_CH), row_spec(SSD_D_INNER), row_spec(SMALL_W),
                  _const_spec(convw.shape), _const_spec(convb.shape),
                  _const_spec(sbias.shape), _const_spec(alog.shape),
                  _const_spec(dskip.shape), _const_spec(normw.shape),
                  _const_spec(expand.shape)],
        out_specs=[row_spec(SSD_D_INNER), row_spec(SMALL_W)],
        out_shape=[jax.ShapeDtypeStruct((nb, LP, SSD_D_INNER), BF16),
                   jax.ShapeDtypeStruct((nb, LP, SMALL_W), F32)],
        scratch_shapes=[pltpu.VMEM((CHUNK + CONV_HALO, SSD_CONV_CH), F32),
                        pltpu.VMEM((SSD_GROUPS, SSD_STATE, SSD_GROUP_W), F32),
                        pltpu.VMEM((1, SMALL_W), F32)],
        compiler_params=pltpu.CompilerParams(
            dimension_semantics=("arbitrary", "arbitrary"), vmem_limit_bytes=VMEM_LIMIT),
        name="ssd",
    )(xbc, z, small, convw, convb, sbias, alog, dskip, normw, expand)


N_TK_CHUNKS = ATT_TK // LANES
N_TQ_CHUNKS = ATT_TQ // LANES
ONES_ROWS = 16
assert ATT_TQ == ATT_TK and PADL <= ATT_TK
Q_SCALE = LOG2E * HEAD_DIM ** -0.5


def _init_softmax_state(m_ref, acc_ref):
    m_ref[...] = jnp.full(m_ref.shape, NEG, F32)
    acc_ref[...] = jnp.zeros_like(acc_ref)


def _run_key_tiles(qb, first, scores, step, buf_a, buf_b):
    n_below = qb - first
    odd = n_below % 2

    @pl.when(odd == 0)
    def _():
        scores(first, buf_a)

    @pl.when(odd == 1)
    def _():
        scores(first, buf_b)
        step(first, buf_b, buf_a)

    def two_below_diagonal(i, carry):
        kb = first + odd + 2 * i
        step(kb, buf_a, buf_b)
        step(kb + 1, buf_b, buf_a)
        return carry

    lax.fori_loop(0, n_below // 2, two_below_diagonal, 0)
    step(qb, buf_a, None)


def _query_operands(qt_ref):
    zero_rows = jnp.zeros((HEAD_DIM, ATT_TQ), BF16)
    return [jnp.concatenate([qt_ref[:HEAD_DIM, :], zero_rows], axis=0),
            jnp.concatenate([zero_rows, qt_ref[HEAD_DIM:, :]], axis=0)]


def _value_rows(vt_ref, row0, n_rows, start):
    vt = vt_ref[row0:row0 + n_rows, pl.ds(start, ATT_TK)]
    return jnp.concatenate([vt, jnp.ones((ONES_ROWS, ATT_TK), BF16)], axis=0)


def _softmax_update(s_t, col_max, vt_aug, m_ref, acc_ref, idx, cols=slice(None)):
    m_old = m_ref[idx, :, cols]
    m_new = jnp.maximum(m_old, col_max)
    alpha = jnp.exp2(m_old - m_new)
    p_t = jnp.exp2(s_t - m_new).astype(BF16)
    pv = jnp.dot(vt_aug, p_t, preferred_element_type=F32)
    acc_ref[idx, :, cols] = alpha * acc_ref[idx, :, cols] + pv
    m_ref[idx, :, cols] = m_new


DIAG_BLOCK = 256


def _diagonal_update(s_ref, bias, vt_ref, row0, n_rows, start, m_ref, acc_ref, idx):
    kpos = lax.broadcasted_iota(jnp.int32, (DIAG_BLOCK, DIAG_BLOCK), 0)
    qpos = lax.broadcasted_iota(jnp.int32, (DIAG_BLOCK, DIAG_BLOCK), 1)
    causal = kpos <= qpos
    for b in range(ATT_TQ // DIAG_BLOCK):
        keys = (b + 1) * DIAG_BLOCK
        cols = slice(b * DIAG_BLOCK, keys)
        s_t = s_ref[:keys, cols]
        if bias is not None:
            s_t = s_t + bias[:keys, cols]
        last = jnp.where(causal, s_t[keys - DIAG_BLOCK:], NEG)
        s_t = last if b == 0 else jnp.concatenate([s_t[:keys - DIAG_BLOCK], last], axis=0)
        col_max = jnp.max(s_t, axis=0, keepdims=True)
        vt = vt_ref[row0:row0 + n_rows, pl.ds(start, keys)]
        vt_aug = jnp.concatenate([vt, jnp.ones((ONES_ROWS, keys), BF16)], axis=0)
        _softmax_update(s_t, col_max, vt_aug, m_ref, acc_ref, idx, cols)


def _fox_body(first_ref, qt_ref, k_ref, vt_ref, c_ref, o_ref, m_ref, acc_ref,
              sa_ref, pa_ref, sb_ref, pb_ref):
    qb = pl.program_id(2)
    n_q = pl.num_programs(2)
    first = first_ref[(pl.program_id(0) * pl.num_programs(1) + pl.program_id(1)) * n_q + qb]
    first = jnp.minimum(first, qb)
    _init_softmax_state(m_ref, acc_ref)
    buf_a, buf_b = (sa_ref, pa_ref), (sb_ref, pb_ref)
    q_heads = _query_operands(qt_ref)
    lane = lax.broadcasted_iota(jnp.int32, (ATT_TK, LANES), 1)
    first_lane = SSD_HEADS + 2 * pl.program_id(1)

    def scores(kb, buf):
        s_dst, max_dst = buf
        start = pl.multiple_of(kb * ATT_TK, ATT_TK)
        k_rows = k_ref[pl.ds(start, ATT_TK), :]
        c_rows = c_ref[pl.ds(start, ATT_TK), :]
        for j in range(2):
            ck = jnp.sum(jnp.where(lane == first_lane + j, c_rows, 0.0), axis=1, keepdims=True)
            s_t = jnp.dot(k_rows, q_heads[j], preferred_element_type=F32) - ck
            s_dst[j] = s_t
            max_dst[j] = jnp.max(s_t, axis=0, keepdims=True)

    def step(kb, src, nxt):
        s_src, max_src = src
        start = pl.multiple_of(kb * ATT_TK, ATT_TK)
        if nxt is not None:
            scores(kb + 1, nxt)
        for j in range(2):
            if nxt is None:
                _diagonal_update(s_src.at[j], None, vt_ref, j * HEAD_DIM, HEAD_DIM, start,
                                 m_ref, acc_ref, j)
            else:
                vt_aug = _value_rows(vt_ref, j * HEAD_DIM, HEAD_DIM, start)
                _softmax_update(s_src[j], max_src[j], vt_aug, m_ref, acc_ref, j)

    _run_key_tiles(qb, first, scores, step, buf_a, buf_b)

    o_t = jnp.concatenate(
        [acc_ref[j, :HEAD_DIM, :] / acc_ref[j, HEAD_DIM:HEAD_DIM + 1, :] for j in range(2)], axis=0)
    o_ref[...] = o_t.T.astype(o_ref.dtype)


def _fox_attention(first, qt, k, vt, c):
    nb = qt.shape[0]
    pairs = FOX_HEADS // 2
    acc_rows = HEAD_DIM + ONES_ROWS
    return pl.pallas_call(
        _fox_body,
        grid_spec=pltpu.PrefetchScalarGridSpec(
            num_scalar_prefetch=1,
            grid=(nb, pairs, LP // ATT_TQ),
            in_specs=[
                pl.BlockSpec((None, LANES, ATT_TQ), lambda bi, hp, qi, _: (bi, hp, qi)),
                pl.BlockSpec((None, LP, LANES), lambda bi, hp, qi, _: (bi, 0, hp)),
                pl.BlockSpec((None, LANES, LP), lambda bi, hp, qi, _: (bi, hp, 0)),
                pl.BlockSpec((None, LP, SMALL_W), lambda bi, hp, qi, _: (bi, 0, 0)),
            ],
            out_specs=pl.BlockSpec((None, ATT_TQ, LANES), lambda bi, hp, qi, _: (bi, qi, hp)),
            scratch_shapes=[pltpu.VMEM((2, 1, ATT_TQ), F32),
                            pltpu.VMEM((2, acc_rows, ATT_TQ), F32)]
                           + [pltpu.VMEM((2, ATT_TK, ATT_TQ), F32),
                              pltpu.VMEM((2, 1, ATT_TQ), F32)] * 2),
        out_shape=jax.ShapeDtypeStruct((nb, LP, FOX_WIDTH), BF16),
        compiler_params=pltpu.CompilerParams(
            dimension_semantics=("parallel", "parallel", "arbitrary"),
            vmem_limit_bytes=VMEM_LIMIT),
        name="fox_attention",
    )(first, qt, k, vt, c)


FORGOTTEN_LOG2 = 160.0
NORM_SLACK = 1.01


def _first_key_tiles(qt, k, c2):
    nb = qt.shape[0]
    n_q = LP // ATT_TQ
    heads = jnp.asarray(np.arange(FOX_WIDTH)[:, None] // HEAD_DIM == np.arange(FOX_HEADS)[None, :], F32)
    q_sq = jnp.einsum("bdl,dh->bhl", (qt * qt).astype(F32), heads)
    k_sq = jnp.einsum("bld,dh->blh", (k * k).astype(F32), heads)
    q_max = NORM_SLACK * jnp.sqrt(jnp.max(q_sq.reshape(nb, FOX_HEADS, n_q, ATT_TQ), axis=-1))
    k_max = NORM_SLACK * jnp.sqrt(jnp.max(k_sq, axis=1))
    c_heads = c2[:, :, SSD_HEADS:SSD_HEADS + FOX_HEADS]
    c_query = c_heads[:, ::ATT_TQ, :].transpose(0, 2, 1)
    c_key = c_heads[:, ATT_TK - 1::ATT_TK, :].transpose(0, 2, 1)
    bound = (2.0 * q_max * k_max[:, :, None])[:, :, :, None] - (c_key[:, :, None, :] - c_query[:, :, :, None])
    needed = bound >= -FORGOTTEN_LOG2
    needed = needed.reshape(nb, FOX_HEADS // 2, 2, n_q, n_q).any(axis=2)
    needed = needed | (jnp.arange(n_q)[None, :] >= jnp.arange(n_q)[:, None])
    return jnp.argmax(needed, axis=-1).astype(jnp.int32).reshape(-1)


def _bias_tiles(bias_ref):
    rows = jnp.broadcast_to(bias_ref[...], (CHUNK, 2 * CHUNK))
    rolled = pltpu.roll(rows, 0, 1, stride=1, stride_axis=0)
    return rolled[:, :CHUNK], rolled[:, CHUNK:]


def _diagonal_bias(tiles):
    zeros = jnp.zeros((CHUNK, CHUNK), F32)
    rows = []
    for a in range(N_TK_CHUNKS):
        rows.append(jnp.concatenate(
            [tiles[b - a] if b - a in (0, 1) else zeros for b in range(N_TQ_CHUNKS)], axis=1))
    return jnp.concatenate(rows, axis=0)


def _diff_body(lambda_init, qt_ref, k_ref, vt_ref, bias_ref, lam_ref, subln_ref, o_ref,
               m_ref, acc_ref, tiles_ref, sa_ref, pa_ref, sb_ref, pb_ref):
    qb = pl.program_id(2)
    _init_softmax_state(m_ref, acc_ref)
    buf_a, buf_b = (sa_ref, pa_ref), (sb_ref, pb_ref)
    tile0, tile1 = _bias_tiles(bias_ref)
    tiles_ref[0] = tile0
    tiles_ref[1] = tile1
    q_parts = _query_operands(qt_ref)
    pad_rows = -(-PADL // CHUNK) * CHUNK
    corner0 = ATT_TK - CHUNK

    def scores(kb, buf):
        s_dst, max_dst = buf
        start = pl.multiple_of(kb * ATT_TK, ATT_TK)
        k_rows = k_ref[pl.ds(start, ATT_TK), :]
        kpos = lax.broadcasted_iota(jnp.int32, (pad_rows, ATT_TQ), 0)
        not_padding = (kpos >= PADL) | (kb > 0)
        for j in range(2):
            s_t = jnp.dot(k_rows, q_parts[j], preferred_element_type=F32)
            s_t = jnp.concatenate(
                [jnp.where(not_padding, s_t[:pad_rows], NEG), s_t[pad_rows:]], axis=0)
            s_dst[j] = s_t
            max_dst[j] = jnp.max(s_t, axis=0, keepdims=True)

    def step(kb, src, nxt):
        s_src, max_src = src
        start = pl.multiple_of(kb * ATT_TK, ATT_TK)
        diagonal = nxt is None
        if diagonal:
            bias = _diagonal_bias((tiles_ref[0], tiles_ref[1]))
            for j in range(2):
                _diagonal_update(s_src.at[j], bias, vt_ref, 0, LANES, start, m_ref, acc_ref, j)
            return
        scores(kb + 1, nxt)
        corner = jnp.where(kb == qb - 1, 1.0, 0.0) * tiles_ref[1]
        vt_aug = _value_rows(vt_ref, 0, LANES, start)
        for j in range(2):
            s_t = s_src[j]
            near = s_t[corner0:, :CHUNK] + corner
            s_t = jnp.concatenate(
                [s_t[:corner0], jnp.concatenate([near, s_t[corner0:, CHUNK:]], axis=1)], axis=0)
            first = jnp.maximum(jnp.max(s_t[:corner0, :CHUNK], axis=0, keepdims=True),
                                jnp.max(near, axis=0, keepdims=True))
            col_max = jnp.concatenate([first, max_src[j][:, CHUNK:]], axis=1)
            _softmax_update(s_t, col_max, vt_aug, m_ref, acc_ref, j)

    _run_key_tiles(qb, 0, scores, step, buf_a, buf_b)

    lam1 = jnp.exp(jnp.sum(lam_ref[0:1, :] * lam_ref[1:2, :], axis=-1, keepdims=True))
    lam2 = jnp.exp(jnp.sum(lam_ref[2:3, :] * lam_ref[3:4, :], axis=-1, keepdims=True))
    lam = lam1 - lam2 + lambda_init
    o_t = (acc_ref[0, :LANES, :] / acc_ref[0, LANES:LANES + 1, :]
           - lam * (acc_ref[1, :LANES, :] / acc_ref[1, LANES:LANES + 1, :]))
    o = o_t.T
    ms = jnp.mean(o * o, axis=-1, keepdims=True)
    o = o * lax.rsqrt(ms + RMS_EPS) * subln_ref[...] * (1.0 - lambda_init)
    o_ref[...] = o.astype(o_ref.dtype)


def _diff_attention(qt, k, vt, bias_rows, lam_rows, subln, lambda_init):
    nb = qt.shape[0]
    acc_rows = LANES + ONES_ROWS
    return pl.pallas_call(
        functools.partial(_diff_body, lambda_init),
        grid=(nb, DIFF_HEADS, LP // ATT_TQ),
        in_specs=[
            pl.BlockSpec((None, LANES, ATT_TQ), lambda bi, hd, qi: (bi, hd, qi)),
            pl.BlockSpec((None, LP, LANES), lambda bi, hd, qi: (bi, 0, hd)),
            pl.BlockSpec((None, LANES, LP), lambda bi, hd, qi: (bi, hd, 0)),
            pl.BlockSpec((None, 1, 2 * CHUNK), lambda bi, hd, qi: (hd, 0, 0)),
            _const_spec(lam_rows.shape),
            _const_spec(subln.shape),
        ],
        out_specs=pl.BlockSpec((None, ATT_TQ, LANES), lambda bi, hd, qi: (bi, qi, hd)),
        out_shape=jax.ShapeDtypeStruct((nb, LP, DIFF_HEADS * LANES), BF16),
        scratch_shapes=[pltpu.VMEM((2, 1, ATT_TQ), F32),
                        pltpu.VMEM((2, acc_rows, ATT_TQ), F32),
                        pltpu.VMEM((2, CHUNK, CHUNK), F32)]
                       + [pltpu.VMEM((2, ATT_TK, ATT_TQ), F32),
                          pltpu.VMEM((2, 1, ATT_TQ), F32)] * 2,
        compiler_params=pltpu.CompilerParams(
            dimension_semantics=("parallel", "parallel", "arbitrary"),
            vmem_limit_bytes=VMEM_LIMIT),
        name="diff_attention",
    )(qt, k, vt, bias_rows, lam_rows, subln)


def _t5_bucket(n):
    max_exact = N_BUCKETS // 2
    nf = jnp.maximum(n, 1).astype(F32)
    large = max_exact + (jnp.log(nf / max_exact) / math.log(128 / max_exact)
                         * (N_BUCKETS - max_exact)).astype(jnp.int32)
    large = jnp.minimum(large, N_BUCKETS - 1)
    return jnp.where(n < max_exact, n, large)


def _relative_bias_rows(rel_table):
    dist = jnp.arange(2 * CHUNK)
    by_dist = LOG2E * (rel_table[_t5_bucket(dist)] - rel_table[N_BUCKETS - 1])
    return by_dist.T[:, None, :].astype(F32)


def _pad_lanes(vec, width=LANES):
    return jnp.pad(vec, (0, width - vec.shape[0]))[None, :].astype(F32)


def kernel(x, meta_tokens, ln_gain, ln_bias, ffn1_w_gate, ffn1_w_up, ffn1_w_down, ffn2_w_gate, ffn2_w_up, ffn2_w_down, even_w_in, even_conv_w, even_conv_b, ssd_dt_bias, ssd_a_log, ssd_d_skip, ssd_norm_w, fox_f_bias, even_w_out, diff_w_qkv, diff_lambda_q1, diff_lambda_k1, diff_lambda_q2, diff_lambda_k2, diff_subln_w, diff_w_o, rel_bias_table):
    nb = x.shape[0]
    lead = jnp.concatenate([jnp.zeros((PADL, D_MODEL), x.dtype), meta_tokens.astype(x.dtype)], axis=0)

    def ln_params(l, i):
        return ln_gain[l, i][None, :], ln_bias[l, i][None, :]

    ffn1 = tuple(w.astype(BF16) for w in (ffn1_w_gate, ffn1_w_up, ffn1_w_down))
    ffn2 = tuple(w.astype(BF16) for w in (ffn2_w_gate, ffn2_w_up, ffn2_w_down))

    hf = _ffn_ln_first(x, lead, *ffn1, 0, *ln_params(0, 0))

    w_in = even_w_in[0]
    o_z, o_xbc = 0, SSD_D_INNER
    o_dt = o_xbc + SSD_CONV_CH
    o_q = o_dt + SSD_HEADS
    o_k, o_v = o_q + FOX_WIDTH, o_q + 2 * FOX_WIDTH
    o_f = o_q + 3 * FOX_WIDTH
    bias = fox_f_bias[0]
    head = jnp.arange(FOX_HEADS)
    before = (bias[None, :] < bias[:, None]) | ((bias[None, :] == bias[:, None])
                                                & (head[None, :] < head[:, None]))
    rank = jnp.sum(before, axis=1)
    order = jnp.argmax(rank[None, :] == head[:, None], axis=1)

    def by_head(w):
        return jnp.take(w.reshape((FOX_HEADS, HEAD_DIM) + w.shape[1:]), order, axis=0).reshape(w.shape)

    wt_q, wt_k, wt_v = (by_head(w.T.astype(BF16)) for w in
                        (w_in[:, o_q:o_k] * Q_SCALE, w_in[:, o_k:o_v], w_in[:, o_v:o_f]))
    w_small = jnp.concatenate(
        [w_in[:, o_dt:o_q], w_in[:, o_f:][:, order],
         jnp.zeros((D_MODEL, SMALL_W - SSD_HEADS - FOX_HEADS), w_in.dtype)], axis=1)
    w_even = jnp.concatenate(
        [w_in[:, o_z:o_dt].astype(BF16), wt_k.T, w_small.astype(BF16)], axis=1)
    wt_even = jnp.concatenate([wt_q, wt_v], axis=0)
    z, xbc, k, small, qt, vt = _proj(
        hf, w_even, wt_even, (SSD_D_INNER, SSD_CONV_CH, FOX_WIDTH, SMALL_W),
        (F32, F32, BF16, F32), (FOX_WIDTH, FOX_WIDTH), nb, "even_in_proj")

    sbias = _pad_lanes(jnp.concatenate([ssd_dt_bias[0], fox_f_bias[0][order]]))
    alog = _pad_lanes(ssd_a_log[0])
    dskip = jnp.repeat(ssd_d_skip[0], SSD_HEAD_DIM)[None, :].astype(F32)
    expand = np.zeros((SMALL_W, SSD_D_INNER), np.float32)
    expand[np.arange(SSD_D_INNER) // SSD_HEAD_DIM, np.arange(SSD_D_INNER)] = 1.0
    y, cfull = _ssd(xbc.reshape(nb, LP, SSD_CONV_CH), z.reshape(nb, LP, SSD_D_INNER),
                    small.reshape(nb, LP, SMALL_W), even_conv_w[0], even_conv_b[0][None, :],
                    sbias, alog, dskip, ssd_norm_w[0][None, :],
                    jnp.asarray(np.concatenate([expand, expand], axis=0), BF16))
    c2 = LOG2E * cfull
    k = k.reshape(nb, LP, FOX_WIDTH)
    ck = jnp.where(jnp.arange(LP)[None, :, None] < PADL, -NEG, c2)
    o = _fox_attention(_first_key_tiles(qt, k, c2), qt, k, vt, ck)
    w_out = even_w_out[0].astype(BF16)
    hf = _mix_ffn_ln(hf, [y.reshape(nb * LP, SSD_D_INNER), o.reshape(nb * LP, FOX_WIDTH)],
                     [w_out[:SSD_D_INNER], by_head(w_out[SSD_D_INNER:])], *ln_params(0, 1),
                     *ffn2, 0, *ln_params(0, 2))

    hf = _ffn_ln(hf, *ffn1, 1, *ln_params(1, 0))
    qw = DIFF_HEADS * 2 * HEAD_DIM
    w_qkv = diff_w_qkv[0]
    wt_diff = jnp.concatenate([w_qkv[:, :qw] * Q_SCALE, w_qkv[:, 2 * qw:]], axis=1).T.astype(BF16)
    k, qt, vt = _proj(hf, w_qkv[:, qw:2 * qw].astype(BF16), wt_diff, (qw,), (BF16,),
                      (qw, DIFF_HEADS * LANES), nb, "diff_qkv_proj")
    lambda_init = 0.8 - 0.6 * math.exp(-0.3 * 1)
    lam_rows = jnp.concatenate(
        [_pad_lanes(diff_lambda_q1[0]), _pad_lanes(diff_lambda_k1[0]),
         _pad_lanes(diff_lambda_q2[0]), _pad_lanes(diff_lambda_k2[0]),
         jnp.zeros((4, LANES), F32)], axis=0)
    o = _diff_attention(qt, k.reshape(nb, LP, qw), vt, _relative_bias_rows(rel_bias_table),
                        lam_rows, diff_subln_w[0][None, :], lambda_init)
    return _ffn_ln_final(hf.reshape(nb, LP, D_MODEL), o, diff_w_o[0].astype(BF16), *ln_params(1, 1),
                         *ffn2, 1, *ln_params(1, 2))
```
